```python
import math
import jax, jax.numpy as jnp
from jax import lax
import numpy as np

D_MODEL = 4096
BATCH = 4
SEQ = 2048
DEPTH = 1
DEC_BATCH = 128
DEC_SEQ = 8
PAST_LEN = 8192
PAGE_SIZE = 128

HEAD_DIM = 128
ATTN_HEADS = D_MODEL // (2 * HEAD_DIM)
KV_HEADS = ATTN_HEADS // 4
Q_PER_KV = ATTN_HEADS // KV_HEADS
ATTN_WIDTH = ATTN_HEADS * HEAD_DIM
WINDOW = 128
BLOCK = 128
N_BUCKETS = 32
MAX_DISTANCE = 128
SSD_HEAD_DIM = 64
SSD_HEADS = D_MODEL // (2 * SSD_HEAD_DIM)
SSD_WIDTH = SSD_HEADS * SSD_HEAD_DIM
SSD_GROUPS = 8
HEADS_PER_GROUP = SSD_HEADS // SSD_GROUPS
SSD_STATE = 128
SSD_CONV = 4
SSD_CHUNK = 128
CONV_DIM = SSD_WIDTH + 2 * SSD_GROUPS * SSD_STATE
MIX_WIDTH = ATTN_WIDTH + SSD_WIDTH
IN_DIM = ATTN_WIDTH + 2 * KV_HEADS * HEAD_DIM + SSD_WIDTH + CONV_DIM + SSD_HEADS
D_FF = (8 * D_MODEL // 3 + 255) // 256 * 256
FFN_CONV = 3
EPS = 1e-6

kernel_name = "hymba_ssd_swa_convffn_step"


def _rms(x, w):
    xf = x.astype(jnp.float32)
    y = xf * lax.rsqrt(jnp.mean(xf * xf, axis=-1, keepdims=True) + EPS)
    return (y * w.astype(jnp.float32)).astype(x.dtype)


def _causal_dwconv(x, prev, w, b):
    K = w.shape[0]
    L = x.shape[1]
    xp = jnp.concatenate([prev.astype(x.dtype), x], axis=1)
    y = b + w[0] * xp[:, 0:L]
    for t in range(1, K):
        y = y + w[t] * xp[:, t:t + L]
    return y, xp[:, L:]


def _t5_bucket(dist):
    n = jnp.maximum(dist, 0)
    max_exact = N_BUCKETS // 2
    nf = jnp.maximum(n, 1).astype(jnp.float32)
    large = max_exact + (jnp.log(nf / max_exact) / math.log(MAX_DISTANCE / max_exact)
                         * (N_BUCKETS - max_exact)).astype(jnp.int32)
    large = jnp.minimum(large, N_BUCKETS - 1)
    return jnp.where(n < max_exact, n, large)


def _attend(q, k, v, dist, valid, sinks, rel_bias):
    n, lq, lk = dist.shape
    bias = rel_bias.astype(jnp.float32)[_t5_bucket(dist)].reshape(n, lq, lk, KV_HEADS, Q_PER_KV)
    bias = jnp.transpose(bias, (0, 3, 4, 1, 2))
    s = jnp.einsum("bnqhgd,bnshd->bnhgqs", q, k, preferred_element_type=jnp.float32)
    s = s * (HEAD_DIM ** -0.5) + bias[None]
    s = jnp.where(valid[None, :, None, None], s, -jnp.inf)
    sink = sinks.astype(jnp.float32).reshape(1, 1, KV_HEADS, Q_PER_KV, 1, 1)
    m = jnp.maximum(jnp.max(s, axis=-1, keepdims=True), sink)
    p = jnp.exp(s - m)
    probs = p / (jnp.sum(p, axis=-1, keepdims=True) + jnp.exp(sink - m))
    return jnp.einsum("bnhgqs,bnshd->bnqhgd", probs.astype(v.dtype), v)


def _attn_prompt(q, k, v, sinks, rel_bias):
    b, L = q.shape[:2]
    nb = L // BLOCK
    qb = q.reshape(b, nb, BLOCK, KV_HEADS, Q_PER_KV, HEAD_DIM)

    def band(t):
        cur = t.reshape(b, nb, BLOCK, KV_HEADS, HEAD_DIM)
        prev = jnp.concatenate([jnp.zeros_like(cur[:, :1]), cur[:, :-1]], axis=1)
        return jnp.concatenate([prev, cur], axis=2)

    i = jnp.arange(BLOCK)[:, None]
    j = jnp.arange(2 * BLOCK)[None, :]
    dist = (i + BLOCK - j)[None]
    kpos = jnp.arange(nb)[:, None, None] * BLOCK - BLOCK + j[None]
    valid = (dist >= 0) & (dist < WINDOW) & (kpos >= 0)
    o = _attend(qb, band(k), band(v), dist, valid, sinks, rel_bias)
    return o.reshape(b, L, ATTN_WIDTH)


def _attn_sample(q, k, v, win_k, win_v, sinks, rel_bias):
    b, L = q.shape[:2]
    keys = jnp.concatenate([win_k.astype(k.dtype), k], axis=1)
    vals = jnp.concatenate([win_v.astype(v.dtype), v], axis=1)
    i = jnp.arange(L)[:, None]
    j = jnp.arange(WINDOW + L)[None, :]
    dist = (i + WINDOW - j)[None]
    kpos = PAST_LEN - WINDOW + j
    valid = (dist >= 0) & (dist < WINDOW) & (kpos[None] >= 0)
    o = _attend(q.reshape(b, 1, L, KV_HEADS, Q_PER_KV, HEAD_DIM), keys[:, None], vals[:, None],
                dist, valid, sinks, rel_bias)
    return o.reshape(b, L, ATTN_WIDTH), keys[:, -WINDOW:], vals[:, -WINDOW:]


def _ssd(x, dt, A, Bm, Cm, h0):
    b, L = x.shape[:2]
    Q = SSD_CHUNK if L % SSD_CHUNK == 0 else L
    nc = L // Q
    G, Hg, P, N = SSD_GROUPS, HEADS_PER_GROUP, SSD_HEAD_DIM, SSD_STATE
    xc = x.reshape(b, nc, Q, G, Hg, P)
    dtc = dt.reshape(b, nc, Q, G, Hg)
    Bc = Bm.reshape(b, nc, Q, G, N)
    Cc = Cm.reshape(b, nc, Q, G, N)
    acs = jnp.cumsum(dtc * A.reshape(G, Hg), axis=2)
    xdt = xc * dtc[..., None]
    at = jnp.moveaxis(acs, 2, -1)
    seg = at[..., :, None] - at[..., None, :]
    causal = jnp.tril(jnp.ones((Q, Q), dtype=bool))
    Lmat = jnp.exp(jnp.where(causal, seg, -jnp.inf))
    cb = jnp.einsum("bclgn,bcsgn->bcgls", Cc, Bc)
    y_diag = jnp.einsum("bcgls,bcghls,bcsghp->bclghp", cb, Lmat, xdt)
    decay_end = jnp.exp(acs[:, :, -1:] - acs)
    st = jnp.einsum("bclgn,bclgh,bclghp->bcghpn", Bc, decay_end, xdt)
    chunk_decay = jnp.exp(acs[:, :, -1])

    def step(h, inp):
        s_c, d_c = inp
        return d_c[..., None, None] * h + s_c, h

    hT, h_in = lax.scan(step, h0.reshape(b, G, Hg, P, N),
                        (jnp.moveaxis(st, 1, 0), jnp.moveaxis(chunk_decay, 1, 0)))
    h_in = jnp.moveaxis(h_in, 0, 1)
    y_off = jnp.einsum("bclgn,bcghpn,bclgh->bclghp", Cc, h_in, jnp.exp(acs))
    y = (y_diag + y_off).reshape(b, L, SSD_HEADS, P)
    return y, hT.reshape(b, SSD_HEADS, P, N)


def _layer(x, win_k, win_v, ssm_h, ssd_prev, ffn_prev, rel_bias,
           mix_norm_w, w_in, q_norm_w, k_norm_w, attn_sinks, ssd_conv_w, ssd_conv_b,
           ssd_dt_bias, ssd_A_log, ssd_D, ssd_norm_w, w_out,
           ffn_norm_w, w_gate, w_up, ffn_conv_w, ffn_conv_b, w_down):
    b, L, _ = x.shape
    h = _rms(x, mix_norm_w)
    proj = h @ w_in
    offs = np.cumsum([ATTN_WIDTH, KV_HEADS * HEAD_DIM, KV_HEADS * HEAD_DIM, SSD_WIDTH, CONV_DIM])
    q, k, v, z, xbc, dt = jnp.split(proj, [int(o) for o in offs], axis=-1)

    q = _rms(q.reshape(b, L, ATTN_HEADS, HEAD_DIM), q_norm_w)
    k = _rms(k.reshape(b, L, KV_HEADS, HEAD_DIM), k_norm_w)
    v = v.reshape(b, L, KV_HEADS, HEAD_DIM)
    if win_k is None:
        attn = _attn_prompt(q, k, v, attn_sinks, rel_bias)
        new_k, new_v = k[:, -WINDOW:], v[:, -WINDOW:]
    else:
        attn, new_k, new_v = _attn_sample(q, k, v, win_k, win_v, attn_sinks, rel_bias)

    xbc_c, new_conv = _causal_dwconv(xbc, ssd_prev, ssd_conv_w, ssd_conv_b)
    xbc_c = jax.nn.silu(xbc_c).astype(jnp.float32)
    xs, Bm, Cm = jnp.split(xbc_c, [SSD_WIDTH, SSD_WIDTH + SSD_GROUPS * SSD_STATE], axis=-1)
    xs = xs.reshape(b, L, SSD_HEADS, SSD_HEAD_DIM)
    dtv = jax.nn.softplus(dt.astype(jnp.float32) + ssd_dt_bias.astype(jnp.float32))
    A = -jnp.exp(ssd_A_log.astype(jnp.float32))
    y, hT = _ssd(xs, dtv, A, Bm.reshape(b, L, SSD_GROUPS, SSD_STATE),
                 Cm.reshape(b, L, SSD_GROUPS, SSD_STATE), ssm_h.astype(jnp.float32))
    y = (y + ssd_D.astype(jnp.float32)[:, None] * xs).reshape(b, L, SSD_WIDTH)
    y = _rms(y * jax.nn.silu(z.astype(jnp.float32)), ssd_norm_w).astype(x.dtype)

    x = x + jnp.concatenate([attn.astype(x.dtype), y], axis=-1) @ w_out

    h2 = _rms(x, ffn_norm_w)
    g = h2 @ w_gate
    u = h2 @ w_up
    gc, new_ffn = _causal_dwconv(g, ffn_prev, ffn_conv_w, ffn_conv_b)
    x = x + (jax.nn.silu(gc) * u) @ w_down
    return x, new_k, new_v, hT.astype(x.dtype), new_conv, new_ffn


def setup_inputs(seed: int = 0) -> dict:
    key = jax.random.key(seed)
    ks = jax.random.split(key, 32)
    f32 = jnp.float32
    nrm = lambda k, shape, s=1.0: (jax.random.normal(k, shape, f32) * s)
    dt0 = jnp.exp(jax.random.uniform(ks[20], (DEPTH, SSD_HEADS), f32) * (math.log(0.1) - math.log(0.001))
                  + math.log(0.001))
    return {
        "x_prompt": nrm(ks[0], (BATCH, SEQ, D_MODEL)),
        "x_sample": nrm(ks[1], (DEC_BATCH, DEC_SEQ, D_MODEL)),
        "state_attn_k": nrm(ks[2], (DEPTH, DEC_BATCH, WINDOW, KV_HEADS, HEAD_DIM)),
        "state_attn_v": nrm(ks[3], (DEPTH, DEC_BATCH, WINDOW, KV_HEADS, HEAD_DIM)),
        "state_ssm": nrm(ks[4], (DEPTH, DEC_BATCH, SSD_HEADS, SSD_HEAD_DIM, SSD_STATE), 0.5),
        "state_ssd_conv": nrm(ks[5], (DEPTH, DEC_BATCH, SSD_CONV - 1, CONV_DIM)),
        "state_ffn_conv": nrm(ks[6], (DEPTH, DEC_BATCH, FFN_CONV - 1, D_FF)),
        "rel_bias": nrm(ks[7], (N_BUCKETS, ATTN_HEADS), 0.5),
        "mix_norm_w": 1.0 + nrm(ks[8], (DEPTH, D_MODEL), 0.02),
        "w_in": nrm(ks[9], (DEPTH, D_MODEL, IN_DIM), D_MODEL ** -0.5),
        "q_norm_w": 1.0 + nrm(ks[10], (DEPTH, HEAD_DIM), 0.02),
        "k_norm_w": 1.0 + nrm(ks[11], (DEPTH, HEAD_DIM), 0.02),
        "attn_sinks": nrm(ks[12], (DEPTH, ATTN_HEADS), 0.5),
        "ssd_conv_w": nrm(ks[13], (DEPTH, SSD_CONV, CONV_DIM), SSD_CONV ** -0.5),
        "ssd_conv_b": nrm(ks[14], (DEPTH, CONV_DIM), 0.01),
        "ssd_dt_bias": dt0 + jnp.log(-jnp.expm1(-dt0)),
        "ssd_A_log": jnp.log(jax.random.uniform(ks[15], (DEPTH, SSD_HEADS), f32, 1.0, 16.0)),
        "ssd_D": 1.0 + nrm(ks[16], (DEPTH, SSD_HEADS), 0.1),
        "ssd_norm_w": 1.0 + nrm(ks[17], (DEPTH, SSD_WIDTH), 0.02),
        "w_out": nrm(ks[18], (DEPTH, MIX_WIDTH, D_MODEL), MIX_WIDTH ** -0.5),
        "ffn_norm_w": 1.0 + nrm(ks[19], (DEPTH, D_MODEL), 0.02),
        "w_gate": nrm(ks[21], (DEPTH, D_MODEL, D_FF), D_MODEL ** -0.5),
        "w_up": nrm(ks[22], (DEPTH, D_MODEL, D_FF), D_MODEL ** -0.5),
        "ffn_conv_w": nrm(ks[23], (DEPTH, FFN_CONV, D_FF), FFN_CONV ** -0.5),
        "ffn_conv_b": nrm(ks[24], (DEPTH, D_FF), 0.01),
        "w_down": nrm(ks[25], (DEPTH, D_FF, D_MODEL), D_FF ** -0.5),
    }


def reference(x_prompt, x_sample, state_attn_k, state_attn_v, state_ssm, state_ssd_conv, state_ffn_conv,
              rel_bias, mix_norm_w, w_in, q_norm_w, k_norm_w, attn_sinks, ssd_conv_w, ssd_conv_b,
              ssd_dt_bias, ssd_A_log, ssd_D, ssd_norm_w, w_out, ffn_norm_w, w_gate, w_up,
              ffn_conv_w, ffn_conv_b, w_down):
    yp, ys = x_prompt, x_sample
    bp = x_prompt.shape[0]
    outs_p, outs_s = [], []
    for l in range(DEPTH):
        lw = (mix_norm_w[l], w_in[l], q_norm_w[l], k_norm_w[l], attn_sinks[l], ssd_conv_w[l], ssd_conv_b[l],
              ssd_dt_bias[l], ssd_A_log[l], ssd_D[l], ssd_norm_w[l], w_out[l],
              ffn_norm_w[l], w_gate[l], w_up[l], ffn_conv_w[l], ffn_conv_b[l], w_down[l])
        h0 = jnp.zeros((bp, SSD_HEADS, SSD_HEAD_DIM, SSD_STATE), jnp.float32)
        c0 = jnp.zeros((bp, SSD_CONV - 1, CONV_DIM), yp.dtype)
        f0 = jnp.zeros((bp, FFN_CONV - 1, D_FF), yp.dtype)
        yp, *sp = _layer(yp, None, None, h0, c0, f0, rel_bias, *lw)
        ys, *ss = _layer(ys, state_attn_k[l], state_attn_v[l], state_ssm[l], state_ssd_conv[l],
                         state_ffn_conv[l], rel_bias, *lw)
        outs_p.append(sp)
        outs_s.append(ss)
    p_k = jnp.stack([o[0] for o in outs_p])
    p_v = jnp.stack([o[1] for o in outs_p])
    p_ssm = jnp.stack([o[2] for o in outs_p])
    p_conv = jnp.stack([o[3] for o in outs_p])
    p_ffn = jnp.stack([o[4] for o in outs_p])
    s_k = jnp.stack([o[0] for o in outs_s])
    s_v = jnp.stack([o[1] for o in outs_s])
    s_ssm = jnp.stack([o[2] for o in outs_s])
    s_conv = jnp.stack([o[3] for o in outs_s])
    s_ffn = jnp.stack([o[4] for o in outs_s])
    return (yp, ys, p_k, p_v, p_ssm, p_conv, p_ffn, s_k, s_v, s_ssm, s_conv, s_ffn)
```

```python
import functools
import math

import numpy as np
import jax
import jax.numpy as jnp
from jax import lax
from jax.experimental import pallas as pl
from jax.experimental.pallas import tpu as pltpu

F32 = jnp.float32
BF16 = jnp.bfloat16

D_MODEL = 4096
HEAD_DIM = 128
ATTN_HEADS = 16
KV_HEADS = 4
Q_PER_KV = 4
ATTN_WIDTH = 2048
WINDOW = 128
N_BUCKETS = 32
MAX_DISTANCE = 128
SSD_HEAD_DIM = 64
SSD_HEADS = 32
SSD_WIDTH = 2048
SSD_GROUPS = 8
SSD_STATE = 128
SSD_CONV = 4
CONV_DIM = 4096
KV_WIDTH = KV_HEADS * HEAD_DIM
D_FF = 11008
FFN_CONV = 3
EPS = 1e-6
NEG = -1e30
SCALE = HEAD_DIM ** -0.5

ROWS = 128
GROUP_ROWS = 4 * SSD_HEAD_DIM
COL_Z = ATTN_WIDTH
COL_XBC = COL_Z + SSD_WIDTH
COL_K = COL_XBC + CONV_DIM
COL_V = COL_K + KV_WIDTH
PROJ_W = COL_V + KV_WIDTH
VMEM_LIMIT_MB = 56


def _params(n_axes, vmem_mb=VMEM_LIMIT_MB):
    return pltpu.CompilerParams(dimension_semantics=("arbitrary",) * n_axes,
                                vmem_limit_bytes=vmem_mb << 20)


def _dot(a, b):
    return jnp.dot(a, b, preferred_element_type=F32)


def _dot_nt(a, b):
    return lax.dot_general(a, b, (((1,), (1,)), ((), ())), preferred_element_type=F32)


def _dot_tn(a, b):
    return lax.dot_general(a, b, (((0,), (0,)), ((), ())), preferred_element_type=F32)


def _split3(x):
    a = x.astype(BF16)
    r = x - a.astype(F32)
    b = r.astype(BF16)
    c = (r - b.astype(F32)).astype(BF16)
    return a, b, c


def _dot3_l(x, w):
    a, b, c = _split3(x)
    return (_dot(a, w) + _dot(b, w)) + _dot(c, w)


def _dot3_r(w, x3):
    return (_dot(w, x3[0]) + _dot(w, x3[1])) + _dot(w, x3[2])


def _sigmoid(x):
    return 1.0 / (1.0 + jnp.exp(-x))


def _rms_rows(x, w):
    r = lax.rsqrt(jnp.mean(x * x, axis=-1, keepdims=True) + EPS)
    return (x * r) * w


def _rmsnorm_body(x_ref, w_ref, o_ref):
    o_ref[...] = _rms_rows(x_ref[...], w_ref[...]).astype(o_ref.dtype)


def _rmsnorm(x, w, rows=256):
    m, d = x.shape
    rows = min(rows, m)
    return pl.pallas_call(
        _rmsnorm_body,
        grid=(m // rows,),
        in_specs=[pl.BlockSpec((rows, d), lambda i: (i, 0)),
                  pl.BlockSpec((1, d), lambda i: (0, 0))],
        out_specs=pl.BlockSpec((rows, d), lambda i: (i, 0)),
        out_shape=jax.ShapeDtypeStruct((m, d), BF16),
        compiler_params=_params(1, 32),
        name="rmsnorm",
    )(x, w.reshape(1, d))


def _mm_body(a_ref, b_ref, *rest, nk, has_res):
    if has_res:
        res_ref, o_ref = rest[0], rest[1]
        scratch = rest[2:]
    else:
        res_ref, o_ref = None, rest[0]
        scratch = rest[1:]
    part = _dot(a_ref[...].astype(BF16), b_ref[...])
    if nk == 1:
        if has_res:
            part = res_ref[...] + part
        o_ref[...] = part.astype(o_ref.dtype)
        return
    acc_ref, = scratch
    k = pl.program_id(2)

    @pl.when(k == 0)
    def _():
        acc_ref[...] = part

    @pl.when(k > 0)
    def _():
        acc_ref[...] += part

    @pl.when(k == nk - 1)
    def _():
        acc = acc_ref[...]
        if has_res:
            acc = res_ref[...] + acc
        o_ref[...] = acc.astype(o_ref.dtype)


def _matmul(a, b, res=None, *, tm, tn, tk=None, out_dtype=F32, name="matmul"):
    m, kdim = a.shape
    n = b.shape[1]
    tm, tn = min(tm, m), min(tn, n)
    tk = kdim if tk is None else tk
    nk = kdim // tk
    in_specs = [pl.BlockSpec((tm, tk), lambda i, j, k: (i, k)),
                pl.BlockSpec((tk, tn), lambda i, j, k: (k, j))]
    args = [a, b]
    if res is not None:
        in_specs.append(pl.BlockSpec((tm, tn), lambda i, j, k: (i, j)))
        args.append(res)
    return pl.pallas_call(
        functools.partial(_mm_body, nk=nk, has_res=res is not None),
        grid=(m // tm, n // tn, nk),
        in_specs=in_specs,
        out_specs=pl.BlockSpec((tm, tn), lambda i, j, k: (i, j)),
        out_shape=jax.ShapeDtypeStruct((m, n), out_dtype),
        scratch_shapes=[pltpu.VMEM((tm, tn), F32)] if nk > 1 else [],
        compiler_params=_params(3),
        name=name,
    )(*args)


def _mm2_body(a1_ref, a2_ref, b1_ref, b2_ref, res_ref, o_ref):
    acc = _dot(a1_ref[...].astype(BF16), b1_ref[...]) + _dot(a2_ref[...].astype(BF16), b2_ref[...])
    o_ref[...] = res_ref[...] + acc


def _out_proj(attn, y, w_out, res, *, tm=1024, tn=1024):
    m = attn.shape[0]
    n = w_out.shape[1]
    tm = min(tm, m)
    return pl.pallas_call(
        _mm2_body,
        grid=(m // tm, n // tn),
        in_specs=[pl.BlockSpec((tm, ATTN_WIDTH), lambda i, j: (i, 0)),
                  pl.BlockSpec((tm, SSD_WIDTH), lambda i, j: (i, 0)),
                  pl.BlockSpec((ATTN_WIDTH, tn), lambda i, j: (0, j)),
                  pl.BlockSpec((SSD_WIDTH, tn), lambda i, j: (1, j)),
                  pl.BlockSpec((tm, tn), lambda i, j: (i, j))],
        out_specs=pl.BlockSpec((tm, tn), lambda i, j: (i, j)),
        out_shape=jax.ShapeDtypeStruct((m, n), F32),
        compiler_params=_params(2),
        name="out_proj",
    )(attn, y, w_out, w_out, res)


def _bucket_table():
    i = np.arange(ROWS)[:, None]
    j = np.arange(2 * ROWS)[None, :]
    dist = i + ROWS - j
    valid = (dist >= 0) & (dist < WINDOW)
    n = np.maximum(dist, 0)
    max_exact = N_BUCKETS // 2

    def large(dtype):
        nf = np.maximum(n, 1).astype(dtype)
        v = np.log(nf / dtype(max_exact)) / dtype(math.log(MAX_DISTANCE / max_exact)) * dtype(N_BUCKETS - max_exact)
        return np.minimum(max_exact + v.astype(np.int32), N_BUCKETS - 1)

    assert (large(np.float32) == large(np.float64)).all()
    bucket = np.where(n < max_exact, n, large(np.float32))
    return np.where(valid, bucket, -1).astype(np.int32)


def _bias_body(rb_ref, bkt_ref, o_ref):
    h = pl.program_id(0)
    bkt = bkt_ref[...]
    acc = jnp.full(bkt.shape, NEG, F32)
    for b in range(N_BUCKETS):
        acc = jnp.where(bkt == b, rb_ref[b, h], acc)
    o_ref[0] = acc


def _bias_table(rel_bias):
    return pl.pallas_call(
        _bias_body,
        grid=(ATTN_HEADS,),
        in_specs=[pl.BlockSpec(memory_space=pltpu.SMEM),
                  pl.BlockSpec((ROWS, 2 * ROWS), lambda h: (0, 0))],
        out_specs=pl.BlockSpec((1, ROWS, 2 * ROWS), lambda h: (h, 0, 0)),
        out_shape=jax.ShapeDtypeStruct((ATTN_HEADS, ROWS, 2 * ROWS), F32),
        compiler_params=_params(1, 16),
        name="bias_table",
    )(rel_bias, jnp.asarray(_bucket_table()))


def _attend_group(qs, kp, kc, vp, vc, bp, bc, sink, neg_prev):
    sp = _dot_nt(qs, kp) * SCALE + bp + neg_prev
    sc = _dot_nt(qs, kc) * SCALE + bc
    m = jnp.maximum(jnp.maximum(jnp.max(sp, axis=-1, keepdims=True),
                                jnp.max(sc, axis=-1, keepdims=True)), sink)
    pp = jnp.exp(sp - m)
    pc = jnp.exp(sc - m)
    den = (jnp.sum(pp, axis=-1, keepdims=True) + jnp.sum(pc, axis=-1, keepdims=True)) + jnp.exp(sink - m)
    inv = 1.0 / den
    return _dot((pp * inv).astype(BF16), vp) + _dot((pc * inv).astype(BF16), vc)


def _attn_prompt_body(sink_ref, q_ref, kc_ref, kp_ref, vc_ref, vp_ref, qw_ref, kw_ref, bias_ref,
                      o_ref, pk_ref, pv_ref):
    i = pl.program_id(1)
    neg_prev = jnp.where(i == 0, NEG, 0.0).astype(F32)
    qw = qw_ref[...]
    kw = kw_ref[...]
    for kv in range(KV_HEADS):
        sl = slice(kv * HEAD_DIM, (kv + 1) * HEAD_DIM)
        kc = _rms_rows(kc_ref[:, sl], kw)
        kp = _rms_rows(kp_ref[:, sl], kw)
        pk_ref[0, :, sl] = kc
        heads = [kv * Q_PER_KV + g for g in range(Q_PER_KV)]
        qs = jnp.concatenate(
            [_rms_rows(q_ref[:, h * HEAD_DIM:(h + 1) * HEAD_DIM], qw) for h in heads], axis=0).astype(BF16)
        sink = jnp.concatenate([jnp.full((ROWS, 1), sink_ref[h], F32) for h in heads], axis=0)
        o = _attend_group(qs, kp.astype(BF16), kc.astype(BF16),
                          vp_ref[:, sl].astype(BF16), vc_ref[:, sl].astype(BF16),
                          bias_ref[kv, :, 0:ROWS], bias_ref[kv, :, ROWS:2 * ROWS], sink, neg_prev)
        for g, h in enumerate(heads):
            o_ref[:, h * HEAD_DIM:(h + 1) * HEAD_DIM] = o[g * ROWS:(g + 1) * ROWS].astype(o_ref.dtype)
    pv_ref[0] = vc_ref[...]


def _attn_prompt(proj, sinks, qw, kw, bias4, nseq, seqlen):
    nb = seqlen // ROWS
    kblk, vblk = COL_K // KV_WIDTH, COL_V // KV_WIDTH
    cur = lambda b, i: b * nb + i
    prev = lambda b, i: b * nb + jnp.maximum(i - 1, 0)
    return pl.pallas_call(
        _attn_prompt_body,
        grid=(nseq, nb),
        in_specs=[pl.BlockSpec(memory_space=pltpu.SMEM),
                  pl.BlockSpec((ROWS, ATTN_WIDTH), lambda b, i: (cur(b, i), 0)),
                  pl.BlockSpec((ROWS, KV_WIDTH), lambda b, i: (cur(b, i), kblk)),
                  pl.BlockSpec((ROWS, KV_WIDTH), lambda b, i: (prev(b, i), kblk)),
                  pl.BlockSpec((ROWS, KV_WIDTH), lambda b, i: (cur(b, i), vblk)),
                  pl.BlockSpec((ROWS, KV_WIDTH), lambda b, i: (prev(b, i), vblk)),
                  pl.BlockSpec((1, HEAD_DIM), lambda b, i: (0, 0)),
                  pl.BlockSpec((1, HEAD_DIM), lambda b, i: (0, 0)),
                  pl.BlockSpec((KV_HEADS, Q_PER_KV * ROWS, 2 * ROWS), lambda b, i: (0, 0, 0))],
        out_specs=[pl.BlockSpec((ROWS, ATTN_WIDTH), lambda b, i: (cur(b, i), 0)),
                   pl.BlockSpec((1, ROWS, KV_WIDTH), lambda b, i: (b, 0, 0)),
                   pl.BlockSpec((1, ROWS, KV_WIDTH), lambda b, i: (b, 0, 0))],
        out_shape=[jax.ShapeDtypeStruct((nseq * seqlen, ATTN_WIDTH), BF16),
                   jax.ShapeDtypeStruct((nseq, ROWS, KV_WIDTH), F32),
                   jax.ShapeDtypeStruct((nseq, ROWS, KV_WIDTH), F32)],
        compiler_params=_params(2, 32),
        name="attn_prompt",
    )(sinks, proj, proj, proj, proj, proj, qw, kw, bias4)


def _attn_sample_body(sink_ref, q_ref, kn_ref, vn_ref, wk_ref, wv_ref, qw_ref, kw_ref, bias_ref,
                      o_ref, sk_ref, sv_ref, *, nsq, ntok):
    qw = qw_ref[...]
    kw = kw_ref[...]
    pad = jnp.zeros((ROWS - ntok, HEAD_DIM), F32)
    for s in range(nsq):
        rows = slice(s * ntok, (s + 1) * ntok)
        for kv in range(KV_HEADS):
            sl = slice(kv * HEAD_DIM, (kv + 1) * HEAD_DIM)
            kn = _rms_rows(kn_ref[rows, sl], kw)
            vn = vn_ref[rows, sl]
            wk = wk_ref[s, :, sl]
            wv = wv_ref[s, :, sl]
            sk_ref[s, 0:ROWS - ntok, sl] = wk[ntok:]
            sk_ref[s, ROWS - ntok:ROWS, sl] = kn
            sv_ref[s, 0:ROWS - ntok, sl] = wv[ntok:]
            sv_ref[s, ROWS - ntok:ROWS, sl] = vn
            heads = [kv * Q_PER_KV + g for g in range(Q_PER_KV)]
            qs = jnp.concatenate(
                [_rms_rows(q_ref[rows, h * HEAD_DIM:(h + 1) * HEAD_DIM], qw) for h in heads], axis=0).astype(BF16)
            sink = jnp.concatenate([jnp.full((ntok, 1), sink_ref[h], F32) for h in heads], axis=0)
            kc = jnp.concatenate([kn, pad], axis=0).astype(BF16)
            vc = jnp.concatenate([vn, pad], axis=0).astype(BF16)
            o = _attend_group(qs, wk.astype(BF16), kc, wv.astype(BF16), vc,
                              bias_ref[kv, :, 0:ROWS], bias_ref[kv, :, ROWS:2 * ROWS], sink, 0.0)
            for g, h in enumerate(heads):
                o_ref[rows, h * HEAD_DIM:(h + 1) * HEAD_DIM] = o[g * ntok:(g + 1) * ntok]


def _attn_sample(proj, win_k, win_v, sinks, qw, kw, bias4s, nseq, ntok, nsq=8):
    kblk, vblk = COL_K // KV_WIDTH, COL_V // KV_WIDTH
    rows = nsq * ntok
    return pl.pallas_call(
        functools.partial(_attn_sample_body, nsq=nsq, ntok=ntok),
        grid=(nseq // nsq,),
        in_specs=[pl.BlockSpec(memory_space=pltpu.SMEM),
                  pl.BlockSpec((rows, ATTN_WIDTH), lambda i: (i, 0)),
                  pl.BlockSpec((rows, KV_WIDTH), lambda i: (i, kblk)),
                  pl.BlockSpec((rows, KV_WIDTH), lambda i: (i, vblk)),
                  pl.BlockSpec((nsq, ROWS, KV_WIDTH), lambda i: (i, 0, 0)),
                  pl.BlockSpec((nsq, ROWS, KV_WIDTH), lambda i: (i, 0, 0)),
                  pl.BlockSpec((1, HEAD_DIM), lambda i: (0, 0)),
                  pl.BlockSpec((1, HEAD_DIM), lambda i: (0, 0)),
                  pl.BlockSpec((KV_HEADS, Q_PER_KV * ntok, 2 * ROWS), lambda i: (0, 0, 0))],
        out_specs=[pl.BlockSpec((rows, ATTN_WIDTH), lambda i: (i, 0)),
                   pl.BlockSpec((nsq, ROWS, KV_WIDTH), lambda i: (i, 0, 0)),
                   pl.BlockSpec((nsq, ROWS, KV_WIDTH), lambda i: (i, 0, 0))],
        out_shape=[jax.ShapeDtypeStruct((nseq * ntok, ATTN_WIDTH), F32),
                   jax.ShapeDtypeStruct((nseq, ROWS, KV_WIDTH), F32),
                   jax.ShapeDtypeStruct((nseq, ROWS, KV_WIDTH), F32)],
        compiler_params=_params(1, 32),
        name="attn_sample",
    )(sinks, proj, proj, proj, win_k, win_v, qw, kw, bias4s)


def _ssd_constants(qs):
    r = np.arange(ROWS)
    same = (r[:, None] // qs) == (r[None, :] // qs)
    mask = same & (r[None, :] <= r[:, None])
    sel = r[None, :] == ((r[:, None] // qs) * qs + qs - 1)
    expand = np.zeros((ROWS, SSD_WIDTH), np.float32)
    for h in range(SSD_HEADS):
        expand[h, h * SSD_HEAD_DIM:(h + 1) * SSD_HEAD_DIM] = 1.0
    to_bf = lambda a: jnp.asarray(a.astype(np.float32), BF16)
    return to_bf(mask), to_bf(sel), to_bf(expand), to_bf(expand.T)


def _ssd_body(xbc_ref, z_ref, dt_ref, cprev_ref, h0_ref, cw_ref, cb_ref, dtb_ref, alog_ref, dexp_ref,
              nw_ref, mask_ref, sel_ref, e_ref, et_ref,
              y_ref, ht_ref,
              h_scr, cbuf, xc_scr, xdt_scr, xdd_scr, eacs_scr, yacc_scr, acs_scr, acst_scr, aclt_scr,
              *, nsq, qs):
    c = pl.program_id(1)
    s = pl.program_id(2)

    @pl.when(c == 0)
    def _():
        h_scr[...] = h0_ref[0]

    if nsq == 1:
        @pl.when(c == 0)
        def _():
            cbuf[5:8, :] = cprev_ref[0]

    @pl.when(s == 0)
    def _chunk_phase():
        ncc = 8 if qs == ROWS else 1
        cwid = CONV_DIM // ncc
        for sq in range(nsq):
            for cc in range(ncc):
                cs = slice(cc * cwid, (cc + 1) * cwid)
                if nsq > 1:
                    cbuf[5:8, cs] = cprev_ref[sq, :, cs]
                cbuf[8:8 + qs, cs] = xbc_ref[sq * qs:(sq + 1) * qs, cs]
                acc = cb_ref[:, cs] + cw_ref[0:1, cs] * cbuf[5:5 + qs, cs]
                for t in range(1, SSD_CONV):
                    acc = acc + cw_ref[t:t + 1, cs] * cbuf[5 + t:5 + t + qs, cs]
                xc_scr[sq * qs:(sq + 1) * qs, cs] = acc * _sigmoid(acc)
                if nsq == 1:
                    cbuf[5:8, cs] = xbc_ref[qs - 3:qs, cs]

        x = dt_ref[...] + dtb_ref[...]
        dt = jnp.maximum(x, 0.0) + jnp.log1p(jnp.exp(-jnp.abs(x)))
        da = dt * (-jnp.exp(alog_ref[...]))
        mask_bf = mask_ref[...]
        acs = _dot3_r(mask_bf, _split3(da))
        acl = _dot3_r(sel_ref[...], _split3(acs))
        acs_scr[...] = acs
        acst_scr[...] = acs.T
        aclt_scr[...] = acl.T
        e = e_ref[...]
        dt_e = _dot3_l(dt, e)
        acs_e = _dot3_l(acs, e)
        acl_e = _dot3_l(acl, e)
        xs = xc_scr[:, 0:SSD_WIDTH]
        xdt = xs * dt_e
        xdt_scr[...] = xdt
        xdd_scr[...] = xdt * jnp.exp(acl_e - acs_e)
        eacs_scr[...] = jnp.exp(acs_e)

        maskb = mask_bf > 0.5
        lane = lax.broadcasted_iota(jnp.int32, (ROWS, 2 * SSD_HEAD_DIM), 1)
        for g in range(SSD_GROUPS):
            bg = xc_scr[:, SSD_WIDTH + g * SSD_STATE:SSD_WIDTH + (g + 1) * SSD_STATE].astype(BF16)
            cg = xc_scr[:, SSD_WIDTH + (SSD_GROUPS + g) * SSD_STATE:
                        SSD_WIDTH + (SSD_GROUPS + g + 1) * SSD_STATE].astype(BF16)
            cb = _dot_nt(cg, bg)
            for pp in range(2):
                ms = []
                for j in range(2):
                    h = 4 * g + 2 * pp + j
                    seg = acs_scr[:, h:h + 1] - acst_scr[h:h + 1, :]
                    ms.append((cb * jnp.exp(jnp.where(maskb, seg, NEG))).astype(BF16))
                c0 = (4 * g + 2 * pp) * SSD_HEAD_DIM
                cols = slice(c0, c0 + 2 * SSD_HEAD_DIM)
                xp = xdt_scr[:, cols]
                rhs = jnp.concatenate([jnp.where(lane < SSD_HEAD_DIM, xp, 0.0),
                                       jnp.where(lane >= SSD_HEAD_DIM, xp, 0.0)], axis=0).astype(BF16)
                yd = _dot(jnp.concatenate(ms, axis=1), rhs)
                yacc_scr[:, cols] = yd + dexp_ref[:, cols] * xc_scr[:, cols]

    if nsq == 1:
        r0 = 0
        rows = slice(0, qs)
    else:
        r0 = pl.multiple_of(s * qs, qs)
        rows = pl.ds(r0, qs)
        ridx = lax.broadcasted_iota(jnp.int32, (ROWS, 1), 0)
        rowmask = jnp.logical_and(ridx >= r0, ridx < r0 + qs)
    tlane = lax.broadcasted_iota(jnp.int32, (ROWS, ROWS), 1)
    xcol = jnp.sum(jnp.where(tlane == r0, aclt_scr[...], 0.0), axis=1, keepdims=True)
    x3 = _split3(jnp.broadcast_to(xcol, (ROWS, ROWS)))
    for g in range(SSD_GROUPS):
        grows = slice(g * GROUP_ROWS, (g + 1) * GROUP_ROWS)
        hs = h_scr[grows, :]
        cg = xc_scr[rows, SSD_WIDTH + (SSD_GROUPS + g) * SSD_STATE:
                    SSD_WIDTH + (SSD_GROUPS + g + 1) * SSD_STATE].astype(BF16)
        yo = _dot_nt(cg, hs.astype(BF16))
        yacc_scr[rows, grows] = yacc_scr[rows, grows] + yo * eacs_scr[rows, grows]
        bg = xc_scr[:, SSD_WIDTH + g * SSD_STATE:SSD_WIDTH + (g + 1) * SSD_STATE]
        if nsq > 1:
            bg = jnp.where(rowmask, bg, 0.0)
        st = _dot_tn(xdd_scr[:, grows].astype(BF16), bg.astype(BF16))
        dec = jnp.exp(_dot3_r(et_ref[grows, :], x3))
        h_scr[grows, :] = dec * hs + st
    ht_ref[0] = h_scr[...]

    @pl.when(s == nsq - 1)
    def _finish():
        z = z_ref[...]
        yg = yacc_scr[...] * (z * _sigmoid(z))
        y_ref[...] = _rms_rows(yg, nw_ref[...]).astype(y_ref.dtype)


def _ssd(proj, dt_raw, conv_prev, h0, cw, cb, dtb, alog, dexp, nw, nseq, seqlen):
    if seqlen % ROWS == 0:
        qs, nsq, nchunk, ngrp = ROWS, 1, seqlen // ROWS, nseq
    else:
        qs, nsq, nchunk = seqlen, ROWS // seqlen, 1
        ngrp = nseq // nsq
    mask, sel, expand, expand_t = _ssd_constants(qs)
    rowblk = lambda b, c, s: b * nchunk + c
    seq = lambda b, c, s: b * nsq + s
    const2 = lambda b, c, s: (0, 0)
    full = lambda shape: pl.BlockSpec(shape, const2)
    return pl.pallas_call(
        functools.partial(_ssd_body, nsq=nsq, qs=qs),
        grid=(ngrp, nchunk, nsq),
        in_specs=[pl.BlockSpec((ROWS, CONV_DIM), lambda b, c, s: (rowblk(b, c, s), COL_XBC // CONV_DIM)),
                  pl.BlockSpec((ROWS, SSD_WIDTH), lambda b, c, s: (rowblk(b, c, s), COL_Z // SSD_WIDTH)),
                  pl.BlockSpec((ROWS, ROWS), lambda b, c, s: (rowblk(b, c, s), 0)),
                  pl.BlockSpec((nsq, SSD_CONV - 1, CONV_DIM), lambda b, c, s: (b, 0, 0)),
                  pl.BlockSpec((1, SSD_WIDTH, SSD_STATE), lambda b, c, s: (seq(b, c, s), 0, 0)),
                  full((SSD_CONV, CONV_DIM)), full((1, CONV_DIM)), full((1, ROWS)), full((1, ROWS)),
                  full((1, SSD_WIDTH)), full((1, SSD_WIDTH)),
                  full((ROWS, ROWS)), full((ROWS, ROWS)), full((ROWS, SSD_WIDTH)), full((SSD_WIDTH, ROWS))],
        out_specs=[pl.BlockSpec((ROWS, SSD_WIDTH), lambda b, c, s: (rowblk(b, c, s), 0)),
                   pl.BlockSpec((1, SSD_WIDTH, SSD_STATE), lambda b, c, s: (seq(b, c, s), 0, 0))],
        out_shape=[jax.ShapeDtypeStruct((nseq * seqlen, SSD_WIDTH), BF16),
                   jax.ShapeDtypeStruct((nseq, SSD_WIDTH, SSD_STATE), F32)],
        scratch_shapes=[pltpu.VMEM((SSD_WIDTH, SSD_STATE), F32),
                        pltpu.VMEM((8 + ROWS, CONV_DIM), F32),
                        pltpu.VMEM((ROWS, CONV_DIM), F32),
                        pltpu.VMEM((ROWS, SSD_WIDTH), F32),
                        pltpu.VMEM((ROWS, SSD_WIDTH), F32),
                        pltpu.VMEM((ROWS, SSD_WIDTH), F32),
                        pltpu.VMEM((ROWS, SSD_WIDTH), F32),
                        pltpu.VMEM((ROWS, ROWS), F32),
                        pltpu.VMEM((ROWS, ROWS), F32),
                        pltpu.VMEM((ROWS, ROWS), F32)],
        compiler_params=_params(3, 40),
        name="ssd",
    )(proj, proj, dt_raw, conv_prev, h0, cw, cb, dtb, alog, dexp, nw, mask, sel, expand, expand_t)


def _ffn_gate_body(h_ref, wg_ref, wu_ref, cw_ref, cb_ref, *rest, period, tiles_per_seq):
    if period is None:
        a_ref, last_ref, carry_scr = rest
    else:
        prev_ref, a_ref, g_ref = rest
    hb = h_ref[...]
    g = _dot(hb, wg_ref[...])
    u = _dot(hb, wu_ref[...])
    tm = g.shape[0]
    row = lax.broadcasted_iota(jnp.int32, g.shape, 0)
    if period is None:
        i, j = pl.program_id(0), pl.program_id(1)
        c8 = jnp.where(i % tiles_per_seq == 0, 0.0, carry_scr[j])
        carry_scr[j] = g[tm - 8:tm]
        last_ref[0] = g[tm - 2:tm]
        g_m1 = jnp.where(row == 0, c8[7:8], pltpu.roll(g, 1, 0))
        g_m2 = jnp.where(row == 0, c8[6:7], jnp.where(row == 1, c8[7:8], pltpu.roll(g, 2, 0)))
    else:
        g_ref[...] = g
        prev = prev_ref[...]
        t = row % period
        g_m1 = jnp.where(t == 0, pltpu.roll(prev, tm - 1, 0), pltpu.roll(g, 1, 0))
        g_m2 = jnp.where(t < 2, prev, pltpu.roll(g, 2, 0))
    gc = cb_ref[...] + cw_ref[0:1] * g_m2
    gc = gc + cw_ref[1:2] * g_m1
    gc = gc + cw_ref[2:3] * g
    a_ref[...] = ((gc * _sigmoid(gc)) * u).astype(a_ref.dtype)


def _ffn_gate(h, w_gate, w_up, cw, cb, *, nseq, seqlen, prev_rows=None, tn=256):
    m = h.shape[0]
    nj = D_FF // tn
    wspec = pl.BlockSpec((D_MODEL, tn), lambda i, j: (0, j))
    cwspec = pl.BlockSpec((FFN_CONV, tn), lambda i, j: (0, j))
    cbspec = pl.BlockSpec((1, tn), lambda i, j: (0, j))
    if prev_rows is None:
        tm = min(1024, seqlen)
        tps = seqlen // tm
        in_specs = [pl.BlockSpec((tm, D_MODEL), lambda i, j: (i, 0)), wspec, wspec, cwspec, cbspec]
        out_specs = [pl.BlockSpec((tm, tn), lambda i, j: (i, j)),
                     pl.BlockSpec((1, FFN_CONV - 1, tn), lambda i, j: (i // tps, 0, j))]
        out_shape = [jax.ShapeDtypeStruct((m, D_FF), BF16),
                     jax.ShapeDtypeStruct((nseq, FFN_CONV - 1, D_FF), F32)]
        scratch = [pltpu.VMEM((nj, 8, tn), F32)]
        args = (h, w_gate, w_up, cw, cb)
        body = functools.partial(_ffn_gate_body, period=None, tiles_per_seq=tps)
    else:
        tm = m
        in_specs = [pl.BlockSpec((tm, D_MODEL), lambda i, j: (i, 0)), wspec, wspec, cwspec, cbspec,
                    pl.BlockSpec((tm, tn), lambda i, j: (i, j))]
        out_specs = [pl.BlockSpec((tm, tn), lambda i, j: (i, j)),
                     pl.BlockSpec((tm, tn), lambda i, j: (i, j))]
        out_shape = [jax.ShapeDtypeStruct((m, D_FF), BF16),
                     jax.ShapeDtypeStruct((m, D_FF), F32)]
        scratch = []
        args = (h, w_gate, w_up, cw, cb, prev_rows)
        body = functools.partial(_ffn_gate_body, period=seqlen, tiles_per_seq=None)
    return pl.pallas_call(
        body,
        grid=(m // tm, nj),
        in_specs=in_specs,
        out_specs=out_specs,
        out_shape=out_shape,
        scratch_shapes=scratch,
        compiler_params=_params(2),
        name="ffn_gate",
    )(*args)


def _layer(x, nseq, seqlen, wts, bias16, attn_state, ssm_state, conv_state, ffn_state):
    m = x.shape[0]
    h = _rmsnorm(x, wts["mix_norm_w"])
    proj = _matmul(h, wts["w_in"], tm=1024, tn=1024, name="in_proj")
    dt_raw = _matmul(h, wts["w_dt"], tm=1024, tn=ROWS, name="dt_proj")

    qw = wts["q_norm_w"].reshape(1, HEAD_DIM)
    kw = wts["k_norm_w"].reshape(1, HEAD_DIM)
    if attn_state is None:
        bias4 = bias16.reshape(KV_HEADS, Q_PER_KV * ROWS, 2 * ROWS)
        attn, new_k, new_v = _attn_prompt(proj, wts["attn_sinks"], qw, kw, bias4, nseq, seqlen)
        conv_prev = jnp.zeros((nseq, SSD_CONV - 1, CONV_DIM), F32)
        h0 = jnp.zeros((nseq, SSD_WIDTH, SSD_STATE), F32)
    else:
        bias4 = bias16[:, :seqlen, :].reshape(KV_HEADS, Q_PER_KV * seqlen, 2 * ROWS)
        win_k, win_v = attn_state
        attn, new_k, new_v = _attn_sample(proj, win_k.reshape(nseq, WINDOW, KV_WIDTH),
                                          win_v.reshape(nseq, WINDOW, KV_WIDTH),
                                          wts["attn_sinks"], qw, kw, bias4, nseq, seqlen)
        conv_prev = conv_state
        h0 = ssm_state.reshape(nseq, SSD_WIDTH, SSD_STATE)
    new_k = new_k.reshape(nseq, WINDOW, KV_HEADS, HEAD_DIM)
    new_v = new_v.reshape(nseq, WINDOW, KV_HEADS, HEAD_DIM)

    y, h_t = _ssd(proj, dt_raw, conv_prev, h0, wts["ssd_conv_w"], wts["ssd_conv_b"], wts["ssd_dt_bias"],
                  wts["ssd_A_log"], wts["ssd_D"], wts["ssd_norm_w"], nseq, seqlen)
    h_t = h_t.reshape(nseq, SSD_HEADS, SSD_HEAD_DIM, SSD_STATE)
    new_conv = proj.reshape(nseq, seqlen, PROJ_W)[:, seqlen - (SSD_CONV - 1):, COL_XBC:COL_XBC + CONV_DIM]

    x1 = _out_proj(attn, y, wts["w_out"], x)
    h2 = _rmsnorm(x1, wts["ffn_norm_w"])
    if ffn_state is None:
        a, new_ffn = _ffn_gate(h2, wts["w_gate"], wts["w_up"], wts["ffn_conv_w"], wts["ffn_conv_b"],
                               nseq=nseq, seqlen=seqlen)
    else:
        prev_rows = jnp.pad(ffn_state, ((0, 0), (0, seqlen - (FFN_CONV - 1)), (0, 0))).reshape(m, D_FF)
        a, g = _ffn_gate(h2, wts["w_gate"], wts["w_up"], wts["ffn_conv_w"], wts["ffn_conv_b"],
                         nseq=nseq, seqlen=seqlen, prev_rows=prev_rows)
        new_ffn = g.reshape(nseq, seqlen, D_FF)[:, seqlen - (FFN_CONV - 1):]
    x2 = _matmul(a, wts["w_down"], x1, tm=512, tn=1024, tk=D_FF // 2, name="down_proj")
    return x2, new_k, new_v, h_t, new_conv, new_ffn


def kernel(x_prompt, x_sample, state_attn_k, state_attn_v, state_ssm, state_ssd_conv, state_ffn_conv, rel_bias, mix_norm_w, w_in, q_norm_w, k_norm_w, attn_sinks, ssd_conv_w, ssd_conv_b, ssd_dt_bias, ssd_A_log, ssd_D, ssd_norm_w, w_out, ffn_norm_w, w_gate, w_up, ffn_conv_w, ffn_conv_b, w_down):
    depth = w_in.shape[0]
    bp, lp, _ = x_prompt.shape
    bs, ls, _ = x_sample.shape
    bias16 = _bias_table(rel_bias)
    yp = x_prompt.reshape(bp * lp, D_MODEL)
    ys = x_sample.reshape(bs * ls, D_MODEL)
    outs_p, outs_s = [], []
    o_q, o_k, o_v, o_z, o_xbc, o_dt = 0, 2048, 2560, 3072, 5120, 9216
    pad_heads = lambda v: jnp.pad(v.reshape(1, SSD_HEADS), ((0, 0), (0, ROWS - SSD_HEADS)))
    for l in range(depth):
        w = w_in[l]
        wts = dict(
            mix_norm_w=mix_norm_w[l],
            w_in=jnp.concatenate([w[:, o_q:o_k], w[:, o_z:o_xbc], w[:, o_xbc:o_dt], w[:, o_k:o_v], w[:, o_v:o_z]],
                                 axis=1).astype(BF16),
            w_dt=jnp.pad(w[:, o_dt:], ((0, 0), (0, ROWS - SSD_HEADS))).astype(BF16),
            q_norm_w=q_norm_w[l], k_norm_w=k_norm_w[l], attn_sinks=attn_sinks[l],
            ssd_conv_w=ssd_conv_w[l], ssd_conv_b=ssd_conv_b[l].reshape(1, CONV_DIM),
            ssd_dt_bias=pad_heads(ssd_dt_bias[l]), ssd_A_log=pad_heads(ssd_A_log[l]),
            ssd_D=jnp.repeat(ssd_D[l], SSD_HEAD_DIM).reshape(1, SSD_WIDTH),
            ssd_norm_w=ssd_norm_w[l].reshape(1, SSD_WIDTH),
            w_out=w_out[l].astype(BF16), ffn_norm_w=ffn_norm_w[l],
            w_gate=w_gate[l].astype(BF16), w_up=w_up[l].astype(BF16),
            ffn_conv_w=ffn_conv_w[l], ffn_conv_b=ffn_conv_b[l].reshape(1, D_FF),
            w_down=w_down[l].astype(BF16),
        )
        yp, *sp = _layer(yp, bp, lp, wts, bias16, None, None, None, None)
        ys, *ss = _layer(ys, bs, ls, wts, bias16, (state_attn_k[l], state_attn_v[l]), state_ssm[l],
                         state_ssd_conv[l], state_ffn_conv[l])
        outs_p.append(sp)
        outs_s.append(ss)
    stack = lambda outs, i: jnp.stack([o[i] for o in outs])
    return (yp.reshape(bp, lp, D_MODEL), ys.reshape(bs, ls, D_MODEL),
            stack(outs_p, 0), stack(outs_p, 1), stack(outs_p, 2), stack(outs_p, 3), stack(outs_p, 4),
            stack(outs_s, 0), stack(outs_s, 1), stack(outs_s, 2), stack(outs_s, 3), stack(outs_s, 4))
```

```python
import functools
import math

import numpy as np
import jax
import jax.numpy as jnp
from jax import lax
from jax.experimental import pallas as pl
from jax.experimental.pallas import tpu as pltpu

F32 = jnp.float32
BF16 = jnp.bfloat16

D_MODEL = 4096
HEAD_DIM = 128
ATTN_HEADS = 16
KV_HEADS = 4
Q_PER_KV = 4
ATTN_WIDTH = 2048
WINDOW = 128
N_BUCKETS = 32
MAX_DISTANCE = 128
SSD_HEAD_DIM = 64
SSD_HEADS = 32
SSD_WIDTH = 2048
SSD_GROUPS = 8
SSD_STATE = 128
SSD_CONV = 4
CONV_DIM = 4096
KV_WIDTH = KV_HEADS * HEAD_DIM
D_FF = 11008
D_FF_PAD = 11264
FFN_CONV = 3
EPS = 1e-6
NEG = -1e30
SCALE = HEAD_DIM ** -0.5

ROWS = 128
GROUP_ROWS = 4 * SSD_HEAD_DIM
COL_Z = ATTN_WIDTH
COL_XBC = COL_Z + SSD_WIDTH
COL_K = COL_XBC + CONV_DIM
COL_V = COL_K + KV_WIDTH
PROJ_W = COL_V + KV_WIDTH
VMEM_LIMIT_MB = 56


def _params(n_axes, vmem_mb=VMEM_LIMIT_MB):
    return pltpu.CompilerParams(dimension_semantics=("arbitrary",) * n_axes,
                                vmem_limit_bytes=vmem_mb << 20)


def _dot(a, b):
    return jnp.dot(a, b, preferred_element_type=F32)


def _dot_nt(a, b):
    return lax.dot_general(a, b, (((1,), (1,)), ((), ())), preferred_element_type=F32)


def _dot_tn(a, b):
    return lax.dot_general(a, b, (((0,), (0,)), ((), ())), preferred_element_type=F32)


def _split3(x):
    a = x.astype(BF16)
    r = x - a.astype(F32)
    b = r.astype(BF16)
    c = (r - b.astype(F32)).astype(BF16)
    return a, b, c


def _dot3_l(x, w):
    a, b, c = _split3(x)
    return (_dot(a, w) + _dot(b, w)) + _dot(c, w)


def _dot3_r(w, x3):
    return (_dot(w, x3[0]) + _dot(w, x3[1])) + _dot(w, x3[2])


def _sigmoid(x):
    return 1.0 / (1.0 + jnp.exp(-x))


def _rms_rows(x, w):
    r = lax.rsqrt(jnp.mean(x * x, axis=-1, keepdims=True) + EPS)
    return (x * r) * w


def _rmsnorm_body(x_ref, w_ref, o_ref):
    o_ref[...] = _rms_rows(x_ref[...], w_ref[...]).astype(o_ref.dtype)


def _rmsnorm(x, w, rows=256):
    m, d = x.shape
    rows = min(rows, m)
    return pl.pallas_call(
        _rmsnorm_body,
        grid=(m // rows,),
        in_specs=[pl.BlockSpec((rows, d), lambda i: (i, 0)),
                  pl.BlockSpec((1, d), lambda i: (0, 0))],
        out_specs=pl.BlockSpec((rows, d), lambda i: (i, 0)),
        out_shape=jax.ShapeDtypeStruct((m, d), BF16),
        compiler_params=_params(1, 32),
        name="rmsnorm",
    )(x, w.reshape(1, d))


def _mm_body(a_ref, b_ref, *rest, nk, has_res):
    if has_res:
        res_ref, o_ref = rest[0], rest[1]
        scratch = rest[2:]
    else:
        res_ref, o_ref = None, rest[0]
        scratch = rest[1:]
    part = _dot(a_ref[...].astype(BF16), b_ref[...])
    if nk == 1:
        if has_res:
            part = res_ref[...] + part
        o_ref[...] = part.astype(o_ref.dtype)
        return
    acc_ref, = scratch
    k = pl.program_id(2)

    @pl.when(k == 0)
    def _():
        acc_ref[...] = part

    @pl.when(k > 0)
    def _():
        acc_ref[...] += part

    @pl.when(k == nk - 1)
    def _():
        acc = acc_ref[...]
        if has_res:
            acc = res_ref[...] + acc
        o_ref[...] = acc.astype(o_ref.dtype)


def _matmul(a, b, res=None, *, tm, tn, tk=None, out_dtype=F32, name="matmul"):
    m, kdim = a.shape
    n = b.shape[1]
    tm, tn = min(tm, m), min(tn, n)
    tk = kdim if tk is None else tk
    nk = kdim // tk
    in_specs = [pl.BlockSpec((tm, tk), lambda i, j, k: (i, k)),
                pl.BlockSpec((tk, tn), lambda i, j, k: (k, j))]
    args = [a, b]
    if res is not None:
        in_specs.append(pl.BlockSpec((tm, tn), lambda i, j, k: (i, j)))
        args.append(res)
    return pl.pallas_call(
        functools.partial(_mm_body, nk=nk, has_res=res is not None),
        grid=(m // tm, n // tn, nk),
        in_specs=in_specs,
        out_specs=pl.BlockSpec((tm, tn), lambda i, j, k: (i, j)),
        out_shape=jax.ShapeDtypeStruct((m, n), out_dtype),
        scratch_shapes=[pltpu.VMEM((tm, tn), F32)] if nk > 1 else [],
        compiler_params=_params(3),
        name=name,
    )(*args)


def _mm2_body(a1_ref, a2_ref, b1_ref, b2_ref, res_ref, o_ref):
    acc = _dot(a1_ref[...].astype(BF16), b1_ref[...]) + _dot(a2_ref[...].astype(BF16), b2_ref[...])
    o_ref[...] = res_ref[...] + acc


def _out_proj(attn, y, w_out, res, *, tm=1024, tn=1024):
    m = attn.shape[0]
    n = w_out.shape[1]
    tm = min(tm, m)
    return pl.pallas_call(
        _mm2_body,
        grid=(m // tm, n // tn),
        in_specs=[pl.BlockSpec((tm, ATTN_WIDTH), lambda i, j: (i, 0)),
                  pl.BlockSpec((tm, SSD_WIDTH), lambda i, j: (i, 0)),
                  pl.BlockSpec((ATTN_WIDTH, tn), lambda i, j: (0, j)),
                  pl.BlockSpec((SSD_WIDTH, tn), lambda i, j: (1, j)),
                  pl.BlockSpec((tm, tn), lambda i, j: (i, j))],
        out_specs=pl.BlockSpec((tm, tn), lambda i, j: (i, j)),
        out_shape=jax.ShapeDtypeStruct((m, n), F32),
        compiler_params=_params(2),
        name="out_proj",
    )(attn, y, w_out, w_out, res)


def _bucket_table():
    i = np.arange(ROWS)[:, None]
    j = np.arange(2 * ROWS)[None, :]
    dist = i + ROWS - j
    valid = (dist >= 0) & (dist < WINDOW)
    n = np.maximum(dist, 0)
    max_exact = N_BUCKETS // 2

    def large(dtype):
        nf = np.maximum(n, 1).astype(dtype)
        v = np.log(nf / dtype(max_exact)) / dtype(math.log(MAX_DISTANCE / max_exact)) * dtype(N_BUCKETS - max_exact)
        return np.minimum(max_exact + v.astype(np.int32), N_BUCKETS - 1)

    assert (large(np.float32) == large(np.float64)).all()
    bucket = np.where(n < max_exact, n, large(np.float32))
    return np.where(valid, bucket, -1).astype(np.int32)


def _bias_body(rb_ref, bkt_ref, o_ref):
    h = pl.program_id(0)
    bkt = bkt_ref[...]
    acc = jnp.full(bkt.shape, NEG, F32)
    for b in range(N_BUCKETS):
        acc = jnp.where(bkt == b, rb_ref[b, h], acc)
    o_ref[0] = acc


def _bias_table(rel_bias):
    return pl.pallas_call(
        _bias_body,
        grid=(ATTN_HEADS,),
        in_specs=[pl.BlockSpec(memory_space=pltpu.SMEM),
                  pl.BlockSpec((ROWS, 2 * ROWS), lambda h: (0, 0))],
        out_specs=pl.BlockSpec((1, ROWS, 2 * ROWS), lambda h: (h, 0, 0)),
        out_shape=jax.ShapeDtypeStruct((ATTN_HEADS, ROWS, 2 * ROWS), F32),
        compiler_params=_params(1, 16),
        name="bias_table",
    )(rel_bias, jnp.asarray(_bucket_table()))


def _attend_group(qs, kp, kc, vp, vc, bp, bc, sink, neg_prev):
    sp = _dot_nt(qs, kp) * SCALE + bp + neg_prev
    sc = _dot_nt(qs, kc) * SCALE + bc
    m = jnp.maximum(jnp.maximum(jnp.max(sp, axis=-1, keepdims=True),
                                jnp.max(sc, axis=-1, keepdims=True)), sink)
    pp = jnp.exp(sp - m)
    pc = jnp.exp(sc - m)
    den = (jnp.sum(pp, axis=-1, keepdims=True) + jnp.sum(pc, axis=-1, keepdims=True)) + jnp.exp(sink - m)
    inv = 1.0 / den
    return _dot((pp * inv).astype(BF16), vp) + _dot((pc * inv).astype(BF16), vc)


def _attn_prompt_body(sink_ref, q_ref, kc_ref, kp_ref, vc_ref, vp_ref, qw_ref, kw_ref, bias_ref,
                      o_ref, pk_ref, pv_ref):
    i = pl.program_id(1)
    neg_prev = jnp.where(i == 0, NEG, 0.0).astype(F32)
    qw = qw_ref[...]
    kw = kw_ref[...]
    for kv in range(KV_HEADS):
        sl = slice(kv * HEAD_DIM, (kv + 1) * HEAD_DIM)
        kc = _rms_rows(kc_ref[:, sl], kw)
        kp = _rms_rows(kp_ref[:, sl], kw)
        pk_ref[0, :, sl] = kc
        heads = [kv * Q_PER_KV + g for g in range(Q_PER_KV)]
        qs = jnp.concatenate(
            [_rms_rows(q_ref[:, h * HEAD_DIM:(h + 1) * HEAD_DIM], qw) for h in heads], axis=0).astype(BF16)
        sink = jnp.concatenate([jnp.full((ROWS, 1), sink_ref[h], F32) for h in heads], axis=0)
        o = _attend_group(qs, kp.astype(BF16), kc.astype(BF16),
                          vp_ref[:, sl].astype(BF16), vc_ref[:, sl].astype(BF16),
                          bias_ref[kv, :, 0:ROWS], bias_ref[kv, :, ROWS:2 * ROWS], sink, neg_prev)
        for g, h in enumerate(heads):
            o_ref[:, h * HEAD_DIM:(h + 1) * HEAD_DIM] = o[g * ROWS:(g + 1) * ROWS].astype(o_ref.dtype)
    pv_ref[0] = vc_ref[...]


def _attn_prompt(proj, sinks, qw, kw, bias4, nseq, seqlen):
    nb = seqlen // ROWS
    kblk, vblk = COL_K // KV_WIDTH, COL_V // KV_WIDTH
    cur = lambda b, i: b * nb + i
    prev = lambda b, i: b * nb + jnp.maximum(i - 1, 0)
    return pl.pallas_call(
        _attn_prompt_body,
        grid=(nseq, nb),
        in_specs=[pl.BlockSpec(memory_space=pltpu.SMEM),
                  pl.BlockSpec((ROWS, ATTN_WIDTH), lambda b, i: (cur(b, i), 0)),
                  pl.BlockSpec((ROWS, KV_WIDTH), lambda b, i: (cur(b, i), kblk)),
                  pl.BlockSpec((ROWS, KV_WIDTH), lambda b, i: (prev(b, i), kblk)),
                  pl.BlockSpec((ROWS, KV_WIDTH), lambda b, i: (cur(b, i), vblk)),
                  pl.BlockSpec((ROWS, KV_WIDTH), lambda b, i: (prev(b, i), vblk)),
                  pl.BlockSpec((1, HEAD_DIM), lambda b, i: (0, 0)),
                  pl.BlockSpec((1, HEAD_DIM), lambda b, i: (0, 0)),
                  pl.BlockSpec((KV_HEADS, Q_PER_KV * ROWS, 2 * ROWS), lambda b, i: (0, 0, 0))],
        out_specs=[pl.BlockSpec((ROWS, ATTN_WIDTH), lambda b, i: (cur(b, i), 0)),
                   pl.BlockSpec((1, ROWS, KV_WIDTH), lambda b, i: (b, 0, 0)),
                   pl.BlockSpec((1, ROWS, KV_WIDTH), lambda b, i: (b, 0, 0))],
        out_shape=[jax.ShapeDtypeStruct((nseq * seqlen, ATTN_WIDTH), BF16),
                   jax.ShapeDtypeStruct((nseq, ROWS, KV_WIDTH), F32),
                   jax.ShapeDtypeStruct((nseq, ROWS, KV_WIDTH), F32)],
        compiler_params=_params(2, 32),
        name="attn_prompt",
    )(sinks, proj, proj, proj, proj, proj, qw, kw, bias4)


def _attn_sample_body(q_ref, kn_ref, vn_ref, wk_ref, wv_ref, qw_ref, kw_ref, bias_ref, sink_ref,
                      o_ref, sk_ref, sv_ref, qn_scr, kn_scr, s_scr, p_scr, *, nsq, ntok):
    qw = qw_ref[...]
    kw = kw_ref[...]
    grows = Q_PER_KV * ntok
    keep = ROWS - ntok
    for h in range(ATTN_HEADS):
        hs = slice(h * HEAD_DIM, (h + 1) * HEAD_DIM)
        qn_scr[:, hs] = _rms_rows(q_ref[:, hs], qw)
    for kv in range(KV_HEADS):
        sl = slice(kv * HEAD_DIM, (kv + 1) * HEAD_DIM)
        kn_scr[:, sl] = _rms_rows(kn_ref[:, sl], kw)
    for s in range(nsq):
        rows = slice(s * ntok, (s + 1) * ntok)
        sk_ref[s, 0:keep, :] = wk_ref[s, ntok:ROWS, :]
        sk_ref[s, keep:ROWS, :] = kn_scr[rows, :]
        sv_ref[s, 0:keep, :] = wv_ref[s, ntok:ROWS, :]
        sv_ref[s, keep:ROWS, :] = vn_ref[rows, :]

    pad = jnp.zeros((keep, HEAD_DIM), F32)
    for s in range(nsq):
        rows = slice(s * ntok, (s + 1) * ntok)
        for kv in range(KV_HEADS):
            sl = slice(kv * HEAD_DIM, (kv + 1) * HEAD_DIM)
            grp = s * KV_HEADS + kv
            qs = jnp.concatenate([qn_scr[rows, (kv * Q_PER_KV + g) * HEAD_DIM:(kv * Q_PER_KV + g + 1) * HEAD_DIM]
                                  for g in range(Q_PER_KV)], axis=0).astype(BF16)
            kcat = jnp.concatenate([wk_ref[s, :, sl], kn_scr[rows, sl], pad], axis=0).astype(BF16)
            s_scr[grp * grows:(grp + 1) * grows, :] = _dot_nt(qs, kcat)

    sc = s_scr[...] * SCALE + bias_ref[...]
    sink = sink_ref[...]
    m = jnp.maximum(jnp.max(sc, axis=-1, keepdims=True), sink)
    p = jnp.exp(sc - m)
    den = jnp.sum(p, axis=-1, keepdims=True) + jnp.exp(sink - m)
    p_scr[...] = (p * (1.0 / den)).astype(BF16)

    for s in range(nsq):
        rows = slice(s * ntok, (s + 1) * ntok)
        for kv in range(KV_HEADS):
            sl = slice(kv * HEAD_DIM, (kv + 1) * HEAD_DIM)
            grp = s * KV_HEADS + kv
            vcat = jnp.concatenate([wv_ref[s, :, sl], vn_ref[rows, sl], pad], axis=0).astype(BF16)
            o = _dot(p_scr[grp * grows:(grp + 1) * grows, :], vcat)
            for g in range(Q_PER_KV):
                h = kv * Q_PER_KV + g
                o_ref[rows, h * HEAD_DIM:(h + 1) * HEAD_DIM] = o[g * ntok:(g + 1) * ntok]


def _attn_sample(proj, win_k, win_v, sinks, qw, kw, bias16, nseq, ntok, nsq=8):
    kblk, vblk = COL_K // KV_WIDTH, COL_V // KV_WIDTH
    rows = nsq * ntok
    srows = nsq * ATTN_HEADS * ntok
    bias = jnp.tile(bias16[:, :ntok, :].reshape(ATTN_HEADS * ntok, 2 * ROWS), (nsq, 1))
    sink = jnp.tile(jnp.repeat(sinks, ntok), nsq).reshape(srows, 1)
    return pl.pallas_call(
        functools.partial(_attn_sample_body, nsq=nsq, ntok=ntok),
        grid=(nseq // nsq,),
        in_specs=[pl.BlockSpec((rows, ATTN_WIDTH), lambda i: (i, 0)),
                  pl.BlockSpec((rows, KV_WIDTH), lambda i: (i, kblk)),
                  pl.BlockSpec((rows, KV_WIDTH), lambda i: (i, vblk)),
                  pl.BlockSpec((nsq, ROWS, KV_WIDTH), lambda i: (i, 0, 0)),
                  pl.BlockSpec((nsq, ROWS, KV_WIDTH), lambda i: (i, 0, 0)),
                  pl.BlockSpec((1, HEAD_DIM), lambda i: (0, 0)),
                  pl.BlockSpec((1, HEAD_DIM), lambda i: (0, 0)),
                  pl.BlockSpec((srows, 2 * ROWS), lambda i: (0, 0)),
                  pl.BlockSpec((srows, 1), lambda i: (0, 0))],
        out_specs=[pl.BlockSpec((rows, ATTN_WIDTH), lambda i: (i, 0)),
                   pl.BlockSpec((nsq, ROWS, KV_WIDTH), lambda i: (i, 0, 0)),
                   pl.BlockSpec((nsq, ROWS, KV_WIDTH), lambda i: (i, 0, 0))],
        out_shape=[jax.ShapeDtypeStruct((nseq * ntok, ATTN_WIDTH), F32),
                   jax.ShapeDtypeStruct((nseq, ROWS, KV_WIDTH), F32),
                   jax.ShapeDtypeStruct((nseq, ROWS, KV_WIDTH), F32)],
        scratch_shapes=[pltpu.VMEM((rows, ATTN_WIDTH), F32),
                        pltpu.VMEM((rows, KV_WIDTH), F32),
                        pltpu.VMEM((srows, 2 * ROWS), F32),
                        pltpu.VMEM((srows, 2 * ROWS), BF16)],
        compiler_params=_params(1, 32),
        name="attn_sample",
    )(proj, proj, proj, win_k, win_v, qw, kw, bias, sink)


def _ssd_constants(qs):
    r = np.arange(ROWS)
    same = (r[:, None] // qs) == (r[None, :] // qs)
    mask = same & (r[None, :] <= r[:, None])
    sel = r[None, :] == ((r[:, None] // qs) * qs + qs - 1)
    expand = np.zeros((ROWS, SSD_WIDTH), np.float32)
    for h in range(SSD_HEADS):
        expand[h, h * SSD_HEAD_DIM:(h + 1) * SSD_HEAD_DIM] = 1.0
    to_bf = lambda a: jnp.asarray(a.astype(np.float32), BF16)
    return to_bf(mask), to_bf(sel), to_bf(expand), to_bf(expand.T)


def _ssd_body(xbc_ref, z_ref, dt_ref, cprev_ref, h0_ref, cw_ref, cb_ref, dtb_ref, alog_ref, dexp_ref,
              nw_ref, mask_ref, sel_ref, e_ref, et_ref,
              y_ref, ht_ref,
              h_scr, cbuf, xc_scr, xdt_scr, xdd_scr, eacs_scr, yacc_scr, acs_scr, acst_scr, aclt_scr,
              *, nsq, qs):
    c = pl.program_id(1)
    s = pl.program_id(2)

    @pl.when(c == 0)
    def _():
        h_scr[...] = h0_ref[0]

    if nsq == 1:
        @pl.when(c == 0)
        def _():
            cbuf[5:8, :] = cprev_ref[0]

    @pl.when(s == 0)
    def _chunk_phase():
        ncc = 8 if qs == ROWS else 1
        cwid = CONV_DIM // ncc
        for sq in range(nsq):
            for cc in range(ncc):
                cs = slice(cc * cwid, (cc + 1) * cwid)
                if nsq > 1:
                    cbuf[5:8, cs] = cprev_ref[sq, :, cs]
                cbuf[8:8 + qs, cs] = xbc_ref[sq * qs:(sq + 1) * qs, cs]
                acc = cb_ref[:, cs] + cw_ref[0:1, cs] * cbuf[5:5 + qs, cs]
                for t in range(1, SSD_CONV):
                    acc = acc + cw_ref[t:t + 1, cs] * cbuf[5 + t:5 + t + qs, cs]
                xc_scr[sq * qs:(sq + 1) * qs, cs] = acc * _sigmoid(acc)
                if nsq == 1:
                    cbuf[5:8, cs] = xbc_ref[qs - 3:qs, cs]

        x = dt_ref[...] + dtb_ref[...]
        dt = jnp.maximum(x, 0.0) + jnp.log1p(jnp.exp(-jnp.abs(x)))
        da = dt * (-jnp.exp(alog_ref[...]))
        mask_bf = mask_ref[...]
        acs = _dot3_r(mask_bf, _split3(da))
        acl = _dot3_r(sel_ref[...], _split3(acs))
        acs_scr[...] = acs
        acst_scr[...] = acs.T
        aclt_scr[...] = acl.T
        e = e_ref[...]
        dt_e = _dot3_l(dt, e)
        acs_e = _dot3_l(acs, e)
        acl_e = _dot3_l(acl, e)
        xs = xc_scr[:, 0:SSD_WIDTH]
        xdt = xs * dt_e
        xdt_scr[...] = xdt
        xdd_scr[...] = xdt * jnp.exp(acl_e - acs_e)
        eacs_scr[...] = jnp.exp(acs_e)

        maskb = mask_bf > 0.5
        lane = lax.broadcasted_iota(jnp.int32, (ROWS, 2 * SSD_HEAD_DIM), 1)
        for g in range(SSD_GROUPS):
            bg = xc_scr[:, SSD_WIDTH + g * SSD_STATE:SSD_WIDTH + (g + 1) * SSD_STATE].astype(BF16)
            cg = xc_scr[:, SSD_WIDTH + (SSD_GROUPS + g) * SSD_STATE:
                        SSD_WIDTH + (SSD_GROUPS + g + 1) * SSD_STATE].astype(BF16)
            cb = _dot_nt(cg, bg)
            for pp in range(2):
                ms = []
                for j in range(2):
                    h = 4 * g + 2 * pp + j
                    seg = acs_scr[:, h:h + 1] - acst_scr[h:h + 1, :]
                    ms.append((cb * jnp.exp(jnp.where(maskb, seg, NEG))).astype(BF16))
                c0 = (4 * g + 2 * pp) * SSD_HEAD_DIM
                cols = slice(c0, c0 + 2 * SSD_HEAD_DIM)
                xp = xdt_scr[:, cols]
                rhs = jnp.concatenate([jnp.where(lane < SSD_HEAD_DIM, xp, 0.0),
                                       jnp.where(lane >= SSD_HEAD_DIM, xp, 0.0)], axis=0).astype(BF16)
                yd = _dot(jnp.concatenate(ms, axis=1), rhs)
                yacc_scr[:, cols] = yd + dexp_ref[:, cols] * xc_scr[:, cols]

    if nsq == 1:
        r0 = 0
        rows = slice(0, qs)
    else:
        r0 = pl.multiple_of(s * qs, qs)
        rows = pl.ds(r0, qs)
        ridx = lax.broadcasted_iota(jnp.int32, (ROWS, 1), 0)
        rowmask = jnp.logical_and(ridx >= r0, ridx < r0 + qs)
    tlane = lax.broadcasted_iota(jnp.int32, (ROWS, ROWS), 1)
    xcol = jnp.sum(jnp.where(tlane == r0, aclt_scr[...], 0.0), axis=1, keepdims=True)
    x3 = _split3(jnp.broadcast_to(xcol, (ROWS, ROWS)))
    for g in range(SSD_GROUPS):
        grows = slice(g * GROUP_ROWS, (g + 1) * GROUP_ROWS)
        hs = h_scr[grows, :]
        cg = xc_scr[rows, SSD_WIDTH + (SSD_GROUPS + g) * SSD_STATE:
                    SSD_WIDTH + (SSD_GROUPS + g + 1) * SSD_STATE].astype(BF16)
        yo = _dot_nt(cg, hs.astype(BF16))
        yacc_scr[rows, grows] = yacc_scr[rows, grows] + yo * eacs_scr[rows, grows]
        bg = xc_scr[:, SSD_WIDTH + g * SSD_STATE:SSD_WIDTH + (g + 1) * SSD_STATE]
        if nsq > 1:
            bg = jnp.where(rowmask, bg, 0.0)
        st = _dot_tn(xdd_scr[:, grows].astype(BF16), bg.astype(BF16))
        dec = jnp.exp(_dot3_r(et_ref[grows, :], x3))
        h_scr[grows, :] = dec * hs + st
    ht_ref[0] = h_scr[...]

    @pl.when(s == nsq - 1)
    def _finish():
        z = z_ref[...]
        yg = yacc_scr[...] * (z * _sigmoid(z))
        y_ref[...] = _rms_rows(yg, nw_ref[...]).astype(y_ref.dtype)


def _ssd(proj, dt_raw, conv_prev, h0, cw, cb, dtb, alog, dexp, nw, nseq, seqlen):
    if seqlen % ROWS == 0:
        qs, nsq, nchunk, ngrp = ROWS, 1, seqlen // ROWS, nseq
    else:
        qs, nsq, nchunk = seqlen, ROWS // seqlen, 1
        ngrp = nseq // nsq
    mask, sel, expand, expand_t = _ssd_constants(qs)
    rowblk = lambda b, c, s: b * nchunk + c
    seq = lambda b, c, s: b * nsq + s
    const2 = lambda b, c, s: (0, 0)
    full = lambda shape: pl.BlockSpec(shape, const2)
    return pl.pallas_call(
        functools.partial(_ssd_body, nsq=nsq, qs=qs),
        grid=(ngrp, nchunk, nsq),
        in_specs=[pl.BlockSpec((ROWS, CONV_DIM), lambda b, c, s: (rowblk(b, c, s), COL_XBC // CONV_DIM)),
                  pl.BlockSpec((ROWS, SSD_WIDTH), lambda b, c, s: (rowblk(b, c, s), COL_Z // SSD_WIDTH)),
                  pl.BlockSpec((ROWS, ROWS), lambda b, c, s: (rowblk(b, c, s), 0)),
                  pl.BlockSpec((nsq, SSD_CONV - 1, CONV_DIM), lambda b, c, s: (b, 0, 0)),
                  pl.BlockSpec((1, SSD_WIDTH, SSD_STATE), lambda b, c, s: (seq(b, c, s), 0, 0)),
                  full((SSD_CONV, CONV_DIM)), full((1, CONV_DIM)), full((1, ROWS)), full((1, ROWS)),
                  full((1, SSD_WIDTH)), full((1, SSD_WIDTH)),
                  full((ROWS, ROWS)), full((ROWS, ROWS)), full((ROWS, SSD_WIDTH)), full((SSD_WIDTH, ROWS))],
        out_specs=[pl.BlockSpec((ROWS, SSD_WIDTH), lambda b, c, s: (rowblk(b, c, s), 0)),
                   pl.BlockSpec((1, SSD_WIDTH, SSD_STATE), lambda b, c, s: (seq(b, c, s), 0, 0))],
        out_shape=[jax.ShapeDtypeStruct((nseq * seqlen, SSD_WIDTH), BF16),
                   jax.ShapeDtypeStruct((nseq, SSD_WIDTH, SSD_STATE), F32)],
        scratch_shapes=[pltpu.VMEM((SSD_WIDTH, SSD_STATE), F32),
                        pltpu.VMEM((8 + ROWS, CONV_DIM), F32),
                        pltpu.VMEM((ROWS, CONV_DIM), F32),
                        pltpu.VMEM((ROWS, SSD_WIDTH), F32),
                        pltpu.VMEM((ROWS, SSD_WIDTH), F32),
                        pltpu.VMEM((ROWS, SSD_WIDTH), F32),
                        pltpu.VMEM((ROWS, SSD_WIDTH), F32),
                        pltpu.VMEM((ROWS, ROWS), F32),
                        pltpu.VMEM((ROWS, ROWS), F32),
                        pltpu.VMEM((ROWS, ROWS), F32)],
        compiler_params=_params(3, 40),
        name="ssd",
    )(proj, proj, dt_raw, conv_prev, h0, cw, cb, dtb, alog, dexp, nw, mask, sel, expand, expand_t)


def _ffn_gate_body(h_ref, wg_ref, wu_ref, cw_ref, cb_ref, *rest, period, tiles_per_seq):
    if period is None:
        a_ref, last_ref, carry_scr = rest
    else:
        prev_ref, a_ref, g_ref = rest
    hb = h_ref[...]
    g = _dot(hb, wg_ref[...])
    u = _dot(hb, wu_ref[...])
    tm = g.shape[0]
    row = lax.broadcasted_iota(jnp.int32, g.shape, 0)
    if period is None:
        i, j = pl.program_id(0), pl.program_id(1)
        c8 = jnp.where(i % tiles_per_seq == 0, 0.0, carry_scr[j])
        carry_scr[j] = g[tm - 8:tm]
        last_ref[0] = g[tm - 2:tm]
        g_m1 = jnp.where(row == 0, c8[7:8], pltpu.roll(g, 1, 0))
        g_m2 = jnp.where(row == 0, c8[6:7], jnp.where(row == 1, c8[7:8], pltpu.roll(g, 2, 0)))
    else:
        g_ref[...] = g
        prev = prev_ref[...]
        t = row & (period - 1)
        g_m1 = jnp.where(t == 0, pltpu.roll(prev, tm - 1, 0), pltpu.roll(g, 1, 0))
        g_m2 = jnp.where(t < 2, prev, pltpu.roll(g, 2, 0))
    gc = cb_ref[...] + cw_ref[0:1] * g_m2
    gc = gc + cw_ref[1:2] * g_m1
    gc = gc + cw_ref[2:3] * g
    a_ref[...] = ((gc * _sigmoid(gc)) * u).astype(a_ref.dtype)


def _ffn_gate(h, w_gate, w_up, cw, cb, *, nseq, seqlen, prev_rows=None, tm=1024, tn=512):
    m = h.shape[0]
    nj = D_FF_PAD // tn
    wspec = pl.BlockSpec((D_MODEL, tn), lambda i, j: (0, j))
    cwspec = pl.BlockSpec((FFN_CONV, tn), lambda i, j: (0, j))
    cbspec = pl.BlockSpec((1, tn), lambda i, j: (0, j))
    if prev_rows is None:
        tm = min(tm, seqlen)
        tps = seqlen // tm
        in_specs = [pl.BlockSpec((tm, D_MODEL), lambda i, j: (i, 0)), wspec, wspec, cwspec, cbspec]
        out_specs = [pl.BlockSpec((tm, tn), lambda i, j: (i, j)),
                     pl.BlockSpec((1, FFN_CONV - 1, tn), lambda i, j: (i, 0, j))]
        out_shape = [jax.ShapeDtypeStruct((m, D_FF_PAD), BF16),
                     jax.ShapeDtypeStruct((m // tm, FFN_CONV - 1, D_FF_PAD), F32)]
        scratch = [pltpu.VMEM((nj, 8, tn), F32)]
        args = (h, w_gate, w_up, cw, cb)
        body = functools.partial(_ffn_gate_body, period=None, tiles_per_seq=tps)
    else:
        assert seqlen & (seqlen - 1) == 0
        tm = m
        in_specs = [pl.BlockSpec((tm, D_MODEL), lambda i, j: (i, 0)), wspec, wspec, cwspec, cbspec,
                    pl.BlockSpec((tm, tn), lambda i, j: (i, j))]
        out_specs = [pl.BlockSpec((tm, tn), lambda i, j: (i, j)),
                     pl.BlockSpec((tm, tn), lambda i, j: (i, j))]
        out_shape = [jax.ShapeDtypeStruct((m, D_FF_PAD), BF16),
                     jax.ShapeDtypeStruct((m, D_FF_PAD), F32)]
        scratch = []
        args = (h, w_gate, w_up, cw, cb, prev_rows)
        body = functools.partial(_ffn_gate_body, period=seqlen, tiles_per_seq=None)
    return pl.pallas_call(
        body,
        grid=(m // tm, nj),
        in_specs=in_specs,
        out_specs=out_specs,
        out_shape=out_shape,
        scratch_shapes=scratch,
        compiler_params=_params(2),
        name="ffn_gate",
    )(*args)


def _down_body(a_ref, w_ref, res_ref, o_ref, *, chunk):
    @pl.when(pl.program_id(2) == 0)
    def _():
        o_ref[...] = res_ref[...]

    a = a_ref[...]
    for c in range(o_ref.shape[1] // chunk):
        cs = slice(c * chunk, (c + 1) * chunk)
        o_ref[:, cs] += _dot(a, w_ref[:, cs])


def _down_proj(a, w_down, res, *, tm=1024, tn=2048, tk=1024):
    m, kdim = a.shape
    n = w_down.shape[1]
    tm = min(tm, m)
    return pl.pallas_call(
        functools.partial(_down_body, chunk=512),
        grid=(n // tn, m // tm, kdim // tk),
        in_specs=[pl.BlockSpec((tm, tk), lambda j, i, k: (i, k)),
                  pl.BlockSpec((tk, tn), lambda j, i, k: (k, j)),
                  pl.BlockSpec((tm, tn), lambda j, i, k: (i, j))],
        out_specs=pl.BlockSpec((tm, tn), lambda j, i, k: (i, j)),
        out_shape=jax.ShapeDtypeStruct((m, n), F32),
        compiler_params=_params(3),
        name="down_proj",
    )(a, w_down, res)


def _layer(x, nseq, seqlen, wts, bias16, attn_state, ssm_state, conv_state, ffn_state):
    m = x.shape[0]
    h = _rmsnorm(x, wts["mix_norm_w"])
    proj = _matmul(h, wts["w_in"], tm=1024, tn=1024, name="in_proj")
    dt_raw = _matmul(h, wts["w_dt"], tm=1024, tn=ROWS, name="dt_proj")

    qw = wts["q_norm_w"].reshape(1, HEAD_DIM)
    kw = wts["k_norm_w"].reshape(1, HEAD_DIM)
    if attn_state is None:
        bias4 = bias16.reshape(KV_HEADS, Q_PER_KV * ROWS, 2 * ROWS)
        attn, new_k, new_v = _attn_prompt(proj, wts["attn_sinks"], qw, kw, bias4, nseq, seqlen)
        conv_prev = jnp.zeros((nseq, SSD_CONV - 1, CONV_DIM), F32)
        h0 = jnp.zeros((nseq, SSD_WIDTH, SSD_STATE), F32)
    else:
        win_k, win_v = attn_state
        attn, new_k, new_v = _attn_sample(proj, win_k.reshape(nseq, WINDOW, KV_WIDTH),
                                          win_v.reshape(nseq, WINDOW, KV_WIDTH),
                                          wts["attn_sinks"], qw, kw, bias16, nseq, seqlen)
        conv_prev = conv_state
        h0 = ssm_state.reshape(nseq, SSD_WIDTH, SSD_STATE)
    new_k = new_k.reshape(nseq, WINDOW, KV_HEADS, HEAD_DIM)
    new_v = new_v.reshape(nseq, WINDOW, KV_HEADS, HEAD_DIM)

    y, h_t = _ssd(proj, dt_raw, conv_prev, h0, wts["ssd_conv_w"], wts["ssd_conv_b"], wts["ssd_dt_bias"],
                  wts["ssd_A_log"], wts["ssd_D"], wts["ssd_norm_w"], nseq, seqlen)
    h_t = h_t.reshape(nseq, SSD_HEADS, SSD_HEAD_DIM, SSD_STATE)
    new_conv = proj.reshape(nseq, seqlen, PROJ_W)[:, seqlen - (SSD_CONV - 1):, COL_XBC:COL_XBC + CONV_DIM]

    x1 = _out_proj(attn, y, wts["w_out"], x)
    h2 = _rmsnorm(x1, wts["ffn_norm_w"])
    if ffn_state is None:
        a, last = _ffn_gate(h2, wts["w_gate"], wts["w_up"], wts["ffn_conv_w"], wts["ffn_conv_b"],
                            nseq=nseq, seqlen=seqlen)
        new_ffn = last.reshape(nseq, -1, FFN_CONV - 1, D_FF_PAD)[:, -1, :, :D_FF]
    else:
        prev_rows = jnp.pad(ffn_state, ((0, 0), (0, seqlen - (FFN_CONV - 1)), (0, D_FF_PAD - D_FF)))
        a, g = _ffn_gate(h2, wts["w_gate"], wts["w_up"], wts["ffn_conv_w"], wts["ffn_conv_b"],
                         nseq=nseq, seqlen=seqlen, prev_rows=prev_rows.reshape(m, D_FF_PAD))
        new_ffn = g.reshape(nseq, seqlen, D_FF_PAD)[:, seqlen - (FFN_CONV - 1):, :D_FF]
    x2 = _down_proj(a, wts["w_down"], x1)
    return x2, new_k, new_v, h_t, new_conv, new_ffn


def kernel(x_prompt, x_sample, state_attn_k, state_attn_v, state_ssm, state_ssd_conv, state_ffn_conv, rel_bias, mix_norm_w, w_in, q_norm_w, k_norm_w, attn_sinks, ssd_conv_w, ssd_conv_b, ssd_dt_bias, ssd_A_log, ssd_D, ssd_norm_w, w_out, ffn_norm_w, w_gate, w_up, ffn_conv_w, ffn_conv_b, w_down):
    depth = w_in.shape[0]
    bp, lp, _ = x_prompt.shape
    bs, ls, _ = x_sample.shape
    bias16 = _bias_table(rel_bias)
    yp = x_prompt.reshape(bp * lp, D_MODEL)
    ys = x_sample.reshape(bs * ls, D_MODEL)
    outs_p, outs_s = [], []
    o_q, o_k, o_v, o_z, o_xbc, o_dt = 0, 2048, 2560, 3072, 5120, 9216
    pad_heads = lambda v: jnp.pad(v.reshape(1, SSD_HEADS), ((0, 0), (0, ROWS - SSD_HEADS)))
    pad_ff = lambda v: jnp.pad(v, ((0, 0), (0, D_FF_PAD - D_FF)))
    for l in range(depth):
        w = w_in[l]
        wts = dict(
            mix_norm_w=mix_norm_w[l],
            w_in=jnp.concatenate([w[:, o_q:o_k], w[:, o_z:o_xbc], w[:, o_xbc:o_dt], w[:, o_k:o_v], w[:, o_v:o_z]],
                                 axis=1).astype(BF16),
            w_dt=jnp.pad(w[:, o_dt:], ((0, 0), (0, ROWS - SSD_HEADS))).astype(BF16),
            q_norm_w=q_norm_w[l], k_norm_w=k_norm_w[l], attn_sinks=attn_sinks[l],
            ssd_conv_w=ssd_conv_w[l], ssd_conv_b=ssd_conv_b[l].reshape(1, CONV_DIM),
            ssd_dt_bias=pad_heads(ssd_dt_bias[l]), ssd_A_log=pad_heads(ssd_A_log[l]),
            ssd_D=jnp.repeat(ssd_D[l], SSD_HEAD_DIM).reshape(1, SSD_WIDTH),
            ssd_norm_w=ssd_norm_w[l].reshape(1, SSD_WIDTH),
            w_out=w_out[l].astype(BF16), ffn_norm_w=ffn_norm_w[l],
            w_gate=pad_ff(w_gate[l]).astype(BF16), w_up=pad_ff(w_up[l]).astype(BF16),
            ffn_conv_w=pad_ff(ffn_conv_w[l]), ffn_conv_b=pad_ff(ffn_conv_b[l].reshape(1, D_FF)),
            w_down=jnp.pad(w_down[l], ((0, D_FF_PAD - D_FF), (0, 0))).astype(BF16),
        )
        yp, *sp = _layer(yp, bp, lp, wts, bias16, None, None, None, None)
        ys, *ss = _layer(ys, bs, ls, wts, bias16, (state_attn_k[l], state_attn_v[l]), state_ssm[l],
                         state_ssd_conv[l], state_ffn_conv[l])
        outs_p.append(sp)
        outs_s.append(ss)
    stack = lambda outs, i: jnp.stack([o[i] for o in outs])
    return (yp.reshape(bp, lp, D_MODEL), ys.reshape(bs, ls, D_MODEL),
            stack(outs_p, 0), stack(outs_p, 1), stack(outs_p, 2), stack(outs_p, 3), stack(outs_p, 4),
            stack(outs_s, 0), stack(outs_s, 1), stack(outs_s, 2), stack(outs_s, 3), stack(outs_s, 4))
```

```python
import functools
import math

import numpy as np
import jax
import jax.numpy as jnp
from jax import lax
from jax.experimental import pallas as pl
from jax.experimental.pallas import tpu as pltpu

F32 = jnp.float32
BF16 = jnp.bfloat16

D_MODEL = 4096
HEAD_DIM = 128
ATTN_HEADS = 16
KV_HEADS = 4
Q_PER_KV = 4
ATTN_WIDTH = 2048
WINDOW = 128
N_BUCKETS = 32
MAX_DISTANCE = 128
SSD_HEAD_DIM = 64
SSD_HEADS = 32
SSD_WIDTH = 2048
SSD_GROUPS = 8
SSD_STATE = 128
SSD_CONV = 4
CONV_DIM = 4096
KV_WIDTH = KV_HEADS * HEAD_DIM
D_FF = 11008
D_FF_PAD = 11264
FFN_CONV = 3
EPS = 1e-6
NEG = -1e30
SCALE = HEAD_DIM ** -0.5

ROWS = 128
GROUP_ROWS = 4 * SSD_HEAD_DIM
COL_Z = ATTN_WIDTH
COL_XBC = COL_Z + SSD_WIDTH
COL_K = COL_XBC + CONV_DIM
COL_V = COL_K + KV_WIDTH
PROJ_W = COL_V + KV_WIDTH
VMEM_LIMIT_MB = 56


def _params(n_axes, vmem_mb=VMEM_LIMIT_MB):
    return pltpu.CompilerParams(dimension_semantics=("arbitrary",) * n_axes,
                                vmem_limit_bytes=vmem_mb << 20)


def _dot(a, b):
    return jnp.dot(a, b, preferred_element_type=F32)


def _dot_nt(a, b):
    return lax.dot_general(a, b, (((1,), (1,)), ((), ())), preferred_element_type=F32)


def _dot_tn(a, b):
    return lax.dot_general(a, b, (((0,), (0,)), ((), ())), preferred_element_type=F32)


def _split3(x):
    a = x.astype(BF16)
    r = x - a.astype(F32)
    b = r.astype(BF16)
    c = (r - b.astype(F32)).astype(BF16)
    return a, b, c


def _dot3_l(x, w):
    a, b, c = _split3(x)
    return (_dot(a, w) + _dot(b, w)) + _dot(c, w)


def _dot3_r(w, x3):
    return (_dot(w, x3[0]) + _dot(w, x3[1])) + _dot(w, x3[2])


def _sigmoid(x):
    return 1.0 / (1.0 + jnp.exp(-x))


def _rms_rows(x, w):
    r = lax.rsqrt(jnp.mean(x * x, axis=-1, keepdims=True) + EPS)
    return (x * r) * w


def _rmsnorm_body(x_ref, w_ref, o_ref):
    o_ref[...] = _rms_rows(x_ref[...], w_ref[...]).astype(o_ref.dtype)


def _rmsnorm(x, w, rows=256):
    m, d = x.shape
    rows = min(rows, m)
    return pl.pallas_call(
        _rmsnorm_body,
        grid=(m // rows,),
        in_specs=[pl.BlockSpec((rows, d), lambda i: (i, 0)),
                  pl.BlockSpec((1, d), lambda i: (0, 0))],
        out_specs=pl.BlockSpec((rows, d), lambda i: (i, 0)),
        out_shape=jax.ShapeDtypeStruct((m, d), BF16),
        compiler_params=_params(1, 32),
        name="rmsnorm",
    )(x, w.reshape(1, d))


def _proj_body(a_ref, b_ref, o_ref):
    o_ref[...] = _dot(a_ref[...], b_ref[...].astype(BF16))


def _in_proj(h, w_in, *, tm=1024, tn=512):
    m = h.shape[0]
    tm = min(tm, m)
    nq, nkv = ATTN_WIDTH // tn, KV_WIDTH // tn
    nzx = (SSD_WIDTH + CONV_DIM) // tn

    def src_block(j):
        return jnp.where(j < nq, j, jnp.where(j < nq + nzx, j + 2 * nkv, j - nzx))

    return pl.pallas_call(
        _proj_body,
        grid=(m // tm, PROJ_W // tn),
        in_specs=[pl.BlockSpec((tm, D_MODEL), lambda i, j: (i, 0)),
                  pl.BlockSpec((D_MODEL, tn), lambda i, j: (0, src_block(j)))],
        out_specs=pl.BlockSpec((tm, tn), lambda i, j: (i, j)),
        out_shape=jax.ShapeDtypeStruct((m, PROJ_W), F32),
        compiler_params=_params(2),
        name="in_proj",
    )(h, w_in)


def _dt_proj(h, w_dt, *, tm=1024):
    m = h.shape[0]
    tm = min(tm, m)
    n = w_dt.shape[1]
    return pl.pallas_call(
        _proj_body,
        grid=(m // tm,),
        in_specs=[pl.BlockSpec((tm, D_MODEL), lambda i: (i, 0)),
                  pl.BlockSpec((D_MODEL, n), lambda i: (0, 0))],
        out_specs=pl.BlockSpec((tm, n), lambda i: (i, 0)),
        out_shape=jax.ShapeDtypeStruct((m, n), F32),
        compiler_params=_params(1),
        name="dt_proj",
    )(h, w_dt)


def _mm2_body(a1_ref, a2_ref, b1_ref, b2_ref, res_ref, o_ref):
    acc = (_dot(a1_ref[...].astype(BF16), b1_ref[...].astype(BF16))
           + _dot(a2_ref[...].astype(BF16), b2_ref[...].astype(BF16)))
    o_ref[...] = res_ref[...] + acc


def _out_proj(attn, y, w_out, res, *, tm=1024, tn=512):
    m = attn.shape[0]
    n = w_out.shape[1]
    tm = min(tm, m)
    return pl.pallas_call(
        _mm2_body,
        grid=(m // tm, n // tn),
        in_specs=[pl.BlockSpec((tm, ATTN_WIDTH), lambda i, j: (i, 0)),
                  pl.BlockSpec((tm, SSD_WIDTH), lambda i, j: (i, 0)),
                  pl.BlockSpec((ATTN_WIDTH, tn), lambda i, j: (0, j)),
                  pl.BlockSpec((SSD_WIDTH, tn), lambda i, j: (1, j)),
                  pl.BlockSpec((tm, tn), lambda i, j: (i, j))],
        out_specs=pl.BlockSpec((tm, tn), lambda i, j: (i, j)),
        out_shape=jax.ShapeDtypeStruct((m, n), F32),
        compiler_params=_params(2),
        name="out_proj",
    )(attn, y, w_out, w_out, res)


def _bucket_table():
    i = np.arange(ROWS)[:, None]
    j = np.arange(2 * ROWS)[None, :]
    dist = i + ROWS - j
    valid = (dist >= 0) & (dist < WINDOW)
    n = np.maximum(dist, 0)
    max_exact = N_BUCKETS // 2

    def large(dtype):
        nf = np.maximum(n, 1).astype(dtype)
        v = np.log(nf / dtype(max_exact)) / dtype(math.log(MAX_DISTANCE / max_exact)) * dtype(N_BUCKETS - max_exact)
        return np.minimum(max_exact + v.astype(np.int32), N_BUCKETS - 1)

    assert (large(np.float32) == large(np.float64)).all()
    bucket = np.where(n < max_exact, n, large(np.float32))
    return np.where(valid, bucket, -1).astype(np.int32)


def _bias_body(rb_ref, bkt_ref, o_ref):
    h = pl.program_id(0)
    bkt = bkt_ref[...]
    acc = jnp.full(bkt.shape, NEG, F32)
    for b in range(N_BUCKETS):
        acc = jnp.where(bkt == b, rb_ref[b, h], acc)
    o_ref[0] = acc


def _bias_table(rel_bias):
    return pl.pallas_call(
        _bias_body,
        grid=(ATTN_HEADS,),
        in_specs=[pl.BlockSpec(memory_space=pltpu.SMEM),
                  pl.BlockSpec((ROWS, 2 * ROWS), lambda h: (0, 0))],
        out_specs=pl.BlockSpec((1, ROWS, 2 * ROWS), lambda h: (h, 0, 0)),
        out_shape=jax.ShapeDtypeStruct((ATTN_HEADS, ROWS, 2 * ROWS), F32),
        compiler_params=_params(1, 16),
        name="bias_table",
    )(rel_bias, jnp.asarray(_bucket_table()))


def _attend_group(qs, kp, kc, vp, vc, bp, bc, sink, neg_prev):
    sp = _dot_nt(qs, kp) * SCALE + bp + neg_prev
    sc = _dot_nt(qs, kc) * SCALE + bc
    m = jnp.maximum(jnp.maximum(jnp.max(sp, axis=-1, keepdims=True),
                                jnp.max(sc, axis=-1, keepdims=True)), sink)
    pp = jnp.exp(sp - m)
    pc = jnp.exp(sc - m)
    den = (jnp.sum(pp, axis=-1, keepdims=True) + jnp.sum(pc, axis=-1, keepdims=True)) + jnp.exp(sink - m)
    inv = 1.0 / den
    return _dot((pp * inv).astype(BF16), vp) + _dot((pc * inv).astype(BF16), vc)


def _attn_prompt_body(sink_ref, q_ref, kc_ref, kp_ref, vc_ref, vp_ref, qw_ref, kw_ref, bias_ref,
                      o_ref, pk_ref, pv_ref):
    i = pl.program_id(1)
    neg_prev = jnp.where(i == 0, NEG, 0.0).astype(F32)
    qw = qw_ref[...]
    kw = kw_ref[...]
    for kv in range(KV_HEADS):
        sl = slice(kv * HEAD_DIM, (kv + 1) * HEAD_DIM)
        kc = _rms_rows(kc_ref[:, sl], kw)
        kp = _rms_rows(kp_ref[:, sl], kw)
        pk_ref[0, :, sl] = kc
        heads = [kv * Q_PER_KV + g for g in range(Q_PER_KV)]
        qs = jnp.concatenate(
            [_rms_rows(q_ref[:, h * HEAD_DIM:(h + 1) * HEAD_DIM], qw) for h in heads], axis=0).astype(BF16)
        sink = jnp.concatenate([jnp.full((ROWS, 1), sink_ref[h], F32) for h in heads], axis=0)
        o = _attend_group(qs, kp.astype(BF16), kc.astype(BF16),
                          vp_ref[:, sl].astype(BF16), vc_ref[:, sl].astype(BF16),
                          bias_ref[kv, :, 0:ROWS], bias_ref[kv, :, ROWS:2 * ROWS], sink, neg_prev)
        for g, h in enumerate(heads):
            o_ref[:, h * HEAD_DIM:(h + 1) * HEAD_DIM] = o[g * ROWS:(g + 1) * ROWS].astype(o_ref.dtype)
    pv_ref[0] = vc_ref[...]


def _attn_prompt(proj, sinks, qw, kw, bias4, nseq, seqlen):
    nb = seqlen // ROWS
    kblk, vblk = COL_K // KV_WIDTH, COL_V // KV_WIDTH
    cur = lambda b, i: b * nb + i
    prev = lambda b, i: b * nb + jnp.maximum(i - 1, 0)
    return pl.pallas_call(
        _attn_prompt_body,
        grid=(nseq, nb),
        in_specs=[pl.BlockSpec(memory_space=pltpu.SMEM),
                  pl.BlockSpec((ROWS, ATTN_WIDTH), lambda b, i: (cur(b, i), 0)),
                  pl.BlockSpec((ROWS, KV_WIDTH), lambda b, i: (cur(b, i), kblk)),
                  pl.BlockSpec((ROWS, KV_WIDTH), lambda b, i: (prev(b, i), kblk)),
                  pl.BlockSpec((ROWS, KV_WIDTH), lambda b, i: (cur(b, i), vblk)),
                  pl.BlockSpec((ROWS, KV_WIDTH), lambda b, i: (prev(b, i), vblk)),
                  pl.BlockSpec((1, HEAD_DIM), lambda b, i: (0, 0)),
                  pl.BlockSpec((1, HEAD_DIM), lambda b, i: (0, 0)),
                  pl.BlockSpec((KV_HEADS, Q_PER_KV * ROWS, 2 * ROWS), lambda b, i: (0, 0, 0))],
        out_specs=[pl.BlockSpec((ROWS, ATTN_WIDTH), lambda b, i: (cur(b, i), 0)),
                   pl.BlockSpec((1, ROWS, KV_WIDTH), lambda b, i: (b, 0, 0)),
                   pl.BlockSpec((1, ROWS, KV_WIDTH), lambda b, i: (b, 0, 0))],
        out_shape=[jax.ShapeDtypeStruct((nseq * seqlen, ATTN_WIDTH), BF16),
                   jax.ShapeDtypeStruct((nseq, ROWS, KV_WIDTH), F32),
                   jax.ShapeDtypeStruct((nseq, ROWS, KV_WIDTH), F32)],
        compiler_params=_params(2, 32),
        name="attn_prompt",
    )(sinks, proj, proj, proj, proj, proj, qw, kw, bias4)


def _attn_sample_body(q_ref, kn_ref, vn_ref, wk_ref, wv_ref, qw_ref, kw_ref, bias_ref, sink_ref,
                      o_ref, sk_ref, sv_ref, qn_scr, kn_scr, s_scr, p_scr, *, nsq, ntok):
    qw = qw_ref[...]
    kw = kw_ref[...]
    grows = Q_PER_KV * ntok
    keep = ROWS - ntok
    for h in range(ATTN_HEADS):
        hs = slice(h * HEAD_DIM, (h + 1) * HEAD_DIM)
        qn_scr[:, hs] = _rms_rows(q_ref[:, hs], qw)
    for kv in range(KV_HEADS):
        sl = slice(kv * HEAD_DIM, (kv + 1) * HEAD_DIM)
        kn_scr[:, sl] = _rms_rows(kn_ref[:, sl], kw)
    for s in range(nsq):
        rows = slice(s * ntok, (s + 1) * ntok)
        sk_ref[s, 0:keep] = wk_ref[s, ntok:ROWS]
        sv_ref[s, 0:keep] = wv_ref[s, ntok:ROWS]
        for kv in range(KV_HEADS):
            sl = slice(kv * HEAD_DIM, (kv + 1) * HEAD_DIM)
            sk_ref[s, keep:ROWS, kv, :] = kn_scr[rows, sl]
            sv_ref[s, keep:ROWS, kv, :] = vn_ref[rows, sl]

    pad = jnp.zeros((keep, HEAD_DIM), F32)
    for s in range(nsq):
        rows = slice(s * ntok, (s + 1) * ntok)
        for kv in range(KV_HEADS):
            sl = slice(kv * HEAD_DIM, (kv + 1) * HEAD_DIM)
            grp = s * KV_HEADS + kv
            qs = jnp.concatenate([qn_scr[rows, (kv * Q_PER_KV + g) * HEAD_DIM:(kv * Q_PER_KV + g + 1) * HEAD_DIM]
                                  for g in range(Q_PER_KV)], axis=0).astype(BF16)
            kcat = jnp.concatenate([wk_ref[s, :, kv, :], kn_scr[rows, sl], pad], axis=0).astype(BF16)
            s_scr[grp * grows:(grp + 1) * grows, :] = _dot_nt(qs, kcat)

    sc = s_scr[...] * SCALE + bias_ref[...]
    sink = sink_ref[...]
    m = jnp.maximum(jnp.max(sc, axis=-1, keepdims=True), sink)
    p = jnp.exp(sc - m)
    den = jnp.sum(p, axis=-1, keepdims=True) + jnp.exp(sink - m)
    p_scr[...] = (p * (1.0 / den)).astype(BF16)

    for s in range(nsq):
        rows = slice(s * ntok, (s + 1) * ntok)
        for kv in range(KV_HEADS):
            sl = slice(kv * HEAD_DIM, (kv + 1) * HEAD_DIM)
            grp = s * KV_HEADS + kv
            vcat = jnp.concatenate([wv_ref[s, :, kv, :], vn_ref[rows, sl], pad], axis=0).astype(BF16)
            o = _dot(p_scr[grp * grows:(grp + 1) * grows, :], vcat)
            for g in range(Q_PER_KV):
                h = kv * Q_PER_KV + g
                o_ref[rows, h * HEAD_DIM:(h + 1) * HEAD_DIM] = o[g * ntok:(g + 1) * ntok]


def _attn_sample(proj, win_k, win_v, sinks, qw, kw, bias16, nseq, ntok, nsq=8):
    kblk, vblk = COL_K // KV_WIDTH, COL_V // KV_WIDTH
    rows = nsq * ntok
    srows = nsq * ATTN_HEADS * ntok
    bias = jnp.tile(bias16[:, :ntok, :].reshape(ATTN_HEADS * ntok, 2 * ROWS), (nsq, 1))
    sink = jnp.tile(jnp.repeat(sinks, ntok), nsq).reshape(srows, 1)
    return pl.pallas_call(
        functools.partial(_attn_sample_body, nsq=nsq, ntok=ntok),
        grid=(nseq // nsq,),
        in_specs=[pl.BlockSpec((rows, ATTN_WIDTH), lambda i: (i, 0)),
                  pl.BlockSpec((rows, KV_WIDTH), lambda i: (i, kblk)),
                  pl.BlockSpec((rows, KV_WIDTH), lambda i: (i, vblk)),
                  pl.BlockSpec((nsq, ROWS, KV_HEADS, HEAD_DIM), lambda i: (i, 0, 0, 0)),
                  pl.BlockSpec((nsq, ROWS, KV_HEADS, HEAD_DIM), lambda i: (i, 0, 0, 0)),
                  pl.BlockSpec((1, HEAD_DIM), lambda i: (0, 0)),
                  pl.BlockSpec((1, HEAD_DIM), lambda i: (0, 0)),
                  pl.BlockSpec((srows, 2 * ROWS), lambda i: (0, 0)),
                  pl.BlockSpec((srows, 1), lambda i: (0, 0))],
        out_specs=[pl.BlockSpec((rows, ATTN_WIDTH), lambda i: (i, 0)),
                   pl.BlockSpec((nsq, ROWS, KV_HEADS, HEAD_DIM), lambda i: (i, 0, 0, 0)),
                   pl.BlockSpec((nsq, ROWS, KV_HEADS, HEAD_DIM), lambda i: (i, 0, 0, 0))],
        out_shape=[jax.ShapeDtypeStruct((nseq * ntok, ATTN_WIDTH), F32),
                   jax.ShapeDtypeStruct((nseq, ROWS, KV_HEADS, HEAD_DIM), F32),
                   jax.ShapeDtypeStruct((nseq, ROWS, KV_HEADS, HEAD_DIM), F32)],
        scratch_shapes=[pltpu.VMEM((rows, ATTN_WIDTH), F32),
                        pltpu.VMEM((rows, KV_WIDTH), F32),
                        pltpu.VMEM((srows, 2 * ROWS), F32),
                        pltpu.VMEM((srows, 2 * ROWS), BF16)],
        compiler_params=_params(1, 32),
        name="attn_sample",
    )(proj, proj, proj, win_k, win_v, qw, kw, bias, sink)


def _ssd_constants(qs):
    r = np.arange(ROWS)
    same = (r[:, None] // qs) == (r[None, :] // qs)
    mask = same & (r[None, :] <= r[:, None])
    sel = r[None, :] == ((r[:, None] // qs) * qs + qs - 1)
    expand = np.zeros((ROWS, SSD_WIDTH), np.float32)
    for h in range(SSD_HEADS):
        expand[h, h * SSD_HEAD_DIM:(h + 1) * SSD_HEAD_DIM] = 1.0
    to_bf = lambda a: jnp.asarray(a.astype(np.float32), BF16)
    return to_bf(mask), to_bf(sel), to_bf(expand), to_bf(expand.T)


def _ssd_body(xbc_ref, z_ref, dt_ref, cprev_ref, h0_ref, cw_ref, cb_ref, dtb_ref, alog_ref, dexp_ref,
              nw_ref, mask_ref, sel_ref, e_ref, et_ref,
              y_ref, ht_ref,
              h_scr, cbuf, xc_scr, xdt_scr, xdd_scr, eacs_scr, yacc_scr, acs_scr, acst_scr, aclt_scr,
              *, nsq, qs):
    c = pl.program_id(1)
    s = pl.program_id(2)

    @pl.when(c == 0)
    def _():
        h_scr[...] = h0_ref[0]

    if nsq == 1:
        @pl.when(c == 0)
        def _():
            cbuf[5:8, :] = cprev_ref[0]

    @pl.when(s == 0)
    def _chunk_phase():
        ncc = 8 if qs == ROWS else 1
        cwid = CONV_DIM // ncc
        for sq in range(nsq):
            for cc in range(ncc):
                cs = slice(cc * cwid, (cc + 1) * cwid)
                if nsq > 1:
                    cbuf[5:8, cs] = cprev_ref[sq, :, cs]
                cbuf[8:8 + qs, cs] = xbc_ref[sq * qs:(sq + 1) * qs, cs]
                acc = cb_ref[:, cs] + cw_ref[0:1, cs] * cbuf[5:5 + qs, cs]
                for t in range(1, SSD_CONV):
                    acc = acc + cw_ref[t:t + 1, cs] * cbuf[5 + t:5 + t + qs, cs]
                xc_scr[sq * qs:(sq + 1) * qs, cs] = acc * _sigmoid(acc)
                if nsq == 1:
                    cbuf[5:8, cs] = xbc_ref[qs - 3:qs, cs]

        x = dt_ref[...] + dtb_ref[...]
        dt = jnp.maximum(x, 0.0) + jnp.log1p(jnp.exp(-jnp.abs(x)))
        da = dt * (-jnp.exp(alog_ref[...]))
        mask_bf = mask_ref[...]
        acs = _dot3_r(mask_bf, _split3(da))
        acl = _dot3_r(sel_ref[...], _split3(acs))
        acs_scr[...] = acs
        acst_scr[...] = acs.T
        aclt_scr[...] = acl.T
        e = e_ref[...]
        dt_e = _dot3_l(dt, e)
        acs_e = _dot3_l(acs, e)
        acl_e = _dot3_l(acl, e)
        xs = xc_scr[:, 0:SSD_WIDTH]
        xdt = xs * dt_e
        xdt_scr[...] = xdt
        xdd_scr[...] = xdt * jnp.exp(acl_e - acs_e)
        eacs_scr[...] = jnp.exp(acs_e)

        maskb = mask_bf > 0.5
        lane = lax.broadcasted_iota(jnp.int32, (ROWS, 2 * SSD_HEAD_DIM), 1)
        for g in range(SSD_GROUPS):
            bg = xc_scr[:, SSD_WIDTH + g * SSD_STATE:SSD_WIDTH + (g + 1) * SSD_STATE].astype(BF16)
            cg = xc_scr[:, SSD_WIDTH + (SSD_GROUPS + g) * SSD_STATE:
                        SSD_WIDTH + (SSD_GROUPS + g + 1) * SSD_STATE].astype(BF16)
            cb = _dot_nt(cg, bg)
            for pp in range(2):
                ms = []
                for j in range(2):
                    h = 4 * g + 2 * pp + j
                    seg = acs_scr[:, h:h + 1] - acst_scr[h:h + 1, :]
                    ms.append((cb * jnp.exp(jnp.where(maskb, seg, NEG))).astype(BF16))
                c0 = (4 * g + 2 * pp) * SSD_HEAD_DIM
                cols = slice(c0, c0 + 2 * SSD_HEAD_DIM)
                xp = xdt_scr[:, cols]
                rhs = jnp.concatenate([jnp.where(lane < SSD_HEAD_DIM, xp, 0.0),
                                       jnp.where(lane >= SSD_HEAD_DIM, xp, 0.0)], axis=0).astype(BF16)
                yd = _dot(jnp.concatenate(ms, axis=1), rhs)
                yacc_scr[:, cols] = yd + dexp_ref[:, cols] * xc_scr[:, cols]

    if nsq == 1:
        r0 = 0
        rows = slice(0, qs)
    else:
        r0 = pl.multiple_of(s * qs, qs)
        rows = pl.ds(r0, qs)
        ridx = lax.broadcasted_iota(jnp.int32, (ROWS, 1), 0)
        rowmask = jnp.logical_and(ridx >= r0, ridx < r0 + qs)
    tlane = lax.broadcasted_iota(jnp.int32, (ROWS, ROWS), 1)
    xcol = jnp.sum(jnp.where(tlane == r0, aclt_scr[...], 0.0), axis=1, keepdims=True)
    x3 = _split3(jnp.broadcast_to(xcol, (ROWS, ROWS)))
    for g in range(SSD_GROUPS):
        grows = slice(g * GROUP_ROWS, (g + 1) * GROUP_ROWS)
        hs = h_scr[grows, :]
        cg = xc_scr[rows, SSD_WIDTH + (SSD_GROUPS + g) * SSD_STATE:
                    SSD_WIDTH + (SSD_GROUPS + g + 1) * SSD_STATE].astype(BF16)
        yo = _dot_nt(cg, hs.astype(BF16))
        yacc_scr[rows, grows] = yacc_scr[rows, grows] + yo * eacs_scr[rows, grows]
        bg = xc_scr[:, SSD_WIDTH + g * SSD_STATE:SSD_WIDTH + (g + 1) * SSD_STATE]
        if nsq > 1:
            bg = jnp.where(rowmask, bg, 0.0)
        st = _dot_tn(xdd_scr[:, grows].astype(BF16), bg.astype(BF16))
        dec = jnp.exp(_dot3_r(et_ref[grows, :], x3))
        h_scr[grows, :] = dec * hs + st
    ht_ref[0] = h_scr[...]

    @pl.when(s == nsq - 1)
    def _finish():
        z = z_ref[...]
        yg = yacc_scr[...] * (z * _sigmoid(z))
        y_ref[...] = _rms_rows(yg, nw_ref[...]).astype(y_ref.dtype)


def _ssd(proj, dt_raw, conv_prev, h0, cw, cb, dtb, alog, dexp, nw, nseq, seqlen):
    if seqlen % ROWS == 0:
        qs, nsq, nchunk, ngrp = ROWS, 1, seqlen // ROWS, nseq
    else:
        qs, nsq, nchunk = seqlen, ROWS // seqlen, 1
        ngrp = nseq // nsq
    mask, sel, expand, expand_t = _ssd_constants(qs)
    rowblk = lambda b, c, s: b * nchunk + c
    seq = lambda b, c, s: b * nsq + s
    const2 = lambda b, c, s: (0, 0)
    full = lambda shape: pl.BlockSpec(shape, const2)
    return pl.pallas_call(
        functools.partial(_ssd_body, nsq=nsq, qs=qs),
        grid=(ngrp, nchunk, nsq),
        in_specs=[pl.BlockSpec((ROWS, CONV_DIM), lambda b, c, s: (rowblk(b, c, s), COL_XBC // CONV_DIM)),
                  pl.BlockSpec((ROWS, SSD_WIDTH), lambda b, c, s: (rowblk(b, c, s), COL_Z // SSD_WIDTH)),
                  pl.BlockSpec((ROWS, ROWS), lambda b, c, s: (rowblk(b, c, s), 0)),
                  pl.BlockSpec((nsq, SSD_CONV - 1, CONV_DIM), lambda b, c, s: (b, 0, 0)),
                  pl.BlockSpec((1, SSD_WIDTH, SSD_STATE), lambda b, c, s: (seq(b, c, s), 0, 0)),
                  full((SSD_CONV, CONV_DIM)), full((1, CONV_DIM)), full((1, ROWS)), full((1, ROWS)),
                  full((1, SSD_WIDTH)), full((1, SSD_WIDTH)),
                  full((ROWS, ROWS)), full((ROWS, ROWS)), full((ROWS, SSD_WIDTH)), full((SSD_WIDTH, ROWS))],
        out_specs=[pl.BlockSpec((ROWS, SSD_WIDTH), lambda b, c, s: (rowblk(b, c, s), 0)),
                   pl.BlockSpec((1, SSD_WIDTH, SSD_STATE), lambda b, c, s: (seq(b, c, s), 0, 0))],
        out_shape=[jax.ShapeDtypeStruct((nseq * seqlen, SSD_WIDTH), BF16),
                   jax.ShapeDtypeStruct((nseq, SSD_WIDTH, SSD_STATE), F32)],
        scratch_shapes=[pltpu.VMEM((SSD_WIDTH, SSD_STATE), F32),
                        pltpu.VMEM((8 + ROWS, CONV_DIM), F32),
                        pltpu.VMEM((ROWS, CONV_DIM), F32),
                        pltpu.VMEM((ROWS, SSD_WIDTH), F32),
                        pltpu.VMEM((ROWS, SSD_WIDTH), F32),
                        pltpu.VMEM((ROWS, SSD_WIDTH), F32),
                        pltpu.VMEM((ROWS, SSD_WIDTH), F32),
                        pltpu.VMEM((ROWS, ROWS), F32),
                        pltpu.VMEM((ROWS, ROWS), F32),
                        pltpu.VMEM((ROWS, ROWS), F32)],
        compiler_params=_params(3, 40),
        name="ssd",
    )(proj, proj, dt_raw, conv_prev, h0, cw, cb, dtb, alog, dexp, nw, mask, sel, expand, expand_t)


def _ffn_gate_body(h_ref, wg_ref, wu_ref, cw_ref, cb_ref, *rest, period, tiles_per_seq):
    if period is None:
        a_ref, last_ref, carry_scr = rest
    else:
        prev_ref, a_ref, g_ref = rest
    hb = h_ref[...]
    tn = wg_ref.shape[1]
    col = pl.program_id(1) * tn + lax.broadcasted_iota(jnp.int32, (1, tn), 1)
    g = _dot(hb, jnp.where(col < D_FF, wg_ref[...], 0))
    u = _dot(hb, jnp.where(col < D_FF, wu_ref[...], 0))
    tm = g.shape[0]
    row = lax.broadcasted_iota(jnp.int32, g.shape, 0)
    if period is None:
        i, j = pl.program_id(0), pl.program_id(1)
        c8 = jnp.where(i % tiles_per_seq == 0, 0.0, carry_scr[j])
        carry_scr[j] = g[tm - 8:tm]
        last_ref[0] = g[tm - 2:tm]
        g_m1 = jnp.where(row == 0, c8[7:8], pltpu.roll(g, 1, 0))
        g_m2 = jnp.where(row == 0, c8[6:7], jnp.where(row == 1, c8[7:8], pltpu.roll(g, 2, 0)))
    else:
        g_ref[...] = g
        prev = prev_ref[...]
        t = row & (period - 1)
        g_m1 = jnp.where(t == 0, pltpu.roll(prev, tm - 1, 0), pltpu.roll(g, 1, 0))
        g_m2 = jnp.where(t < 2, prev, pltpu.roll(g, 2, 0))
    gc = cb_ref[...] + cw_ref[0:1] * g_m2
    gc = gc + cw_ref[1:2] * g_m1
    gc = gc + cw_ref[2:3] * g
    a_ref[...] = ((gc * _sigmoid(gc)) * u).astype(a_ref.dtype)


def _ffn_gate(h, w_gate, w_up, cw, cb, *, nseq, seqlen, prev_rows=None, tm=1024, tn=512):
    m = h.shape[0]
    nj = D_FF_PAD // tn
    wspec = pl.BlockSpec((D_MODEL, tn), lambda i, j: (0, j))
    cwspec = pl.BlockSpec((FFN_CONV, tn), lambda i, j: (0, j))
    cbspec = pl.BlockSpec((1, tn), lambda i, j: (0, j))
    if prev_rows is None:
        tm = min(tm, seqlen)
        tps = seqlen // tm
        in_specs = [pl.BlockSpec((tm, D_MODEL), lambda i, j: (i, 0)), wspec, wspec, cwspec, cbspec]
        out_specs = [pl.BlockSpec((tm, tn), lambda i, j: (i, j)),
                     pl.BlockSpec((1, FFN_CONV - 1, tn), lambda i, j: (i, 0, j))]
        out_shape = [jax.ShapeDtypeStruct((m, D_FF_PAD), BF16),
                     jax.ShapeDtypeStruct((m // tm, FFN_CONV - 1, D_FF_PAD), F32)]
        scratch = [pltpu.VMEM((nj, 8, tn), F32)]
        args = (h, w_gate, w_up, cw, cb)
        body = functools.partial(_ffn_gate_body, period=None, tiles_per_seq=tps)
    else:
        assert seqlen & (seqlen - 1) == 0
        tm = m
        in_specs = [pl.BlockSpec((tm, D_MODEL), lambda i, j: (i, 0)), wspec, wspec, cwspec, cbspec,
                    pl.BlockSpec((tm, tn), lambda i, j: (i, j))]
        out_specs = [pl.BlockSpec((tm, tn), lambda i, j: (i, j)),
                     pl.BlockSpec((tm, tn), lambda i, j: (i, j))]
        out_shape = [jax.ShapeDtypeStruct((m, D_FF_PAD), BF16),
                     jax.ShapeDtypeStruct((m, D_FF_PAD), F32)]
        scratch = []
        args = (h, w_gate, w_up, cw, cb, prev_rows)
        body = functools.partial(_ffn_gate_body, period=seqlen, tiles_per_seq=None)
    return pl.pallas_call(
        body,
        grid=(m // tm, nj),
        in_specs=in_specs,
        out_specs=out_specs,
        out_shape=out_shape,
        scratch_shapes=scratch,
        compiler_params=_params(2),
        name="ffn_gate",
    )(*args)


def _down_body(a_ref, w_ref, res_ref, o_ref, *, chunk):
    @pl.when(pl.program_id(2) == 0)
    def _():
        o_ref[...] = res_ref[...]

    a = a_ref[...]
    tk = w_ref.shape[0]
    row = pl.program_id(2) * tk + lax.broadcasted_iota(jnp.int32, (tk, 1), 0)
    for c in range(o_ref.shape[1] // chunk):
        cs = slice(c * chunk, (c + 1) * chunk)
        o_ref[:, cs] += _dot(a, jnp.where(row < D_FF, w_ref[:, cs], 0))


def _down_proj(a, w_down, res, *, tm=1024, tn=2048, tk=1024):
    m, kdim = a.shape
    n = w_down.shape[1]
    tm = min(tm, m)
    return pl.pallas_call(
        functools.partial(_down_body, chunk=512),
        grid=(n // tn, m // tm, kdim // tk),
        in_specs=[pl.BlockSpec((tm, tk), lambda j, i, k: (i, k)),
                  pl.BlockSpec((tk, tn), lambda j, i, k: (k, j)),
                  pl.BlockSpec((tm, tn), lambda j, i, k: (i, j))],
        out_specs=pl.BlockSpec((tm, tn), lambda j, i, k: (i, j)),
        out_shape=jax.ShapeDtypeStruct((m, n), F32),
        compiler_params=_params(3),
        name="down_proj",
    )(a, w_down, res)


def _layer(x, nseq, seqlen, wts, bias16, attn_state, ssm_state, conv_state, ffn_state):
    m = x.shape[0]
    h = _rmsnorm(x, wts["mix_norm_w"])
    proj = _in_proj(h, wts["w_in"])
    dt_raw = _dt_proj(h, wts["w_dt"])

    qw = wts["q_norm_w"].reshape(1, HEAD_DIM)
    kw = wts["k_norm_w"].reshape(1, HEAD_DIM)
    if attn_state is None:
        bias4 = bias16.reshape(KV_HEADS, Q_PER_KV * ROWS, 2 * ROWS)
        attn, new_k, new_v = _attn_prompt(proj, wts["attn_sinks"], qw, kw, bias4, nseq, seqlen)
        new_k = new_k.reshape(nseq, WINDOW, KV_HEADS, HEAD_DIM)
        new_v = new_v.reshape(nseq, WINDOW, KV_HEADS, HEAD_DIM)
        conv_prev = jnp.zeros((nseq, SSD_CONV - 1, CONV_DIM), F32)
        h0 = jnp.zeros((nseq, SSD_WIDTH, SSD_STATE), F32)
    else:
        win_k, win_v = attn_state
        attn, new_k, new_v = _attn_sample(proj, win_k, win_v, wts["attn_sinks"], qw, kw, bias16, nseq, seqlen)
        conv_prev = conv_state
        h0 = ssm_state.reshape(nseq, SSD_WIDTH, SSD_STATE)

    y, h_t = _ssd(proj, dt_raw, conv_prev, h0, wts["ssd_conv_w"], wts["ssd_conv_b"], wts["ssd_dt_bias"],
                  wts["ssd_A_log"], wts["ssd_D"], wts["ssd_norm_w"], nseq, seqlen)
    h_t = h_t.reshape(nseq, SSD_HEADS, SSD_HEAD_DIM, SSD_STATE)
    new_conv = proj.reshape(nseq, seqlen, PROJ_W)[:, seqlen - (SSD_CONV - 1):, COL_XBC:COL_XBC + CONV_DIM]

    x1 = _out_proj(attn, y, wts["w_out"], x)
    h2 = _rmsnorm(x1, wts["ffn_norm_w"])
    if ffn_state is None:
        a, last = _ffn_gate(h2, wts["w_gate"], wts["w_up"], wts["ffn_conv_w"], wts["ffn_conv_b"],
                            nseq=nseq, seqlen=seqlen)
        new_ffn = last.reshape(nseq, -1, FFN_CONV - 1, D_FF_PAD)[:, -1, :, :D_FF]
    else:
        prev_rows = jnp.pad(ffn_state, ((0, 0), (0, seqlen - (FFN_CONV - 1)), (0, D_FF_PAD - D_FF)))
        a, g = _ffn_gate(h2, wts["w_gate"], wts["w_up"], wts["ffn_conv_w"], wts["ffn_conv_b"],
                         nseq=nseq, seqlen=seqlen, prev_rows=prev_rows.reshape(m, D_FF_PAD))
        new_ffn = g.reshape(nseq, seqlen, D_FF_PAD)[:, seqlen - (FFN_CONV - 1):, :D_FF]
    x2 = _down_proj(a, wts["w_down"], x1)
    return x2, new_k, new_v, h_t, new_conv, new_ffn


def kernel(x_prompt, x_sample, state_attn_k, state_attn_v, state_ssm, state_ssd_conv, state_ffn_conv, rel_bias, mix_norm_w, w_in, q_norm_w, k_norm_w, attn_sinks, ssd_conv_w, ssd_conv_b, ssd_dt_bias, ssd_A_log, ssd_D, ssd_norm_w, w_out, ffn_norm_w, w_gate, w_up, ffn_conv_w, ffn_conv_b, w_down):
    depth = w_in.shape[0]
    bp, lp, _ = x_prompt.shape
    bs, ls, _ = x_sample.shape
    bias16 = _bias_table(rel_bias)
    yp = x_prompt.reshape(bp * lp, D_MODEL)
    ys = x_sample.reshape(bs * ls, D_MODEL)
    outs_p, outs_s = [], []
    pad_heads = lambda v: jnp.pad(v.reshape(1, SSD_HEADS), ((0, 0), (0, ROWS - SSD_HEADS)))
    pad_ff = lambda v: jnp.pad(v, ((0, 0), (0, D_FF_PAD - D_FF)))
    for l in range(depth):
        w = w_in[l]
        wts = dict(
            mix_norm_w=mix_norm_w[l],
            w_in=w,
            w_dt=jnp.pad(w[:, PROJ_W:], ((0, 0), (0, ROWS - SSD_HEADS))).astype(BF16),
            q_norm_w=q_norm_w[l], k_norm_w=k_norm_w[l], attn_sinks=attn_sinks[l],
            ssd_conv_w=ssd_conv_w[l], ssd_conv_b=ssd_conv_b[l].reshape(1, CONV_DIM),
            ssd_dt_bias=pad_heads(ssd_dt_bias[l]), ssd_A_log=pad_heads(ssd_A_log[l]),
            ssd_D=jnp.repeat(ssd_D[l], SSD_HEAD_DIM).reshape(1, SSD_WIDTH),
            ssd_norm_w=ssd_norm_w[l].reshape(1, SSD_WIDTH),
            w_out=w_out[l], ffn_norm_w=ffn_norm_w[l],
            w_gate=w_gate[l].astype(BF16), w_up=w_up[l].astype(BF16),
            ffn_conv_w=pad_ff(ffn_conv_w[l]), ffn_conv_b=pad_ff(ffn_conv_b[l].reshape(1, D_FF)),
            w_down=w_down[l].astype(BF16),
        )
        yp, *sp = _layer(yp, bp, lp, wts, bias16, None, None, None, None)
        ys, *ss = _layer(ys, bs, ls, wts, bias16, (state_attn_k[l], state_attn_v[l]), state_ssm[l],
                         state_ssd_conv[l], state_ffn_conv[l])
        outs_p.append(sp)
        outs_s.append(ss)
    stack = lambda outs, i: jnp.stack([o[i] for o in outs])
    return (yp.reshape(bp, lp, D_MODEL), ys.reshape(bs, ls, D_MODEL),
            stack(outs_p, 0), stack(outs_p, 1), stack(outs_p, 2), stack(outs_p, 3), stack(outs_p, 4),
            stack(outs_s, 0), stack(outs_s, 1), stack(outs_s, 2), stack(outs_s, 3), stack(outs_s, 4))
```

```python
import functools
import math

import numpy as np
import jax
import jax.numpy as jnp
from jax import lax
from jax.experimental import pallas as pl
from jax.experimental.pallas import tpu as pltpu

F32 = jnp.float32
BF16 = jnp.bfloat16

D_MODEL = 4096
HEAD_DIM = 128
ATTN_HEADS = 16
KV_HEADS = 4
Q_PER_KV = 4
ATTN_WIDTH = 2048
WINDOW = 128
N_BUCKETS = 32
MAX_DISTANCE = 128
SSD_HEAD_DIM = 64
SSD_HEADS = 32
SSD_WIDTH = 2048
SSD_GROUPS = 8
SSD_STATE = 128
SSD_CONV = 4
CONV_DIM = 4096
KV_WIDTH = KV_HEADS * HEAD_DIM
D_FF = 11008
D_FF_PAD = 11264
FFN_CONV = 3
EPS = 1e-6
NEG = -1e30
SCALE = HEAD_DIM ** -0.5

ROWS = 128
GROUP_ROWS = 4 * SSD_HEAD_DIM
COL_Z = ATTN_WIDTH
COL_XBC = COL_Z + SSD_WIDTH
COL_K = COL_XBC + CONV_DIM
COL_V = COL_K + KV_WIDTH
PROJ_W = COL_V + KV_WIDTH
VMEM_LIMIT_MB = 56


def _params(n_axes, vmem_mb=VMEM_LIMIT_MB):
    return pltpu.CompilerParams(dimension_semantics=("arbitrary",) * n_axes,
                                vmem_limit_bytes=vmem_mb << 20)


def _dot(a, b):
    return jnp.dot(a, b, preferred_element_type=F32)


def _dot_nt(a, b):
    return lax.dot_general(a, b, (((1,), (1,)), ((), ())), preferred_element_type=F32)


def _dot_tn(a, b):
    return lax.dot_general(a, b, (((0,), (0,)), ((), ())), preferred_element_type=F32)


def _split3(x):
    a = x.astype(BF16)
    r = x - a.astype(F32)
    b = r.astype(BF16)
    c = (r - b.astype(F32)).astype(BF16)
    return a, b, c


def _dot3_l(x, w):
    a, b, c = _split3(x)
    return (_dot(a, w) + _dot(b, w)) + _dot(c, w)


def _dot3_r(w, x3):
    return (_dot(w, x3[0]) + _dot(w, x3[1])) + _dot(w, x3[2])


def _sigmoid(x):
    return 1.0 / (1.0 + jnp.exp(-x))


def _rms_rows(x, w):
    r = lax.rsqrt(jnp.mean(x * x, axis=-1, keepdims=True) + EPS)
    return (x * r) * w


def _rmsnorm_body(x_ref, w_ref, o_ref):
    o_ref[...] = _rms_rows(x_ref[...], w_ref[...]).astype(o_ref.dtype)


def _rmsnorm(x, w, rows=256):
    m, d = x.shape
    rows = min(rows, m)
    return pl.pallas_call(
        _rmsnorm_body,
        grid=(m // rows,),
        in_specs=[pl.BlockSpec((rows, d), lambda i: (i, 0)),
                  pl.BlockSpec((1, d), lambda i: (0, 0))],
        out_specs=pl.BlockSpec((rows, d), lambda i: (i, 0)),
        out_shape=jax.ShapeDtypeStruct((m, d), BF16),
        compiler_params=_params(1, 32),
        name="rmsnorm",
    )(x, w.reshape(1, d))


def _proj_body(a_ref, bt_ref, o_ref, *, valid_rows):
    bt = bt_ref[...]
    if valid_rows is not None:
        row = lax.broadcasted_iota(jnp.int32, (bt.shape[0], 1), 0)
        bt = jnp.where(row < valid_rows, bt, 0.0)
    o_ref[...] = _dot_nt(a_ref[...], bt.astype(BF16))


def _in_proj(h, w_in_t, *, tm=1024, tn=512):
    m = h.shape[0]
    tm = min(tm, m)
    nq, nkv = ATTN_WIDTH // tn, KV_WIDTH // tn
    nzx = (SSD_WIDTH + CONV_DIM) // tn

    def src_block(j):
        return jnp.where(j < nq, j, jnp.where(j < nq + nzx, j + 2 * nkv, j - nzx))

    return pl.pallas_call(
        functools.partial(_proj_body, valid_rows=None),
        grid=(m // tm, PROJ_W // tn),
        in_specs=[pl.BlockSpec((tm, D_MODEL), lambda i, j: (i, 0)),
                  pl.BlockSpec((tn, D_MODEL), lambda i, j: (src_block(j), 0))],
        out_specs=pl.BlockSpec((tm, tn), lambda i, j: (i, j)),
        out_shape=jax.ShapeDtypeStruct((m, PROJ_W), F32),
        compiler_params=_params(2),
        name="in_proj",
    )(h, w_in_t)


def _dt_proj(h, w_in_t, *, tm=1024):
    m = h.shape[0]
    tm = min(tm, m)
    return pl.pallas_call(
        functools.partial(_proj_body, valid_rows=SSD_HEADS),
        grid=(m // tm,),
        in_specs=[pl.BlockSpec((tm, D_MODEL), lambda i: (i, 0)),
                  pl.BlockSpec((ROWS, D_MODEL), lambda i: (PROJ_W // ROWS, 0))],
        out_specs=pl.BlockSpec((tm, ROWS), lambda i: (i, 0)),
        out_shape=jax.ShapeDtypeStruct((m, ROWS), F32),
        compiler_params=_params(1),
        name="dt_proj",
    )(h, w_in_t)


def _mm2_body(a1_ref, a2_ref, b1_ref, b2_ref, res_ref, o_ref):
    acc = (_dot(a1_ref[...].astype(BF16), b1_ref[...].astype(BF16))
           + _dot(a2_ref[...].astype(BF16), b2_ref[...].astype(BF16)))
    o_ref[...] = res_ref[...] + acc


def _out_proj(attn, y, w_out, res, *, tm=1024, tn=512):
    m = attn.shape[0]
    n = w_out.shape[1]
    tm = min(tm, m)
    return pl.pallas_call(
        _mm2_body,
        grid=(m // tm, n // tn),
        in_specs=[pl.BlockSpec((tm, ATTN_WIDTH), lambda i, j: (i, 0)),
                  pl.BlockSpec((tm, SSD_WIDTH), lambda i, j: (i, 0)),
                  pl.BlockSpec((ATTN_WIDTH, tn), lambda i, j: (0, j)),
                  pl.BlockSpec((SSD_WIDTH, tn), lambda i, j: (1, j)),
                  pl.BlockSpec((tm, tn), lambda i, j: (i, j))],
        out_specs=pl.BlockSpec((tm, tn), lambda i, j: (i, j)),
        out_shape=jax.ShapeDtypeStruct((m, n), F32),
        compiler_params=_params(2),
        name="out_proj",
    )(attn, y, w_out, w_out, res)


def _bucket_table():
    i = np.arange(ROWS)[:, None]
    j = np.arange(2 * ROWS)[None, :]
    dist = i + ROWS - j
    valid = (dist >= 0) & (dist < WINDOW)
    n = np.maximum(dist, 0)
    max_exact = N_BUCKETS // 2

    def large(dtype):
        nf = np.maximum(n, 1).astype(dtype)
        v = np.log(nf / dtype(max_exact)) / dtype(math.log(MAX_DISTANCE / max_exact)) * dtype(N_BUCKETS - max_exact)
        return np.minimum(max_exact + v.astype(np.int32), N_BUCKETS - 1)

    assert (large(np.float32) == large(np.float64)).all()
    bucket = np.where(n < max_exact, n, large(np.float32))
    return np.where(valid, bucket, -1).astype(np.int32)


def _bias_body(rb_ref, bkt_ref, o_ref):
    h = pl.program_id(0)
    bkt = bkt_ref[...]
    acc = jnp.full(bkt.shape, NEG, F32)
    for b in range(N_BUCKETS):
        acc = jnp.where(bkt == b, rb_ref[b, h], acc)
    o_ref[0] = acc


def _bias_table(rel_bias):
    return pl.pallas_call(
        _bias_body,
        grid=(ATTN_HEADS,),
        in_specs=[pl.BlockSpec(memory_space=pltpu.SMEM),
                  pl.BlockSpec((ROWS, 2 * ROWS), lambda h: (0, 0))],
        out_specs=pl.BlockSpec((1, ROWS, 2 * ROWS), lambda h: (h, 0, 0)),
        out_shape=jax.ShapeDtypeStruct((ATTN_HEADS, ROWS, 2 * ROWS), F32),
        compiler_params=_params(1, 16),
        name="bias_table",
    )(rel_bias, jnp.asarray(_bucket_table()))


def _attend_group(qs, kp, kc, vp, vc, bp, bc, sink, neg_prev):
    sp = _dot_nt(qs, kp) * SCALE + bp + neg_prev
    sc = _dot_nt(qs, kc) * SCALE + bc
    m = jnp.maximum(jnp.maximum(jnp.max(sp, axis=-1, keepdims=True),
                                jnp.max(sc, axis=-1, keepdims=True)), sink)
    pp = jnp.exp(sp - m)
    pc = jnp.exp(sc - m)
    den = (jnp.sum(pp, axis=-1, keepdims=True) + jnp.sum(pc, axis=-1, keepdims=True)) + jnp.exp(sink - m)
    inv = 1.0 / den
    return _dot((pp * inv).astype(BF16), vp) + _dot((pc * inv).astype(BF16), vc)


def _attn_prompt_body(sink_ref, q_ref, kc_ref, kp_ref, vc_ref, vp_ref, qw_ref, kw_ref, bias_ref,
                      o_ref, pk_ref, pv_ref):
    i = pl.program_id(1)
    neg_prev = jnp.where(i == 0, NEG, 0.0).astype(F32)
    qw = qw_ref[...]
    kw = kw_ref[...]
    for kv in range(KV_HEADS):
        sl = slice(kv * HEAD_DIM, (kv + 1) * HEAD_DIM)
        kc = _rms_rows(kc_ref[:, sl], kw)
        kp = _rms_rows(kp_ref[:, sl], kw)
        pk_ref[0, :, sl] = kc
        heads = [kv * Q_PER_KV + g for g in range(Q_PER_KV)]
        qs = jnp.concatenate(
            [_rms_rows(q_ref[:, h * HEAD_DIM:(h + 1) * HEAD_DIM], qw) for h in heads], axis=0).astype(BF16)
        sink = jnp.concatenate([jnp.full((ROWS, 1), sink_ref[h], F32) for h in heads], axis=0)
        o = _attend_group(qs, kp.astype(BF16), kc.astype(BF16),
                          vp_ref[:, sl].astype(BF16), vc_ref[:, sl].astype(BF16),
                          bias_ref[kv, :, 0:ROWS], bias_ref[kv, :, ROWS:2 * ROWS], sink, neg_prev)
        for g, h in enumerate(heads):
            o_ref[:, h * HEAD_DIM:(h + 1) * HEAD_DIM] = o[g * ROWS:(g + 1) * ROWS].astype(o_ref.dtype)
    pv_ref[0] = vc_ref[...]


def _attn_prompt(proj, sinks, qw, kw, bias4, nseq, seqlen):
    nb = seqlen // ROWS
    kblk, vblk = COL_K // KV_WIDTH, COL_V // KV_WIDTH
    cur = lambda b, i: b * nb + i
    prev = lambda b, i: b * nb + jnp.maximum(i - 1, 0)
    return pl.pallas_call(
        _attn_prompt_body,
        grid=(nseq, nb),
        in_specs=[pl.BlockSpec(memory_space=pltpu.SMEM),
                  pl.BlockSpec((ROWS, ATTN_WIDTH), lambda b, i: (cur(b, i), 0)),
                  pl.BlockSpec((ROWS, KV_WIDTH), lambda b, i: (cur(b, i), kblk)),
                  pl.BlockSpec((ROWS, KV_WIDTH), lambda b, i: (prev(b, i), kblk)),
                  pl.BlockSpec((ROWS, KV_WIDTH), lambda b, i: (cur(b, i), vblk)),
                  pl.BlockSpec((ROWS, KV_WIDTH), lambda b, i: (prev(b, i), vblk)),
                  pl.BlockSpec((1, HEAD_DIM), lambda b, i: (0, 0)),
                  pl.BlockSpec((1, HEAD_DIM), lambda b, i: (0, 0)),
                  pl.BlockSpec((KV_HEADS, Q_PER_KV * ROWS, 2 * ROWS), lambda b, i: (0, 0, 0))],
        out_specs=[pl.BlockSpec((ROWS, ATTN_WIDTH), lambda b, i: (cur(b, i), 0)),
                   pl.BlockSpec((1, ROWS, KV_WIDTH), lambda b, i: (b, 0, 0)),
                   pl.BlockSpec((1, ROWS, KV_WIDTH), lambda b, i: (b, 0, 0))],
        out_shape=[jax.ShapeDtypeStruct((nseq * seqlen, ATTN_WIDTH), BF16),
                   jax.ShapeDtypeStruct((nseq, ROWS, KV_WIDTH), F32),
                   jax.ShapeDtypeStruct((nseq, ROWS, KV_WIDTH), F32)],
        compiler_params=_params(2, 32),
        name="attn_prompt",
    )(sinks, proj, proj, proj, proj, proj, qw, kw, bias4)


def _attn_sample_body(q_ref, kn_ref, vn_ref, wk_ref, wv_ref, qw_ref, kw_ref, bias_ref, sink_ref,
                      o_ref, sk_ref, sv_ref, qn_scr, kn_scr, s_scr, p_scr, *, nsq, ntok):
    qw = qw_ref[...]
    kw = kw_ref[...]
    grows = Q_PER_KV * ntok
    keep = ROWS - ntok
    for h in range(ATTN_HEADS):
        hs = slice(h * HEAD_DIM, (h + 1) * HEAD_DIM)
        qn_scr[:, hs] = _rms_rows(q_ref[:, hs], qw)
    for kv in range(KV_HEADS):
        sl = slice(kv * HEAD_DIM, (kv + 1) * HEAD_DIM)
        kn_scr[:, sl] = _rms_rows(kn_ref[:, sl], kw)
    for s in range(nsq):
        rows = slice(s * ntok, (s + 1) * ntok)
        sk_ref[s, 0:keep] = wk_ref[s, ntok:ROWS]
        sv_ref[s, 0:keep] = wv_ref[s, ntok:ROWS]
        for kv in range(KV_HEADS):
            sl = slice(kv * HEAD_DIM, (kv + 1) * HEAD_DIM)
            sk_ref[s, keep:ROWS, kv, :] = kn_scr[rows, sl]
            sv_ref[s, keep:ROWS, kv, :] = vn_ref[rows, sl]

    pad = jnp.zeros((keep, HEAD_DIM), F32)
    for s in range(nsq):
        rows = slice(s * ntok, (s + 1) * ntok)
        for kv in range(KV_HEADS):
            sl = slice(kv * HEAD_DIM, (kv + 1) * HEAD_DIM)
            grp = s * KV_HEADS + kv
            qs = jnp.concatenate([qn_scr[rows, (kv * Q_PER_KV + g) * HEAD_DIM:(kv * Q_PER_KV + g + 1) * HEAD_DIM]
                                  for g in range(Q_PER_KV)], axis=0).astype(BF16)
            kcat = jnp.concatenate([wk_ref[s, :, kv, :], kn_scr[rows, sl], pad], axis=0).astype(BF16)
            s_scr[grp * grows:(grp + 1) * grows, :] = _dot_nt(qs, kcat)

    sc = s_scr[...] * SCALE + bias_ref[...]
    sink = sink_ref[...]
    m = jnp.maximum(jnp.max(sc, axis=-1, keepdims=True), sink)
    p = jnp.exp(sc - m)
    den = jnp.sum(p, axis=-1, keepdims=True) + jnp.exp(sink - m)
    p_scr[...] = (p * (1.0 / den)).astype(BF16)

    for s in range(nsq):
        rows = slice(s * ntok, (s + 1) * ntok)
        for kv in range(KV_HEADS):
            sl = slice(kv * HEAD_DIM, (kv + 1) * HEAD_DIM)
            grp = s * KV_HEADS + kv
            vcat = jnp.concatenate([wv_ref[s, :, kv, :], vn_ref[rows, sl], pad], axis=0).astype(BF16)
            o = _dot(p_scr[grp * grows:(grp + 1) * grows, :], vcat)
            for g in range(Q_PER_KV):
                h = kv * Q_PER_KV + g
                o_ref[rows, h * HEAD_DIM:(h + 1) * HEAD_DIM] = o[g * ntok:(g + 1) * ntok]


def _attn_sample(proj, win_k, win_v, sinks, qw, kw, bias16, nseq, ntok, nsq=8):
    kblk, vblk = COL_K // KV_WIDTH, COL_V // KV_WIDTH
    rows = nsq * ntok
    srows = nsq * ATTN_HEADS * ntok
    bias = jnp.tile(bias16[:, :ntok, :].reshape(ATTN_HEADS * ntok, 2 * ROWS), (nsq, 1))
    sink = jnp.tile(jnp.repeat(sinks, ntok), nsq).reshape(srows, 1)
    return pl.pallas_call(
        functools.partial(_attn_sample_body, nsq=nsq, ntok=ntok),
        grid=(nseq // nsq,),
        in_specs=[pl.BlockSpec((rows, ATTN_WIDTH), lambda i: (i, 0)),
                  pl.BlockSpec((rows, KV_WIDTH), lambda i: (i, kblk)),
                  pl.BlockSpec((rows, KV_WIDTH), lambda i: (i, vblk)),
                  pl.BlockSpec((nsq, ROWS, KV_HEADS, HEAD_DIM), lambda i: (i, 0, 0, 0)),
                  pl.BlockSpec((nsq, ROWS, KV_HEADS, HEAD_DIM), lambda i: (i, 0, 0, 0)),
                  pl.BlockSpec((1, HEAD_DIM), lambda i: (0, 0)),
                  pl.BlockSpec((1, HEAD_DIM), lambda i: (0, 0)),
                  pl.BlockSpec((srows, 2 * ROWS), lambda i: (0, 0)),
                  pl.BlockSpec((srows, 1), lambda i: (0, 0))],
        out_specs=[pl.BlockSpec((rows, ATTN_WIDTH), lambda i: (i, 0)),
                   pl.BlockSpec((nsq, ROWS, KV_HEADS, HEAD_DIM), lambda i: (i, 0, 0, 0)),
                   pl.BlockSpec((nsq, ROWS, KV_HEADS, HEAD_DIM), lambda i: (i, 0, 0, 0))],
        out_shape=[jax.ShapeDtypeStruct((nseq * ntok, ATTN_WIDTH), F32),
                   jax.ShapeDtypeStruct((nseq, ROWS, KV_HEADS, HEAD_DIM), F32),
                   jax.ShapeDtypeStruct((nseq, ROWS, KV_HEADS, HEAD_DIM), F32)],
        scratch_shapes=[pltpu.VMEM((rows, ATTN_WIDTH), F32),
                        pltpu.VMEM((rows, KV_WIDTH), F32),
                        pltpu.VMEM((srows, 2 * ROWS), F32),
                        pltpu.VMEM((srows, 2 * ROWS), BF16)],
        compiler_params=_params(1, 32),
        name="attn_sample",
    )(proj, proj, proj, win_k, win_v, qw, kw, bias, sink)


def _ssd_constants(qs):
    r = np.arange(ROWS)
    same = (r[:, None] // qs) == (r[None, :] // qs)
    mask = same & (r[None, :] <= r[:, None])
    sel = r[None, :] == ((r[:, None] // qs) * qs + qs - 1)
    expand = np.zeros((ROWS, SSD_WIDTH), np.float32)
    for h in range(SSD_HEADS):
        expand[h, h * SSD_HEAD_DIM:(h + 1) * SSD_HEAD_DIM] = 1.0
    to_bf = lambda a: jnp.asarray(a.astype(np.float32), BF16)
    return to_bf(mask), to_bf(sel), to_bf(expand), to_bf(expand.T)


def _ssd_body(xbc_ref, z_ref, dt_ref, cprev_ref, h0_ref, cw_ref, cb_ref, dtb_ref, alog_ref, dexp_ref,
              nw_ref, mask_ref, sel_ref, e_ref, et_ref,
              y_ref, ht_ref,
              h_scr, cbuf, xc_scr, xdt_scr, xdd_scr, eacs_scr, yacc_scr, acs_scr, acst_scr, aclt_scr,
              *, nsq, qs):
    c = pl.program_id(1)
    s = pl.program_id(2)

    @pl.when(c == 0)
    def _():
        h_scr[...] = h0_ref[0]

    if nsq == 1:
        @pl.when(c == 0)
        def _():
            cbuf[5:8, :] = cprev_ref[0]

    @pl.when(s == 0)
    def _chunk_phase():
        ncc = 8 if qs == ROWS else 1
        cwid = CONV_DIM // ncc
        for sq in range(nsq):
            for cc in range(ncc):
                cs = slice(cc * cwid, (cc + 1) * cwid)
                if nsq > 1:
                    cbuf[5:8, cs] = cprev_ref[sq, :, cs]
                cbuf[8:8 + qs, cs] = xbc_ref[sq * qs:(sq + 1) * qs, cs]
                acc = cb_ref[:, cs] + cw_ref[0:1, cs] * cbuf[5:5 + qs, cs]
                for t in range(1, SSD_CONV):
                    acc = acc + cw_ref[t:t + 1, cs] * cbuf[5 + t:5 + t + qs, cs]
                xc_scr[sq * qs:(sq + 1) * qs, cs] = acc * _sigmoid(acc)
                if nsq == 1:
                    cbuf[5:8, cs] = xbc_ref[qs - 3:qs, cs]

        x = dt_ref[...] + dtb_ref[...]
        dt = jnp.maximum(x, 0.0) + jnp.log1p(jnp.exp(-jnp.abs(x)))
        da = dt * (-jnp.exp(alog_ref[...]))
        mask_bf = mask_ref[...]
        acs = _dot3_r(mask_bf, _split3(da))
        acl = _dot3_r(sel_ref[...], _split3(acs))
        acs_scr[...] = acs
        acst_scr[...] = acs.T
        aclt_scr[...] = acl.T
        e = e_ref[...]
        dt_e = _dot3_l(dt, e)
        acs_e = _dot3_l(acs, e)
        acl_e = _dot3_l(acl, e)
        xs = xc_scr[:, 0:SSD_WIDTH]
        xdt = xs * dt_e
        xdt_scr[...] = xdt
        xdd_scr[...] = xdt * jnp.exp(acl_e - acs_e)
        eacs_scr[...] = jnp.exp(acs_e)

        maskb = mask_bf > 0.5
        lane = lax.broadcasted_iota(jnp.int32, (ROWS, 2 * SSD_HEAD_DIM), 1)
        for g in range(SSD_GROUPS):
            bg = xc_scr[:, SSD_WIDTH + g * SSD_STATE:SSD_WIDTH + (g + 1) * SSD_STATE].astype(BF16)
            cg = xc_scr[:, SSD_WIDTH + (SSD_GROUPS + g) * SSD_STATE:
                        SSD_WIDTH + (SSD_GROUPS + g + 1) * SSD_STATE].astype(BF16)
            cb = _dot_nt(cg, bg)
            for pp in range(2):
                ms = []
                for j in range(2):
                    h = 4 * g + 2 * pp + j
                    seg = acs_scr[:, h:h + 1] - acst_scr[h:h + 1, :]
                    ms.append((cb * jnp.exp(jnp.where(maskb, seg, NEG))).astype(BF16))
                c0 = (4 * g + 2 * pp) * SSD_HEAD_DIM
                cols = slice(c0, c0 + 2 * SSD_HEAD_DIM)
                xp = xdt_scr[:, cols]
                rhs = jnp.concatenate([jnp.where(lane < SSD_HEAD_DIM, xp, 0.0),
                                       jnp.where(lane >= SSD_HEAD_DIM, xp, 0.0)], axis=0).astype(BF16)
                yd = _dot(jnp.concatenate(ms, axis=1), rhs)
                yacc_scr[:, cols] = yd + dexp_ref[:, cols] * xc_scr[:, cols]

    if nsq == 1:
        r0 = 0
        rows = slice(0, qs)
    else:
        r0 = pl.multiple_of(s * qs, qs)
        rows = pl.ds(r0, qs)
        ridx = lax.broadcasted_iota(jnp.int32, (ROWS, 1), 0)
        rowmask = jnp.logical_and(ridx >= r0, ridx < r0 + qs)
    tlane = lax.broadcasted_iota(jnp.int32, (ROWS, ROWS), 1)
    xcol = jnp.sum(jnp.where(tlane == r0, aclt_scr[...], 0.0), axis=1, keepdims=True)
    x3 = _split3(jnp.broadcast_to(xcol, (ROWS, ROWS)))
    c_rows = xc_scr[rows, SSD_WIDTH + SSD_GROUPS * SSD_STATE:CONV_DIM].astype(BF16)
    yo, st = [], []
    for g in range(SSD_GROUPS):
        grows = slice(g * GROUP_ROWS, (g + 1) * GROUP_ROWS)
        yo.append(_dot_nt(c_rows[:, g * SSD_STATE:(g + 1) * SSD_STATE], h_scr[grows, :].astype(BF16)))
        bg = xc_scr[:, SSD_WIDTH + g * SSD_STATE:SSD_WIDTH + (g + 1) * SSD_STATE]
        if nsq > 1:
            bg = jnp.where(rowmask, bg, 0.0)
        st.append(_dot_tn(xdd_scr[:, grows].astype(BF16), bg.astype(BF16)))
    yacc_scr[rows, :] = yacc_scr[rows, :] + jnp.concatenate(yo, axis=1) * eacs_scr[rows, :]
    dec = jnp.exp(_dot3_r(et_ref[...], x3))
    h_new = dec * h_scr[...] + jnp.concatenate(st, axis=0)
    h_scr[...] = h_new
    ht_ref[0] = h_new

    @pl.when(s == nsq - 1)
    def _finish():
        z = z_ref[...]
        yg = yacc_scr[...] * (z * _sigmoid(z))
        y_ref[...] = _rms_rows(yg, nw_ref[...]).astype(y_ref.dtype)


def _ssd(proj, dt_raw, conv_prev, h0, cw, cb, dtb, alog, dexp, nw, nseq, seqlen):
    if seqlen % ROWS == 0:
        qs, nsq, nchunk, ngrp = ROWS, 1, seqlen // ROWS, nseq
    else:
        qs, nsq, nchunk = seqlen, ROWS // seqlen, 1
        ngrp = nseq // nsq
    mask, sel, expand, expand_t = _ssd_constants(qs)
    rowblk = lambda b, c, s: b * nchunk + c
    seq = lambda b, c, s: b * nsq + s
    const2 = lambda b, c, s: (0, 0)
    full = lambda shape: pl.BlockSpec(shape, const2)
    return pl.pallas_call(
        functools.partial(_ssd_body, nsq=nsq, qs=qs),
        grid=(ngrp, nchunk, nsq),
        in_specs=[pl.BlockSpec((ROWS, CONV_DIM), lambda b, c, s: (rowblk(b, c, s), COL_XBC // CONV_DIM)),
                  pl.BlockSpec((ROWS, SSD_WIDTH), lambda b, c, s: (rowblk(b, c, s), COL_Z // SSD_WIDTH)),
                  pl.BlockSpec((ROWS, ROWS), lambda b, c, s: (rowblk(b, c, s), 0)),
                  pl.BlockSpec((nsq, SSD_CONV - 1, CONV_DIM), lambda b, c, s: (b, 0, 0)),
                  pl.BlockSpec((1, SSD_WIDTH, SSD_STATE), lambda b, c, s: (seq(b, c, s), 0, 0)),
                  full((SSD_CONV, CONV_DIM)), full((1, CONV_DIM)), full((1, ROWS)), full((1, ROWS)),
                  full((1, SSD_WIDTH)), full((1, SSD_WIDTH)),
                  full((ROWS, ROWS)), full((ROWS, ROWS)), full((ROWS, SSD_WIDTH)), full((SSD_WIDTH, ROWS))],
        out_specs=[pl.BlockSpec((ROWS, SSD_WIDTH), lambda b, c, s: (rowblk(b, c, s), 0)),
                   pl.BlockSpec((1, SSD_WIDTH, SSD_STATE), lambda b, c, s: (seq(b, c, s), 0, 0))],
        out_shape=[jax.ShapeDtypeStruct((nseq * seqlen, SSD_WIDTH), BF16),
                   jax.ShapeDtypeStruct((nseq, SSD_WIDTH, SSD_STATE), F32)],
        scratch_shapes=[pltpu.VMEM((SSD_WIDTH, SSD_STATE), F32),
                        pltpu.VMEM((8 + ROWS, CONV_DIM), F32),
                        pltpu.VMEM((ROWS, CONV_DIM), F32),
                        pltpu.VMEM((ROWS, SSD_WIDTH), F32),
                        pltpu.VMEM((ROWS, SSD_WIDTH), F32),
                        pltpu.VMEM((ROWS, SSD_WIDTH), F32),
                        pltpu.VMEM((ROWS, SSD_WIDTH), F32),
                        pltpu.VMEM((ROWS, ROWS), F32),
                        pltpu.VMEM((ROWS, ROWS), F32),
                        pltpu.VMEM((ROWS, ROWS), F32)],
        compiler_params=_params(3, 40),
        name="ssd",
    )(proj, proj, dt_raw, conv_prev, h0, cw, cb, dtb, alog, dexp, nw, mask, sel, expand, expand_t)


def _ffn_gate_body(h_ref, wg_ref, wu_ref, cw_ref, cb_ref, *rest, period, tiles_per_seq):
    if period is None:
        a_ref, last_ref, carry_scr = rest
    else:
        prev_ref, a_ref, g_ref = rest
    hb = h_ref[...]
    tn = wg_ref.shape[1]
    col = pl.program_id(1) * tn + lax.broadcasted_iota(jnp.int32, (1, tn), 1)
    g = _dot(hb, jnp.where(col < D_FF, wg_ref[...], 0))
    u = _dot(hb, jnp.where(col < D_FF, wu_ref[...], 0))
    tm = g.shape[0]
    row = lax.broadcasted_iota(jnp.int32, g.shape, 0)
    if period is None:
        i, j = pl.program_id(0), pl.program_id(1)
        c8 = jnp.where(i % tiles_per_seq == 0, 0.0, carry_scr[j])
        carry_scr[j] = g[tm - 8:tm]
        last_ref[0] = g[tm - 2:tm]
        g_m1 = jnp.where(row == 0, c8[7:8], pltpu.roll(g, 1, 0))
        g_m2 = jnp.where(row == 0, c8[6:7], jnp.where(row == 1, c8[7:8], pltpu.roll(g, 2, 0)))
    else:
        g_ref[...] = g
        prev = prev_ref[...]
        t = row & (period - 1)
        g_m1 = jnp.where(t == 0, pltpu.roll(prev, tm - 1, 0), pltpu.roll(g, 1, 0))
        g_m2 = jnp.where(t < 2, prev, pltpu.roll(g, 2, 0))
    gc = cb_ref[...] + cw_ref[0:1] * g_m2
    gc = gc + cw_ref[1:2] * g_m1
    gc = gc + cw_ref[2:3] * g
    a_ref[...] = ((gc * _sigmoid(gc)) * u).astype(a_ref.dtype)


def _ffn_gate(h, w_gate, w_up, cw, cb, *, nseq, seqlen, prev_rows=None, tm=1024, tn=512):
    m = h.shape[0]
    nj = D_FF_PAD // tn
    wspec = pl.BlockSpec((D_MODEL, tn), lambda i, j: (0, j))
    cwspec = pl.BlockSpec((FFN_CONV, tn), lambda i, j: (0, j))
    cbspec = pl.BlockSpec((1, tn), lambda i, j: (0, j))
    if prev_rows is None:
        tm = min(tm, seqlen)
        tps = seqlen // tm
        in_specs = [pl.BlockSpec((tm, D_MODEL), lambda i, j: (i, 0)), wspec, wspec, cwspec, cbspec]
        out_specs = [pl.BlockSpec((tm, tn), lambda i, j: (i, j)),
                     pl.BlockSpec((1, FFN_CONV - 1, tn), lambda i, j: (i, 0, j))]
        out_shape = [jax.ShapeDtypeStruct((m, D_FF_PAD), BF16),
                     jax.ShapeDtypeStruct((m // tm, FFN_CONV - 1, D_FF_PAD), F32)]
        scratch = [pltpu.VMEM((nj, 8, tn), F32)]
        args = (h, w_gate, w_up, cw, cb)
        body = functools.partial(_ffn_gate_body, period=None, tiles_per_seq=tps)
    else:
        assert seqlen & (seqlen - 1) == 0
        tm = m
        in_specs = [pl.BlockSpec((tm, D_MODEL), lambda i, j: (i, 0)), wspec, wspec, cwspec, cbspec,
                    pl.BlockSpec((tm, tn), lambda i, j: (i, j))]
        out_specs = [pl.BlockSpec((tm, tn), lambda i, j: (i, j)),
                     pl.BlockSpec((tm, tn), lambda i, j: (i, j))]
        out_shape = [jax.ShapeDtypeStruct((m, D_FF_PAD), BF16),
                     jax.ShapeDtypeStruct((m, D_FF_PAD), F32)]
        scratch = []
        args = (h, w_gate, w_up, cw, cb, prev_rows)
        body = functools.partial(_ffn_gate_body, period=seqlen, tiles_per_seq=None)
    return pl.pallas_call(
        body,
        grid=(m // tm, nj),
        in_specs=in_specs,
        out_specs=out_specs,
        out_shape=out_shape,
        scratch_shapes=scratch,
        compiler_params=_params(2),
        name="ffn_gate",
    )(*args)


def _down_body(a_ref, w_ref, res_ref, o_ref, *, chunk):
    @pl.when(pl.program_id(2) == 0)
    def _():
        o_ref[...] = res_ref[...]

    a = a_ref[...]
    tk = w_ref.shape[0]
    row = pl.program_id(2) * tk + lax.broadcasted_iota(jnp.int32, (tk, 1), 0)
    for c in range(o_ref.shape[1] // chunk):
        cs = slice(c * chunk, (c + 1) * chunk)
        o_ref[:, cs] += _dot(a, jnp.where(row < D_FF, w_ref[:, cs], 0))


def _down_proj(a, w_down, res, *, tm=1024, tn=2048, tk=1024):
    m, kdim = a.shape
    n = w_down.shape[1]
    tm = min(tm, m)
    return pl.pallas_call(
        functools.partial(_down_body, chunk=512),
        grid=(n // tn, m // tm, kdim // tk),
        in_specs=[pl.BlockSpec((tm, tk), lambda j, i, k: (i, k)),
                  pl.BlockSpec((tk, tn), lambda j, i, k: (k, j)),
                  pl.BlockSpec((tm, tn), lambda j, i, k: (i, j))],
        out_specs=pl.BlockSpec((tm, tn), lambda j, i, k: (i, j)),
        out_shape=jax.ShapeDtypeStruct((m, n), F32),
        compiler_params=_params(3),
        name="down_proj",
    )(a, w_down, res)


def _layer(x, nseq, seqlen, wts, bias16, attn_state, ssm_state, conv_state, ffn_state):
    m = x.shape[0]
    h = _rmsnorm(x, wts["mix_norm_w"])
    proj = _in_proj(h, wts["w_in_t"])
    dt_raw = _dt_proj(h, wts["w_in_t"])

    qw = wts["q_norm_w"].reshape(1, HEAD_DIM)
    kw = wts["k_norm_w"].reshape(1, HEAD_DIM)
    if attn_state is None:
        bias4 = bias16.reshape(KV_HEADS, Q_PER_KV * ROWS, 2 * ROWS)
        attn, new_k, new_v = _attn_prompt(proj, wts["attn_sinks"], qw, kw, bias4, nseq, seqlen)
        new_k = new_k.reshape(nseq, WINDOW, KV_HEADS, HEAD_DIM)
        new_v = new_v.reshape(nseq, WINDOW, KV_HEADS, HEAD_DIM)
        conv_prev = jnp.zeros((nseq, SSD_CONV - 1, CONV_DIM), F32)
        h0 = jnp.zeros((nseq, SSD_WIDTH, SSD_STATE), F32)
    else:
        win_k, win_v = attn_state
        attn, new_k, new_v = _attn_sample(proj, win_k, win_v, wts["attn_sinks"], qw, kw, bias16, nseq, seqlen)
        conv_prev = conv_state
        h0 = ssm_state.reshape(nseq, SSD_WIDTH, SSD_STATE)

    y, h_t = _ssd(proj, dt_raw, conv_prev, h0, wts["ssd_conv_w"], wts["ssd_conv_b"], wts["ssd_dt_bias"],
                  wts["ssd_A_log"], wts["ssd_D"], wts["ssd_norm_w"], nseq, seqlen)
    h_t = h_t.reshape(nseq, SSD_HEADS, SSD_HEAD_DIM, SSD_STATE)
    new_conv = proj.reshape(nseq, seqlen, PROJ_W)[:, seqlen - (SSD_CONV - 1):, COL_XBC:COL_XBC + CONV_DIM]

    x1 = _out_proj(attn, y, wts["w_out"], x)
    h2 = _rmsnorm(x1, wts["ffn_norm_w"])
    if ffn_state is None:
        a, last = _ffn_gate(h2, wts["w_gate"], wts["w_up"], wts["ffn_conv_w"], wts["ffn_conv_b"],
                            nseq=nseq, seqlen=seqlen)
        new_ffn = last.reshape(nseq, -1, FFN_CONV - 1, D_FF_PAD)[:, -1, :, :D_FF]
    else:
        prev_rows = jnp.pad(ffn_state, ((0, 0), (0, seqlen - (FFN_CONV - 1)), (0, D_FF_PAD - D_FF)))
        a, g = _ffn_gate(h2, wts["w_gate"], wts["w_up"], wts["ffn_conv_w"], wts["ffn_conv_b"],
                         nseq=nseq, seqlen=seqlen, prev_rows=prev_rows.reshape(m, D_FF_PAD))
        new_ffn = g.reshape(nseq, seqlen, D_FF_PAD)[:, seqlen - (FFN_CONV - 1):, :D_FF]
    x2 = _down_proj(a, wts["w_down"], x1)
    return x2, new_k, new_v, h_t, new_conv, new_ffn


def kernel(x_prompt, x_sample, state_attn_k, state_attn_v, state_ssm, state_ssd_conv, state_ffn_conv, rel_bias, mix_norm_w, w_in, q_norm_w, k_norm_w, attn_sinks, ssd_conv_w, ssd_conv_b, ssd_dt_bias, ssd_A_log, ssd_D, ssd_norm_w, w_out, ffn_norm_w, w_gate, w_up, ffn_conv_w, ffn_conv_b, w_down):
    depth = w_in.shape[0]
    bp, lp, _ = x_prompt.shape
    bs, ls, _ = x_sample.shape
    bias16 = _bias_table(rel_bias)
    yp = x_prompt.reshape(bp * lp, D_MODEL)
    ys = x_sample.reshape(bs * ls, D_MODEL)
    outs_p, outs_s = [], []
    pad_heads = lambda v: jnp.pad(v.reshape(1, SSD_HEADS), ((0, 0), (0, ROWS - SSD_HEADS)))
    pad_ff = lambda v: jnp.pad(v, ((0, 0), (0, D_FF_PAD - D_FF)))
    for l in range(depth):
        wts = dict(
            mix_norm_w=mix_norm_w[l],
            w_in_t=w_in[l].T,

            q_norm_w=q_norm_w[l], k_norm_w=k_norm_w[l], attn_sinks=attn_sinks[l],
            ssd_conv_w=ssd_conv_w[l], ssd_conv_b=ssd_conv_b[l].reshape(1, CONV_DIM),
            ssd_dt_bias=pad_heads(ssd_dt_bias[l]), ssd_A_log=pad_heads(ssd_A_log[l]),
            ssd_D=jnp.repeat(ssd_D[l], SSD_HEAD_DIM).reshape(1, SSD_WIDTH),
            ssd_norm_w=ssd_norm_w[l].reshape(1, SSD_WIDTH),
            w_out=w_out[l], ffn_norm_w=ffn_norm_w[l],
            w_gate=w_gate[l].astype(BF16), w_up=w_up[l].astype(BF16),
            ffn_conv_w=pad_ff(ffn_conv_w[l]), ffn_conv_b=pad_ff(ffn_conv_b[l].reshape(1, D_FF)),
            w_down=w_down[l].astype(BF16),
        )
        yp, *sp = _layer(yp, bp, lp, wts, bias16, None, None, None, None)
        ys, *ss = _layer(ys, bs, ls, wts, bias16, (state_attn_k[l], state_attn_v[l]), state_ssm[l],
                         state_ssd_conv[l], state_ffn_conv[l])
        outs_p.append(sp)
        outs_s.append(ss)
    stack = lambda outs, i: jnp.stack([o[i] for o in outs])
    return (yp.reshape(bp, lp, D_MODEL), ys.reshape(bs, ls, D_MODEL),
            stack(outs_p, 0), stack(outs_p, 1), stack(outs_p, 2), stack(outs_p, 3), stack(outs_p, 4),
            stack(outs_s, 0), stack(outs_s, 1), stack(outs_s, 2), stack(outs_s, 3), stack(outs_s, 4))
```

```python
import functools
import math

import numpy as np
import jax
import jax.numpy as jnp
from jax import lax
from jax.experimental import pallas as pl
from jax.experimental.pallas import tpu as pltpu

F32 = jnp.float32
BF16 = jnp.bfloat16

D_MODEL = 4096
HEAD_DIM = 128
ATTN_HEADS = 16
KV_HEADS = 4
Q_PER_KV = 4
ATTN_WIDTH = 2048
WINDOW = 128
N_BUCKETS = 32
MAX_DISTANCE = 128
SSD_HEAD_DIM = 64
SSD_HEADS = 32
SSD_WIDTH = 2048
SSD_GROUPS = 8
SSD_STATE = 128
SSD_CONV = 4
CONV_DIM = 4096
KV_WIDTH = KV_HEADS * HEAD_DIM
D_FF = 11008
D_FF_PAD = 11264
FFN_CONV = 3
EPS = 1e-6
NEG = -1e30
SCALE = HEAD_DIM ** -0.5

ROWS = 128
GROUP_ROWS = 4 * SSD_HEAD_DIM
COL_Z = ATTN_WIDTH
COL_XBC = COL_Z + SSD_WIDTH
COL_K = COL_XBC + CONV_DIM
COL_V = COL_K + KV_WIDTH
PROJ_W = COL_V + KV_WIDTH
VMEM_LIMIT_MB = 56


def _params(n_axes, vmem_mb=VMEM_LIMIT_MB):
    return pltpu.CompilerParams(dimension_semantics=("arbitrary",) * n_axes,
                                vmem_limit_bytes=vmem_mb << 20)


def _dot(a, b):
    return jnp.dot(a, b, preferred_element_type=F32)


def _dot_nt(a, b):
    return lax.dot_general(a, b, (((1,), (1,)), ((), ())), preferred_element_type=F32)


def _dot_tn(a, b):
    return lax.dot_general(a, b, (((0,), (0,)), ((), ())), preferred_element_type=F32)


def _split3(x):
    a = x.astype(BF16)
    r = x - a.astype(F32)
    b = r.astype(BF16)
    c = (r - b.astype(F32)).astype(BF16)
    return a, b, c


def _dot3_l(x, w):
    a, b, c = _split3(x)
    return (_dot(a, w) + _dot(b, w)) + _dot(c, w)


def _dot3_r(w, x3):
    return (_dot(w, x3[0]) + _dot(w, x3[1])) + _dot(w, x3[2])


def _sigmoid(x):
    return 1.0 / (1.0 + jnp.exp(-x))


def _rms_rows(x, w):
    r = lax.rsqrt(jnp.mean(x * x, axis=-1, keepdims=True) + EPS)
    return (x * r) * w


def _rmsnorm_body(x_ref, w_ref, o_ref):
    o_ref[...] = _rms_rows(x_ref[...], w_ref[...]).astype(o_ref.dtype)


def _rmsnorm(x, w, rows=256):
    m, d = x.shape
    rows = min(rows, m)
    return pl.pallas_call(
        _rmsnorm_body,
        grid=(m // rows,),
        in_specs=[pl.BlockSpec((rows, d), lambda i: (i, 0)),
                  pl.BlockSpec((1, d), lambda i: (0, 0))],
        out_specs=pl.BlockSpec((rows, d), lambda i: (i, 0)),
        out_shape=jax.ShapeDtypeStruct((m, d), BF16),
        compiler_params=_params(1, 32),
        name="rmsnorm",
    )(x, w.reshape(1, d))


def _in_proj_body(a_ref, bt_ref, dtw_ref, o_ref, dt_ref):
    a = a_ref[...]
    o_ref[...] = _dot_nt(a, bt_ref[...].astype(BF16))

    @pl.when(pl.program_id(1) == 0)
    def _():
        row = lax.broadcasted_iota(jnp.int32, (dtw_ref.shape[0], 1), 0)
        dtw = jnp.where(row < SSD_HEADS, dtw_ref[...], 0.0)
        dt_ref[...] = _dot_nt(a, dtw.astype(BF16))


def _in_proj(h, w_in_t, *, tm=1024, tn=512):
    m = h.shape[0]
    tm = min(tm, m)
    nq, nkv = ATTN_WIDTH // tn, KV_WIDTH // tn
    nzx = (SSD_WIDTH + CONV_DIM) // tn

    def src_block(j):
        return jnp.where(j < nq, j, jnp.where(j < nq + nzx, j + 2 * nkv, j - nzx))

    return pl.pallas_call(
        _in_proj_body,
        grid=(m // tm, PROJ_W // tn),
        in_specs=[pl.BlockSpec((tm, D_MODEL), lambda i, j: (i, 0)),
                  pl.BlockSpec((tn, D_MODEL), lambda i, j: (src_block(j), 0)),
                  pl.BlockSpec((ROWS, D_MODEL), lambda i, j: (PROJ_W // ROWS, 0))],
        out_specs=[pl.BlockSpec((tm, tn), lambda i, j: (i, j)),
                   pl.BlockSpec((tm, ROWS), lambda i, j: (i, 0))],
        out_shape=[jax.ShapeDtypeStruct((m, PROJ_W), F32),
                   jax.ShapeDtypeStruct((m, ROWS), F32)],
        compiler_params=_params(2),
        name="in_proj",
    )(h, w_in_t, w_in_t)


def _mm2_body(a1_ref, a2_ref, b1_ref, b2_ref, res_ref, o_ref):
    acc = (_dot(a1_ref[...].astype(BF16), b1_ref[...].astype(BF16))
           + _dot(a2_ref[...].astype(BF16), b2_ref[...].astype(BF16)))
    o_ref[...] = res_ref[...] + acc


def _out_proj(attn, y, w_out, res, *, tm=1024, tn=512):
    m = attn.shape[0]
    n = w_out.shape[1]
    tm = min(tm, m)
    return pl.pallas_call(
        _mm2_body,
        grid=(m // tm, n // tn),
        in_specs=[pl.BlockSpec((tm, ATTN_WIDTH), lambda i, j: (i, 0)),
                  pl.BlockSpec((tm, SSD_WIDTH), lambda i, j: (i, 0)),
                  pl.BlockSpec((ATTN_WIDTH, tn), lambda i, j: (0, j)),
                  pl.BlockSpec((SSD_WIDTH, tn), lambda i, j: (1, j)),
                  pl.BlockSpec((tm, tn), lambda i, j: (i, j))],
        out_specs=pl.BlockSpec((tm, tn), lambda i, j: (i, j)),
        out_shape=jax.ShapeDtypeStruct((m, n), F32),
        compiler_params=_params(2),
        name="out_proj",
    )(attn, y, w_out, w_out, res)


def _bucket_table():
    i = np.arange(ROWS)[:, None]
    j = np.arange(2 * ROWS)[None, :]
    dist = i + ROWS - j
    valid = (dist >= 0) & (dist < WINDOW)
    n = np.maximum(dist, 0)
    max_exact = N_BUCKETS // 2

    def large(dtype):
        nf = np.maximum(n, 1).astype(dtype)
        v = np.log(nf / dtype(max_exact)) / dtype(math.log(MAX_DISTANCE / max_exact)) * dtype(N_BUCKETS - max_exact)
        return np.minimum(max_exact + v.astype(np.int32), N_BUCKETS - 1)

    assert (large(np.float32) == large(np.float64)).all()
    bucket = np.where(n < max_exact, n, large(np.float32))
    return np.where(valid, bucket, -1).astype(np.int32)


def _bias_body(rb_ref, bkt_ref, o_ref):
    h = pl.program_id(0)
    bkt = bkt_ref[...]
    acc = jnp.full(bkt.shape, NEG, F32)
    for b in range(N_BUCKETS):
        acc = jnp.where(bkt == b, rb_ref[b, h], acc)
    o_ref[0] = acc


def _bias_table(rel_bias):
    return pl.pallas_call(
        _bias_body,
        grid=(ATTN_HEADS,),
        in_specs=[pl.BlockSpec(memory_space=pltpu.SMEM),
                  pl.BlockSpec((ROWS, 2 * ROWS), lambda h: (0, 0))],
        out_specs=pl.BlockSpec((1, ROWS, 2 * ROWS), lambda h: (h, 0, 0)),
        out_shape=jax.ShapeDtypeStruct((ATTN_HEADS, ROWS, 2 * ROWS), F32),
        compiler_params=_params(1, 16),
        name="bias_table",
    )(rel_bias, jnp.asarray(_bucket_table()))


def _cast_job(w, nsteps):
    rows = -(-w.shape[0] // nsteps)
    rows = -(-rows // 16) * 16
    return rows if rows * w.shape[1] * 4 <= (8 << 20) else None


def _cast_specs(w, rows, step_of):
    nblk = -(-w.shape[0] // rows)
    spec = pl.BlockSpec((rows, w.shape[1]), lambda *g: (jnp.minimum(step_of(*g), nblk - 1), 0))
    return spec, spec, jax.ShapeDtypeStruct(w.shape, BF16)


def _softmax_sink(s_scr, bias, sink, p_scr):
    sc = s_scr[...] * SCALE + bias
    m = jnp.maximum(jnp.max(sc, axis=-1, keepdims=True), sink)
    p = jnp.exp(sc - m)
    den = jnp.sum(p, axis=-1, keepdims=True) + jnp.exp(sink - m)
    p_scr[...] = (p * (1.0 / den)).astype(BF16)


def _attn_prompt_body(q_ref, kc_ref, kp_ref, vc_ref, vp_ref, qw_ref, kw_ref, bias_ref, sink_ref, *rest, cast):
    if cast:
        wsrc_ref, o_ref, pk_ref, pv_ref, wdst_ref, qn_scr, kn_scr, s_scr, p_scr = rest
        wdst_ref[...] = wsrc_ref[...].astype(BF16)
    else:
        o_ref, pk_ref, pv_ref, qn_scr, kn_scr, s_scr, p_scr = rest
    qw = qw_ref[...]
    kw = kw_ref[...]
    grows = Q_PER_KV * ROWS
    for h in range(ATTN_HEADS):
        hs = slice(h * HEAD_DIM, (h + 1) * HEAD_DIM)
        qn_scr[:, hs] = _rms_rows(q_ref[:, hs], qw)
    for kv in range(KV_HEADS):
        sl = slice(kv * HEAD_DIM, (kv + 1) * HEAD_DIM)
        kn_scr[0:ROWS, sl] = _rms_rows(kp_ref[:, sl], kw)
        kn_scr[ROWS:2 * ROWS, sl] = _rms_rows(kc_ref[:, sl], kw)
    pk_ref[0] = kn_scr[ROWS:2 * ROWS, :]
    pv_ref[0] = vc_ref[...]

    for kv in range(KV_HEADS):
        sl = slice(kv * HEAD_DIM, (kv + 1) * HEAD_DIM)
        qs = jnp.concatenate([qn_scr[:, (kv * Q_PER_KV + g) * HEAD_DIM:(kv * Q_PER_KV + g + 1) * HEAD_DIM]
                              for g in range(Q_PER_KV)], axis=0).astype(BF16)
        s_scr[kv * grows:(kv + 1) * grows, :] = _dot_nt(qs, kn_scr[:, sl].astype(BF16))

    col = lax.broadcasted_iota(jnp.int32, (1, 2 * ROWS), 1)
    no_prev = jnp.where(jnp.logical_and(pl.program_id(1) == 0, col < ROWS), NEG, 0.0)
    _softmax_sink(s_scr, bias_ref[...] + no_prev, sink_ref[...], p_scr)

    for kv in range(KV_HEADS):
        sl = slice(kv * HEAD_DIM, (kv + 1) * HEAD_DIM)
        vcat = jnp.concatenate([vp_ref[:, sl], vc_ref[:, sl]], axis=0).astype(BF16)
        o = _dot(p_scr[kv * grows:(kv + 1) * grows, :], vcat)
        for g in range(Q_PER_KV):
            h = kv * Q_PER_KV + g
            o_ref[:, h * HEAD_DIM:(h + 1) * HEAD_DIM] = o[g * ROWS:(g + 1) * ROWS].astype(o_ref.dtype)


def _attn_prompt(proj, sinks, qw, kw, bias16, nseq, seqlen, cast_w=None):
    nb = seqlen // ROWS
    kblk, vblk = COL_K // KV_WIDTH, COL_V // KV_WIDTH
    srows = ATTN_HEADS * ROWS
    cur = lambda b, i: b * nb + i
    prev = lambda b, i: b * nb + jnp.maximum(i - 1, 0)
    const = lambda b, i: (0, 0)
    in_specs = [pl.BlockSpec((ROWS, ATTN_WIDTH), lambda b, i: (cur(b, i), 0)),
                pl.BlockSpec((ROWS, KV_WIDTH), lambda b, i: (cur(b, i), kblk)),
                pl.BlockSpec((ROWS, KV_WIDTH), lambda b, i: (prev(b, i), kblk)),
                pl.BlockSpec((ROWS, KV_WIDTH), lambda b, i: (cur(b, i), vblk)),
                pl.BlockSpec((ROWS, KV_WIDTH), lambda b, i: (prev(b, i), vblk)),
                pl.BlockSpec((1, HEAD_DIM), const),
                pl.BlockSpec((1, HEAD_DIM), const),
                pl.BlockSpec((srows, 2 * ROWS), const),
                pl.BlockSpec((srows, 1), const)]
    out_specs = [pl.BlockSpec((ROWS, ATTN_WIDTH), lambda b, i: (cur(b, i), 0)),
                 pl.BlockSpec((1, ROWS, KV_WIDTH), lambda b, i: (b, 0, 0)),
                 pl.BlockSpec((1, ROWS, KV_WIDTH), lambda b, i: (b, 0, 0))]
    out_shape = [jax.ShapeDtypeStruct((nseq * seqlen, ATTN_WIDTH), BF16),
                 jax.ShapeDtypeStruct((nseq, ROWS, KV_WIDTH), F32),
                 jax.ShapeDtypeStruct((nseq, ROWS, KV_WIDTH), F32)]
    args = [proj, proj, proj, proj, proj, qw, kw, bias16.reshape(srows, 2 * ROWS),
            jnp.repeat(sinks, ROWS).reshape(srows, 1)]
    rows = None if cast_w is None else _cast_job(cast_w, nseq * nb)
    if rows is not None:
        ispec, ospec, oshape = _cast_specs(cast_w, rows, cur)
        in_specs.append(ispec)
        out_specs.append(ospec)
        out_shape.append(oshape)
        args.append(cast_w)
    outs = pl.pallas_call(
        functools.partial(_attn_prompt_body, cast=rows is not None),
        grid=(nseq, nb),
        in_specs=in_specs,
        out_specs=out_specs,
        out_shape=out_shape,
        scratch_shapes=[pltpu.VMEM((ROWS, ATTN_WIDTH), F32),
                        pltpu.VMEM((2 * ROWS, KV_WIDTH), F32),
                        pltpu.VMEM((srows, 2 * ROWS), F32),
                        pltpu.VMEM((srows, 2 * ROWS), BF16)],
        compiler_params=_params(2, 40),
        name="attn_prompt",
    )(*args)
    return outs if rows is not None else (*outs, None)


def _attn_sample_body(q_ref, kn_ref, vn_ref, wk_ref, wv_ref, qw_ref, kw_ref, bias_ref, sink_ref,
                      o_ref, sk_ref, sv_ref, qn_scr, kn_scr, s_scr, p_scr, *, nsq, ntok):
    qw = qw_ref[...]
    kw = kw_ref[...]
    grows = Q_PER_KV * ntok
    keep = ROWS - ntok
    for h in range(ATTN_HEADS):
        hs = slice(h * HEAD_DIM, (h + 1) * HEAD_DIM)
        qn_scr[:, hs] = _rms_rows(q_ref[:, hs], qw)
    for kv in range(KV_HEADS):
        sl = slice(kv * HEAD_DIM, (kv + 1) * HEAD_DIM)
        kn_scr[:, sl] = _rms_rows(kn_ref[:, sl], kw)
    for s in range(nsq):
        rows = slice(s * ntok, (s + 1) * ntok)
        sk_ref[s, 0:keep] = wk_ref[s, ntok:ROWS]
        sv_ref[s, 0:keep] = wv_ref[s, ntok:ROWS]
        for kv in range(KV_HEADS):
            sl = slice(kv * HEAD_DIM, (kv + 1) * HEAD_DIM)
            sk_ref[s, keep:ROWS, kv, :] = kn_scr[rows, sl]
            sv_ref[s, keep:ROWS, kv, :] = vn_ref[rows, sl]

    pad = jnp.zeros((keep, HEAD_DIM), F32)
    for s in range(nsq):
        rows = slice(s * ntok, (s + 1) * ntok)
        for kv in range(KV_HEADS):
            sl = slice(kv * HEAD_DIM, (kv + 1) * HEAD_DIM)
            grp = s * KV_HEADS + kv
            qs = jnp.concatenate([qn_scr[rows, (kv * Q_PER_KV + g) * HEAD_DIM:(kv * Q_PER_KV + g + 1) * HEAD_DIM]
                                  for g in range(Q_PER_KV)], axis=0).astype(BF16)
            kcat = jnp.concatenate([wk_ref[s, :, kv, :], kn_scr[rows, sl], pad], axis=0).astype(BF16)
            s_scr[grp * grows:(grp + 1) * grows, :] = _dot_nt(qs, kcat)

    _softmax_sink(s_scr, bias_ref[...], sink_ref[...], p_scr)

    for s in range(nsq):
        rows = slice(s * ntok, (s + 1) * ntok)
        for kv in range(KV_HEADS):
            sl = slice(kv * HEAD_DIM, (kv + 1) * HEAD_DIM)
            grp = s * KV_HEADS + kv
            vcat = jnp.concatenate([wv_ref[s, :, kv, :], vn_ref[rows, sl], pad], axis=0).astype(BF16)
            o = _dot(p_scr[grp * grows:(grp + 1) * grows, :], vcat)
            for g in range(Q_PER_KV):
                h = kv * Q_PER_KV + g
                o_ref[rows, h * HEAD_DIM:(h + 1) * HEAD_DIM] = o[g * ntok:(g + 1) * ntok]


def _attn_sample(proj, win_k, win_v, sinks, qw, kw, bias16, nseq, ntok, nsq=8):
    kblk, vblk = COL_K // KV_WIDTH, COL_V // KV_WIDTH
    rows = nsq * ntok
    srows = nsq * ATTN_HEADS * ntok
    bias = jnp.tile(bias16[:, :ntok, :].reshape(ATTN_HEADS * ntok, 2 * ROWS), (nsq, 1))
    sink = jnp.tile(jnp.repeat(sinks, ntok), nsq).reshape(srows, 1)
    return pl.pallas_call(
        functools.partial(_attn_sample_body, nsq=nsq, ntok=ntok),
        grid=(nseq // nsq,),
        in_specs=[pl.BlockSpec((rows, ATTN_WIDTH), lambda i: (i, 0)),
                  pl.BlockSpec((rows, KV_WIDTH), lambda i: (i, kblk)),
                  pl.BlockSpec((rows, KV_WIDTH), lambda i: (i, vblk)),
                  pl.BlockSpec((nsq, ROWS, KV_HEADS, HEAD_DIM), lambda i: (i, 0, 0, 0)),
                  pl.BlockSpec((nsq, ROWS, KV_HEADS, HEAD_DIM), lambda i: (i, 0, 0, 0)),
                  pl.BlockSpec((1, HEAD_DIM), lambda i: (0, 0)),
                  pl.BlockSpec((1, HEAD_DIM), lambda i: (0, 0)),
                  pl.BlockSpec((srows, 2 * ROWS), lambda i: (0, 0)),
                  pl.BlockSpec((srows, 1), lambda i: (0, 0))],
        out_specs=[pl.BlockSpec((rows, ATTN_WIDTH), lambda i: (i, 0)),
                   pl.BlockSpec((nsq, ROWS, KV_HEADS, HEAD_DIM), lambda i: (i, 0, 0, 0)),
                   pl.BlockSpec((nsq, ROWS, KV_HEADS, HEAD_DIM), lambda i: (i, 0, 0, 0))],
        out_shape=[jax.ShapeDtypeStruct((nseq * ntok, ATTN_WIDTH), F32),
                   jax.ShapeDtypeStruct((nseq, ROWS, KV_HEADS, HEAD_DIM), F32),
                   jax.ShapeDtypeStruct((nseq, ROWS, KV_HEADS, HEAD_DIM), F32)],
        scratch_shapes=[pltpu.VMEM((rows, ATTN_WIDTH), F32),
                        pltpu.VMEM((rows, KV_WIDTH), F32),
                        pltpu.VMEM((srows, 2 * ROWS), F32),
                        pltpu.VMEM((srows, 2 * ROWS), BF16)],
        compiler_params=_params(1, 32),
        name="attn_sample",
    )(proj, proj, proj, win_k, win_v, qw, kw, bias, sink)


def _ssd_constants(qs):
    r = np.arange(ROWS)
    same = (r[:, None] // qs) == (r[None, :] // qs)
    mask = same & (r[None, :] <= r[:, None])
    sel = r[None, :] == ((r[:, None] // qs) * qs + qs - 1)
    expand = np.zeros((ROWS, SSD_WIDTH), np.float32)
    for h in range(SSD_HEADS):
        expand[h, h * SSD_HEAD_DIM:(h + 1) * SSD_HEAD_DIM] = 1.0
    to_bf = lambda a: jnp.asarray(a.astype(np.float32), BF16)
    return to_bf(mask), to_bf(sel), to_bf(expand), to_bf(expand.T)


def _ssd_body(xbc_ref, z_ref, dt_ref, cprev_ref, h0_ref, cw_ref, cb_ref, dtb_ref, alog_ref, dexp_ref,
              nw_ref, mask_ref, sel_ref, e_ref, et_ref, *rest, nsq, qs, cast):
    if cast:
        wsrc_ref, y_ref, ht_ref, wdst_ref = rest[:4]
        wdst_ref[...] = wsrc_ref[...].astype(BF16)
        rest = rest[4:]
    else:
        y_ref, ht_ref = rest[:2]
        rest = rest[2:]
    h_scr, cbuf, xc_scr, xdt_scr, xdd_scr, eacs_scr, yacc_scr, acs_scr, acst_scr, aclt_scr = rest
    c = pl.program_id(1)
    s = pl.program_id(2)

    @pl.when(c == 0)
    def _():
        h_scr[...] = h0_ref[0]

    if nsq == 1:
        @pl.when(c == 0)
        def _():
            cbuf[5:8, :] = cprev_ref[0]

    @pl.when(s == 0)
    def _chunk_phase():
        ncc = 8 if qs == ROWS else 1
        cwid = CONV_DIM // ncc
        for sq in range(nsq):
            for cc in range(ncc):
                cs = slice(cc * cwid, (cc + 1) * cwid)
                if nsq > 1:
                    cbuf[5:8, cs] = cprev_ref[sq, :, cs]
                cbuf[8:8 + qs, cs] = xbc_ref[sq * qs:(sq + 1) * qs, cs]
                acc = cb_ref[:, cs] + cw_ref[0:1, cs] * cbuf[5:5 + qs, cs]
                for t in range(1, SSD_CONV):
                    acc = acc + cw_ref[t:t + 1, cs] * cbuf[5 + t:5 + t + qs, cs]
                xc_scr[sq * qs:(sq + 1) * qs, cs] = acc * _sigmoid(acc)
                if nsq == 1:
                    cbuf[5:8, cs] = xbc_ref[qs - 3:qs, cs]

        x = dt_ref[...] + dtb_ref[...]
        dt = jnp.maximum(x, 0.0) + jnp.log1p(jnp.exp(-jnp.abs(x)))
        da = dt * (-jnp.exp(alog_ref[...]))
        mask_bf = mask_ref[...]
        acs = _dot3_r(mask_bf, _split3(da))
        acl = _dot3_r(sel_ref[...], _split3(acs))
        acs_scr[...] = acs
        acst_scr[...] = acs.T
        aclt_scr[...] = acl.T
        e = e_ref[...]
        dt_e = _dot3_l(dt, e)
        acs_e = _dot3_l(acs, e)
        acl_e = _dot3_l(acl, e)
        xs = xc_scr[:, 0:SSD_WIDTH]
        xdt = xs * dt_e
        xdt_scr[...] = xdt
        xdd_scr[...] = xdt * jnp.exp(acl_e - acs_e)
        eacs_scr[...] = jnp.exp(acs_e)

        maskb = mask_bf > 0.5
        lane = lax.broadcasted_iota(jnp.int32, (ROWS, 2 * SSD_HEAD_DIM), 1)
        for g in range(SSD_GROUPS):
            bg = xc_scr[:, SSD_WIDTH + g * SSD_STATE:SSD_WIDTH + (g + 1) * SSD_STATE].astype(BF16)
            cg = xc_scr[:, SSD_WIDTH + (SSD_GROUPS + g) * SSD_STATE:
                        SSD_WIDTH + (SSD_GROUPS + g + 1) * SSD_STATE].astype(BF16)
            cb = _dot_nt(cg, bg)
            for pp in range(2):
                ms = []
                for j in range(2):
                    h = 4 * g + 2 * pp + j
                    seg = acs_scr[:, h:h + 1] - acst_scr[h:h + 1, :]
                    ms.append((cb * jnp.exp(jnp.where(maskb, seg, NEG))).astype(BF16))
                c0 = (4 * g + 2 * pp) * SSD_HEAD_DIM
                cols = slice(c0, c0 + 2 * SSD_HEAD_DIM)
                xp = xdt_scr[:, cols]
                rhs = jnp.concatenate([jnp.where(lane < SSD_HEAD_DIM, xp, 0.0),
                                       jnp.where(lane >= SSD_HEAD_DIM, xp, 0.0)], axis=0).astype(BF16)
                yd = _dot(jnp.concatenate(ms, axis=1), rhs)
                yacc_scr[:, cols] = yd + dexp_ref[:, cols] * xc_scr[:, cols]

    if nsq == 1:
        r0 = 0
        rows = slice(0, qs)
    else:
        r0 = pl.multiple_of(s * qs, qs)
        rows = pl.ds(r0, qs)
        ridx = lax.broadcasted_iota(jnp.int32, (ROWS, 1), 0)
        rowmask = jnp.logical_and(ridx >= r0, ridx < r0 + qs)
    tlane = lax.broadcasted_iota(jnp.int32, (ROWS, ROWS), 1)
    xcol = jnp.sum(jnp.where(tlane == r0, aclt_scr[...], 0.0), axis=1, keepdims=True)
    x3 = _split3(jnp.broadcast_to(xcol, (ROWS, ROWS)))
    c_rows = xc_scr[rows, SSD_WIDTH + SSD_GROUPS * SSD_STATE:CONV_DIM].astype(BF16)
    yo, st = [], []
    for g in range(SSD_GROUPS):
        grows = slice(g * GROUP_ROWS, (g + 1) * GROUP_ROWS)
        yo.append(_dot_nt(c_rows[:, g * SSD_STATE:(g + 1) * SSD_STATE], h_scr[grows, :].astype(BF16)))
        bg = xc_scr[:, SSD_WIDTH + g * SSD_STATE:SSD_WIDTH + (g + 1) * SSD_STATE]
        if nsq > 1:
            bg = jnp.where(rowmask, bg, 0.0)
        st.append(_dot_tn(xdd_scr[:, grows].astype(BF16), bg.astype(BF16)))
    yacc_scr[rows, :] = yacc_scr[rows, :] + jnp.concatenate(yo, axis=1) * eacs_scr[rows, :]
    dec = jnp.exp(_dot3_r(et_ref[...], x3))
    h_new = dec * h_scr[...] + jnp.concatenate(st, axis=0)
    h_scr[...] = h_new
    ht_ref[0] = h_new

    @pl.when(s == nsq - 1)
    def _finish():
        z = z_ref[...]
        yg = yacc_scr[...] * (z * _sigmoid(z))
        y_ref[...] = _rms_rows(yg, nw_ref[...]).astype(y_ref.dtype)


def _ssd(proj, dt_raw, conv_prev, h0, cw, cb, dtb, alog, dexp, nw, nseq, seqlen, cast_w=None):
    if seqlen % ROWS == 0:
        qs, nsq, nchunk, ngrp = ROWS, 1, seqlen // ROWS, nseq
    else:
        qs, nsq, nchunk = seqlen, ROWS // seqlen, 1
        ngrp = nseq // nsq
    mask, sel, expand, expand_t = _ssd_constants(qs)
    rowblk = lambda b, c, s: b * nchunk + c
    seq = lambda b, c, s: b * nsq + s
    const2 = lambda b, c, s: (0, 0)
    full = lambda shape: pl.BlockSpec(shape, const2)
    in_specs = [pl.BlockSpec((ROWS, CONV_DIM), lambda b, c, s: (rowblk(b, c, s), COL_XBC // CONV_DIM)),
                pl.BlockSpec((ROWS, SSD_WIDTH), lambda b, c, s: (rowblk(b, c, s), COL_Z // SSD_WIDTH)),
                pl.BlockSpec((ROWS, ROWS), lambda b, c, s: (rowblk(b, c, s), 0)),
                pl.BlockSpec((nsq, SSD_CONV - 1, CONV_DIM), lambda b, c, s: (b, 0, 0)),
                pl.BlockSpec((1, SSD_WIDTH, SSD_STATE), lambda b, c, s: (seq(b, c, s), 0, 0)),
                full((SSD_CONV, CONV_DIM)), full((1, CONV_DIM)), full((1, ROWS)), full((1, ROWS)),
                full((1, SSD_WIDTH)), full((1, SSD_WIDTH)),
                full((ROWS, ROWS)), full((ROWS, ROWS)), full((ROWS, SSD_WIDTH)), full((SSD_WIDTH, ROWS))]
    out_specs = [pl.BlockSpec((ROWS, SSD_WIDTH), lambda b, c, s: (rowblk(b, c, s), 0)),
                 pl.BlockSpec((1, SSD_WIDTH, SSD_STATE), lambda b, c, s: (seq(b, c, s), 0, 0))]
    out_shape = [jax.ShapeDtypeStruct((nseq * seqlen, SSD_WIDTH), BF16),
                 jax.ShapeDtypeStruct((nseq, SSD_WIDTH, SSD_STATE), F32)]
    args = [proj, proj, dt_raw, conv_prev, h0, cw, cb, dtb, alog, dexp, nw, mask, sel, expand, expand_t]
    rows = None if cast_w is None else _cast_job(cast_w, ngrp * nchunk * nsq)
    if rows is not None:
        ispec, ospec, oshape = _cast_specs(cast_w, rows, lambda b, c, s: (b * nchunk + c) * nsq + s)
        in_specs.append(ispec)
        out_specs.append(ospec)
        out_shape.append(oshape)
        args.append(cast_w)
    outs = pl.pallas_call(
        functools.partial(_ssd_body, nsq=nsq, qs=qs, cast=rows is not None),
        grid=(ngrp, nchunk, nsq),
        in_specs=in_specs,
        out_specs=out_specs,
        out_shape=out_shape,
        scratch_shapes=[pltpu.VMEM((SSD_WIDTH, SSD_STATE), F32),
                        pltpu.VMEM((8 + ROWS, CONV_DIM), F32),
                        pltpu.VMEM((ROWS, CONV_DIM), F32),
                        pltpu.VMEM((ROWS, SSD_WIDTH), F32),
                        pltpu.VMEM((ROWS, SSD_WIDTH), F32),
                        pltpu.VMEM((ROWS, SSD_WIDTH), F32),
                        pltpu.VMEM((ROWS, SSD_WIDTH), F32),
                        pltpu.VMEM((ROWS, ROWS), F32),
                        pltpu.VMEM((ROWS, ROWS), F32),
                        pltpu.VMEM((ROWS, ROWS), F32)],
        compiler_params=_params(3, 48),
        name="ssd",
    )(*args)
    return outs if rows is not None else (*outs, None)


def _ffn_gate_body(h_ref, wg_ref, wu_ref, cw_ref, cb_ref, *rest, period, tiles_per_seq, cast=False):
    if cast:
        wsrc_ref, a_ref, last_ref, wdst_ref, carry_scr = rest
        wdst_ref[...] = wsrc_ref[...].astype(BF16)
    elif period is None:
        a_ref, last_ref, carry_scr = rest
    else:
        prev_ref, a_ref, g_ref = rest
    hb = h_ref[...]
    tn = wg_ref.shape[1]
    col = pl.program_id(1) * tn + lax.broadcasted_iota(jnp.int32, (1, tn), 1)
    g = _dot(hb, jnp.where(col < D_FF, wg_ref[...], 0))
    u = _dot(hb, jnp.where(col < D_FF, wu_ref[...], 0))
    tm = g.shape[0]
    row = lax.broadcasted_iota(jnp.int32, g.shape, 0)
    if period is None:
        i, j = pl.program_id(0), pl.program_id(1)
        c8 = jnp.where(i % tiles_per_seq == 0, 0.0, carry_scr[j])
        carry_scr[j] = g[tm - 8:tm]
        last_ref[0] = g[tm - 2:tm]
        g_m1 = jnp.where(row == 0, c8[7:8], pltpu.roll(g, 1, 0))
        g_m2 = jnp.where(row == 0, c8[6:7], jnp.where(row == 1, c8[7:8], pltpu.roll(g, 2, 0)))
    else:
        g_ref[...] = g
        prev = prev_ref[...]
        t = row & (period - 1)
        g_m1 = jnp.where(t == 0, pltpu.roll(prev, tm - 1, 0), pltpu.roll(g, 1, 0))
        g_m2 = jnp.where(t < 2, prev, pltpu.roll(g, 2, 0))
    gc = cb_ref[...] + cw_ref[0:1] * g_m2
    gc = gc + cw_ref[1:2] * g_m1
    gc = gc + cw_ref[2:3] * g
    a_ref[...] = ((gc * _sigmoid(gc)) * u).astype(a_ref.dtype)


def _ffn_gate(h, w_gate, w_up, cw, cb, *, nseq, seqlen, prev_rows=None, cast_w=None, tm=1024, tn=512):
    m = h.shape[0]
    nj = D_FF_PAD // tn
    wspec = pl.BlockSpec((D_MODEL, tn), lambda i, j: (0, j))
    cwspec = pl.BlockSpec((FFN_CONV, tn), lambda i, j: (0, j))
    cbspec = pl.BlockSpec((1, tn), lambda i, j: (0, j))
    if prev_rows is None:
        tm = min(tm, seqlen)
        tps = seqlen // tm
        in_specs = [pl.BlockSpec((tm, D_MODEL), lambda i, j: (i, 0)), wspec, wspec, cwspec, cbspec]
        out_specs = [pl.BlockSpec((tm, tn), lambda i, j: (i, j)),
                     pl.BlockSpec((1, FFN_CONV - 1, tn), lambda i, j: (i, 0, j))]
        out_shape = [jax.ShapeDtypeStruct((m, D_FF_PAD), BF16),
                     jax.ShapeDtypeStruct((m // tm, FFN_CONV - 1, D_FF_PAD), F32)]
        scratch = [pltpu.VMEM((nj, 8, tn), F32)]
        args = (h, w_gate, w_up, cw, cb)
        rows = None if cast_w is None else _cast_job(cast_w, (m // tm) * nj)
        if rows is not None:
            ispec, ospec, oshape = _cast_specs(cast_w, rows, lambda i, j: i * nj + j)
            in_specs.append(ispec)
            out_specs.append(ospec)
            out_shape.append(oshape)
            args = args + (cast_w,)
        body = functools.partial(_ffn_gate_body, period=None, tiles_per_seq=tps, cast=rows is not None)
    else:
        assert seqlen & (seqlen - 1) == 0
        tm = m
        in_specs = [pl.BlockSpec((tm, D_MODEL), lambda i, j: (i, 0)), wspec, wspec, cwspec, cbspec,
                    pl.BlockSpec((tm, tn), lambda i, j: (i, j))]
        out_specs = [pl.BlockSpec((tm, tn), lambda i, j: (i, j)),
                     pl.BlockSpec((tm, tn), lambda i, j: (i, j))]
        out_shape = [jax.ShapeDtypeStruct((m, D_FF_PAD), BF16),
                     jax.ShapeDtypeStruct((m, D_FF_PAD), F32)]
        scratch = []
        args = (h, w_gate, w_up, cw, cb, prev_rows)
        body = functools.partial(_ffn_gate_body, period=seqlen, tiles_per_seq=None)
    outs = pl.pallas_call(
        body,
        grid=(m // tm, nj),
        in_specs=in_specs,
        out_specs=out_specs,
        out_shape=out_shape,
        scratch_shapes=scratch,
        compiler_params=_params(2),
        name="ffn_gate",
    )(*args)
    return outs if len(outs) == 3 else (*outs, None)


def _down_body(a_ref, w_ref, res_ref, o_ref, *, chunk):
    @pl.when(pl.program_id(2) == 0)
    def _():
        o_ref[...] = res_ref[...]

    a = a_ref[...]
    tk = w_ref.shape[0]
    row = pl.program_id(2) * tk + lax.broadcasted_iota(jnp.int32, (tk, 1), 0)
    for c in range(o_ref.shape[1] // chunk):
        cs = slice(c * chunk, (c + 1) * chunk)
        o_ref[:, cs] += _dot(a, jnp.where(row < D_FF, w_ref[:, cs], 0))


def _down_proj(a, w_down, res, *, tm=1024, tn=2048, tk=1024):
    m, kdim = a.shape
    n = w_down.shape[1]
    tm = min(tm, m)
    return pl.pallas_call(
        functools.partial(_down_body, chunk=512),
        grid=(n // tn, m // tm, kdim // tk),
        in_specs=[pl.BlockSpec((tm, tk), lambda j, i, k: (i, k)),
                  pl.BlockSpec((tk, tn), lambda j, i, k: (k, j)),
                  pl.BlockSpec((tm, tn), lambda j, i, k: (i, j))],
        out_specs=pl.BlockSpec((tm, tn), lambda j, i, k: (i, j)),
        out_shape=jax.ShapeDtypeStruct((m, n), F32),
        compiler_params=_params(3),
        name="down_proj",
    )(a, w_down, res)


def _remember_bf16(bf16_w, wts, name, hosted):
    if name not in bf16_w:
        bf16_w[name] = hosted if hosted is not None else wts[name].astype(BF16)


def _layer(x, nseq, seqlen, wts, bf16_w, bias16, attn_state, ssm_state, conv_state, ffn_state):
    m = x.shape[0]
    todo = lambda name: None if name in bf16_w else wts[name]
    h = _rmsnorm(x, wts["mix_norm_w"])
    proj, dt_raw = _in_proj(h, wts["w_in_t"])

    qw = wts["q_norm_w"].reshape(1, HEAD_DIM)
    kw = wts["k_norm_w"].reshape(1, HEAD_DIM)
    if attn_state is None:
        attn, new_k, new_v, hosted = _attn_prompt(proj, wts["attn_sinks"], qw, kw, bias16, nseq, seqlen,
                                                  cast_w=todo("w_gate"))
        _remember_bf16(bf16_w, wts, "w_gate", hosted)
        new_k = new_k.reshape(nseq, WINDOW, KV_HEADS, HEAD_DIM)
        new_v = new_v.reshape(nseq, WINDOW, KV_HEADS, HEAD_DIM)
        conv_prev = jnp.zeros((nseq, SSD_CONV - 1, CONV_DIM), F32)
        h0 = jnp.zeros((nseq, SSD_WIDTH, SSD_STATE), F32)
    else:
        win_k, win_v = attn_state
        attn, new_k, new_v = _attn_sample(proj, win_k, win_v, wts["attn_sinks"], qw, kw, bias16, nseq, seqlen)
        _remember_bf16(bf16_w, wts, "w_gate", None)
        conv_prev = conv_state
        h0 = ssm_state.reshape(nseq, SSD_WIDTH, SSD_STATE)

    y, h_t, hosted = _ssd(proj, dt_raw, conv_prev, h0, wts["ssd_conv_w"], wts["ssd_conv_b"], wts["ssd_dt_bias"],
                          wts["ssd_A_log"], wts["ssd_D"], wts["ssd_norm_w"], nseq, seqlen, cast_w=todo("w_up"))
    _remember_bf16(bf16_w, wts, "w_up", hosted)
    h_t = h_t.reshape(nseq, SSD_HEADS, SSD_HEAD_DIM, SSD_STATE)
    new_conv = proj.reshape(nseq, seqlen, PROJ_W)[:, seqlen - (SSD_CONV - 1):, COL_XBC:COL_XBC + CONV_DIM]

    x1 = _out_proj(attn, y, wts["w_out"], x)
    h2 = _rmsnorm(x1, wts["ffn_norm_w"])
    if ffn_state is None:
        a, last, hosted = _ffn_gate(h2, bf16_w["w_gate"], bf16_w["w_up"], wts["ffn_conv_w"], wts["ffn_conv_b"],
                                    nseq=nseq, seqlen=seqlen, cast_w=todo("w_down"))
        new_ffn = last.reshape(nseq, -1, FFN_CONV - 1, D_FF_PAD)[:, -1, :, :D_FF]
    else:
        prev_rows = jnp.pad(ffn_state, ((0, 0), (0, seqlen - (FFN_CONV - 1)), (0, D_FF_PAD - D_FF)))
        a, g, hosted = _ffn_gate(h2, bf16_w["w_gate"], bf16_w["w_up"], wts["ffn_conv_w"], wts["ffn_conv_b"],
                                 nseq=nseq, seqlen=seqlen, prev_rows=prev_rows.reshape(m, D_FF_PAD))
        new_ffn = g.reshape(nseq, seqlen, D_FF_PAD)[:, seqlen - (FFN_CONV - 1):, :D_FF]
    _remember_bf16(bf16_w, wts, "w_down", hosted)
    x2 = _down_proj(a, bf16_w["w_down"], x1)
    return x2, new_k, new_v, h_t, new_conv, new_ffn


def kernel(x_prompt, x_sample, state_attn_k, state_attn_v, state_ssm, state_ssd_conv, state_ffn_conv, rel_bias, mix_norm_w, w_in, q_norm_w, k_norm_w, attn_sinks, ssd_conv_w, ssd_conv_b, ssd_dt_bias, ssd_A_log, ssd_D, ssd_norm_w, w_out, ffn_norm_w, w_gate, w_up, ffn_conv_w, ffn_conv_b, w_down):
    depth = w_in.shape[0]
    bp, lp, _ = x_prompt.shape
    bs, ls, _ = x_sample.shape
    bias16 = _bias_table(rel_bias)
    yp = x_prompt.reshape(bp * lp, D_MODEL)
    ys = x_sample.reshape(bs * ls, D_MODEL)
    outs_p, outs_s = [], []
    pad_heads = lambda v: jnp.pad(v.reshape(1, SSD_HEADS), ((0, 0), (0, ROWS - SSD_HEADS)))
    pad_ff = lambda v: jnp.pad(v, ((0, 0), (0, D_FF_PAD - D_FF)))
    for l in range(depth):
        wts = dict(
            mix_norm_w=mix_norm_w[l],
            w_in_t=w_in[l].T,

            q_norm_w=q_norm_w[l], k_norm_w=k_norm_w[l], attn_sinks=attn_sinks[l],
            ssd_conv_w=ssd_conv_w[l], ssd_conv_b=ssd_conv_b[l].reshape(1, CONV_DIM),
            ssd_dt_bias=pad_heads(ssd_dt_bias[l]), ssd_A_log=pad_heads(ssd_A_log[l]),
            ssd_D=jnp.repeat(ssd_D[l], SSD_HEAD_DIM).reshape(1, SSD_WIDTH),
            ssd_norm_w=ssd_norm_w[l].reshape(1, SSD_WIDTH),
            w_out=w_out[l], ffn_norm_w=ffn_norm_w[l],
            w_gate=w_gate[l], w_up=w_up[l],
            ffn_conv_w=pad_ff(ffn_conv_w[l]), ffn_conv_b=pad_ff(ffn_conv_b[l].reshape(1, D_FF)),
            w_down=w_down[l],
        )
        bf16_w = {}
        yp, *sp = _layer(yp, bp, lp, wts, bf16_w, bias16, None, None, None, None)
        ys, *ss = _layer(ys, bs, ls, wts, bf16_w, bias16, (state_attn_k[l], state_attn_v[l]), state_ssm[l],
                         state_ssd_conv[l], state_ffn_conv[l])
        outs_p.append(sp)
        outs_s.append(ss)
    stack = lambda outs, i: jnp.stack([o[i] for o in outs])
    return (yp.reshape(bp, lp, D_MODEL), ys.reshape(bs, ls, D_MODEL),
            stack(outs_p, 0), stack(outs_p, 1), stack(outs_p, 2), stack(outs_p, 3), stack(outs_p, 4),
            stack(outs_s, 0), stack(outs_s, 1), stack(outs_s, 2), stack(outs_s, 3), stack(outs_s, 4))
```

```python
import functools
import math

import numpy as np
import jax
import jax.numpy as jnp
from jax import lax
from jax.experimental import pallas as pl
from jax.experimental.pallas import tpu as pltpu

F32 = jnp.float32
BF16 = jnp.bfloat16

D_MODEL = 4096
HEAD_DIM = 128
ATTN_HEADS = 16
KV_HEADS = 4
Q_PER_KV = 4
ATTN_WIDTH = 2048
WINDOW = 128
N_BUCKETS = 32
MAX_DISTANCE = 128
SSD_HEAD_DIM = 64
SSD_HEADS = 32
SSD_WIDTH = 2048
SSD_GROUPS = 8
SSD_STATE = 128
SSD_CONV = 4
CONV_DIM = 4096
KV_WIDTH = KV_HEADS * HEAD_DIM
D_FF = 11008
D_FF_PAD = 11264
FFN_CONV = 3
EPS = 1e-6
NEG = -1e30
SCALE = HEAD_DIM ** -0.5

ROWS = 128
GROUP_ROWS = 4 * SSD_HEAD_DIM
COL_Z = ATTN_WIDTH
COL_XBC = COL_Z + SSD_WIDTH
COL_K = COL_XBC + CONV_DIM
COL_V = COL_K + KV_WIDTH
PROJ_W = COL_V + KV_WIDTH
VMEM_LIMIT_MB = 56


def _params(n_axes, vmem_mb=VMEM_LIMIT_MB):
    return pltpu.CompilerParams(dimension_semantics=("arbitrary",) * n_axes,
                                vmem_limit_bytes=vmem_mb << 20)


def _dot(a, b):
    return jnp.dot(a, b, preferred_element_type=F32)


def _dot_nt(a, b):
    return lax.dot_general(a, b, (((1,), (1,)), ((), ())), preferred_element_type=F32)


def _dot_tn(a, b):
    return lax.dot_general(a, b, (((0,), (0,)), ((), ())), preferred_element_type=F32)


def _split3(x):
    a = x.astype(BF16)
    r = x - a.astype(F32)
    b = r.astype(BF16)
    c = (r - b.astype(F32)).astype(BF16)
    return a, b, c


def _dot3_l(x, w):
    a, b, c = _split3(x)
    return (_dot(a, w) + _dot(b, w)) + _dot(c, w)


def _dot3_r(w, x3):
    return (_dot(w, x3[0]) + _dot(w, x3[1])) + _dot(w, x3[2])


def _sigmoid(x):
    return 1.0 / (1.0 + jnp.exp(-x))


def _rms_rows(x, w):
    r = lax.rsqrt(jnp.mean(x * x, axis=-1, keepdims=True) + EPS)
    return (x * r) * w


def _rmsnorm_body(x_ref, w_ref, o_ref):
    o_ref[...] = _rms_rows(x_ref[...], w_ref[...]).astype(o_ref.dtype)


def _rmsnorm(x, w, rows=256):
    m, d = x.shape
    rows = min(rows, m)
    return pl.pallas_call(
        _rmsnorm_body,
        grid=(m // rows,),
        in_specs=[pl.BlockSpec((rows, d), lambda i: (i, 0)),
                  pl.BlockSpec((1, d), lambda i: (0, 0))],
        out_specs=pl.BlockSpec((rows, d), lambda i: (i, 0)),
        out_shape=jax.ShapeDtypeStruct((m, d), BF16),
        compiler_params=_params(1, 32),
        name="rmsnorm",
    )(x, w.reshape(1, d))


def _in_proj_body(a_ref, bt_ref, dtw_ref, o_ref, dt_ref):
    a = a_ref[...]
    o_ref[...] = _dot_nt(a, bt_ref[...].astype(BF16))

    @pl.when(pl.program_id(1) == 0)
    def _():
        row = lax.broadcasted_iota(jnp.int32, (dtw_ref.shape[0], 1), 0)
        dtw = jnp.where(row < SSD_HEADS, dtw_ref[...], 0.0)
        dt_ref[...] = _dot_nt(a, dtw.astype(BF16))


def _in_proj(h, w_in_t, *, tm=1024, tn=512):
    m = h.shape[0]
    tm = min(tm, m)
    nq, nkv = ATTN_WIDTH // tn, KV_WIDTH // tn
    nzx = (SSD_WIDTH + CONV_DIM) // tn

    def src_block(j):
        return jnp.where(j < nq, j, jnp.where(j < nq + nzx, j + 2 * nkv, j - nzx))

    return pl.pallas_call(
        _in_proj_body,
        grid=(m // tm, PROJ_W // tn),
        in_specs=[pl.BlockSpec((tm, D_MODEL), lambda i, j: (i, 0)),
                  pl.BlockSpec((tn, D_MODEL), lambda i, j: (src_block(j), 0)),
                  pl.BlockSpec((ROWS, D_MODEL), lambda i, j: (PROJ_W // ROWS, 0))],
        out_specs=[pl.BlockSpec((tm, tn), lambda i, j: (i, j)),
                   pl.BlockSpec((tm, ROWS), lambda i, j: (i, 0))],
        out_shape=[jax.ShapeDtypeStruct((m, PROJ_W), F32),
                   jax.ShapeDtypeStruct((m, ROWS), F32)],
        compiler_params=_params(2),
        name="in_proj",
    )(h, w_in_t, w_in_t)


def _out_proj_body(a1_ref, a2_ref, b1_ref, b2_ref, res_ref, nw_ref, o_ref, xw_ref, ssq_ref):
    acc = (_dot(a1_ref[...].astype(BF16), b1_ref[...].astype(BF16))
           + _dot(a2_ref[...].astype(BF16), b2_ref[...].astype(BF16)))
    x1 = res_ref[...] + acc
    o_ref[...] = x1
    xw_ref[...] = (x1 * nw_ref[...]).astype(xw_ref.dtype)
    ssq_ref[0] = jnp.sum(x1 * x1, axis=-1, keepdims=True)


def _out_proj(attn, y, w_out, res, norm_w, *, tm=1024, tn=512):
    m = attn.shape[0]
    n = w_out.shape[1]
    tm = min(tm, m)
    return pl.pallas_call(
        _out_proj_body,
        grid=(m // tm, n // tn),
        in_specs=[pl.BlockSpec((tm, ATTN_WIDTH), lambda i, j: (i, 0)),
                  pl.BlockSpec((tm, SSD_WIDTH), lambda i, j: (i, 0)),
                  pl.BlockSpec((ATTN_WIDTH, tn), lambda i, j: (0, j)),
                  pl.BlockSpec((SSD_WIDTH, tn), lambda i, j: (1, j)),
                  pl.BlockSpec((tm, tn), lambda i, j: (i, j)),
                  pl.BlockSpec((1, tn), lambda i, j: (0, j))],
        out_specs=[pl.BlockSpec((tm, tn), lambda i, j: (i, j)),
                   pl.BlockSpec((tm, tn), lambda i, j: (i, j)),
                   pl.BlockSpec((1, tm, 1), lambda i, j: (j, i, 0))],
        out_shape=[jax.ShapeDtypeStruct((m, n), F32),
                   jax.ShapeDtypeStruct((m, n), BF16),
                   jax.ShapeDtypeStruct((n // tn, m, 1), F32)],
        compiler_params=_params(2),
        name="out_proj",
    )(attn, y, w_out, w_out, res, norm_w.reshape(1, n))


def _bucket_table():
    i = np.arange(ROWS)[:, None]
    j = np.arange(2 * ROWS)[None, :]
    dist = i + ROWS - j
    valid = (dist >= 0) & (dist < WINDOW)
    n = np.maximum(dist, 0)
    max_exact = N_BUCKETS // 2

    def large(dtype):
        nf = np.maximum(n, 1).astype(dtype)
        v = np.log(nf / dtype(max_exact)) / dtype(math.log(MAX_DISTANCE / max_exact)) * dtype(N_BUCKETS - max_exact)
        return np.minimum(max_exact + v.astype(np.int32), N_BUCKETS - 1)

    assert (large(np.float32) == large(np.float64)).all()
    bucket = np.where(n < max_exact, n, large(np.float32))
    return np.where(valid, bucket, -1).astype(np.int32)


def _bias_body(rb_ref, bkt_ref, o_ref):
    h = pl.program_id(0)
    bkt = bkt_ref[...]
    acc = jnp.full(bkt.shape, NEG, F32)
    for b in range(N_BUCKETS):
        acc = jnp.where(bkt == b, rb_ref[b, h], acc)
    o_ref[0] = acc


def _bias_table(rel_bias):
    return pl.pallas_call(
        _bias_body,
        grid=(ATTN_HEADS,),
        in_specs=[pl.BlockSpec(memory_space=pltpu.SMEM),
                  pl.BlockSpec((ROWS, 2 * ROWS), lambda h: (0, 0))],
        out_specs=pl.BlockSpec((1, ROWS, 2 * ROWS), lambda h: (h, 0, 0)),
        out_shape=jax.ShapeDtypeStruct((ATTN_HEADS, ROWS, 2 * ROWS), F32),
        compiler_params=_params(1, 16),
        name="bias_table",
    )(rel_bias, jnp.asarray(_bucket_table()))


def _cast_job(w, nsteps):
    rows = -(-w.shape[0] // nsteps)
    rows = -(-rows // 16) * 16
    return rows if rows * w.shape[1] * 4 <= (8 << 20) else None


def _cast_specs(w, rows, step_of):
    nblk = -(-w.shape[0] // rows)
    spec = pl.BlockSpec((rows, w.shape[1]), lambda *g: (jnp.minimum(step_of(*g), nblk - 1), 0))
    return spec, spec, jax.ShapeDtypeStruct(w.shape, BF16)


def _softmax_sink(s_scr, bias, sink, p_scr):
    sc = s_scr[...] * SCALE + bias
    m = jnp.maximum(jnp.max(sc, axis=-1, keepdims=True), sink)
    p = jnp.exp(sc - m)
    den = jnp.sum(p, axis=-1, keepdims=True) + jnp.exp(sink - m)
    p_scr[...] = (p * (1.0 / den)).astype(BF16)


def _attn_prompt_body(q_ref, kc_ref, kp_ref, vc_ref, vp_ref, qw_ref, kw_ref, bias_ref, sink_ref, *rest, ncast):
    o_ref, pk_ref, pv_ref = rest[ncast:ncast + 3]
    for src_ref, dst_ref in zip(rest[:ncast], rest[ncast + 3:2 * ncast + 3]):
        dst_ref[...] = src_ref[...].astype(BF16)
    qn_scr, kn_scr, s_scr, p_scr = rest[2 * ncast + 3:]
    qw = qw_ref[...]
    kw = kw_ref[...]
    grows = Q_PER_KV * ROWS
    for h in range(ATTN_HEADS):
        hs = slice(h * HEAD_DIM, (h + 1) * HEAD_DIM)
        qn_scr[:, hs] = _rms_rows(q_ref[:, hs], qw)
    for kv in range(KV_HEADS):
        sl = slice(kv * HEAD_DIM, (kv + 1) * HEAD_DIM)
        kn_scr[0:ROWS, sl] = _rms_rows(kp_ref[:, sl], kw)
        kn_scr[ROWS:2 * ROWS, sl] = _rms_rows(kc_ref[:, sl], kw)
    pk_ref[0] = kn_scr[ROWS:2 * ROWS, :]
    pv_ref[0] = vc_ref[...]

    for kv in range(KV_HEADS):
        sl = slice(kv * HEAD_DIM, (kv + 1) * HEAD_DIM)
        qs = jnp.concatenate([qn_scr[:, (kv * Q_PER_KV + g) * HEAD_DIM:(kv * Q_PER_KV + g + 1) * HEAD_DIM]
                              for g in range(Q_PER_KV)], axis=0).astype(BF16)
        s_scr[kv * grows:(kv + 1) * grows, :] = _dot_nt(qs, kn_scr[:, sl].astype(BF16))

    col = lax.broadcasted_iota(jnp.int32, (1, 2 * ROWS), 1)
    no_prev = jnp.where(jnp.logical_and(pl.program_id(1) == 0, col < ROWS), NEG, 0.0)
    _softmax_sink(s_scr, bias_ref[...] + no_prev, sink_ref[...], p_scr)

    for kv in range(KV_HEADS):
        sl = slice(kv * HEAD_DIM, (kv + 1) * HEAD_DIM)
        vcat = jnp.concatenate([vp_ref[:, sl], vc_ref[:, sl]], axis=0).astype(BF16)
        o = _dot(p_scr[kv * grows:(kv + 1) * grows, :], vcat)
        for g in range(Q_PER_KV):
            h = kv * Q_PER_KV + g
            o_ref[:, h * HEAD_DIM:(h + 1) * HEAD_DIM] = o[g * ROWS:(g + 1) * ROWS].astype(o_ref.dtype)


def _attn_prompt(proj, sinks, qw, kw, bias16, nseq, seqlen, cast_ws=()):
    nb = seqlen // ROWS
    kblk, vblk = COL_K // KV_WIDTH, COL_V // KV_WIDTH
    srows = ATTN_HEADS * ROWS
    cur = lambda b, i: b * nb + i
    prev = lambda b, i: b * nb + jnp.maximum(i - 1, 0)
    const = lambda b, i: (0, 0)
    in_specs = [pl.BlockSpec((ROWS, ATTN_WIDTH), lambda b, i: (cur(b, i), 0)),
                pl.BlockSpec((ROWS, KV_WIDTH), lambda b, i: (cur(b, i), kblk)),
                pl.BlockSpec((ROWS, KV_WIDTH), lambda b, i: (prev(b, i), kblk)),
                pl.BlockSpec((ROWS, KV_WIDTH), lambda b, i: (cur(b, i), vblk)),
                pl.BlockSpec((ROWS, KV_WIDTH), lambda b, i: (prev(b, i), vblk)),
                pl.BlockSpec((1, HEAD_DIM), const),
                pl.BlockSpec((1, HEAD_DIM), const),
                pl.BlockSpec((srows, 2 * ROWS), const),
                pl.BlockSpec((srows, 1), const)]
    out_specs = [pl.BlockSpec((ROWS, ATTN_WIDTH), lambda b, i: (cur(b, i), 0)),
                 pl.BlockSpec((1, ROWS, KV_WIDTH), lambda b, i: (b, 0, 0)),
                 pl.BlockSpec((1, ROWS, KV_WIDTH), lambda b, i: (b, 0, 0))]
    out_shape = [jax.ShapeDtypeStruct((nseq * seqlen, ATTN_WIDTH), BF16),
                 jax.ShapeDtypeStruct((nseq, ROWS, KV_WIDTH), F32),
                 jax.ShapeDtypeStruct((nseq, ROWS, KV_WIDTH), F32)]
    args = [proj, proj, proj, proj, proj, qw, kw, bias16.reshape(srows, 2 * ROWS),
            jnp.repeat(sinks, ROWS).reshape(srows, 1)]
    rows = [_cast_job(w, nseq * nb) for w in cast_ws]
    hosted = bool(cast_ws) and all(r is not None for r in rows)
    if hosted:
        for w, r in zip(cast_ws, rows):
            ispec, ospec, oshape = _cast_specs(w, r, cur)
            in_specs.append(ispec)
            out_specs.append(ospec)
            out_shape.append(oshape)
            args.append(w)
    ncast = len(cast_ws) if hosted else 0
    outs = pl.pallas_call(
        functools.partial(_attn_prompt_body, ncast=ncast),
        grid=(nseq, nb),
        in_specs=in_specs,
        out_specs=out_specs,
        out_shape=out_shape,
        scratch_shapes=[pltpu.VMEM((ROWS, ATTN_WIDTH), F32),
                        pltpu.VMEM((2 * ROWS, KV_WIDTH), F32),
                        pltpu.VMEM((srows, 2 * ROWS), F32),
                        pltpu.VMEM((srows, 2 * ROWS), BF16)],
        compiler_params=_params(2, 48),
        name="attn_prompt",
    )(*args)
    return (*outs[:3], tuple(outs[3:]) if hosted else None)


def _attn_sample_body(q_ref, kn_ref, vn_ref, wk_ref, wv_ref, qw_ref, kw_ref, bias_ref, sink_ref,
                      o_ref, sk_ref, sv_ref, qn_scr, kn_scr, s_scr, p_scr, *, nsq, ntok):
    qw = qw_ref[...]
    kw = kw_ref[...]
    grows = Q_PER_KV * ntok
    keep = ROWS - ntok
    for h in range(ATTN_HEADS):
        hs = slice(h * HEAD_DIM, (h + 1) * HEAD_DIM)
        qn_scr[:, hs] = _rms_rows(q_ref[:, hs], qw)
    for kv in range(KV_HEADS):
        sl = slice(kv * HEAD_DIM, (kv + 1) * HEAD_DIM)
        kn_scr[:, sl] = _rms_rows(kn_ref[:, sl], kw)
    for s in range(nsq):
        rows = slice(s * ntok, (s + 1) * ntok)
        sk_ref[s, 0:keep] = wk_ref[s, ntok:ROWS]
        sv_ref[s, 0:keep] = wv_ref[s, ntok:ROWS]
        for kv in range(KV_HEADS):
            sl = slice(kv * HEAD_DIM, (kv + 1) * HEAD_DIM)
            sk_ref[s, keep:ROWS, kv, :] = kn_scr[rows, sl]
            sv_ref[s, keep:ROWS, kv, :] = vn_ref[rows, sl]

    pad = jnp.zeros((keep, HEAD_DIM), F32)
    for s in range(nsq):
        rows = slice(s * ntok, (s + 1) * ntok)
        for kv in range(KV_HEADS):
            sl = slice(kv * HEAD_DIM, (kv + 1) * HEAD_DIM)
            grp = s * KV_HEADS + kv
            qs = jnp.concatenate([qn_scr[rows, (kv * Q_PER_KV + g) * HEAD_DIM:(kv * Q_PER_KV + g + 1) * HEAD_DIM]
                                  for g in range(Q_PER_KV)], axis=0).astype(BF16)
            kcat = jnp.concatenate([wk_ref[s, :, kv, :], kn_scr[rows, sl], pad], axis=0).astype(BF16)
            s_scr[grp * grows:(grp + 1) * grows, :] = _dot_nt(qs, kcat)

    _softmax_sink(s_scr, bias_ref[...], sink_ref[...], p_scr)

    for s in range(nsq):
        rows = slice(s * ntok, (s + 1) * ntok)
        for kv in range(KV_HEADS):
            sl = slice(kv * HEAD_DIM, (kv + 1) * HEAD_DIM)
            grp = s * KV_HEADS + kv
            vcat = jnp.concatenate([wv_ref[s, :, kv, :], vn_ref[rows, sl], pad], axis=0).astype(BF16)
            o = _dot(p_scr[grp * grows:(grp + 1) * grows, :], vcat)
            for g in range(Q_PER_KV):
                h = kv * Q_PER_KV + g
                o_ref[rows, h * HEAD_DIM:(h + 1) * HEAD_DIM] = o[g * ntok:(g + 1) * ntok]


def _attn_sample(proj, win_k, win_v, sinks, qw, kw, bias16, nseq, ntok, nsq=8):
    kblk, vblk = COL_K // KV_WIDTH, COL_V // KV_WIDTH
    rows = nsq * ntok
    srows = nsq * ATTN_HEADS * ntok
    bias = jnp.tile(bias16[:, :ntok, :].reshape(ATTN_HEADS * ntok, 2 * ROWS), (nsq, 1))
    sink = jnp.tile(jnp.repeat(sinks, ntok), nsq).reshape(srows, 1)
    return pl.pallas_call(
        functools.partial(_attn_sample_body, nsq=nsq, ntok=ntok),
        grid=(nseq // nsq,),
        in_specs=[pl.BlockSpec((rows, ATTN_WIDTH), lambda i: (i, 0)),
                  pl.BlockSpec((rows, KV_WIDTH), lambda i: (i, kblk)),
                  pl.BlockSpec((rows, KV_WIDTH), lambda i: (i, vblk)),
                  pl.BlockSpec((nsq, ROWS, KV_HEADS, HEAD_DIM), lambda i: (i, 0, 0, 0)),
                  pl.BlockSpec((nsq, ROWS, KV_HEADS, HEAD_DIM), lambda i: (i, 0, 0, 0)),
                  pl.BlockSpec((1, HEAD_DIM), lambda i: (0, 0)),
                  pl.BlockSpec((1, HEAD_DIM), lambda i: (0, 0)),
                  pl.BlockSpec((srows, 2 * ROWS), lambda i: (0, 0)),
                  pl.BlockSpec((srows, 1), lambda i: (0, 0))],
        out_specs=[pl.BlockSpec((rows, ATTN_WIDTH), lambda i: (i, 0)),
                   pl.BlockSpec((nsq, ROWS, KV_HEADS, HEAD_DIM), lambda i: (i, 0, 0, 0)),
                   pl.BlockSpec((nsq, ROWS, KV_HEADS, HEAD_DIM), lambda i: (i, 0, 0, 0))],
        out_shape=[jax.ShapeDtypeStruct((nseq * ntok, ATTN_WIDTH), F32),
                   jax.ShapeDtypeStruct((nseq, ROWS, KV_HEADS, HEAD_DIM), F32),
                   jax.ShapeDtypeStruct((nseq, ROWS, KV_HEADS, HEAD_DIM), F32)],
        scratch_shapes=[pltpu.VMEM((rows, ATTN_WIDTH), F32),
                        pltpu.VMEM((rows, KV_WIDTH), F32),
                        pltpu.VMEM((srows, 2 * ROWS), F32),
                        pltpu.VMEM((srows, 2 * ROWS), BF16)],
        compiler_params=_params(1, 32),
        name="attn_sample",
    )(proj, proj, proj, win_k, win_v, qw, kw, bias, sink)


def _ssd_constants(qs):
    r = np.arange(ROWS)
    same = (r[:, None] // qs) == (r[None, :] // qs)
    mask = same & (r[None, :] <= r[:, None])
    sel = r[None, :] == ((r[:, None] // qs) * qs + qs - 1)
    expand = np.zeros((ROWS, SSD_WIDTH), np.float32)
    for h in range(SSD_HEADS):
        expand[h, h * SSD_HEAD_DIM:(h + 1) * SSD_HEAD_DIM] = 1.0
    to_bf = lambda a: jnp.asarray(a.astype(np.float32), BF16)
    return to_bf(mask), to_bf(sel), to_bf(expand), to_bf(expand.T)


def _ssd_body(xbc_ref, z_ref, dt_ref, cprev_ref, h0_ref, cw_ref, cb_ref, dtb_ref, alog_ref, dexp_ref,
              nw_ref, mask_ref, sel_ref, e_ref, et_ref, *rest, nsq, qs, cast):
    if cast:
        wsrc_ref, y_ref, ht_ref, wdst_ref = rest[:4]
        wdst_ref[...] = wsrc_ref[...].astype(BF16)
        rest = rest[4:]
    else:
        y_ref, ht_ref = rest[:2]
        rest = rest[2:]
    h_scr, cbuf, xc_scr, xdt_scr, xdd_scr, eacs_scr, yacc_scr, acs_scr, acst_scr, aclt_scr = rest
    c = pl.program_id(1)
    s = pl.program_id(2)

    @pl.when(c == 0)
    def _():
        h_scr[...] = h0_ref[0]

    if nsq == 1:
        @pl.when(c == 0)
        def _():
            cbuf[5:8, :] = cprev_ref[0]

    @pl.when(s == 0)
    def _chunk_phase():
        ncc = 8 if qs == ROWS else 1
        cwid = CONV_DIM // ncc
        for sq in range(nsq):
            for cc in range(ncc):
                cs = slice(cc * cwid, (cc + 1) * cwid)
                if nsq > 1:
                    cbuf[5:8, cs] = cprev_ref[sq, :, cs]
                cbuf[8:8 + qs, cs] = xbc_ref[sq * qs:(sq + 1) * qs, cs]
                acc = cb_ref[:, cs] + cw_ref[0:1, cs] * cbuf[5:5 + qs, cs]
                for t in range(1, SSD_CONV):
                    acc = acc + cw_ref[t:t + 1, cs] * cbuf[5 + t:5 + t + qs, cs]
                xc_scr[sq * qs:(sq + 1) * qs, cs] = acc * _sigmoid(acc)
                if nsq == 1:
                    cbuf[5:8, cs] = xbc_ref[qs - 3:qs, cs]

        x = dt_ref[...] + dtb_ref[...]
        dt = jnp.maximum(x, 0.0) + jnp.log1p(jnp.exp(-jnp.abs(x)))
        da = dt * (-jnp.exp(alog_ref[...]))
        mask_bf = mask_ref[...]
        acs = _dot3_r(mask_bf, _split3(da))
        acl = _dot3_r(sel_ref[...], _split3(acs))
        acs_scr[...] = acs
        acst_scr[...] = acs.T
        aclt_scr[...] = acl.T
        e = e_ref[...]
        dt_e = _dot3_l(dt, e)
        acs_e = _dot3_l(acs, e)
        acl_e = _dot3_l(acl, e)
        xs = xc_scr[:, 0:SSD_WIDTH]
        xdt = xs * dt_e
        xdt_scr[...] = xdt
        xdd_scr[...] = xdt * jnp.exp(acl_e - acs_e)
        eacs_scr[...] = jnp.exp(acs_e)

        maskb = mask_bf > 0.5
        lane = lax.broadcasted_iota(jnp.int32, (ROWS, 2 * SSD_HEAD_DIM), 1)
        for g in range(SSD_GROUPS):
            bg = xc_scr[:, SSD_WIDTH + g * SSD_STATE:SSD_WIDTH + (g + 1) * SSD_STATE].astype(BF16)
            cg = xc_scr[:, SSD_WIDTH + (SSD_GROUPS + g) * SSD_STATE:
                        SSD_WIDTH + (SSD_GROUPS + g + 1) * SSD_STATE].astype(BF16)
            cb = _dot_nt(cg, bg)
            for pp in range(2):
                ms = []
                for j in range(2):
                    h = 4 * g + 2 * pp + j
                    seg = acs_scr[:, h:h + 1] - acst_scr[h:h + 1, :]
                    ms.append((cb * jnp.exp(jnp.where(maskb, seg, NEG))).astype(BF16))
                c0 = (4 * g + 2 * pp) * SSD_HEAD_DIM
                cols = slice(c0, c0 + 2 * SSD_HEAD_DIM)
                xp = xdt_scr[:, cols]
                rhs = jnp.concatenate([jnp.where(lane < SSD_HEAD_DIM, xp, 0.0),
                                       jnp.where(lane >= SSD_HEAD_DIM, xp, 0.0)], axis=0).astype(BF16)
                yd = _dot(jnp.concatenate(ms, axis=1), rhs)
                yacc_scr[:, cols] = yd + dexp_ref[:, cols] * xc_scr[:, cols]

    if nsq == 1:
        r0 = 0
        rows = slice(0, qs)
    else:
        r0 = pl.multiple_of(s * qs, qs)
        rows = pl.ds(r0, qs)
        ridx = lax.broadcasted_iota(jnp.int32, (ROWS, 1), 0)
        rowmask = jnp.logical_and(ridx >= r0, ridx < r0 + qs)
    tlane = lax.broadcasted_iota(jnp.int32, (ROWS, ROWS), 1)
    xcol = jnp.sum(jnp.where(tlane == r0, aclt_scr[...], 0.0), axis=1, keepdims=True)
    x3 = _split3(jnp.broadcast_to(xcol, (ROWS, ROWS)))
    c_rows = xc_scr[rows, SSD_WIDTH + SSD_GROUPS * SSD_STATE:CONV_DIM].astype(BF16)
    yo, st = [], []
    for g in range(SSD_GROUPS):
        grows = slice(g * GROUP_ROWS, (g + 1) * GROUP_ROWS)
        yo.append(_dot_nt(c_rows[:, g * SSD_STATE:(g + 1) * SSD_STATE], h_scr[grows, :].astype(BF16)))
        bg = xc_scr[:, SSD_WIDTH + g * SSD_STATE:SSD_WIDTH + (g + 1) * SSD_STATE]
        if nsq > 1:
            bg = jnp.where(rowmask, bg, 0.0)
        st.append(_dot_tn(xdd_scr[:, grows].astype(BF16), bg.astype(BF16)))
    yacc_scr[rows, :] = yacc_scr[rows, :] + jnp.concatenate(yo, axis=1) * eacs_scr[rows, :]
    dec = jnp.exp(_dot3_r(et_ref[...], x3))
    h_new = dec * h_scr[...] + jnp.concatenate(st, axis=0)
    h_scr[...] = h_new
    ht_ref[0] = h_new

    @pl.when(s == nsq - 1)
    def _finish():
        z = z_ref[...]
        yg = yacc_scr[...] * (z * _sigmoid(z))
        y_ref[...] = _rms_rows(yg, nw_ref[...]).astype(y_ref.dtype)


def _ssd(proj, dt_raw, conv_prev, h0, cw, cb, dtb, alog, dexp, nw, nseq, seqlen, cast_w=None):
    if seqlen % ROWS == 0:
        qs, nsq, nchunk, ngrp = ROWS, 1, seqlen // ROWS, nseq
    else:
        qs, nsq, nchunk = seqlen, ROWS // seqlen, 1
        ngrp = nseq // nsq
    mask, sel, expand, expand_t = _ssd_constants(qs)
    rowblk = lambda b, c, s: b * nchunk + c
    seq = lambda b, c, s: b * nsq + s
    const2 = lambda b, c, s: (0, 0)
    full = lambda shape: pl.BlockSpec(shape, const2)
    in_specs = [pl.BlockSpec((ROWS, CONV_DIM), lambda b, c, s: (rowblk(b, c, s), COL_XBC // CONV_DIM)),
                pl.BlockSpec((ROWS, SSD_WIDTH), lambda b, c, s: (rowblk(b, c, s), COL_Z // SSD_WIDTH)),
                pl.BlockSpec((ROWS, ROWS), lambda b, c, s: (rowblk(b, c, s), 0)),
                pl.BlockSpec((nsq, SSD_CONV - 1, CONV_DIM), lambda b, c, s: (b, 0, 0)),
                pl.BlockSpec((1, SSD_WIDTH, SSD_STATE), lambda b, c, s: (seq(b, c, s), 0, 0)),
                full((SSD_CONV, CONV_DIM)), full((1, CONV_DIM)), full((1, ROWS)), full((1, ROWS)),
                full((1, SSD_WIDTH)), full((1, SSD_WIDTH)),
                full((ROWS, ROWS)), full((ROWS, ROWS)), full((ROWS, SSD_WIDTH)), full((SSD_WIDTH, ROWS))]
    out_specs = [pl.BlockSpec((ROWS, SSD_WIDTH), lambda b, c, s: (rowblk(b, c, s), 0)),
                 pl.BlockSpec((1, SSD_WIDTH, SSD_STATE), lambda b, c, s: (seq(b, c, s), 0, 0))]
    out_shape = [jax.ShapeDtypeStruct((nseq * seqlen, SSD_WIDTH), BF16),
                 jax.ShapeDtypeStruct((nseq, SSD_WIDTH, SSD_STATE), F32)]
    args = [proj, proj, dt_raw, conv_prev, h0, cw, cb, dtb, alog, dexp, nw, mask, sel, expand, expand_t]
    rows = None if cast_w is None else _cast_job(cast_w, ngrp * nchunk * nsq)
    if rows is not None:
        ispec, ospec, oshape = _cast_specs(cast_w, rows, lambda b, c, s: (b * nchunk + c) * nsq + s)
        in_specs.append(ispec)
        out_specs.append(ospec)
        out_shape.append(oshape)
        args.append(cast_w)
    outs = pl.pallas_call(
        functools.partial(_ssd_body, nsq=nsq, qs=qs, cast=rows is not None),
        grid=(ngrp, nchunk, nsq),
        in_specs=in_specs,
        out_specs=out_specs,
        out_shape=out_shape,
        scratch_shapes=[pltpu.VMEM((SSD_WIDTH, SSD_STATE), F32),
                        pltpu.VMEM((8 + ROWS, CONV_DIM), F32),
                        pltpu.VMEM((ROWS, CONV_DIM), F32),
                        pltpu.VMEM((ROWS, SSD_WIDTH), F32),
                        pltpu.VMEM((ROWS, SSD_WIDTH), F32),
                        pltpu.VMEM((ROWS, SSD_WIDTH), F32),
                        pltpu.VMEM((ROWS, SSD_WIDTH), F32),
                        pltpu.VMEM((ROWS, ROWS), F32),
                        pltpu.VMEM((ROWS, ROWS), F32),
                        pltpu.VMEM((ROWS, ROWS), F32)],
        compiler_params=_params(3, 48),
        name="ssd",
    )(*args)
    return outs if rows is not None else (*outs, None)


def _ffn_gate_body(xw_ref, ssq_ref, wg_ref, wu_ref, cw_ref, cb_ref, *rest, period, tiles_per_seq, cast=False):
    if cast:
        wsrc_ref, a_ref, last_ref, wdst_ref, carry_scr = rest
        wdst_ref[...] = wsrc_ref[...].astype(BF16)
    elif period is None:
        a_ref, last_ref, carry_scr = rest
    else:
        prev_ref, a_ref, g_ref = rest
    r = lax.rsqrt(jnp.sum(ssq_ref[...], axis=0) * (1.0 / D_MODEL) + EPS)
    xw = xw_ref[...]
    tn = wg_ref.shape[1]
    col = pl.program_id(1) * tn + lax.broadcasted_iota(jnp.int32, (1, tn), 1)
    g = _dot(xw, jnp.where(col < D_FF, wg_ref[...], 0)) * r
    u = _dot(xw, jnp.where(col < D_FF, wu_ref[...], 0)) * r
    tm = g.shape[0]
    row = lax.broadcasted_iota(jnp.int32, g.shape, 0)
    if period is None:
        i, j = pl.program_id(0), pl.program_id(1)
        c8 = jnp.where(i % tiles_per_seq == 0, 0.0, carry_scr[j])
        carry_scr[j] = g[tm - 8:tm]
        last_ref[0] = g[tm - 2:tm]
        g_m1 = jnp.where(row == 0, c8[7:8], pltpu.roll(g, 1, 0))
        g_m2 = jnp.where(row == 0, c8[6:7], jnp.where(row == 1, c8[7:8], pltpu.roll(g, 2, 0)))
    else:
        g_ref[...] = g
        prev = prev_ref[...]
        t = row & (period - 1)
        g_m1 = jnp.where(t == 0, pltpu.roll(prev, tm - 1, 0), pltpu.roll(g, 1, 0))
        g_m2 = jnp.where(t < 2, prev, pltpu.roll(g, 2, 0))
    gc = cb_ref[...] + cw_ref[0:1] * g_m2
    gc = gc + cw_ref[1:2] * g_m1
    gc = gc + cw_ref[2:3] * g
    a_ref[...] = ((gc * _sigmoid(gc)) * u).astype(a_ref.dtype)


def _ffn_gate(xw, ssq, w_gate, w_up, cw, cb, *, nseq, seqlen, prev_rows=None, cast_w=None, tm=1024, tn=512):
    m = xw.shape[0]
    nj = D_FF_PAD // tn
    nparts = ssq.shape[0]
    wspec = pl.BlockSpec((D_MODEL, tn), lambda i, j: (0, j))
    cwspec = pl.BlockSpec((FFN_CONV, tn), lambda i, j: (0, j))
    cbspec = pl.BlockSpec((1, tn), lambda i, j: (0, j))
    if prev_rows is None:
        tm = min(tm, seqlen)
        tps = seqlen // tm
        in_specs = [pl.BlockSpec((tm, D_MODEL), lambda i, j: (i, 0)),
                    pl.BlockSpec((nparts, tm, 1), lambda i, j: (0, i, 0)), wspec, wspec, cwspec, cbspec]
        out_specs = [pl.BlockSpec((tm, tn), lambda i, j: (i, j)),
                     pl.BlockSpec((1, FFN_CONV - 1, tn), lambda i, j: (i, 0, j))]
        out_shape = [jax.ShapeDtypeStruct((m, D_FF_PAD), BF16),
                     jax.ShapeDtypeStruct((m // tm, FFN_CONV - 1, D_FF_PAD), F32)]
        scratch = [pltpu.VMEM((nj, 8, tn), F32)]
        args = (xw, ssq, w_gate, w_up, cw, cb)
        rows = None if cast_w is None else _cast_job(cast_w, (m // tm) * nj)
        if rows is not None:
            ispec, ospec, oshape = _cast_specs(cast_w, rows, lambda i, j: i * nj + j)
            in_specs.append(ispec)
            out_specs.append(ospec)
            out_shape.append(oshape)
            args = args + (cast_w,)
        body = functools.partial(_ffn_gate_body, period=None, tiles_per_seq=tps, cast=rows is not None)
    else:
        assert seqlen & (seqlen - 1) == 0
        tm = m
        in_specs = [pl.BlockSpec((tm, D_MODEL), lambda i, j: (i, 0)),
                    pl.BlockSpec((nparts, tm, 1), lambda i, j: (0, i, 0)), wspec, wspec, cwspec, cbspec,
                    pl.BlockSpec((tm, tn), lambda i, j: (i, j))]
        out_specs = [pl.BlockSpec((tm, tn), lambda i, j: (i, j)),
                     pl.BlockSpec((tm, tn), lambda i, j: (i, j))]
        out_shape = [jax.ShapeDtypeStruct((m, D_FF_PAD), BF16),
                     jax.ShapeDtypeStruct((m, D_FF_PAD), F32)]
        scratch = []
        args = (xw, ssq, w_gate, w_up, cw, cb, prev_rows)
        body = functools.partial(_ffn_gate_body, period=seqlen, tiles_per_seq=None)
    outs = pl.pallas_call(
        body,
        grid=(m // tm, nj),
        in_specs=in_specs,
        out_specs=out_specs,
        out_shape=out_shape,
        scratch_shapes=scratch,
        compiler_params=_params(2),
        name="ffn_gate",
    )(*args)
    return outs if len(outs) == 3 else (*outs, None)


def _down_body(a_ref, w_ref, res_ref, o_ref, *, chunk):
    @pl.when(pl.program_id(2) == 0)
    def _():
        o_ref[...] = res_ref[...]

    a = a_ref[...]
    tk = w_ref.shape[0]
    row = pl.program_id(2) * tk + lax.broadcasted_iota(jnp.int32, (tk, 1), 0)
    for c in range(o_ref.shape[1] // chunk):
        cs = slice(c * chunk, (c + 1) * chunk)
        o_ref[:, cs] += _dot(a, jnp.where(row < D_FF, w_ref[:, cs], 0))


def _down_proj(a, w_down, res, *, tm=1024, tn=2048, tk=1024):
    m, kdim = a.shape
    n = w_down.shape[1]
    tm = min(tm, m)
    return pl.pallas_call(
        functools.partial(_down_body, chunk=512),
        grid=(n // tn, m // tm, kdim // tk),
        in_specs=[pl.BlockSpec((tm, tk), lambda j, i, k: (i, k)),
                  pl.BlockSpec((tk, tn), lambda j, i, k: (k, j)),
                  pl.BlockSpec((tm, tn), lambda j, i, k: (i, j))],
        out_specs=pl.BlockSpec((tm, tn), lambda j, i, k: (i, j)),
        out_shape=jax.ShapeDtypeStruct((m, n), F32),
        compiler_params=_params(3),
        name="down_proj",
    )(a, w_down, res)


def _remember_bf16(bf16_w, wts, name, hosted):
    if name not in bf16_w:
        bf16_w[name] = hosted if hosted is not None else wts[name].astype(BF16)


def _layer(x, nseq, seqlen, wts, bf16_w, bias16, attn_state, ssm_state, conv_state, ffn_state):
    m = x.shape[0]
    todo = lambda name: None if name in bf16_w else wts[name]
    h = _rmsnorm(x, wts["mix_norm_w"])
    proj, dt_raw = _in_proj(h, wts["w_in_t"])

    qw = wts["q_norm_w"].reshape(1, HEAD_DIM)
    kw = wts["k_norm_w"].reshape(1, HEAD_DIM)
    if attn_state is None:
        names = [n for n in ("w_gate", "w_out") if n not in bf16_w]
        attn, new_k, new_v, hosted = _attn_prompt(proj, wts["attn_sinks"], qw, kw, bias16, nseq, seqlen,
                                                  cast_ws=tuple(wts[n] for n in names))
        for k, n in enumerate(names):
            _remember_bf16(bf16_w, wts, n, None if hosted is None else hosted[k])
        new_k = new_k.reshape(nseq, WINDOW, KV_HEADS, HEAD_DIM)
        new_v = new_v.reshape(nseq, WINDOW, KV_HEADS, HEAD_DIM)
        conv_prev = jnp.zeros((nseq, SSD_CONV - 1, CONV_DIM), F32)
        h0 = jnp.zeros((nseq, SSD_WIDTH, SSD_STATE), F32)
    else:
        win_k, win_v = attn_state
        attn, new_k, new_v = _attn_sample(proj, win_k, win_v, wts["attn_sinks"], qw, kw, bias16, nseq, seqlen)
        _remember_bf16(bf16_w, wts, "w_gate", None)
        _remember_bf16(bf16_w, wts, "w_out", None)
        conv_prev = conv_state
        h0 = ssm_state.reshape(nseq, SSD_WIDTH, SSD_STATE)

    y, h_t, hosted = _ssd(proj, dt_raw, conv_prev, h0, wts["ssd_conv_w"], wts["ssd_conv_b"], wts["ssd_dt_bias"],
                          wts["ssd_A_log"], wts["ssd_D"], wts["ssd_norm_w"], nseq, seqlen, cast_w=todo("w_up"))
    _remember_bf16(bf16_w, wts, "w_up", hosted)
    h_t = h_t.reshape(nseq, SSD_HEADS, SSD_HEAD_DIM, SSD_STATE)
    new_conv = proj.reshape(nseq, seqlen, PROJ_W)[:, seqlen - (SSD_CONV - 1):, COL_XBC:COL_XBC + CONV_DIM]

    x1, xw, ssq = _out_proj(attn, y, bf16_w["w_out"], x, wts["ffn_norm_w"])
    if ffn_state is None:
        a, last, hosted = _ffn_gate(xw, ssq, bf16_w["w_gate"], bf16_w["w_up"], wts["ffn_conv_w"],
                                    wts["ffn_conv_b"], nseq=nseq, seqlen=seqlen, cast_w=todo("w_down"))
        new_ffn = last.reshape(nseq, -1, FFN_CONV - 1, D_FF_PAD)[:, -1, :, :D_FF]
    else:
        prev_rows = jnp.pad(ffn_state, ((0, 0), (0, seqlen - (FFN_CONV - 1)), (0, D_FF_PAD - D_FF)))
        a, g, hosted = _ffn_gate(xw, ssq, bf16_w["w_gate"], bf16_w["w_up"], wts["ffn_conv_w"], wts["ffn_conv_b"],
                                 nseq=nseq, seqlen=seqlen, prev_rows=prev_rows.reshape(m, D_FF_PAD))
        new_ffn = g.reshape(nseq, seqlen, D_FF_PAD)[:, seqlen - (FFN_CONV - 1):, :D_FF]
    _remember_bf16(bf16_w, wts, "w_down", hosted)
    x2 = _down_proj(a, bf16_w["w_down"], x1)
    return x2, new_k, new_v, h_t, new_conv, new_ffn


def kernel(x_prompt, x_sample, state_attn_k, state_attn_v, state_ssm, state_ssd_conv, state_ffn_conv, rel_bias, mix_norm_w, w_in, q_norm_w, k_norm_w, attn_sinks, ssd_conv_w, ssd_conv_b, ssd_dt_bias, ssd_A_log, ssd_D, ssd_norm_w, w_out, ffn_norm_w, w_gate, w_up, ffn_conv_w, ffn_conv_b, w_down):
    depth = w_in.shape[0]
    bp, lp, _ = x_prompt.shape
    bs, ls, _ = x_sample.shape
    bias16 = _bias_table(rel_bias)
    yp = x_prompt.reshape(bp * lp, D_MODEL)
    ys = x_sample.reshape(bs * ls, D_MODEL)
    outs_p, outs_s = [], []
    pad_heads = lambda v: jnp.pad(v.reshape(1, SSD_HEADS), ((0, 0), (0, ROWS - SSD_HEADS)))
    pad_ff = lambda v: jnp.pad(v, ((0, 0), (0, D_FF_PAD - D_FF)))
    for l in range(depth):
        wts = dict(
            mix_norm_w=mix_norm_w[l],
            w_in_t=w_in[l].T,

            q_norm_w=q_norm_w[l], k_norm_w=k_norm_w[l], attn_sinks=attn_sinks[l],
            ssd_conv_w=ssd_conv_w[l], ssd_conv_b=ssd_conv_b[l].reshape(1, CONV_DIM),
            ssd_dt_bias=pad_heads(ssd_dt_bias[l]), ssd_A_log=pad_heads(ssd_A_log[l]),
            ssd_D=jnp.repeat(ssd_D[l], SSD_HEAD_DIM).reshape(1, SSD_WIDTH),
            ssd_norm_w=ssd_norm_w[l].reshape(1, SSD_WIDTH),
            w_out=w_out[l], ffn_norm_w=ffn_norm_w[l],
            w_gate=w_gate[l], w_up=w_up[l],
            ffn_conv_w=pad_ff(ffn_conv_w[l]), ffn_conv_b=pad_ff(ffn_conv_b[l].reshape(1, D_FF)),
            w_down=w_down[l],
        )
        bf16_w = {}
        yp, *sp = _layer(yp, bp, lp, wts, bf16_w, bias16, None, None, None, None)
        ys, *ss = _layer(ys, bs, ls, wts, bf16_w, bias16, (state_attn_k[l], state_attn_v[l]), state_ssm[l],
                         state_ssd_conv[l], state_ffn_conv[l])
        outs_p.append(sp)
        outs_s.append(ss)
    stack = lambda outs, i: jnp.stack([o[i] for o in outs])
    return (yp.reshape(bp, lp, D_MODEL), ys.reshape(bs, ls, D_MODEL),
            stack(outs_p, 0), stack(outs_p, 1), stack(outs_p, 2), stack(outs_p, 3), stack(outs_p, 4),
            stack(outs_s, 0), stack(outs_s, 1), stack(outs_s, 2), stack(outs_s, 3), stack(outs_s, 4))
```

```python
import functools
import math

import numpy as np
import jax
import jax.numpy as jnp
from jax import lax
from jax.experimental import pallas as pl
from jax.experimental.pallas import tpu as pltpu

F32 = jnp.float32
BF16 = jnp.bfloat16

D_MODEL = 4096
HEAD_DIM = 128
ATTN_HEADS = 16
KV_HEADS = 4
Q_PER_KV = 4
ATTN_WIDTH = 2048
WINDOW = 128
N_BUCKETS = 32
MAX_DISTANCE = 128
SSD_HEAD_DIM = 64
SSD_HEADS = 32
SSD_WIDTH = 2048
SSD_GROUPS = 8
SSD_STATE = 128
SSD_CONV = 4
CONV_DIM = 4096
KV_WIDTH = KV_HEADS * HEAD_DIM
D_FF = 11008
D_FF_PAD = 11264
FFN_CONV = 3
EPS = 1e-6
NEG = -1e30
SCALE = HEAD_DIM ** -0.5

ROWS = 128
GROUP_ROWS = 4 * SSD_HEAD_DIM
COL_Z = ATTN_WIDTH
COL_XBC = COL_Z + SSD_WIDTH
COL_K = COL_XBC + CONV_DIM
COL_V = COL_K + KV_WIDTH
PROJ_W = COL_V + KV_WIDTH
VMEM_LIMIT_MB = 56


def _params(n_axes, vmem_mb=VMEM_LIMIT_MB):
    return pltpu.CompilerParams(dimension_semantics=("arbitrary",) * n_axes,
                                vmem_limit_bytes=vmem_mb << 20)


def _dot(a, b):
    return jnp.dot(a, b, preferred_element_type=F32)


def _dot_nt(a, b):
    return lax.dot_general(a, b, (((1,), (1,)), ((), ())), preferred_element_type=F32)


def _dot_tn(a, b):
    return lax.dot_general(a, b, (((0,), (0,)), ((), ())), preferred_element_type=F32)


def _split3(x):
    a = x.astype(BF16)
    r = x - a.astype(F32)
    b = r.astype(BF16)
    c = (r - b.astype(F32)).astype(BF16)
    return a, b, c


def _dot3_l(x, w):
    a, b, c = _split3(x)
    return (_dot(a, w) + _dot(b, w)) + _dot(c, w)


def _dot3_r(w, x3):
    return (_dot(w, x3[0]) + _dot(w, x3[1])) + _dot(w, x3[2])


def _sigmoid(x):
    return 1.0 / (1.0 + jnp.exp(-x))


def _rms_rows(x, w):
    r = lax.rsqrt(jnp.mean(x * x, axis=-1, keepdims=True) + EPS)
    return (x * r) * w


def _rmsnorm_body(x_ref, w_ref, o_ref):
    o_ref[...] = _rms_rows(x_ref[...], w_ref[...]).astype(o_ref.dtype)


def _rmsnorm(x, w, rows=256):
    m, d = x.shape
    rows = min(rows, m)
    return pl.pallas_call(
        _rmsnorm_body,
        grid=(m // rows,),
        in_specs=[pl.BlockSpec((rows, d), lambda i: (i, 0)),
                  pl.BlockSpec((1, d), lambda i: (0, 0))],
        out_specs=pl.BlockSpec((rows, d), lambda i: (i, 0)),
        out_shape=jax.ShapeDtypeStruct((m, d), BF16),
        compiler_params=_params(1, 32),
        name="rmsnorm",
    )(x, w.reshape(1, d))


def _in_proj_body(a_ref, bt_ref, dtw_ref, o_ref, dt_ref):
    a = a_ref[...]
    o_ref[...] = _dot_nt(a, bt_ref[...].astype(BF16))

    @pl.when(pl.program_id(1) == 0)
    def _():
        row = lax.broadcasted_iota(jnp.int32, (dtw_ref.shape[0], 1), 0)
        dtw = jnp.where(row < SSD_HEADS, dtw_ref[...], 0.0)
        dt_ref[...] = _dot_nt(a, dtw.astype(BF16))


def _in_proj(h, w_in_t, *, tm=1024, tn=512):
    m = h.shape[0]
    tm = min(tm, m)
    nq, nkv = ATTN_WIDTH // tn, KV_WIDTH // tn
    nzx = (SSD_WIDTH + CONV_DIM) // tn

    def src_block(j):
        return jnp.where(j < nq, j, jnp.where(j < nq + nzx, j + 2 * nkv, j - nzx))

    return pl.pallas_call(
        _in_proj_body,
        grid=(m // tm, PROJ_W // tn),
        in_specs=[pl.BlockSpec((tm, D_MODEL), lambda i, j: (i, 0)),
                  pl.BlockSpec((tn, D_MODEL), lambda i, j: (src_block(j), 0)),
                  pl.BlockSpec((ROWS, D_MODEL), lambda i, j: (PROJ_W // ROWS, 0))],
        out_specs=[pl.BlockSpec((tm, tn), lambda i, j: (i, j)),
                   pl.BlockSpec((tm, ROWS), lambda i, j: (i, 0))],
        out_shape=[jax.ShapeDtypeStruct((m, PROJ_W), F32),
                   jax.ShapeDtypeStruct((m, ROWS), F32)],
        compiler_params=_params(2),
        name="in_proj",
    )(h, w_in_t, w_in_t)


def _out_proj_body(a1_ref, a2_ref, b1_ref, b2_ref, res_ref, nw_ref, o_ref, xw_ref, ssq_ref):
    acc = (_dot(a1_ref[...].astype(BF16), b1_ref[...].astype(BF16))
           + _dot(a2_ref[...].astype(BF16), b2_ref[...].astype(BF16)))
    x1 = res_ref[...] + acc
    o_ref[...] = x1
    xw_ref[...] = (x1 * nw_ref[...]).astype(xw_ref.dtype)
    ssq_ref[0] = jnp.sum(x1 * x1, axis=-1, keepdims=True)


def _out_proj(attn, y, w_out, res, norm_w, *, tm=1024, tn=512):
    m = attn.shape[0]
    n = w_out.shape[1]
    tm = min(tm, m)
    return pl.pallas_call(
        _out_proj_body,
        grid=(m // tm, n // tn),
        in_specs=[pl.BlockSpec((tm, ATTN_WIDTH), lambda i, j: (i, 0)),
                  pl.BlockSpec((tm, SSD_WIDTH), lambda i, j: (i, 0)),
                  pl.BlockSpec((ATTN_WIDTH, tn), lambda i, j: (0, j)),
                  pl.BlockSpec((SSD_WIDTH, tn), lambda i, j: (1, j)),
                  pl.BlockSpec((tm, tn), lambda i, j: (i, j)),
                  pl.BlockSpec((1, tn), lambda i, j: (0, j))],
        out_specs=[pl.BlockSpec((tm, tn), lambda i, j: (i, j)),
                   pl.BlockSpec((tm, tn), lambda i, j: (i, j)),
                   pl.BlockSpec((1, tm, 1), lambda i, j: (j, i, 0))],
        out_shape=[jax.ShapeDtypeStruct((m, n), F32),
                   jax.ShapeDtypeStruct((m, n), BF16),
                   jax.ShapeDtypeStruct((n // tn, m, 1), F32)],
        compiler_params=_params(2),
        name="out_proj",
    )(attn, y, w_out, w_out, res, norm_w.reshape(1, n))


def _bucket_table():
    i = np.arange(ROWS)[:, None]
    j = np.arange(2 * ROWS)[None, :]
    dist = i + ROWS - j
    valid = (dist >= 0) & (dist < WINDOW)
    n = np.maximum(dist, 0)
    max_exact = N_BUCKETS // 2

    def large(dtype):
        nf = np.maximum(n, 1).astype(dtype)
        v = np.log(nf / dtype(max_exact)) / dtype(math.log(MAX_DISTANCE / max_exact)) * dtype(N_BUCKETS - max_exact)
        return np.minimum(max_exact + v.astype(np.int32), N_BUCKETS - 1)

    assert (large(np.float32) == large(np.float64)).all()
    bucket = np.where(n < max_exact, n, large(np.float32))
    return np.where(valid, bucket, -1).astype(np.int32)


def _bias_body(rb_ref, bkt_ref, o_ref):
    h = pl.program_id(0)
    bkt = bkt_ref[...]
    acc = jnp.full(bkt.shape, NEG, F32)
    for b in range(N_BUCKETS):
        acc = jnp.where(bkt == b, rb_ref[b, h], acc)
    o_ref[0] = acc


def _bias_table(rel_bias):
    return pl.pallas_call(
        _bias_body,
        grid=(ATTN_HEADS,),
        in_specs=[pl.BlockSpec(memory_space=pltpu.SMEM),
                  pl.BlockSpec((ROWS, 2 * ROWS), lambda h: (0, 0))],
        out_specs=pl.BlockSpec((1, ROWS, 2 * ROWS), lambda h: (h, 0, 0)),
        out_shape=jax.ShapeDtypeStruct((ATTN_HEADS, ROWS, 2 * ROWS), F32),
        compiler_params=_params(1, 16),
        name="bias_table",
    )(rel_bias, jnp.asarray(_bucket_table()))


def _cast_job(w, nsteps):
    rows = -(-w.shape[0] // nsteps)
    rows = -(-rows // 16) * 16
    return rows if rows * w.shape[1] * 4 <= (8 << 20) else None


def _cast_specs(w, rows, step_of):
    nblk = -(-w.shape[0] // rows)
    spec = pl.BlockSpec((rows, w.shape[1]), lambda *g: (jnp.minimum(step_of(*g), nblk - 1), 0))
    return spec, spec, jax.ShapeDtypeStruct(w.shape, BF16)


def _softmax_sink(s_scr, bias, sink, p_scr):
    sc = s_scr[...] * SCALE + bias
    m = jnp.maximum(jnp.max(sc, axis=-1, keepdims=True), sink)
    p = jnp.exp(sc - m)
    den = jnp.sum(p, axis=-1, keepdims=True) + jnp.exp(sink - m)
    p_scr[...] = (p * (1.0 / den)).astype(BF16)


def _attn_prompt_body(q_ref, kc_ref, kp_ref, vc_ref, vp_ref, qw_ref, kw_ref, bias_ref, sink_ref, *rest, ncast):
    o_ref, pk_ref, pv_ref = rest[ncast:ncast + 3]
    for src_ref, dst_ref in zip(rest[:ncast], rest[ncast + 3:2 * ncast + 3]):
        dst_ref[...] = src_ref[...].astype(BF16)
    qn_scr, kn_scr, s_scr, p_scr = rest[2 * ncast + 3:]
    qw = qw_ref[...]
    kw = kw_ref[...]
    grows = Q_PER_KV * ROWS
    for h in range(ATTN_HEADS):
        hs = slice(h * HEAD_DIM, (h + 1) * HEAD_DIM)
        qn_scr[:, hs] = _rms_rows(q_ref[:, hs], qw)
    for kv in range(KV_HEADS):
        sl = slice(kv * HEAD_DIM, (kv + 1) * HEAD_DIM)
        kn_scr[0:ROWS, sl] = _rms_rows(kp_ref[:, sl], kw)
        kn_scr[ROWS:2 * ROWS, sl] = _rms_rows(kc_ref[:, sl], kw)
    pk_ref[0] = kn_scr[ROWS:2 * ROWS, :]
    pv_ref[0] = vc_ref[...]

    for kv in range(KV_HEADS):
        sl = slice(kv * HEAD_DIM, (kv + 1) * HEAD_DIM)
        qs = jnp.concatenate([qn_scr[:, (kv * Q_PER_KV + g) * HEAD_DIM:(kv * Q_PER_KV + g + 1) * HEAD_DIM]
                              for g in range(Q_PER_KV)], axis=0).astype(BF16)
        s_scr[kv * grows:(kv + 1) * grows, :] = _dot_nt(qs, kn_scr[:, sl].astype(BF16))

    col = lax.broadcasted_iota(jnp.int32, (1, 2 * ROWS), 1)
    no_prev = jnp.where(jnp.logical_and(pl.program_id(1) == 0, col < ROWS), NEG, 0.0)
    _softmax_sink(s_scr, bias_ref[...] + no_prev, sink_ref[...], p_scr)

    for kv in range(KV_HEADS):
        sl = slice(kv * HEAD_DIM, (kv + 1) * HEAD_DIM)
        vcat = jnp.concatenate([vp_ref[:, sl], vc_ref[:, sl]], axis=0).astype(BF16)
        o = _dot(p_scr[kv * grows:(kv + 1) * grows, :], vcat)
        for g in range(Q_PER_KV):
            h = kv * Q_PER_KV + g
            o_ref[:, h * HEAD_DIM:(h + 1) * HEAD_DIM] = o[g * ROWS:(g + 1) * ROWS].astype(o_ref.dtype)


def _attn_prompt(proj, sinks, qw, kw, bias16, nseq, seqlen, cast_ws=()):
    nb = seqlen // ROWS
    kblk, vblk = COL_K // KV_WIDTH, COL_V // KV_WIDTH
    srows = ATTN_HEADS * ROWS
    cur = lambda b, i: b * nb + i
    prev = lambda b, i: b * nb + jnp.maximum(i - 1, 0)
    const = lambda b, i: (0, 0)
    in_specs = [pl.BlockSpec((ROWS, ATTN_WIDTH), lambda b, i: (cur(b, i), 0)),
                pl.BlockSpec((ROWS, KV_WIDTH), lambda b, i: (cur(b, i), kblk)),
                pl.BlockSpec((ROWS, KV_WIDTH), lambda b, i: (prev(b, i), kblk)),
                pl.BlockSpec((ROWS, KV_WIDTH), lambda b, i: (cur(b, i), vblk)),
                pl.BlockSpec((ROWS, KV_WIDTH), lambda b, i: (prev(b, i), vblk)),
                pl.BlockSpec((1, HEAD_DIM), const),
                pl.BlockSpec((1, HEAD_DIM), const),
                pl.BlockSpec((srows, 2 * ROWS), const),
                pl.BlockSpec((srows, 1), const)]
    out_specs = [pl.BlockSpec((ROWS, ATTN_WIDTH), lambda b, i: (cur(b, i), 0)),
                 pl.BlockSpec((1, ROWS, KV_WIDTH), lambda b, i: (b, 0, 0)),
                 pl.BlockSpec((1, ROWS, KV_WIDTH), lambda b, i: (b, 0, 0))]
    out_shape = [jax.ShapeDtypeStruct((nseq * seqlen, ATTN_WIDTH), BF16),
                 jax.ShapeDtypeStruct((nseq, ROWS, KV_WIDTH), F32),
                 jax.ShapeDtypeStruct((nseq, ROWS, KV_WIDTH), F32)]
    args = [proj, proj, proj, proj, proj, qw, kw, bias16.reshape(srows, 2 * ROWS),
            jnp.repeat(sinks, ROWS).reshape(srows, 1)]
    rows = [_cast_job(w, nseq * nb) for w in cast_ws]
    hosted = bool(cast_ws) and all(r is not None for r in rows)
    if hosted:
        for w, r in zip(cast_ws, rows):
            ispec, ospec, oshape = _cast_specs(w, r, cur)
            in_specs.append(ispec)
            out_specs.append(ospec)
            out_shape.append(oshape)
            args.append(w)
    ncast = len(cast_ws) if hosted else 0
    outs = pl.pallas_call(
        functools.partial(_attn_prompt_body, ncast=ncast),
        grid=(nseq, nb),
        in_specs=in_specs,
        out_specs=out_specs,
        out_shape=out_shape,
        scratch_shapes=[pltpu.VMEM((ROWS, ATTN_WIDTH), F32),
                        pltpu.VMEM((2 * ROWS, KV_WIDTH), F32),
                        pltpu.VMEM((srows, 2 * ROWS), F32),
                        pltpu.VMEM((srows, 2 * ROWS), BF16)],
        compiler_params=_params(2, 48),
        name="attn_prompt",
    )(*args)
    return (*outs[:3], tuple(outs[3:]) if hosted else None)


def _attn_sample_body(q_ref, kn_ref, vn_ref, wk_ref, wv_ref, qw_ref, kw_ref, bias_ref, sink_ref,
                      o_ref, sk_ref, sv_ref, qn_scr, kn_scr, s_scr, p_scr, *, nsq, ntok):
    qw = qw_ref[...]
    kw = kw_ref[...]
    grows = Q_PER_KV * ntok
    keep = ROWS - ntok
    for h in range(ATTN_HEADS):
        hs = slice(h * HEAD_DIM, (h + 1) * HEAD_DIM)
        qn_scr[:, hs] = _rms_rows(q_ref[:, hs], qw)
    for kv in range(KV_HEADS):
        sl = slice(kv * HEAD_DIM, (kv + 1) * HEAD_DIM)
        kn_scr[:, sl] = _rms_rows(kn_ref[:, sl], kw)
    for s in range(nsq):
        rows = slice(s * ntok, (s + 1) * ntok)
        sk_ref[s, 0:keep] = wk_ref[s, ntok:ROWS]
        sv_ref[s, 0:keep] = wv_ref[s, ntok:ROWS]
        for kv in range(KV_HEADS):
            sl = slice(kv * HEAD_DIM, (kv + 1) * HEAD_DIM)
            sk_ref[s, keep:ROWS, kv, :] = kn_scr[rows, sl]
            sv_ref[s, keep:ROWS, kv, :] = vn_ref[rows, sl]

    pad = jnp.zeros((keep, HEAD_DIM), F32)
    for s in range(nsq):
        rows = slice(s * ntok, (s + 1) * ntok)
        for kv in range(KV_HEADS):
            sl = slice(kv * HEAD_DIM, (kv + 1) * HEAD_DIM)
            grp = s * KV_HEADS + kv
            qs = jnp.concatenate([qn_scr[rows, (kv * Q_PER_KV + g) * HEAD_DIM:(kv * Q_PER_KV + g + 1) * HEAD_DIM]
                                  for g in range(Q_PER_KV)], axis=0).astype(BF16)
            kcat = jnp.concatenate([wk_ref[s, :, kv, :], kn_scr[rows, sl], pad], axis=0).astype(BF16)
            s_scr[grp * grows:(grp + 1) * grows, :] = _dot_nt(qs, kcat)

    _softmax_sink(s_scr, bias_ref[...], sink_ref[...], p_scr)

    for s in range(nsq):
        rows = slice(s * ntok, (s + 1) * ntok)
        for kv in range(KV_HEADS):
            sl = slice(kv * HEAD_DIM, (kv + 1) * HEAD_DIM)
            grp = s * KV_HEADS + kv
            vcat = jnp.concatenate([wv_ref[s, :, kv, :], vn_ref[rows, sl], pad], axis=0).astype(BF16)
            o = _dot(p_scr[grp * grows:(grp + 1) * grows, :], vcat)
            for g in range(Q_PER_KV):
                h = kv * Q_PER_KV + g
                o_ref[rows, h * HEAD_DIM:(h + 1) * HEAD_DIM] = o[g * ntok:(g + 1) * ntok]


def _attn_sample(proj, win_k, win_v, sinks, qw, kw, bias16, nseq, ntok, nsq=8):
    kblk, vblk = COL_K // KV_WIDTH, COL_V // KV_WIDTH
    rows = nsq * ntok
    srows = nsq * ATTN_HEADS * ntok
    bias = jnp.tile(bias16[:, :ntok, :].reshape(ATTN_HEADS * ntok, 2 * ROWS), (nsq, 1))
    sink = jnp.tile(jnp.repeat(sinks, ntok), nsq).reshape(srows, 1)
    return pl.pallas_call(
        functools.partial(_attn_sample_body, nsq=nsq, ntok=ntok),
        grid=(nseq // nsq,),
        in_specs=[pl.BlockSpec((rows, ATTN_WIDTH), lambda i: (i, 0)),
                  pl.BlockSpec((rows, KV_WIDTH), lambda i: (i, kblk)),
                  pl.BlockSpec((rows, KV_WIDTH), lambda i: (i, vblk)),
                  pl.BlockSpec((nsq, ROWS, KV_HEADS, HEAD_DIM), lambda i: (i, 0, 0, 0)),
                  pl.BlockSpec((nsq, ROWS, KV_HEADS, HEAD_DIM), lambda i: (i, 0, 0, 0)),
                  pl.BlockSpec((1, HEAD_DIM), lambda i: (0, 0)),
                  pl.BlockSpec((1, HEAD_DIM), lambda i: (0, 0)),
                  pl.BlockSpec((srows, 2 * ROWS), lambda i: (0, 0)),
                  pl.BlockSpec((srows, 1), lambda i: (0, 0))],
        out_specs=[pl.BlockSpec((rows, ATTN_WIDTH), lambda i: (i, 0)),
                   pl.BlockSpec((nsq, ROWS, KV_HEADS, HEAD_DIM), lambda i: (i, 0, 0, 0)),
                   pl.BlockSpec((nsq, ROWS, KV_HEADS, HEAD_DIM), lambda i: (i, 0, 0, 0))],
        out_shape=[jax.ShapeDtypeStruct((nseq * ntok, ATTN_WIDTH), F32),
                   jax.ShapeDtypeStruct((nseq, ROWS, KV_HEADS, HEAD_DIM), F32),
                   jax.ShapeDtypeStruct((nseq, ROWS, KV_HEADS, HEAD_DIM), F32)],
        scratch_shapes=[pltpu.VMEM((rows, ATTN_WIDTH), F32),
                        pltpu.VMEM((rows, KV_WIDTH), F32),
                        pltpu.VMEM((srows, 2 * ROWS), F32),
                        pltpu.VMEM((srows, 2 * ROWS), BF16)],
        compiler_params=_params(1, 32),
        name="attn_sample",
    )(proj, proj, proj, win_k, win_v, qw, kw, bias, sink)


def _ssd_constants(qs):
    r = np.arange(ROWS)
    same = (r[:, None] // qs) == (r[None, :] // qs)
    mask = same & (r[None, :] <= r[:, None])
    sel = r[None, :] == ((r[:, None] // qs) * qs + qs - 1)
    expand = np.zeros((ROWS, SSD_WIDTH), np.float32)
    for h in range(SSD_HEADS):
        expand[h, h * SSD_HEAD_DIM:(h + 1) * SSD_HEAD_DIM] = 1.0
    to_bf = lambda a: jnp.asarray(a.astype(np.float32), BF16)
    return to_bf(mask), to_bf(sel), to_bf(expand)


def _ssd_body(xbc_ref, z_ref, dt_ref, cprev_ref, h0_ref, cw_ref, cb_ref, dtb_ref, alog_ref, dexp_ref,
              nw_ref, mask_ref, sel_ref, e_ref, *rest, nsq, qs, ncast):
    y_ref, ht_ref = rest[ncast:ncast + 2]
    for src_ref, dst_ref in zip(rest[:ncast], rest[ncast + 2:2 * ncast + 2]):
        dst_ref[...] = src_ref[...].astype(BF16)
    h_scr, cbuf, xc_scr, xdt_scr, xdd_scr, eacs_scr, yacc_scr, acs_scr, acst_scr, aclt_scr = rest[2 * ncast + 2:]
    c = pl.program_id(1)
    s = pl.program_id(2)

    @pl.when(c == 0)
    def _():
        h_scr[...] = h0_ref[0]

    if nsq == 1:
        @pl.when(c == 0)
        def _():
            cbuf[5:8, :] = cprev_ref[0]

    @pl.when(s == 0)
    def _chunk_phase():
        ncc = 8 if qs == ROWS else 1
        cwid = CONV_DIM // ncc
        for sq in range(nsq):
            for cc in range(ncc):
                cs = slice(cc * cwid, (cc + 1) * cwid)
                if nsq > 1:
                    cbuf[5:8, cs] = cprev_ref[sq, :, cs]
                cbuf[8:8 + qs, cs] = xbc_ref[sq * qs:(sq + 1) * qs, cs]
                acc = cb_ref[:, cs] + cw_ref[0:1, cs] * cbuf[5:5 + qs, cs]
                for t in range(1, SSD_CONV):
                    acc = acc + cw_ref[t:t + 1, cs] * cbuf[5 + t:5 + t + qs, cs]
                xc_scr[sq * qs:(sq + 1) * qs, cs] = acc * _sigmoid(acc)
                if nsq == 1:
                    cbuf[5:8, cs] = xbc_ref[qs - 3:qs, cs]

        x = dt_ref[...] + dtb_ref[...]
        dt = jnp.maximum(x, 0.0) + jnp.log1p(jnp.exp(-jnp.abs(x)))
        da = dt * (-jnp.exp(alog_ref[...]))
        mask_bf = mask_ref[...]
        acs = _dot3_r(mask_bf, _split3(da))
        acl = _dot3_r(sel_ref[...], _split3(acs))
        acs_scr[...] = acs
        acst_scr[...] = acs.T
        aclt_scr[...] = acl.T
        e = e_ref[...]
        dt_e = _dot3_l(dt, e)
        acs_e = _dot3_l(acs, e)
        acl_e = _dot3_l(acl, e)
        xs = xc_scr[:, 0:SSD_WIDTH]
        xdt = xs * dt_e
        xdt_scr[...] = xdt
        xdd_scr[...] = xdt * jnp.exp(acl_e - acs_e)
        eacs_scr[...] = jnp.exp(acs_e)

        maskb = mask_bf > 0.5
        lane = lax.broadcasted_iota(jnp.int32, (ROWS, 2 * SSD_HEAD_DIM), 1)
        for g in range(SSD_GROUPS):
            bg = xc_scr[:, SSD_WIDTH + g * SSD_STATE:SSD_WIDTH + (g + 1) * SSD_STATE].astype(BF16)
            cg = xc_scr[:, SSD_WIDTH + (SSD_GROUPS + g) * SSD_STATE:
                        SSD_WIDTH + (SSD_GROUPS + g + 1) * SSD_STATE].astype(BF16)
            cb = _dot_nt(cg, bg)
            for pp in range(2):
                ms = []
                for j in range(2):
                    h = 4 * g + 2 * pp + j
                    seg = acs_scr[:, h:h + 1] - acst_scr[h:h + 1, :]
                    ms.append((cb * jnp.exp(jnp.where(maskb, seg, NEG))).astype(BF16))
                c0 = (4 * g + 2 * pp) * SSD_HEAD_DIM
                cols = slice(c0, c0 + 2 * SSD_HEAD_DIM)
                xp = xdt_scr[:, cols]
                rhs = jnp.concatenate([jnp.where(lane < SSD_HEAD_DIM, xp, 0.0),
                                       jnp.where(lane >= SSD_HEAD_DIM, xp, 0.0)], axis=0).astype(BF16)
                yd = _dot(jnp.concatenate(ms, axis=1), rhs)
                yacc_scr[:, cols] = yd + dexp_ref[:, cols] * xc_scr[:, cols]

    if nsq == 1:
        r0 = 0
        rows = slice(0, qs)
    else:
        r0 = pl.multiple_of(s * qs, qs)
        rows = pl.ds(r0, qs)
        ridx = lax.broadcasted_iota(jnp.int32, (ROWS, 1), 0)
        rowmask = jnp.logical_and(ridx >= r0, ridx < r0 + qs)
    tlane = lax.broadcasted_iota(jnp.int32, (ROWS, ROWS), 1)
    xcol = jnp.sum(jnp.where(tlane == r0, aclt_scr[...], 0.0), axis=1, keepdims=True)
    dec = jnp.exp(jnp.broadcast_to(xcol, (ROWS, ROWS)))
    c_rows = xc_scr[rows, SSD_WIDTH + SSD_GROUPS * SSD_STATE:CONV_DIM].astype(BF16)
    yo, st = [], []
    for g in range(SSD_GROUPS):
        grows = slice(g * GROUP_ROWS, (g + 1) * GROUP_ROWS)
        yo.append(_dot_nt(c_rows[:, g * SSD_STATE:(g + 1) * SSD_STATE], h_scr[grows, :].astype(BF16)))
        bg = xc_scr[:, SSD_WIDTH + g * SSD_STATE:SSD_WIDTH + (g + 1) * SSD_STATE]
        if nsq > 1:
            bg = jnp.where(rowmask, bg, 0.0)
        st.append(_dot_tn(xdd_scr[:, grows].astype(BF16), bg.astype(BF16)))
    yacc_scr[rows, :] = yacc_scr[rows, :] + jnp.concatenate(yo, axis=1) * eacs_scr[rows, :]
    for h in range(SSD_HEADS):
        hrows = slice(h * SSD_HEAD_DIM, (h + 1) * SSD_HEAD_DIM)
        g, off = divmod(h * SSD_HEAD_DIM, GROUP_ROWS)
        h_new = dec[h:h + 1, :] * h_scr[hrows, :] + st[g][off:off + SSD_HEAD_DIM]
        h_scr[hrows, :] = h_new
        ht_ref[0, hrows, :] = h_new

    @pl.when(s == nsq - 1)
    def _finish():
        z = z_ref[...]
        yg = yacc_scr[...] * (z * _sigmoid(z))
        y_ref[...] = _rms_rows(yg, nw_ref[...]).astype(y_ref.dtype)


def _ssd(proj, dt_raw, conv_prev, h0, cw, cb, dtb, alog, dexp, nw, nseq, seqlen, cast_ws=()):
    if seqlen % ROWS == 0:
        qs, nsq, nchunk, ngrp = ROWS, 1, seqlen // ROWS, nseq
    else:
        qs, nsq, nchunk = seqlen, ROWS // seqlen, 1
        ngrp = nseq // nsq
    mask, sel, expand = _ssd_constants(qs)
    rowblk = lambda b, c, s: b * nchunk + c
    seq = lambda b, c, s: b * nsq + s
    const2 = lambda b, c, s: (0, 0)
    full = lambda shape: pl.BlockSpec(shape, const2)
    in_specs = [pl.BlockSpec((ROWS, CONV_DIM), lambda b, c, s: (rowblk(b, c, s), COL_XBC // CONV_DIM)),
                pl.BlockSpec((ROWS, SSD_WIDTH), lambda b, c, s: (rowblk(b, c, s), COL_Z // SSD_WIDTH)),
                pl.BlockSpec((ROWS, ROWS), lambda b, c, s: (rowblk(b, c, s), 0)),
                pl.BlockSpec((nsq, SSD_CONV - 1, CONV_DIM), lambda b, c, s: (b, 0, 0)),
                pl.BlockSpec((1, SSD_WIDTH, SSD_STATE), lambda b, c, s: (seq(b, c, s), 0, 0)),
                full((SSD_CONV, CONV_DIM)), full((1, CONV_DIM)), full((1, ROWS)), full((1, ROWS)),
                full((1, SSD_WIDTH)), full((1, SSD_WIDTH)),
                full((ROWS, ROWS)), full((ROWS, ROWS)), full((ROWS, SSD_WIDTH))]
    out_specs = [pl.BlockSpec((ROWS, SSD_WIDTH), lambda b, c, s: (rowblk(b, c, s), 0)),
                 pl.BlockSpec((1, SSD_WIDTH, SSD_STATE), lambda b, c, s: (seq(b, c, s), 0, 0))]
    out_shape = [jax.ShapeDtypeStruct((nseq * seqlen, SSD_WIDTH), BF16),
                 jax.ShapeDtypeStruct((nseq, SSD_WIDTH, SSD_STATE), F32)]
    args = [proj, proj, dt_raw, conv_prev, h0, cw, cb, dtb, alog, dexp, nw, mask, sel, expand]
    rows = [_cast_job(w, ngrp * nchunk * nsq) for w in cast_ws]
    hosted = bool(cast_ws) and all(r is not None for r in rows)
    if hosted:
        for w, r in zip(cast_ws, rows):
            ispec, ospec, oshape = _cast_specs(w, r, lambda b, c, s: (b * nchunk + c) * nsq + s)
            in_specs.append(ispec)
            out_specs.append(ospec)
            out_shape.append(oshape)
            args.append(w)
    outs = pl.pallas_call(
        functools.partial(_ssd_body, nsq=nsq, qs=qs, ncast=len(cast_ws) if hosted else 0),
        grid=(ngrp, nchunk, nsq),
        in_specs=in_specs,
        out_specs=out_specs,
        out_shape=out_shape,
        scratch_shapes=[pltpu.VMEM((SSD_WIDTH, SSD_STATE), F32),
                        pltpu.VMEM((8 + ROWS, CONV_DIM), F32),
                        pltpu.VMEM((ROWS, CONV_DIM), F32),
                        pltpu.VMEM((ROWS, SSD_WIDTH), F32),
                        pltpu.VMEM((ROWS, SSD_WIDTH), F32),
                        pltpu.VMEM((ROWS, SSD_WIDTH), F32),
                        pltpu.VMEM((ROWS, SSD_WIDTH), F32),
                        pltpu.VMEM((ROWS, ROWS), F32),
                        pltpu.VMEM((ROWS, ROWS), F32),
                        pltpu.VMEM((ROWS, ROWS), F32)],
        compiler_params=_params(3, 48),
        name="ssd",
    )(*args)
    return (*outs[:2], tuple(outs[2:]) if hosted else None)


def _ffn_gate_body(xw_ref, ssq_ref, wg_ref, wu_ref, cw_ref, cb_ref, *rest, period, tiles_per_seq, cast=False):
    if cast:
        wsrc_ref, a_ref, last_ref, wdst_ref, carry_scr = rest
        wdst_ref[...] = wsrc_ref[...].astype(BF16)
    elif period is None:
        a_ref, last_ref, carry_scr = rest
    else:
        prev_ref, a_ref, g_ref = rest
    r = lax.rsqrt(jnp.sum(ssq_ref[...], axis=0) * (1.0 / D_MODEL) + EPS)
    xw = xw_ref[...]
    tn = wg_ref.shape[1]
    col = pl.program_id(1) * tn + lax.broadcasted_iota(jnp.int32, (1, tn), 1)
    g = _dot(xw, jnp.where(col < D_FF, wg_ref[...], 0)) * r
    u = _dot(xw, jnp.where(col < D_FF, wu_ref[...], 0)) * r
    tm = g.shape[0]
    row = lax.broadcasted_iota(jnp.int32, g.shape, 0)
    if period is None:
        i, j = pl.program_id(0), pl.program_id(1)
        c8 = jnp.where(i % tiles_per_seq == 0, 0.0, carry_scr[j])
        carry_scr[j] = g[tm - 8:tm]
        last_ref[0] = g[tm - 2:tm]
        g_m1 = jnp.where(row == 0, c8[7:8], pltpu.roll(g, 1, 0))
        g_m2 = jnp.where(row == 0, c8[6:7], jnp.where(row == 1, c8[7:8], pltpu.roll(g, 2, 0)))
    else:
        g_ref[...] = g
        prev = prev_ref[...]
        t = row & (period - 1)
        g_m1 = jnp.where(t == 0, pltpu.roll(prev, tm - 1, 0), pltpu.roll(g, 1, 0))
        g_m2 = jnp.where(t < 2, prev, pltpu.roll(g, 2, 0))
    gc = cb_ref[...] + cw_ref[0:1] * g_m2
    gc = gc + cw_ref[1:2] * g_m1
    gc = gc + cw_ref[2:3] * g
    a_ref[...] = ((gc * _sigmoid(gc)) * u).astype(a_ref.dtype)


def _ffn_gate(xw, ssq, w_gate, w_up, cw, cb, *, nseq, seqlen, prev_rows=None, cast_w=None, tm=1024, tn=512):
    m = xw.shape[0]
    nj = D_FF_PAD // tn
    nparts = ssq.shape[0]
    wspec = pl.BlockSpec((D_MODEL, tn), lambda i, j: (0, j))
    cwspec = pl.BlockSpec((FFN_CONV, tn), lambda i, j: (0, j))
    cbspec = pl.BlockSpec((1, tn), lambda i, j: (0, j))
    if prev_rows is None:
        tm = min(tm, seqlen)
        tps = seqlen // tm
        in_specs = [pl.BlockSpec((tm, D_MODEL), lambda i, j: (i, 0)),
                    pl.BlockSpec((nparts, tm, 1), lambda i, j: (0, i, 0)), wspec, wspec, cwspec, cbspec]
        out_specs = [pl.BlockSpec((tm, tn), lambda i, j: (i, j)),
                     pl.BlockSpec((1, FFN_CONV - 1, tn), lambda i, j: (i, 0, j))]
        out_shape = [jax.ShapeDtypeStruct((m, D_FF_PAD), BF16),
                     jax.ShapeDtypeStruct((m // tm, FFN_CONV - 1, D_FF_PAD), F32)]
        scratch = [pltpu.VMEM((nj, 8, tn), F32)]
        args = (xw, ssq, w_gate, w_up, cw, cb)
        rows = None if cast_w is None else _cast_job(cast_w, (m // tm) * nj)
        if rows is not None:
            ispec, ospec, oshape = _cast_specs(cast_w, rows, lambda i, j: i * nj + j)
            in_specs.append(ispec)
            out_specs.append(ospec)
            out_shape.append(oshape)
            args = args + (cast_w,)
        body = functools.partial(_ffn_gate_body, period=None, tiles_per_seq=tps, cast=rows is not None)
    else:
        assert seqlen & (seqlen - 1) == 0
        tm = m
        in_specs = [pl.BlockSpec((tm, D_MODEL), lambda i, j: (i, 0)),
                    pl.BlockSpec((nparts, tm, 1), lambda i, j: (0, i, 0)), wspec, wspec, cwspec, cbspec,
                    pl.BlockSpec((tm, tn), lambda i, j: (i, j))]
        out_specs = [pl.BlockSpec((tm, tn), lambda i, j: (i, j)),
                     pl.BlockSpec((tm, tn), lambda i, j: (i, j))]
        out_shape = [jax.ShapeDtypeStruct((m, D_FF_PAD), BF16),
                     jax.ShapeDtypeStruct((m, D_FF_PAD), F32)]
        scratch = []
        args = (xw, ssq, w_gate, w_up, cw, cb, prev_rows)
        body = functools.partial(_ffn_gate_body, period=seqlen, tiles_per_seq=None)
    outs = pl.pallas_call(
        body,
        grid=(m // tm, nj),
        in_specs=in_specs,
        out_specs=out_specs,
        out_shape=out_shape,
        scratch_shapes=scratch,
        compiler_params=_params(2),
        name="ffn_gate",
    )(*args)
    return outs if len(outs) == 3 else (*outs, None)


def _down_body(a_ref, w_ref, res_ref, o_ref, *, chunk):
    @pl.when(pl.program_id(2) == 0)
    def _():
        o_ref[...] = res_ref[...]

    a = a_ref[...]
    tk = w_ref.shape[0]
    row = pl.program_id(2) * tk + lax.broadcasted_iota(jnp.int32, (tk, 1), 0)
    for c in range(o_ref.shape[1] // chunk):
        cs = slice(c * chunk, (c + 1) * chunk)
        o_ref[:, cs] += _dot(a, jnp.where(row < D_FF, w_ref[:, cs], 0))


def _down_proj(a, w_down, res, *, tm=1024, tn=2048, tk=1024):
    m, kdim = a.shape
    n = w_down.shape[1]
    tm = min(tm, m)
    return pl.pallas_call(
        functools.partial(_down_body, chunk=512),
        grid=(n // tn, m // tm, kdim // tk),
        in_specs=[pl.BlockSpec((tm, tk), lambda j, i, k: (i, k)),
                  pl.BlockSpec((tk, tn), lambda j, i, k: (k, j)),
                  pl.BlockSpec((tm, tn), lambda j, i, k: (i, j))],
        out_specs=pl.BlockSpec((tm, tn), lambda j, i, k: (i, j)),
        out_shape=jax.ShapeDtypeStruct((m, n), F32),
        compiler_params=_params(3),
        name="down_proj",
    )(a, w_down, res)


def _remember_bf16(bf16_w, wts, name, hosted):
    if name not in bf16_w:
        bf16_w[name] = hosted if hosted is not None else wts[name].astype(BF16)


def _layer(x, nseq, seqlen, wts, bf16_w, bias16, attn_state, ssm_state, conv_state, ffn_state):
    m = x.shape[0]
    todo = lambda name: None if name in bf16_w else wts[name]
    h = _rmsnorm(x, wts["mix_norm_w"])
    proj, dt_raw = _in_proj(h, wts["w_in_t"])
    new_conv = proj.reshape(nseq, seqlen, PROJ_W)[:, seqlen - (SSD_CONV - 1):, COL_XBC:COL_XBC + CONV_DIM]

    qw = wts["q_norm_w"].reshape(1, HEAD_DIM)
    kw = wts["k_norm_w"].reshape(1, HEAD_DIM)
    if attn_state is None:
        names = [n for n in ("w_gate",) if n not in bf16_w]
        attn, new_k, new_v, hosted = _attn_prompt(proj, wts["attn_sinks"], qw, kw, bias16, nseq, seqlen,
                                                  cast_ws=tuple(wts[n] for n in names))
        for k, n in enumerate(names):
            _remember_bf16(bf16_w, wts, n, None if hosted is None else hosted[k])
        new_k = new_k.reshape(nseq, WINDOW, KV_HEADS, HEAD_DIM)
        new_v = new_v.reshape(nseq, WINDOW, KV_HEADS, HEAD_DIM)
        conv_prev = jnp.zeros((nseq, SSD_CONV - 1, CONV_DIM), F32)
        h0 = jnp.zeros((nseq, SSD_WIDTH, SSD_STATE), F32)
    else:
        win_k, win_v = attn_state
        attn, new_k, new_v = _attn_sample(proj, win_k, win_v, wts["attn_sinks"], qw, kw, bias16, nseq, seqlen)
        _remember_bf16(bf16_w, wts, "w_gate", None)
        conv_prev = conv_state
        h0 = ssm_state.reshape(nseq, SSD_WIDTH, SSD_STATE)

    names = [n for n in ("w_up", "w_out") if n not in bf16_w]
    y, h_t, hosted = _ssd(proj, dt_raw, conv_prev, h0, wts["ssd_conv_w"], wts["ssd_conv_b"], wts["ssd_dt_bias"],
                          wts["ssd_A_log"], wts["ssd_D"], wts["ssd_norm_w"], nseq, seqlen,
                          cast_ws=tuple(wts[n] for n in names))
    for k, n in enumerate(names):
        _remember_bf16(bf16_w, wts, n, None if hosted is None else hosted[k])
    h_t = h_t.reshape(nseq, SSD_HEADS, SSD_HEAD_DIM, SSD_STATE)

    x1, xw, ssq = _out_proj(attn, y, bf16_w["w_out"], x, wts["ffn_norm_w"])
    if ffn_state is None:
        a, last, hosted = _ffn_gate(xw, ssq, bf16_w["w_gate"], bf16_w["w_up"], wts["ffn_conv_w"],
                                    wts["ffn_conv_b"], nseq=nseq, seqlen=seqlen, cast_w=todo("w_down"))
        new_ffn = last.reshape(nseq, -1, FFN_CONV - 1, D_FF_PAD)[:, -1, :, :D_FF]
    else:
        prev_rows = jnp.pad(ffn_state, ((0, 0), (0, seqlen - (FFN_CONV - 1)), (0, D_FF_PAD - D_FF)))
        a, g, hosted = _ffn_gate(xw, ssq, bf16_w["w_gate"], bf16_w["w_up"], wts["ffn_conv_w"], wts["ffn_conv_b"],
                                 nseq=nseq, seqlen=seqlen, prev_rows=prev_rows.reshape(m, D_FF_PAD))
        new_ffn = g.reshape(nseq, seqlen, D_FF_PAD)[:, seqlen - (FFN_CONV - 1):, :D_FF]
    _remember_bf16(bf16_w, wts, "w_down", hosted)
    x2 = _down_proj(a, bf16_w["w_down"], x1)
    return x2, new_k, new_v, h_t, new_conv, new_ffn


def kernel(x_prompt, x_sample, state_attn_k, state_attn_v, state_ssm, state_ssd_conv, state_ffn_conv, rel_bias, mix_norm_w, w_in, q_norm_w, k_norm_w, attn_sinks, ssd_conv_w, ssd_conv_b, ssd_dt_bias, ssd_A_log, ssd_D, ssd_norm_w, w_out, ffn_norm_w, w_gate, w_up, ffn_conv_w, ffn_conv_b, w_down):
    depth = w_in.shape[0]
    bp, lp, _ = x_prompt.shape
    bs, ls, _ = x_sample.shape
    bias16 = _bias_table(rel_bias)
    yp = x_prompt.reshape(bp * lp, D_MODEL)
    ys = x_sample.reshape(bs * ls, D_MODEL)
    outs_p, outs_s = [], []
    pad_heads = lambda v: jnp.pad(v.reshape(1, SSD_HEADS), ((0, 0), (0, ROWS - SSD_HEADS)))
    pad_ff = lambda v: jnp.pad(v, ((0, 0), (0, D_FF_PAD - D_FF)))
    for l in range(depth):
        wts = dict(
            mix_norm_w=mix_norm_w[l],
            w_in_t=w_in[l].T,

            q_norm_w=q_norm_w[l], k_norm_w=k_norm_w[l], attn_sinks=attn_sinks[l],
            ssd_conv_w=ssd_conv_w[l], ssd_conv_b=ssd_conv_b[l].reshape(1, CONV_DIM),
            ssd_dt_bias=pad_heads(ssd_dt_bias[l]), ssd_A_log=pad_heads(ssd_A_log[l]),
            ssd_D=jnp.repeat(ssd_D[l], SSD_HEAD_DIM).reshape(1, SSD_WIDTH),
            ssd_norm_w=ssd_norm_w[l].reshape(1, SSD_WIDTH),
            w_out=w_out[l], ffn_norm_w=ffn_norm_w[l],
            w_gate=w_gate[l], w_up=w_up[l],
            ffn_conv_w=pad_ff(ffn_conv_w[l]), ffn_conv_b=pad_ff(ffn_conv_b[l].reshape(1, D_FF)),
            w_down=w_down[l],
        )
        bf16_w = {}
        yp, *sp = _layer(yp, bp, lp, wts, bf16_w, bias16, None, None, None, None)
        ys, *ss = _layer(ys, bs, ls, wts, bf16_w, bias16, (state_attn_k[l], state_attn_v[l]), state_ssm[l],
                         state_ssd_conv[l], state_ffn_conv[l])
        outs_p.append(sp)
        outs_s.append(ss)
    stack = lambda outs, i: jnp.stack([o[i] for o in outs])
    return (yp.reshape(bp, lp, D_MODEL), ys.reshape(bs, ls, D_MODEL),
            stack(outs_p, 0), stack(outs_p, 1), stack(outs_p, 2), stack(outs_p, 3), stack(outs_p, 4),
            stack(outs_s, 0), stack(outs_s, 1), stack(outs_s, 2), stack(outs_s, 3), stack(outs_s, 4))
```

```python
import functools
import math

import numpy as np
import jax
import jax.numpy as jnp
from jax import lax
from jax.experimental import pallas as pl
from jax.experimental.pallas import tpu as pltpu

F32 = jnp.float32
BF16 = jnp.bfloat16

D_MODEL = 4096
HEAD_DIM = 128
ATTN_HEADS = 16
KV_HEADS = 4
Q_PER_KV = 4
ATTN_WIDTH = 2048
WINDOW = 128
N_BUCKETS = 32
MAX_DISTANCE = 128
SSD_HEAD_DIM = 64
SSD_HEADS = 32
SSD_WIDTH = 2048
SSD_GROUPS = 8
SSD_STATE = 128
SSD_CONV = 4
CONV_DIM = 4096
KV_WIDTH = KV_HEADS * HEAD_DIM
D_FF = 11008
D_FF_PAD = 11264
FFN_CONV = 3
EPS = 1e-6
NEG = -1e30
SCALE = HEAD_DIM ** -0.5

ROWS = 128
GROUP_ROWS = 4 * SSD_HEAD_DIM
COL_Z = ATTN_WIDTH
COL_XBC = COL_Z + SSD_WIDTH
COL_K = COL_XBC + CONV_DIM
COL_V = COL_K + KV_WIDTH
PROJ_W = COL_V + KV_WIDTH
VMEM_LIMIT_MB = 56


def _params(n_axes, vmem_mb=VMEM_LIMIT_MB):
    return pltpu.CompilerParams(dimension_semantics=("arbitrary",) * n_axes,
                                vmem_limit_bytes=vmem_mb << 20)


def _dot(a, b):
    return jnp.dot(a, b, preferred_element_type=F32)


def _dot_nt(a, b):
    return lax.dot_general(a, b, (((1,), (1,)), ((), ())), preferred_element_type=F32)


def _dot_tn(a, b):
    return lax.dot_general(a, b, (((0,), (0,)), ((), ())), preferred_element_type=F32)


def _split3(x):
    a = x.astype(BF16)
    r = x - a.astype(F32)
    b = r.astype(BF16)
    c = (r - b.astype(F32)).astype(BF16)
    return a, b, c


def _dot3_l(x, w):
    a, b, c = _split3(x)
    return (_dot(a, w) + _dot(b, w)) + _dot(c, w)


def _dot3_r(w, x3):
    return (_dot(w, x3[0]) + _dot(w, x3[1])) + _dot(w, x3[2])


def _sigmoid(x):
    return 1.0 / (1.0 + jnp.exp(-x))


def _rms_rows(x, w):
    r = lax.rsqrt(jnp.mean(x * x, axis=-1, keepdims=True) + EPS)
    return (x * r) * w


def _rmsnorm_body(x_ref, w_ref, o_ref):
    o_ref[...] = _rms_rows(x_ref[...], w_ref[...]).astype(o_ref.dtype)


def _rmsnorm(x, w, rows=256):
    m, d = x.shape
    rows = min(rows, m)
    return pl.pallas_call(
        _rmsnorm_body,
        grid=(m // rows,),
        in_specs=[pl.BlockSpec((rows, d), lambda i: (i, 0)),
                  pl.BlockSpec((1, d), lambda i: (0, 0))],
        out_specs=pl.BlockSpec((rows, d), lambda i: (i, 0)),
        out_shape=jax.ShapeDtypeStruct((m, d), BF16),
        compiler_params=_params(1, 32),
        name="rmsnorm",
    )(x, w.reshape(1, d))


def _in_proj_body(a_ref, bt_ref, dtw_ref, o_ref, dt_ref):
    a = a_ref[...]
    o_ref[...] = _dot_nt(a, bt_ref[...].astype(BF16))

    @pl.when(pl.program_id(1) == 0)
    def _():
        row = lax.broadcasted_iota(jnp.int32, (dtw_ref.shape[0], 1), 0)
        dtw = jnp.where(row < SSD_HEADS, dtw_ref[...], 0.0)
        dt_ref[...] = _dot_nt(a, dtw.astype(BF16))


def _in_proj(h, w_in_t, *, tm=1024, tn=512):
    m = h.shape[0]
    tm = min(tm, m)
    nq, nkv = ATTN_WIDTH // tn, KV_WIDTH // tn
    nzx = (SSD_WIDTH + CONV_DIM) // tn

    def src_block(j):
        return jnp.where(j < nq, j, jnp.where(j < nq + nzx, j + 2 * nkv, j - nzx))

    return pl.pallas_call(
        _in_proj_body,
        grid=(m // tm, PROJ_W // tn),
        in_specs=[pl.BlockSpec((tm, D_MODEL), lambda i, j: (i, 0)),
                  pl.BlockSpec((tn, D_MODEL), lambda i, j: (src_block(j), 0)),
                  pl.BlockSpec((ROWS, D_MODEL), lambda i, j: (PROJ_W // ROWS, 0))],
        out_specs=[pl.BlockSpec((tm, tn), lambda i, j: (i, j)),
                   pl.BlockSpec((tm, ROWS), lambda i, j: (i, 0))],
        out_shape=[jax.ShapeDtypeStruct((m, PROJ_W), F32),
                   jax.ShapeDtypeStruct((m, ROWS), F32)],
        compiler_params=_params(2),
        name="in_proj",
    )(h, w_in_t, w_in_t)


def _out_proj_body(a1_ref, a2_ref, b1_ref, b2_ref, res_ref, nw_ref, o_ref, xw_ref, ssq_ref):
    acc = (_dot(a1_ref[...].astype(BF16), b1_ref[...].astype(BF16))
           + _dot(a2_ref[...].astype(BF16), b2_ref[...].astype(BF16)))
    x1 = res_ref[...] + acc
    o_ref[...] = x1
    xw_ref[...] = (x1 * nw_ref[...]).astype(xw_ref.dtype)
    ssq_ref[0] = jnp.sum(x1 * x1, axis=-1, keepdims=True)


def _out_proj(attn, y, w_out, res, norm_w, *, tm=1024, tn=512):
    m = attn.shape[0]
    n = w_out.shape[1]
    tm = min(tm, m)
    return pl.pallas_call(
        _out_proj_body,
        grid=(m // tm, n // tn),
        in_specs=[pl.BlockSpec((tm, ATTN_WIDTH), lambda i, j: (i, 0)),
                  pl.BlockSpec((tm, SSD_WIDTH), lambda i, j: (i, 0)),
                  pl.BlockSpec((ATTN_WIDTH, tn), lambda i, j: (0, j)),
                  pl.BlockSpec((SSD_WIDTH, tn), lambda i, j: (1, j)),
                  pl.BlockSpec((tm, tn), lambda i, j: (i, j)),
                  pl.BlockSpec((1, tn), lambda i, j: (0, j))],
        out_specs=[pl.BlockSpec((tm, tn), lambda i, j: (i, j)),
                   pl.BlockSpec((tm, tn), lambda i, j: (i, j)),
                   pl.BlockSpec((1, tm, 1), lambda i, j: (j, i, 0))],
        out_shape=[jax.ShapeDtypeStruct((m, n), F32),
                   jax.ShapeDtypeStruct((m, n), BF16),
                   jax.ShapeDtypeStruct((n // tn, m, 1), F32)],
        compiler_params=_params(2),
        name="out_proj",
    )(attn, y, w_out, w_out, res, norm_w.reshape(1, n))


def _bucket_table():
    i = np.arange(ROWS)[:, None]
    j = np.arange(2 * ROWS)[None, :]
    dist = i + ROWS - j
    valid = (dist >= 0) & (dist < WINDOW)
    n = np.maximum(dist, 0)
    max_exact = N_BUCKETS // 2

    def large(dtype):
        nf = np.maximum(n, 1).astype(dtype)
        v = np.log(nf / dtype(max_exact)) / dtype(math.log(MAX_DISTANCE / max_exact)) * dtype(N_BUCKETS - max_exact)
        return np.minimum(max_exact + v.astype(np.int32), N_BUCKETS - 1)

    assert (large(np.float32) == large(np.float64)).all()
    bucket = np.where(n < max_exact, n, large(np.float32))
    return np.where(valid, bucket, -1).astype(np.int32)


def _bias_body(rb_ref, bkt_ref, o_ref):
    h = pl.program_id(0)
    bkt = bkt_ref[...]
    acc = jnp.full(bkt.shape, NEG, F32)
    for b in range(N_BUCKETS):
        acc = jnp.where(bkt == b, rb_ref[b, h], acc)
    o_ref[0] = acc


def _bias_table(rel_bias):
    return pl.pallas_call(
        _bias_body,
        grid=(ATTN_HEADS,),
        in_specs=[pl.BlockSpec(memory_space=pltpu.SMEM),
                  pl.BlockSpec((ROWS, 2 * ROWS), lambda h: (0, 0))],
        out_specs=pl.BlockSpec((1, ROWS, 2 * ROWS), lambda h: (h, 0, 0)),
        out_shape=jax.ShapeDtypeStruct((ATTN_HEADS, ROWS, 2 * ROWS), F32),
        compiler_params=_params(1, 16),
        name="bias_table",
    )(rel_bias, jnp.asarray(_bucket_table()))


def _cast_job(w, nsteps):
    rows = -(-w.shape[0] // nsteps)
    rows = -(-rows // 16) * 16
    return rows if rows * w.shape[1] * 4 <= (8 << 20) else None


def _cast_specs(w, rows, step_of):
    nblk = -(-w.shape[0] // rows)
    spec = pl.BlockSpec((rows, w.shape[1]), lambda *g: (jnp.minimum(step_of(*g), nblk - 1), 0))
    return spec, spec, jax.ShapeDtypeStruct(w.shape, BF16)


def _softmax_sink(s_scr, bias, sink, p_scr):
    sc = s_scr[...] * SCALE + bias
    m = jnp.maximum(jnp.max(sc, axis=-1, keepdims=True), sink)
    p = jnp.exp(sc - m)
    den = jnp.sum(p, axis=-1, keepdims=True) + jnp.exp(sink - m)
    p_scr[...] = (p * (1.0 / den)).astype(BF16)


def _attn_prompt_body(q_ref, kc_ref, kp_ref, vc_ref, vp_ref, qw_ref, kw_ref, bias_ref, sink_ref, *rest, ncast):
    o_ref, pk_ref, pv_ref = rest[ncast:ncast + 3]
    for src_ref, dst_ref in zip(rest[:ncast], rest[ncast + 3:2 * ncast + 3]):
        dst_ref[...] = src_ref[...].astype(BF16)
    qn_scr, kn_scr, s_scr, p_scr = rest[2 * ncast + 3:]
    qw = qw_ref[...]
    kw = kw_ref[...]
    grows = Q_PER_KV * ROWS
    for h in range(ATTN_HEADS):
        hs = slice(h * HEAD_DIM, (h + 1) * HEAD_DIM)
        qn_scr[:, hs] = _rms_rows(q_ref[:, hs], qw)
    for kv in range(KV_HEADS):
        sl = slice(kv * HEAD_DIM, (kv + 1) * HEAD_DIM)
        kn_scr[0:ROWS, sl] = _rms_rows(kp_ref[:, sl], kw)
        kn_scr[ROWS:2 * ROWS, sl] = _rms_rows(kc_ref[:, sl], kw)
    pk_ref[0] = kn_scr[ROWS:2 * ROWS, :]
    pv_ref[0] = vc_ref[...]

    for kv in range(KV_HEADS):
        sl = slice(kv * HEAD_DIM, (kv + 1) * HEAD_DIM)
        qs = jnp.concatenate([qn_scr[:, (kv * Q_PER_KV + g) * HEAD_DIM:(kv * Q_PER_KV + g + 1) * HEAD_DIM]
                              for g in range(Q_PER_KV)], axis=0).astype(BF16)
        s_scr[kv * grows:(kv + 1) * grows, :] = _dot_nt(qs, kn_scr[:, sl].astype(BF16))

    col = lax.broadcasted_iota(jnp.int32, (1, 2 * ROWS), 1)
    no_prev = jnp.where(jnp.logical_and(pl.program_id(1) == 0, col < ROWS), NEG, 0.0)
    _softmax_sink(s_scr, bias_ref[...] + no_prev, sink_ref[...], p_scr)

    for kv in range(KV_HEADS):
        sl = slice(kv * HEAD_DIM, (kv + 1) * HEAD_DIM)
        vcat = jnp.concatenate([vp_ref[:, sl], vc_ref[:, sl]], axis=0).astype(BF16)
        o = _dot(p_scr[kv * grows:(kv + 1) * grows, :], vcat)
        for g in range(Q_PER_KV):
            h = kv * Q_PER_KV + g
            o_ref[:, h * HEAD_DIM:(h + 1) * HEAD_DIM] = o[g * ROWS:(g + 1) * ROWS].astype(o_ref.dtype)


def _attn_prompt(proj, sinks, qw, kw, bias16, nseq, seqlen, cast_ws=()):
    nb = seqlen // ROWS
    kblk, vblk = COL_K // KV_WIDTH, COL_V // KV_WIDTH
    srows = ATTN_HEADS * ROWS
    cur = lambda b, i: b * nb + i
    prev = lambda b, i: b * nb + jnp.maximum(i - 1, 0)
    const = lambda b, i: (0, 0)
    in_specs = [pl.BlockSpec((ROWS, ATTN_WIDTH), lambda b, i: (cur(b, i), 0)),
                pl.BlockSpec((ROWS, KV_WIDTH), lambda b, i: (cur(b, i), kblk)),
                pl.BlockSpec((ROWS, KV_WIDTH), lambda b, i: (prev(b, i), kblk)),
                pl.BlockSpec((ROWS, KV_WIDTH), lambda b, i: (cur(b, i), vblk)),
                pl.BlockSpec((ROWS, KV_WIDTH), lambda b, i: (prev(b, i), vblk)),
                pl.BlockSpec((1, HEAD_DIM), const),
                pl.BlockSpec((1, HEAD_DIM), const),
                pl.BlockSpec((srows, 2 * ROWS), const),
                pl.BlockSpec((srows, 1), const)]
    out_specs = [pl.BlockSpec((ROWS, ATTN_WIDTH), lambda b, i: (cur(b, i), 0)),
                 pl.BlockSpec((1, ROWS, KV_WIDTH), lambda b, i: (b, 0, 0)),
                 pl.BlockSpec((1, ROWS, KV_WIDTH), lambda b, i: (b, 0, 0))]
    out_shape = [jax.ShapeDtypeStruct((nseq * seqlen, ATTN_WIDTH), BF16),
                 jax.ShapeDtypeStruct((nseq, ROWS, KV_WIDTH), F32),
                 jax.ShapeDtypeStruct((nseq, ROWS, KV_WIDTH), F32)]
    args = [proj, proj, proj, proj, proj, qw, kw, bias16.reshape(srows, 2 * ROWS),
            jnp.repeat(sinks, ROWS).reshape(srows, 1)]
    rows = [_cast_job(w, nseq * nb) for w in cast_ws]
    hosted = bool(cast_ws) and all(r is not None for r in rows)
    if hosted:
        for w, r in zip(cast_ws, rows):
            ispec, ospec, oshape = _cast_specs(w, r, cur)
            in_specs.append(ispec)
            out_specs.append(ospec)
            out_shape.append(oshape)
            args.append(w)
    ncast = len(cast_ws) if hosted else 0
    outs = pl.pallas_call(
        functools.partial(_attn_prompt_body, ncast=ncast),
        grid=(nseq, nb),
        in_specs=in_specs,
        out_specs=out_specs,
        out_shape=out_shape,
        scratch_shapes=[pltpu.VMEM((ROWS, ATTN_WIDTH), F32),
                        pltpu.VMEM((2 * ROWS, KV_WIDTH), F32),
                        pltpu.VMEM((srows, 2 * ROWS), F32),
                        pltpu.VMEM((srows, 2 * ROWS), BF16)],
        compiler_params=_params(2, 48),
        name="attn_prompt",
    )(*args)
    return (*outs[:3], tuple(outs[3:]) if hosted else None)


def _attn_sample_body(q_ref, kn_ref, vn_ref, wk_ref, wv_ref, qw_ref, kw_ref, bias_ref, sink_ref,
                      o_ref, sk_ref, sv_ref, qn_scr, kn_scr, s_scr, p_scr, *, nsq, ntok):
    qw = qw_ref[...]
    kw = kw_ref[...]
    grows = Q_PER_KV * ntok
    keep = ROWS - ntok
    for h in range(ATTN_HEADS):
        hs = slice(h * HEAD_DIM, (h + 1) * HEAD_DIM)
        qn_scr[:, hs] = _rms_rows(q_ref[:, hs], qw)
    for kv in range(KV_HEADS):
        sl = slice(kv * HEAD_DIM, (kv + 1) * HEAD_DIM)
        kn_scr[:, sl] = _rms_rows(kn_ref[:, sl], kw)
    for s in range(nsq):
        rows = slice(s * ntok, (s + 1) * ntok)
        sk_ref[s, 0:keep] = wk_ref[s, ntok:ROWS]
        sv_ref[s, 0:keep] = wv_ref[s, ntok:ROWS]
        for kv in range(KV_HEADS):
            sl = slice(kv * HEAD_DIM, (kv + 1) * HEAD_DIM)
            sk_ref[s, keep:ROWS, kv, :] = kn_scr[rows, sl]
            sv_ref[s, keep:ROWS, kv, :] = vn_ref[rows, sl]

    pad = jnp.zeros((keep, HEAD_DIM), F32)
    for s in range(nsq):
        rows = slice(s * ntok, (s + 1) * ntok)
        for kv in range(KV_HEADS):
            sl = slice(kv * HEAD_DIM, (kv + 1) * HEAD_DIM)
            grp = s * KV_HEADS + kv
            qs = jnp.concatenate([qn_scr[rows, (kv * Q_PER_KV + g) * HEAD_DIM:(kv * Q_PER_KV + g + 1) * HEAD_DIM]
                                  for g in range(Q_PER_KV)], axis=0).astype(BF16)
            kcat = jnp.concatenate([wk_ref[s, :, kv, :], kn_scr[rows, sl], pad], axis=0).astype(BF16)
            s_scr[grp * grows:(grp + 1) * grows, :] = _dot_nt(qs, kcat)

    _softmax_sink(s_scr, bias_ref[...], sink_ref[...], p_scr)

    for s in range(nsq):
        rows = slice(s * ntok, (s + 1) * ntok)
        for kv in range(KV_HEADS):
            sl = slice(kv * HEAD_DIM, (kv + 1) * HEAD_DIM)
            grp = s * KV_HEADS + kv
            vcat = jnp.concatenate([wv_ref[s, :, kv, :], vn_ref[rows, sl], pad], axis=0).astype(BF16)
            o = _dot(p_scr[grp * grows:(grp + 1) * grows, :], vcat)
            for g in range(Q_PER_KV):
                h = kv * Q_PER_KV + g
                o_ref[rows, h * HEAD_DIM:(h + 1) * HEAD_DIM] = o[g * ntok:(g + 1) * ntok]


def _attn_sample(proj, win_k, win_v, sinks, qw, kw, bias16, nseq, ntok, nsq=8):
    kblk, vblk = COL_K // KV_WIDTH, COL_V // KV_WIDTH
    rows = nsq * ntok
    srows = nsq * ATTN_HEADS * ntok
    bias = jnp.tile(bias16[:, :ntok, :].reshape(ATTN_HEADS * ntok, 2 * ROWS), (nsq, 1))
    sink = jnp.tile(jnp.repeat(sinks, ntok), nsq).reshape(srows, 1)
    return pl.pallas_call(
        functools.partial(_attn_sample_body, nsq=nsq, ntok=ntok),
        grid=(nseq // nsq,),
        in_specs=[pl.BlockSpec((rows, ATTN_WIDTH), lambda i: (i, 0)),
                  pl.BlockSpec((rows, KV_WIDTH), lambda i: (i, kblk)),
                  pl.BlockSpec((rows, KV_WIDTH), lambda i: (i, vblk)),
                  pl.BlockSpec((nsq, ROWS, KV_HEADS, HEAD_DIM), lambda i: (i, 0, 0, 0)),
                  pl.BlockSpec((nsq, ROWS, KV_HEADS, HEAD_DIM), lambda i: (i, 0, 0, 0)),
                  pl.BlockSpec((1, HEAD_DIM), lambda i: (0, 0)),
                  pl.BlockSpec((1, HEAD_DIM), lambda i: (0, 0)),
                  pl.BlockSpec((srows, 2 * ROWS), lambda i: (0, 0)),
                  pl.BlockSpec((srows, 1), lambda i: (0, 0))],
        out_specs=[pl.BlockSpec((rows, ATTN_WIDTH), lambda i: (i, 0)),
                   pl.BlockSpec((nsq, ROWS, KV_HEADS, HEAD_DIM), lambda i: (i, 0, 0, 0)),
                   pl.BlockSpec((nsq, ROWS, KV_HEADS, HEAD_DIM), lambda i: (i, 0, 0, 0))],
        out_shape=[jax.ShapeDtypeStruct((nseq * ntok, ATTN_WIDTH), F32),
                   jax.ShapeDtypeStruct((nseq, ROWS, KV_HEADS, HEAD_DIM), F32),
                   jax.ShapeDtypeStruct((nseq, ROWS, KV_HEADS, HEAD_DIM), F32)],
        scratch_shapes=[pltpu.VMEM((rows, ATTN_WIDTH), F32),
                        pltpu.VMEM((rows, KV_WIDTH), F32),
                        pltpu.VMEM((srows, 2 * ROWS), F32),
                        pltpu.VMEM((srows, 2 * ROWS), BF16)],
        compiler_params=_params(1, 32),
        name="attn_sample",
    )(proj, proj, proj, win_k, win_v, qw, kw, bias, sink)


def _ssd_constants(qs):
    r = np.arange(ROWS)
    same = (r[:, None] // qs) == (r[None, :] // qs)
    mask = same & (r[None, :] <= r[:, None])
    sel = r[None, :] == ((r[:, None] // qs) * qs + qs - 1)
    expand = np.zeros((ROWS, SSD_WIDTH), np.float32)
    for h in range(SSD_HEADS):
        expand[h, h * SSD_HEAD_DIM:(h + 1) * SSD_HEAD_DIM] = 1.0
    to_bf = lambda a: jnp.asarray(a.astype(np.float32), BF16)
    return to_bf(mask), to_bf(sel), to_bf(expand)


def _ssd_body(xbc_ref, z_ref, dt_ref, cprev_ref, h0_ref, cw_ref, cb_ref, dtb_ref, alog_ref, dexp_ref,
              nw_ref, mask_ref, sel_ref, e_ref, *rest, nsq, qs, ncast):
    y_ref, ht_ref = rest[ncast:ncast + 2]
    for src_ref, dst_ref in zip(rest[:ncast], rest[ncast + 2:2 * ncast + 2]):
        dst_ref[...] = src_ref[...].astype(BF16)
    h_scr, cbuf, xc_scr, xdt_scr, xdd_scr, eacs_scr, yacc_scr, acs_scr, acst_scr, aclt_scr = rest[2 * ncast + 2:]
    c = pl.program_id(1)
    s = pl.program_id(2)

    @pl.when(c == 0)
    def _():
        h_scr[...] = h0_ref[0]

    @pl.when(jnp.logical_and(c == 0, s == 0))
    def _():
        cbuf[0:8, :] = jnp.zeros((8, CONV_DIM), F32)
        if nsq == 1:
            cbuf[5:8, :] = cprev_ref[0]

    @pl.when(s == 0)
    def _chunk_phase():
        ncc = 8 if qs == ROWS else 1
        cwid = CONV_DIM // ncc
        nrow = 8 + qs
        for sq in range(nsq):
            for cc in range(ncc):
                cs = slice(cc * cwid, (cc + 1) * cwid)
                if nsq > 1:
                    cbuf[5:8, cs] = cprev_ref[sq, :, cs]
                cbuf[8:nrow, cs] = xbc_ref[sq * qs:(sq + 1) * qs, cs]
                xa = cbuf[0:nrow, cs]
                acc = cb_ref[:, cs] + cw_ref[SSD_CONV - 1:SSD_CONV, cs] * xa[8:nrow]
                for t in range(SSD_CONV - 1):
                    acc = acc + cw_ref[t:t + 1, cs] * pltpu.roll(xa, nrow - (5 + t), 0)[0:qs]
                xc_scr[sq * qs:(sq + 1) * qs, cs] = acc * _sigmoid(acc)
                if nsq == 1:
                    cbuf[5:8, cs] = xbc_ref[qs - 3:qs, cs]

        x = dt_ref[...] + dtb_ref[...]
        dt = jnp.maximum(x, 0.0) + jnp.log1p(jnp.exp(-jnp.abs(x)))
        da = dt * (-jnp.exp(alog_ref[...]))
        mask_bf = mask_ref[...]
        acs = _dot3_r(mask_bf, _split3(da))
        acl = _dot3_r(sel_ref[...], _split3(acs))
        acs_scr[...] = acs
        acst_scr[...] = acs.T
        aclt_scr[...] = acl.T
        e = e_ref[...]
        xs = xc_scr[:, 0:SSD_WIDTH]
        xdt_scr[...] = xs * _dot3_l(dt, e)
        xdd_scr[...] = xs * _dot3_l(dt * jnp.exp(acl - acs), e)
        eacs_scr[...] = _dot3_l(jnp.exp(acs), e)

        maskb = mask_bf > 0.5
        lane = lax.broadcasted_iota(jnp.int32, (ROWS, 2 * SSD_HEAD_DIM), 1)
        for g in range(SSD_GROUPS):
            bg = xc_scr[:, SSD_WIDTH + g * SSD_STATE:SSD_WIDTH + (g + 1) * SSD_STATE].astype(BF16)
            cg = xc_scr[:, SSD_WIDTH + (SSD_GROUPS + g) * SSD_STATE:
                        SSD_WIDTH + (SSD_GROUPS + g + 1) * SSD_STATE].astype(BF16)
            cb = _dot_nt(cg, bg)
            for pp in range(2):
                ms = []
                for j in range(2):
                    h = 4 * g + 2 * pp + j
                    seg = acs_scr[:, h:h + 1] - acst_scr[h:h + 1, :]
                    ms.append((cb * jnp.exp(jnp.where(maskb, seg, NEG))).astype(BF16))
                c0 = (4 * g + 2 * pp) * SSD_HEAD_DIM
                cols = slice(c0, c0 + 2 * SSD_HEAD_DIM)
                xp = xdt_scr[:, cols]
                rhs = jnp.concatenate([jnp.where(lane < SSD_HEAD_DIM, xp, 0.0),
                                       jnp.where(lane >= SSD_HEAD_DIM, xp, 0.0)], axis=0).astype(BF16)
                yd = _dot(jnp.concatenate(ms, axis=1), rhs)
                yacc_scr[:, cols] = yd + dexp_ref[:, cols] * xc_scr[:, cols]

    if nsq == 1:
        r0 = 0
        rows = slice(0, qs)
    else:
        r0 = pl.multiple_of(s * qs, qs)
        rows = pl.ds(r0, qs)
        ridx = lax.broadcasted_iota(jnp.int32, (ROWS, 1), 0)
        rowmask = jnp.logical_and(ridx >= r0, ridx < r0 + qs)
    tlane = lax.broadcasted_iota(jnp.int32, (ROWS, ROWS), 1)
    xcol = jnp.sum(jnp.where(tlane == r0, aclt_scr[...], 0.0), axis=1, keepdims=True)
    dec = jnp.exp(jnp.broadcast_to(xcol, (ROWS, ROWS)))
    c_rows = xc_scr[rows, SSD_WIDTH + SSD_GROUPS * SSD_STATE:CONV_DIM].astype(BF16)
    yo, st = [], []
    for g in range(SSD_GROUPS):
        grows = slice(g * GROUP_ROWS, (g + 1) * GROUP_ROWS)
        yo.append(_dot_nt(c_rows[:, g * SSD_STATE:(g + 1) * SSD_STATE], h_scr[grows, :].astype(BF16)))
        bg = xc_scr[:, SSD_WIDTH + g * SSD_STATE:SSD_WIDTH + (g + 1) * SSD_STATE]
        if nsq > 1:
            bg = jnp.where(rowmask, bg, 0.0)
        st.append(_dot_tn(xdd_scr[:, grows].astype(BF16), bg.astype(BF16)))
    yacc_scr[rows, :] = yacc_scr[rows, :] + jnp.concatenate(yo, axis=1) * eacs_scr[rows, :]
    for h in range(SSD_HEADS):
        hrows = slice(h * SSD_HEAD_DIM, (h + 1) * SSD_HEAD_DIM)
        g, off = divmod(h * SSD_HEAD_DIM, GROUP_ROWS)
        h_new = dec[h:h + 1, :] * h_scr[hrows, :] + st[g][off:off + SSD_HEAD_DIM]
        h_scr[hrows, :] = h_new
        ht_ref[0, hrows, :] = h_new

    @pl.when(s == nsq - 1)
    def _finish():
        z = z_ref[...]
        yg = yacc_scr[...] * (z * _sigmoid(z))
        y_ref[...] = _rms_rows(yg, nw_ref[...]).astype(y_ref.dtype)


def _ssd(proj, dt_raw, conv_prev, h0, cw, cb, dtb, alog, dexp, nw, nseq, seqlen, cast_ws=()):
    if seqlen % ROWS == 0:
        qs, nsq, nchunk, ngrp = ROWS, 1, seqlen // ROWS, nseq
    else:
        qs, nsq, nchunk = seqlen, ROWS // seqlen, 1
        ngrp = nseq // nsq
    mask, sel, expand = _ssd_constants(qs)
    rowblk = lambda b, c, s: b * nchunk + c
    seq = lambda b, c, s: b * nsq + s
    const2 = lambda b, c, s: (0, 0)
    full = lambda shape: pl.BlockSpec(shape, const2)
    in_specs = [pl.BlockSpec((ROWS, CONV_DIM), lambda b, c, s: (rowblk(b, c, s), COL_XBC // CONV_DIM)),
                pl.BlockSpec((ROWS, SSD_WIDTH), lambda b, c, s: (rowblk(b, c, s), COL_Z // SSD_WIDTH)),
                pl.BlockSpec((ROWS, ROWS), lambda b, c, s: (rowblk(b, c, s), 0)),
                pl.BlockSpec((nsq, SSD_CONV - 1, CONV_DIM), lambda b, c, s: (b, 0, 0)),
                pl.BlockSpec((1, SSD_WIDTH, SSD_STATE), lambda b, c, s: (seq(b, c, s), 0, 0)),
                full((SSD_CONV, CONV_DIM)), full((1, CONV_DIM)), full((1, ROWS)), full((1, ROWS)),
                full((1, SSD_WIDTH)), full((1, SSD_WIDTH)),
                full((ROWS, ROWS)), full((ROWS, ROWS)), full((ROWS, SSD_WIDTH))]
    out_specs = [pl.BlockSpec((ROWS, SSD_WIDTH), lambda b, c, s: (rowblk(b, c, s), 0)),
                 pl.BlockSpec((1, SSD_WIDTH, SSD_STATE), lambda b, c, s: (seq(b, c, s), 0, 0))]
    out_shape = [jax.ShapeDtypeStruct((nseq * seqlen, SSD_WIDTH), BF16),
                 jax.ShapeDtypeStruct((nseq, SSD_WIDTH, SSD_STATE), F32)]
    args = [proj, proj, dt_raw, conv_prev, h0, cw, cb, dtb, alog, dexp, nw, mask, sel, expand]
    rows = [_cast_job(w, ngrp * nchunk * nsq) for w in cast_ws]
    hosted = bool(cast_ws) and all(r is not None for r in rows)
    if hosted:
        for w, r in zip(cast_ws, rows):
            ispec, ospec, oshape = _cast_specs(w, r, lambda b, c, s: (b * nchunk + c) * nsq + s)
            in_specs.append(ispec)
            out_specs.append(ospec)
            out_shape.append(oshape)
            args.append(w)
    outs = pl.pallas_call(
        functools.partial(_ssd_body, nsq=nsq, qs=qs, ncast=len(cast_ws) if hosted else 0),
        grid=(ngrp, nchunk, nsq),
        in_specs=in_specs,
        out_specs=out_specs,
        out_shape=out_shape,
        scratch_shapes=[pltpu.VMEM((SSD_WIDTH, SSD_STATE), F32),
                        pltpu.VMEM((8 + ROWS, CONV_DIM), F32),
                        pltpu.VMEM((ROWS, CONV_DIM), F32),
                        pltpu.VMEM((ROWS, SSD_WIDTH), F32),
                        pltpu.VMEM((ROWS, SSD_WIDTH), F32),
                        pltpu.VMEM((ROWS, SSD_WIDTH), F32),
                        pltpu.VMEM((ROWS, SSD_WIDTH), F32),
                        pltpu.VMEM((ROWS, ROWS), F32),
                        pltpu.VMEM((ROWS, ROWS), F32),
                        pltpu.VMEM((ROWS, ROWS), F32)],
        compiler_params=_params(3, 48),
        name="ssd",
    )(*args)
    return (*outs[:2], tuple(outs[2:]) if hosted else None)


def _ffn_gate_body(xw_ref, ssq_ref, wg_ref, wu_ref, cw_ref, cb_ref, *rest, period, tiles_per_seq, cast=False):
    if cast:
        wsrc_ref, a_ref, last_ref, wdst_ref, carry_scr = rest
        wdst_ref[...] = wsrc_ref[...].astype(BF16)
    elif period is None:
        a_ref, last_ref, carry_scr = rest
    else:
        prev_ref, a_ref, g_ref = rest
    r = lax.rsqrt(jnp.sum(ssq_ref[...], axis=0) * (1.0 / D_MODEL) + EPS)
    xw = xw_ref[...]
    tn = wg_ref.shape[1]
    col = pl.program_id(1) * tn + lax.broadcasted_iota(jnp.int32, (1, tn), 1)
    g = _dot(xw, jnp.where(col < D_FF, wg_ref[...], 0)) * r
    u = _dot(xw, jnp.where(col < D_FF, wu_ref[...], 0)) * r
    tm = g.shape[0]
    row = lax.broadcasted_iota(jnp.int32, g.shape, 0)
    if period is None:
        i, j = pl.program_id(0), pl.program_id(1)
        c8 = jnp.where(i % tiles_per_seq == 0, 0.0, carry_scr[j])
        carry_scr[j] = g[tm - 8:tm]
        last_ref[0] = g[tm - 2:tm]
        g_m1 = jnp.where(row == 0, c8[7:8], pltpu.roll(g, 1, 0))
        g_m2 = jnp.where(row == 0, c8[6:7], jnp.where(row == 1, c8[7:8], pltpu.roll(g, 2, 0)))
    else:
        g_ref[...] = g
        prev = prev_ref[...]
        t = row & (period - 1)
        g_m1 = jnp.where(t == 0, pltpu.roll(prev, tm - 1, 0), pltpu.roll(g, 1, 0))
        g_m2 = jnp.where(t < 2, prev, pltpu.roll(g, 2, 0))
    gc = cb_ref[...] + cw_ref[0:1] * g_m2
    gc = gc + cw_ref[1:2] * g_m1
    gc = gc + cw_ref[2:3] * g
    a_ref[...] = ((gc * _sigmoid(gc)) * u).astype(a_ref.dtype)


def _ffn_gate(xw, ssq, w_gate, w_up, cw, cb, *, nseq, seqlen, prev_rows=None, cast_w=None, tm=1024, tn=512):
    m = xw.shape[0]
    nj = D_FF_PAD // tn
    nparts = ssq.shape[0]
    wspec = pl.BlockSpec((D_MODEL, tn), lambda i, j: (0, j))
    cwspec = pl.BlockSpec((FFN_CONV, tn), lambda i, j: (0, j))
    cbspec = pl.BlockSpec((1, tn), lambda i, j: (0, j))
    if prev_rows is None:
        tm = min(tm, seqlen)
        tps = seqlen // tm
        in_specs = [pl.BlockSpec((tm, D_MODEL), lambda i, j: (i, 0)),
                    pl.BlockSpec((nparts, tm, 1), lambda i, j: (0, i, 0)), wspec, wspec, cwspec, cbspec]
        out_specs = [pl.BlockSpec((tm, tn), lambda i, j: (i, j)),
                     pl.BlockSpec((1, FFN_CONV - 1, tn), lambda i, j: (i, 0, j))]
        out_shape = [jax.ShapeDtypeStruct((m, D_FF_PAD), BF16),
                     jax.ShapeDtypeStruct((m // tm, FFN_CONV - 1, D_FF_PAD), F32)]
        scratch = [pltpu.VMEM((nj, 8, tn), F32)]
        args = (xw, ssq, w_gate, w_up, cw, cb)
        rows = None if cast_w is None else _cast_job(cast_w, (m // tm) * nj)
        if rows is not None:
            ispec, ospec, oshape = _cast_specs(cast_w, rows, lambda i, j: i * nj + j)
            in_specs.append(ispec)
            out_specs.append(ospec)
            out_shape.append(oshape)
            args = args + (cast_w,)
        body = functools.partial(_ffn_gate_body, period=None, tiles_per_seq=tps, cast=rows is not None)
    else:
        assert seqlen & (seqlen - 1) == 0
        tm = m
        in_specs = [pl.BlockSpec((tm, D_MODEL), lambda i, j: (i, 0)),
                    pl.BlockSpec((nparts, tm, 1), lambda i, j: (0, i, 0)), wspec, wspec, cwspec, cbspec,
                    pl.BlockSpec((tm, tn), lambda i, j: (i, j))]
        out_specs = [pl.BlockSpec((tm, tn), lambda i, j: (i, j)),
                     pl.BlockSpec((tm, tn), lambda i, j: (i, j))]
        out_shape = [jax.ShapeDtypeStruct((m, D_FF_PAD), BF16),
                     jax.ShapeDtypeStruct((m, D_FF_PAD), F32)]
        scratch = []
        args = (xw, ssq, w_gate, w_up, cw, cb, prev_rows)
        body = functools.partial(_ffn_gate_body, period=seqlen, tiles_per_seq=None)
    outs = pl.pallas_call(
        body,
        grid=(m // tm, nj),
        in_specs=in_specs,
        out_specs=out_specs,
        out_shape=out_shape,
        scratch_shapes=scratch,
        compiler_params=_params(2),
        name="ffn_gate",
    )(*args)
    return outs if len(outs) == 3 else (*outs, None)


def _down_body(a_ref, w_ref, res_ref, o_ref, *, chunk):
    @pl.when(pl.program_id(2) == 0)
    def _():
        o_ref[...] = res_ref[...]

    a = a_ref[...]
    tk = w_ref.shape[0]
    row = pl.program_id(2) * tk + lax.broadcasted_iota(jnp.int32, (tk, 1), 0)
    for c in range(o_ref.shape[1] // chunk):
        cs = slice(c * chunk, (c + 1) * chunk)
        o_ref[:, cs] += _dot(a, jnp.where(row < D_FF, w_ref[:, cs], 0))


def _down_proj(a, w_down, res, *, tm=1024, tn=2048, tk=1024):
    m, kdim = a.shape
    n = w_down.shape[1]
    tm = min(tm, m)
    return pl.pallas_call(
        functools.partial(_down_body, chunk=512),
        grid=(n // tn, m // tm, kdim // tk),
        in_specs=[pl.BlockSpec((tm, tk), lambda j, i, k: (i, k)),
                  pl.BlockSpec((tk, tn), lambda j, i, k: (k, j)),
                  pl.BlockSpec((tm, tn), lambda j, i, k: (i, j))],
        out_specs=pl.BlockSpec((tm, tn), lambda j, i, k: (i, j)),
        out_shape=jax.ShapeDtypeStruct((m, n), F32),
        compiler_params=_params(3),
        name="down_proj",
    )(a, w_down, res)


def _remember_bf16(bf16_w, wts, name, hosted):
    if name not in bf16_w:
        bf16_w[name] = hosted if hosted is not None else wts[name].astype(BF16)


def _layer(x, nseq, seqlen, wts, bf16_w, bias16, attn_state, ssm_state, conv_state, ffn_state):
    m = x.shape[0]
    todo = lambda name: None if name in bf16_w else wts[name]
    h = _rmsnorm(x, wts["mix_norm_w"])
    proj, dt_raw = _in_proj(h, wts["w_in_t"])
    new_conv = proj.reshape(nseq, seqlen, PROJ_W)[:, seqlen - (SSD_CONV - 1):, COL_XBC:COL_XBC + CONV_DIM]

    qw = wts["q_norm_w"].reshape(1, HEAD_DIM)
    kw = wts["k_norm_w"].reshape(1, HEAD_DIM)
    if attn_state is None:
        names = [n for n in ("w_gate",) if n not in bf16_w]
        attn, new_k, new_v, hosted = _attn_prompt(proj, wts["attn_sinks"], qw, kw, bias16, nseq, seqlen,
                                                  cast_ws=tuple(wts[n] for n in names))
        for k, n in enumerate(names):
            _remember_bf16(bf16_w, wts, n, None if hosted is None else hosted[k])
        new_k = new_k.reshape(nseq, WINDOW, KV_HEADS, HEAD_DIM)
        new_v = new_v.reshape(nseq, WINDOW, KV_HEADS, HEAD_DIM)
        conv_prev = jnp.zeros((nseq, SSD_CONV - 1, CONV_DIM), F32)
        h0 = jnp.zeros((nseq, SSD_WIDTH, SSD_STATE), F32)
    else:
        win_k, win_v = attn_state
        attn, new_k, new_v = _attn_sample(proj, win_k, win_v, wts["attn_sinks"], qw, kw, bias16, nseq, seqlen)
        _remember_bf16(bf16_w, wts, "w_gate", None)
        conv_prev = conv_state
        h0 = ssm_state.reshape(nseq, SSD_WIDTH, SSD_STATE)

    names = [n for n in ("w_up", "w_out") if n not in bf16_w]
    y, h_t, hosted = _ssd(proj, dt_raw, conv_prev, h0, wts["ssd_conv_w"], wts["ssd_conv_b"], wts["ssd_dt_bias"],
                          wts["ssd_A_log"], wts["ssd_D"], wts["ssd_norm_w"], nseq, seqlen,
                          cast_ws=tuple(wts[n] for n in names))
    for k, n in enumerate(names):
        _remember_bf16(bf16_w, wts, n, None if hosted is None else hosted[k])
    h_t = h_t.reshape(nseq, SSD_HEADS, SSD_HEAD_DIM, SSD_STATE)

    x1, xw, ssq = _out_proj(attn, y, bf16_w["w_out"], x, wts["ffn_norm_w"])
    if ffn_state is None:
        a, last, hosted = _ffn_gate(xw, ssq, bf16_w["w_gate"], bf16_w["w_up"], wts["ffn_conv_w"],
                                    wts["ffn_conv_b"], nseq=nseq, seqlen=seqlen, cast_w=todo("w_down"))
        new_ffn = last.reshape(nseq, -1, FFN_CONV - 1, D_FF_PAD)[:, -1, :, :D_FF]
    else:
        prev_rows = jnp.pad(ffn_state, ((0, 0), (0, seqlen - (FFN_CONV - 1)), (0, D_FF_PAD - D_FF)))
        a, g, hosted = _ffn_gate(xw, ssq, bf16_w["w_gate"], bf16_w["w_up"], wts["ffn_conv_w"], wts["ffn_conv_b"],
                                 nseq=nseq, seqlen=seqlen, prev_rows=prev_rows.reshape(m, D_FF_PAD))
        new_ffn = g.reshape(nseq, seqlen, D_FF_PAD)[:, seqlen - (FFN_CONV - 1):, :D_FF]
    _remember_bf16(bf16_w, wts, "w_down", hosted)
    x2 = _down_proj(a, bf16_w["w_down"], x1)
    return x2, new_k, new_v, h_t, new_conv, new_ffn


def kernel(x_prompt, x_sample, state_attn_k, state_attn_v, state_ssm, state_ssd_conv, state_ffn_conv, rel_bias, mix_norm_w, w_in, q_norm_w, k_norm_w, attn_sinks, ssd_conv_w, ssd_conv_b, ssd_dt_bias, ssd_A_log, ssd_D, ssd_norm_w, w_out, ffn_norm_w, w_gate, w_up, ffn_conv_w, ffn_conv_b, w_down):
    depth = w_in.shape[0]
    bp, lp, _ = x_prompt.shape
    bs, ls, _ = x_sample.shape
    bias16 = _bias_table(rel_bias)
    yp = x_prompt.reshape(bp * lp, D_MODEL)
    ys = x_sample.reshape(bs * ls, D_MODEL)
    outs_p, outs_s = [], []
    pad_heads = lambda v: jnp.pad(v.reshape(1, SSD_HEADS), ((0, 0), (0, ROWS - SSD_HEADS)))
    pad_ff = lambda v: jnp.pad(v, ((0, 0), (0, D_FF_PAD - D_FF)))
    for l in range(depth):
        wts = dict(
            mix_norm_w=mix_norm_w[l],
            w_in_t=w_in[l].T,

            q_norm_w=q_norm_w[l], k_norm_w=k_norm_w[l], attn_sinks=attn_sinks[l],
            ssd_conv_w=ssd_conv_w[l], ssd_conv_b=ssd_conv_b[l].reshape(1, CONV_DIM),
            ssd_dt_bias=pad_heads(ssd_dt_bias[l]), ssd_A_log=pad_heads(ssd_A_log[l]),
            ssd_D=jnp.repeat(ssd_D[l], SSD_HEAD_DIM).reshape(1, SSD_WIDTH),
            ssd_norm_w=ssd_norm_w[l].reshape(1, SSD_WIDTH),
            w_out=w_out[l], ffn_norm_w=ffn_norm_w[l],
            w_gate=w_gate[l], w_up=w_up[l],
            ffn_conv_w=pad_ff(ffn_conv_w[l]), ffn_conv_b=pad_ff(ffn_conv_b[l].reshape(1, D_FF)),
            w_down=w_down[l],
        )
        bf16_w = {}
        yp, *sp = _layer(yp, bp, lp, wts, bf16_w, bias16, None, None, None, None)
        ys, *ss = _layer(ys, bs, ls, wts, bf16_w, bias16, (state_attn_k[l], state_attn_v[l]), state_ssm[l],
                         state_ssd_conv[l], state_ffn_conv[l])
        outs_p.append(sp)
        outs_s.append(ss)
    stack = lambda outs, i: jnp.stack([o[i] for o in outs])
    return (yp.reshape(bp, lp, D_MODEL), ys.reshape(bs, ls, D_MODEL),
            stack(outs_p, 0), stack(outs_p, 1), stack(outs_p, 2), stack(outs_p, 3), stack(outs_p, 4),
            stack(outs_s, 0), stack(outs_s, 1), stack(outs_s, 2), stack(outs_s, 3), stack(outs_s, 4))
```

```python
import functools
import math

import numpy as np
import jax
import jax.numpy as jnp
from jax import lax
from jax.experimental import pallas as pl
from jax.experimental.pallas import tpu as pltpu

F32 = jnp.float32
BF16 = jnp.bfloat16

D_MODEL = 4096
HEAD_DIM = 128
ATTN_HEADS = 16
KV_HEADS = 4
Q_PER_KV = 4
ATTN_WIDTH = 2048
WINDOW = 128
N_BUCKETS = 32
MAX_DISTANCE = 128
SSD_HEAD_DIM = 64
SSD_HEADS = 32
SSD_WIDTH = 2048
SSD_GROUPS = 8
SSD_STATE = 128
SSD_CONV = 4
CONV_DIM = 4096
KV_WIDTH = KV_HEADS * HEAD_DIM
D_FF = 11008
D_FF_PAD = 11264
FFN_CONV = 3
EPS = 1e-6
NEG = -1e30
SCALE = HEAD_DIM ** -0.5

ROWS = 128
GROUP_ROWS = 4 * SSD_HEAD_DIM
COL_Z = ATTN_WIDTH
COL_XBC = COL_Z + SSD_WIDTH
COL_K = COL_XBC + CONV_DIM
COL_V = COL_K + KV_WIDTH
PROJ_W = COL_V + KV_WIDTH
VMEM_LIMIT_MB = 56


def _params(n_axes, vmem_mb=VMEM_LIMIT_MB):
    return pltpu.CompilerParams(dimension_semantics=("arbitrary",) * n_axes,
                                vmem_limit_bytes=vmem_mb << 20)


def _dot(a, b):
    return jnp.dot(a, b, preferred_element_type=F32)


def _dot_nt(a, b):
    return lax.dot_general(a, b, (((1,), (1,)), ((), ())), preferred_element_type=F32)


def _dot_tn(a, b):
    return lax.dot_general(a, b, (((0,), (0,)), ((), ())), preferred_element_type=F32)


def _split3(x):
    a = x.astype(BF16)
    r = x - a.astype(F32)
    b = r.astype(BF16)
    c = (r - b.astype(F32)).astype(BF16)
    return a, b, c


def _dot3_l(x, w):
    a, b, c = _split3(x)
    return (_dot(a, w) + _dot(b, w)) + _dot(c, w)


def _dot3_r(w, x3):
    return (_dot(w, x3[0]) + _dot(w, x3[1])) + _dot(w, x3[2])


def _sigmoid(x):
    return 1.0 / (1.0 + jnp.exp(-x))


def _rms_rows(x, w):
    r = lax.rsqrt(jnp.mean(x * x, axis=-1, keepdims=True) + EPS)
    return (x * r) * w


def _rmsnorm_body(x_ref, w_ref, o_ref):
    o_ref[...] = _rms_rows(x_ref[...], w_ref[...]).astype(o_ref.dtype)


def _rmsnorm(x, w, rows=256):
    m, d = x.shape
    rows = min(rows, m)
    return pl.pallas_call(
        _rmsnorm_body,
        grid=(m // rows,),
        in_specs=[pl.BlockSpec((rows, d), lambda i: (i, 0)),
                  pl.BlockSpec((1, d), lambda i: (0, 0))],
        out_specs=pl.BlockSpec((rows, d), lambda i: (i, 0)),
        out_shape=jax.ShapeDtypeStruct((m, d), BF16),
        compiler_params=_params(1, 32),
        name="rmsnorm",
    )(x, w.reshape(1, d))


def _in_proj_body(a_ref, bt_ref, dtw_ref, o_ref, dt_ref, *wcopy_ref):
    a = a_ref[...]
    bt = bt_ref[...].astype(BF16)
    o_ref[...] = _dot_nt(a, bt)
    if wcopy_ref:
        wcopy_ref[0][...] = bt

    @pl.when(pl.program_id(1) == 0)
    def _():
        row = lax.broadcasted_iota(jnp.int32, (dtw_ref.shape[0], 1), 0)
        dtw = jnp.where(row < SSD_HEADS, dtw_ref[...], 0.0)
        dt_ref[...] = _dot_nt(a, dtw.astype(BF16))


def _in_proj(h, w_in_t, w_main=None, *, keep_bf16=False, tm=1024):
    m = h.shape[0]
    tm = min(tm, m)
    tn = 512 if w_main is None else 1024
    nq, nkv2 = ATTN_WIDTH // tn, 2 * KV_WIDTH // tn
    nzx = (SSD_WIDTH + CONV_DIM) // tn

    def src_block(j):
        return jnp.where(j < nq, j, jnp.where(j < nq + nzx, j + nkv2, j - nzx))

    wspec = pl.BlockSpec((tn, D_MODEL), lambda i, j: (src_block(j), 0))
    out_specs = [pl.BlockSpec((tm, tn), lambda i, j: (i, j)),
                 pl.BlockSpec((tm, ROWS), lambda i, j: (i, 0))]
    out_shape = [jax.ShapeDtypeStruct((m, PROJ_W), F32),
                 jax.ShapeDtypeStruct((m, ROWS), F32)]
    if keep_bf16:
        assert w_main is None and m == tm
        out_specs.append(wspec)
        out_shape.append(jax.ShapeDtypeStruct((PROJ_W, D_MODEL), BF16))
    return pl.pallas_call(
        _in_proj_body,
        grid=(m // tm, PROJ_W // tn),
        in_specs=[pl.BlockSpec((tm, D_MODEL), lambda i, j: (i, 0)),
                  wspec,
                  pl.BlockSpec((ROWS, D_MODEL), lambda i, j: (PROJ_W // ROWS, 0))],
        out_specs=out_specs,
        out_shape=out_shape,
        compiler_params=_params(2),
        name="in_proj",
    )(h, w_in_t if w_main is None else w_main, w_in_t)


def _out_proj_body(a1_ref, a2_ref, b1_ref, b2_ref, res_ref, nw_ref, o_ref, xw_ref, ssq_ref):
    acc = (_dot(a1_ref[...].astype(BF16), b1_ref[...].astype(BF16))
           + _dot(a2_ref[...].astype(BF16), b2_ref[...].astype(BF16)))
    x1 = res_ref[...] + acc
    o_ref[...] = x1
    xw_ref[...] = (x1 * nw_ref[...]).astype(xw_ref.dtype)
    ssq_ref[0] = jnp.sum(x1 * x1, axis=-1, keepdims=True)


def _out_proj(attn, y, w_out, res, norm_w, *, tm=1024, tn=512):
    m = attn.shape[0]
    n = w_out.shape[1]
    tm = min(tm, m)
    return pl.pallas_call(
        _out_proj_body,
        grid=(m // tm, n // tn),
        in_specs=[pl.BlockSpec((tm, ATTN_WIDTH), lambda i, j: (i, 0)),
                  pl.BlockSpec((tm, SSD_WIDTH), lambda i, j: (i, 0)),
                  pl.BlockSpec((ATTN_WIDTH, tn), lambda i, j: (0, j)),
                  pl.BlockSpec((SSD_WIDTH, tn), lambda i, j: (1, j)),
                  pl.BlockSpec((tm, tn), lambda i, j: (i, j)),
                  pl.BlockSpec((1, tn), lambda i, j: (0, j))],
        out_specs=[pl.BlockSpec((tm, tn), lambda i, j: (i, j)),
                   pl.BlockSpec((tm, tn), lambda i, j: (i, j)),
                   pl.BlockSpec((1, tm, 1), lambda i, j: (j, i, 0))],
        out_shape=[jax.ShapeDtypeStruct((m, n), F32),
                   jax.ShapeDtypeStruct((m, n), BF16),
                   jax.ShapeDtypeStruct((n // tn, m, 1), F32)],
        compiler_params=_params(2),
        name="out_proj",
    )(attn, y, w_out, w_out, res, norm_w.reshape(1, n))


def _bucket_table():
    i = np.arange(ROWS)[:, None]
    j = np.arange(2 * ROWS)[None, :]
    dist = i + ROWS - j
    valid = (dist >= 0) & (dist < WINDOW)
    n = np.maximum(dist, 0)
    max_exact = N_BUCKETS // 2

    def large(dtype):
        nf = np.maximum(n, 1).astype(dtype)
        v = np.log(nf / dtype(max_exact)) / dtype(math.log(MAX_DISTANCE / max_exact)) * dtype(N_BUCKETS - max_exact)
        return np.minimum(max_exact + v.astype(np.int32), N_BUCKETS - 1)

    assert (large(np.float32) == large(np.float64)).all()
    bucket = np.where(n < max_exact, n, large(np.float32))
    return np.where(valid, bucket, -1).astype(np.int32)


def _bias_body(rb_ref, bkt_ref, o_ref):
    h = pl.program_id(0)
    bkt = bkt_ref[...]
    acc = jnp.full(bkt.shape, NEG, F32)
    for b in range(N_BUCKETS):
        acc = jnp.where(bkt == b, rb_ref[b, h], acc)
    o_ref[0] = acc


def _bias_table(rel_bias):
    return pl.pallas_call(
        _bias_body,
        grid=(ATTN_HEADS,),
        in_specs=[pl.BlockSpec(memory_space=pltpu.SMEM),
                  pl.BlockSpec((ROWS, 2 * ROWS), lambda h: (0, 0))],
        out_specs=pl.BlockSpec((1, ROWS, 2 * ROWS), lambda h: (h, 0, 0)),
        out_shape=jax.ShapeDtypeStruct((ATTN_HEADS, ROWS, 2 * ROWS), F32),
        compiler_params=_params(1, 16),
        name="bias_table",
    )(rel_bias, jnp.asarray(_bucket_table()))


def _cast_job(w, nsteps):
    rows = -(-w.shape[0] // nsteps)
    rows = -(-rows // 16) * 16
    return rows if rows * w.shape[1] * 4 <= (8 << 20) else None


def _cast_specs(w, rows, step_of):
    nblk = -(-w.shape[0] // rows)
    spec = pl.BlockSpec((rows, w.shape[1]), lambda *g: (jnp.minimum(step_of(*g), nblk - 1), 0))
    return spec, spec, jax.ShapeDtypeStruct(w.shape, BF16)


def _softmax_sink(s_scr, bias, sink, p_scr):
    sc = s_scr[...] * SCALE + bias
    m = jnp.maximum(jnp.max(sc, axis=-1, keepdims=True), sink)
    p = jnp.exp(sc - m)
    den = jnp.sum(p, axis=-1, keepdims=True) + jnp.exp(sink - m)
    p_scr[...] = (p * (1.0 / den)).astype(BF16)


def _attn_prompt_body(q_ref, kc_ref, kp_ref, vc_ref, vp_ref, qw_ref, kw_ref, bias_ref, sink_ref, *rest, ncast):
    o_ref, pk_ref, pv_ref = rest[ncast:ncast + 3]
    for src_ref, dst_ref in zip(rest[:ncast], rest[ncast + 3:2 * ncast + 3]):
        dst_ref[...] = src_ref[...].astype(BF16)
    qn_scr, kn_scr, s_scr, p_scr = rest[2 * ncast + 3:]
    qw = qw_ref[...]
    kw = kw_ref[...]
    grows = Q_PER_KV * ROWS
    for h in range(ATTN_HEADS):
        hs = slice(h * HEAD_DIM, (h + 1) * HEAD_DIM)
        qn_scr[:, hs] = _rms_rows(q_ref[:, hs], qw)
    for kv in range(KV_HEADS):
        sl = slice(kv * HEAD_DIM, (kv + 1) * HEAD_DIM)
        kn_scr[0:ROWS, sl] = _rms_rows(kp_ref[:, sl], kw)
        kn_scr[ROWS:2 * ROWS, sl] = _rms_rows(kc_ref[:, sl], kw)
    pk_ref[0] = kn_scr[ROWS:2 * ROWS, :]
    pv_ref[0] = vc_ref[...]

    for kv in range(KV_HEADS):
        sl = slice(kv * HEAD_DIM, (kv + 1) * HEAD_DIM)
        qs = jnp.concatenate([qn_scr[:, (kv * Q_PER_KV + g) * HEAD_DIM:(kv * Q_PER_KV + g + 1) * HEAD_DIM]
                              for g in range(Q_PER_KV)], axis=0).astype(BF16)
        s_scr[kv * grows:(kv + 1) * grows, :] = _dot_nt(qs, kn_scr[:, sl].astype(BF16))

    col = lax.broadcasted_iota(jnp.int32, (1, 2 * ROWS), 1)
    no_prev = jnp.where(jnp.logical_and(pl.program_id(1) == 0, col < ROWS), NEG, 0.0)
    _softmax_sink(s_scr, bias_ref[...] + no_prev, sink_ref[...], p_scr)

    for kv in range(KV_HEADS):
        sl = slice(kv * HEAD_DIM, (kv + 1) * HEAD_DIM)
        vcat = jnp.concatenate([vp_ref[:, sl], vc_ref[:, sl]], axis=0).astype(BF16)
        o = _dot(p_scr[kv * grows:(kv + 1) * grows, :], vcat)
        for g in range(Q_PER_KV):
            h = kv * Q_PER_KV + g
            o_ref[:, h * HEAD_DIM:(h + 1) * HEAD_DIM] = o[g * ROWS:(g + 1) * ROWS].astype(o_ref.dtype)


def _attn_prompt(proj, sinks, qw, kw, bias16, nseq, seqlen, cast_ws=()):
    nb = seqlen // ROWS
    kblk, vblk = COL_K // KV_WIDTH, COL_V // KV_WIDTH
    srows = ATTN_HEADS * ROWS
    cur = lambda b, i: b * nb + i
    prev = lambda b, i: b * nb + jnp.maximum(i - 1, 0)
    const = lambda b, i: (0, 0)
    in_specs = [pl.BlockSpec((ROWS, ATTN_WIDTH), lambda b, i: (cur(b, i), 0)),
                pl.BlockSpec((ROWS, KV_WIDTH), lambda b, i: (cur(b, i), kblk)),
                pl.BlockSpec((ROWS, KV_WIDTH), lambda b, i: (prev(b, i), kblk)),
                pl.BlockSpec((ROWS, KV_WIDTH), lambda b, i: (cur(b, i), vblk)),
                pl.BlockSpec((ROWS, KV_WIDTH), lambda b, i: (prev(b, i), vblk)),
                pl.BlockSpec((1, HEAD_DIM), const),
                pl.BlockSpec((1, HEAD_DIM), const),
                pl.BlockSpec((srows, 2 * ROWS), const),
                pl.BlockSpec((srows, 1), const)]
    out_specs = [pl.BlockSpec((ROWS, ATTN_WIDTH), lambda b, i: (cur(b, i), 0)),
                 pl.BlockSpec((1, ROWS, KV_WIDTH), lambda b, i: (b, 0, 0)),
                 pl.BlockSpec((1, ROWS, KV_WIDTH), lambda b, i: (b, 0, 0))]
    out_shape = [jax.ShapeDtypeStruct((nseq * seqlen, ATTN_WIDTH), BF16),
                 jax.ShapeDtypeStruct((nseq, ROWS, KV_WIDTH), F32),
                 jax.ShapeDtypeStruct((nseq, ROWS, KV_WIDTH), F32)]
    args = [proj, proj, proj, proj, proj, qw, kw, bias16.reshape(srows, 2 * ROWS),
            jnp.repeat(sinks, ROWS).reshape(srows, 1)]
    rows = [_cast_job(w, nseq * nb) for w in cast_ws]
    hosted = bool(cast_ws) and all(r is not None for r in rows)
    if hosted:
        for w, r in zip(cast_ws, rows):
            ispec, ospec, oshape = _cast_specs(w, r, cur)
            in_specs.append(ispec)
            out_specs.append(ospec)
            out_shape.append(oshape)
            args.append(w)
    ncast = len(cast_ws) if hosted else 0
    outs = pl.pallas_call(
        functools.partial(_attn_prompt_body, ncast=ncast),
        grid=(nseq, nb),
        in_specs=in_specs,
        out_specs=out_specs,
        out_shape=out_shape,
        scratch_shapes=[pltpu.VMEM((ROWS, ATTN_WIDTH), F32),
                        pltpu.VMEM((2 * ROWS, KV_WIDTH), F32),
                        pltpu.VMEM((srows, 2 * ROWS), F32),
                        pltpu.VMEM((srows, 2 * ROWS), BF16)],
        compiler_params=_params(2, 48),
        name="attn_prompt",
    )(*args)
    return (*outs[:3], tuple(outs[3:]) if hosted else None)


def _attn_sample_body(q_ref, kn_ref, vn_ref, wk_ref, wv_ref, qw_ref, kw_ref, bias_ref, sink_ref,
                      o_ref, sk_ref, sv_ref, qn_scr, kn_scr, s_scr, p_scr, *, nsq, ntok):
    qw = qw_ref[...]
    kw = kw_ref[...]
    grows = Q_PER_KV * ntok
    keep = ROWS - ntok
    for h in range(ATTN_HEADS):
        hs = slice(h * HEAD_DIM, (h + 1) * HEAD_DIM)
        qn_scr[:, hs] = _rms_rows(q_ref[:, hs], qw)
    for kv in range(KV_HEADS):
        sl = slice(kv * HEAD_DIM, (kv + 1) * HEAD_DIM)
        kn_scr[:, sl] = _rms_rows(kn_ref[:, sl], kw)
    for s in range(nsq):
        rows = slice(s * ntok, (s + 1) * ntok)
        sk_ref[s, 0:keep] = wk_ref[s, ntok:ROWS]
        sv_ref[s, 0:keep] = wv_ref[s, ntok:ROWS]
        for kv in range(KV_HEADS):
            sl = slice(kv * HEAD_DIM, (kv + 1) * HEAD_DIM)
            sk_ref[s, keep:ROWS, kv, :] = kn_scr[rows, sl]
            sv_ref[s, keep:ROWS, kv, :] = vn_ref[rows, sl]

    pad = jnp.zeros((keep, HEAD_DIM), F32)
    for s in range(nsq):
        rows = slice(s * ntok, (s + 1) * ntok)
        for kv in range(KV_HEADS):
            sl = slice(kv * HEAD_DIM, (kv + 1) * HEAD_DIM)
            grp = s * KV_HEADS + kv
            qs = jnp.concatenate([qn_scr[rows, (kv * Q_PER_KV + g) * HEAD_DIM:(kv * Q_PER_KV + g + 1) * HEAD_DIM]
                                  for g in range(Q_PER_KV)], axis=0).astype(BF16)
            kcat = jnp.concatenate([wk_ref[s, :, kv, :], kn_scr[rows, sl], pad], axis=0).astype(BF16)
            s_scr[grp * grows:(grp + 1) * grows, :] = _dot_nt(qs, kcat)

    _softmax_sink(s_scr, bias_ref[...], sink_ref[...], p_scr)

    for s in range(nsq):
        rows = slice(s * ntok, (s + 1) * ntok)
        for kv in range(KV_HEADS):
            sl = slice(kv * HEAD_DIM, (kv + 1) * HEAD_DIM)
            grp = s * KV_HEADS + kv
            vcat = jnp.concatenate([wv_ref[s, :, kv, :], vn_ref[rows, sl], pad], axis=0).astype(BF16)
            o = _dot(p_scr[grp * grows:(grp + 1) * grows, :], vcat)
            for g in range(Q_PER_KV):
                h = kv * Q_PER_KV + g
                o_ref[rows, h * HEAD_DIM:(h + 1) * HEAD_DIM] = o[g * ntok:(g + 1) * ntok]


def _attn_sample(proj, win_k, win_v, sinks, qw, kw, bias16, nseq, ntok, nsq=8):
    kblk, vblk = COL_K // KV_WIDTH, COL_V // KV_WIDTH
    rows = nsq * ntok
    srows = nsq * ATTN_HEADS * ntok
    bias = jnp.tile(bias16[:, :ntok, :].reshape(ATTN_HEADS * ntok, 2 * ROWS), (nsq, 1))
    sink = jnp.tile(jnp.repeat(sinks, ntok), nsq).reshape(srows, 1)
    return pl.pallas_call(
        functools.partial(_attn_sample_body, nsq=nsq, ntok=ntok),
        grid=(nseq // nsq,),
        in_specs=[pl.BlockSpec((rows, ATTN_WIDTH), lambda i: (i, 0)),
                  pl.BlockSpec((rows, KV_WIDTH), lambda i: (i, kblk)),
                  pl.BlockSpec((rows, KV_WIDTH), lambda i: (i, vblk)),
                  pl.BlockSpec((nsq, ROWS, KV_HEADS, HEAD_DIM), lambda i: (i, 0, 0, 0)),
                  pl.BlockSpec((nsq, ROWS, KV_HEADS, HEAD_DIM), lambda i: (i, 0, 0, 0)),
                  pl.BlockSpec((1, HEAD_DIM), lambda i: (0, 0)),
                  pl.BlockSpec((1, HEAD_DIM), lambda i: (0, 0)),
                  pl.BlockSpec((srows, 2 * ROWS), lambda i: (0, 0)),
                  pl.BlockSpec((srows, 1), lambda i: (0, 0))],
        out_specs=[pl.BlockSpec((rows, ATTN_WIDTH), lambda i: (i, 0)),
                   pl.BlockSpec((nsq, ROWS, KV_HEADS, HEAD_DIM), lambda i: (i, 0, 0, 0)),
                   pl.BlockSpec((nsq, ROWS, KV_HEADS, HEAD_DIM), lambda i: (i, 0, 0, 0))],
        out_shape=[jax.ShapeDtypeStruct((nseq * ntok, ATTN_WIDTH), F32),
                   jax.ShapeDtypeStruct((nseq, ROWS, KV_HEADS, HEAD_DIM), F32),
                   jax.ShapeDtypeStruct((nseq, ROWS, KV_HEADS, HEAD_DIM), F32)],
        scratch_shapes=[pltpu.VMEM((rows, ATTN_WIDTH), F32),
                        pltpu.VMEM((rows, KV_WIDTH), F32),
                        pltpu.VMEM((srows, 2 * ROWS), F32),
                        pltpu.VMEM((srows, 2 * ROWS), BF16)],
        compiler_params=_params(1, 32),
        name="attn_sample",
    )(proj, proj, proj, win_k, win_v, qw, kw, bias, sink)


def _ssd_constants(qs):
    r = np.arange(ROWS)
    same = (r[:, None] // qs) == (r[None, :] // qs)
    mask = same & (r[None, :] <= r[:, None])
    sel = r[None, :] == ((r[:, None] // qs) * qs + qs - 1)
    expand = np.zeros((ROWS, SSD_WIDTH), np.float32)
    for h in range(SSD_HEADS):
        expand[h, h * SSD_HEAD_DIM:(h + 1) * SSD_HEAD_DIM] = 1.0
    to_bf = lambda a: jnp.asarray(a.astype(np.float32), BF16)
    return to_bf(mask), to_bf(sel), to_bf(expand)


def _ssd_body(xbc_ref, z_ref, dt_ref, cprev_ref, h0_ref, cw_ref, cb_ref, dtb_ref, alog_ref, dexp_ref,
              nw_ref, mask_ref, sel_ref, e_ref, *rest, nsq, qs, ncast):
    y_ref, ht_ref = rest[ncast:ncast + 2]
    for src_ref, dst_ref in zip(rest[:ncast], rest[ncast + 2:2 * ncast + 2]):
        dst_ref[...] = src_ref[...].astype(BF16)
    h_scr, cbuf, xc_scr, xdt_scr, xdd_scr, eacs_scr, yacc_scr, acs_scr, acst_scr, aclt_scr = rest[2 * ncast + 2:]
    c = pl.program_id(1)
    s = pl.program_id(2)

    @pl.when(c == 0)
    def _():
        h_scr[...] = h0_ref[0]

    @pl.when(jnp.logical_and(c == 0, s == 0))
    def _():
        cbuf[0:8, :] = jnp.zeros((8, CONV_DIM), F32)
        if nsq == 1:
            cbuf[5:8, :] = cprev_ref[0]

    @pl.when(s == 0)
    def _chunk_phase():
        ncc = 8 if qs == ROWS else 1
        cwid = CONV_DIM // ncc
        nrow = 8 + qs
        for sq in range(nsq):
            for cc in range(ncc):
                cs = slice(cc * cwid, (cc + 1) * cwid)
                if nsq > 1:
                    cbuf[5:8, cs] = cprev_ref[sq, :, cs]
                cbuf[8:nrow, cs] = xbc_ref[sq * qs:(sq + 1) * qs, cs]
                xa = cbuf[0:nrow, cs]
                acc = cb_ref[:, cs] + cw_ref[SSD_CONV - 1:SSD_CONV, cs] * xa[8:nrow]
                for t in range(SSD_CONV - 1):
                    acc = acc + cw_ref[t:t + 1, cs] * pltpu.roll(xa, nrow - (5 + t), 0)[0:qs]
                xc_scr[sq * qs:(sq + 1) * qs, cs] = acc * _sigmoid(acc)
                if nsq == 1:
                    cbuf[5:8, cs] = xbc_ref[qs - 3:qs, cs]

        x = dt_ref[...] + dtb_ref[...]
        dt = jnp.maximum(x, 0.0) + jnp.log1p(jnp.exp(-jnp.abs(x)))
        da = dt * (-jnp.exp(alog_ref[...]))
        mask_bf = mask_ref[...]
        acs = _dot3_r(mask_bf, _split3(da))
        acl = _dot3_r(sel_ref[...], _split3(acs))
        acs_scr[...] = acs
        acst_scr[...] = acs.T
        aclt_scr[...] = acl.T
        e = e_ref[...]
        xs = xc_scr[:, 0:SSD_WIDTH]
        xdt_scr[...] = xs * _dot3_l(dt, e)
        xdd_scr[...] = xs * _dot3_l(dt * jnp.exp(acl - acs), e)
        eacs_scr[...] = _dot3_l(jnp.exp(acs), e)

        maskb = mask_bf > 0.5
        lane = lax.broadcasted_iota(jnp.int32, (ROWS, 2 * SSD_HEAD_DIM), 1)
        for g in range(SSD_GROUPS):
            bg = xc_scr[:, SSD_WIDTH + g * SSD_STATE:SSD_WIDTH + (g + 1) * SSD_STATE].astype(BF16)
            cg = xc_scr[:, SSD_WIDTH + (SSD_GROUPS + g) * SSD_STATE:
                        SSD_WIDTH + (SSD_GROUPS + g + 1) * SSD_STATE].astype(BF16)
            cb = _dot_nt(cg, bg)
            for pp in range(2):
                ms = []
                for j in range(2):
                    h = 4 * g + 2 * pp + j
                    seg = acs_scr[:, h:h + 1] - acst_scr[h:h + 1, :]
                    ms.append((cb * jnp.exp(jnp.where(maskb, seg, NEG))).astype(BF16))
                c0 = (4 * g + 2 * pp) * SSD_HEAD_DIM
                cols = slice(c0, c0 + 2 * SSD_HEAD_DIM)
                xp = xdt_scr[:, cols]
                rhs = jnp.concatenate([jnp.where(lane < SSD_HEAD_DIM, xp, 0.0),
                                       jnp.where(lane >= SSD_HEAD_DIM, xp, 0.0)], axis=0).astype(BF16)
                yd = _dot(jnp.concatenate(ms, axis=1), rhs)
                yacc_scr[:, cols] = yd + dexp_ref[:, cols] * xc_scr[:, cols]

    if nsq == 1:
        r0 = 0
        rows = slice(0, qs)
    else:
        r0 = pl.multiple_of(s * qs, qs)
        rows = pl.ds(r0, qs)
        ridx = lax.broadcasted_iota(jnp.int32, (ROWS, 1), 0)
        rowmask = jnp.logical_and(ridx >= r0, ridx < r0 + qs)
    tlane = lax.broadcasted_iota(jnp.int32, (ROWS, ROWS), 1)
    xcol = jnp.sum(jnp.where(tlane == r0, aclt_scr[...], 0.0), axis=1, keepdims=True)
    dec = jnp.exp(jnp.broadcast_to(xcol, (ROWS, ROWS)))
    c_rows = xc_scr[rows, SSD_WIDTH + SSD_GROUPS * SSD_STATE:CONV_DIM].astype(BF16)
    yo, st = [], []
    for g in range(SSD_GROUPS):
        grows = slice(g * GROUP_ROWS, (g + 1) * GROUP_ROWS)
        yo.append(_dot_nt(c_rows[:, g * SSD_STATE:(g + 1) * SSD_STATE], h_scr[grows, :].astype(BF16)))
        bg = xc_scr[:, SSD_WIDTH + g * SSD_STATE:SSD_WIDTH + (g + 1) * SSD_STATE]
        if nsq > 1:
            bg = jnp.where(rowmask, bg, 0.0)
        st.append(_dot_tn(xdd_scr[:, grows].astype(BF16), bg.astype(BF16)))
    yacc_scr[rows, :] = yacc_scr[rows, :] + jnp.concatenate(yo, axis=1) * eacs_scr[rows, :]
    for h in range(SSD_HEADS):
        hrows = slice(h * SSD_HEAD_DIM, (h + 1) * SSD_HEAD_DIM)
        g, off = divmod(h * SSD_HEAD_DIM, GROUP_ROWS)
        h_new = dec[h:h + 1, :] * h_scr[hrows, :] + st[g][off:off + SSD_HEAD_DIM]
        h_scr[hrows, :] = h_new
        ht_ref[0, hrows, :] = h_new

    @pl.when(s == nsq - 1)
    def _finish():
        z = z_ref[...]
        yg = yacc_scr[...] * (z * _sigmoid(z))
        y_ref[...] = _rms_rows(yg, nw_ref[...]).astype(y_ref.dtype)


def _ssd(proj, dt_raw, conv_prev, h0, cw, cb, dtb, alog, dexp, nw, nseq, seqlen, cast_ws=()):
    if seqlen % ROWS == 0:
        qs, nsq, nchunk, ngrp = ROWS, 1, seqlen // ROWS, nseq
    else:
        qs, nsq, nchunk = seqlen, ROWS // seqlen, 1
        ngrp = nseq // nsq
    mask, sel, expand = _ssd_constants(qs)
    rowblk = lambda b, c, s: b * nchunk + c
    seq = lambda b, c, s: b * nsq + s
    const2 = lambda b, c, s: (0, 0)
    full = lambda shape: pl.BlockSpec(shape, const2)
    in_specs = [pl.BlockSpec((ROWS, CONV_DIM), lambda b, c, s: (rowblk(b, c, s), COL_XBC // CONV_DIM)),
                pl.BlockSpec((ROWS, SSD_WIDTH), lambda b, c, s: (rowblk(b, c, s), COL_Z // SSD_WIDTH)),
                pl.BlockSpec((ROWS, ROWS), lambda b, c, s: (rowblk(b, c, s), 0)),
                pl.BlockSpec((nsq, SSD_CONV - 1, CONV_DIM), lambda b, c, s: (b, 0, 0)),
                pl.BlockSpec((1, SSD_WIDTH, SSD_STATE), lambda b, c, s: (seq(b, c, s), 0, 0)),
                full((SSD_CONV, CONV_DIM)), full((1, CONV_DIM)), full((1, ROWS)), full((1, ROWS)),
                full((1, SSD_WIDTH)), full((1, SSD_WIDTH)),
                full((ROWS, ROWS)), full((ROWS, ROWS)), full((ROWS, SSD_WIDTH))]
    out_specs = [pl.BlockSpec((ROWS, SSD_WIDTH), lambda b, c, s: (rowblk(b, c, s), 0)),
                 pl.BlockSpec((1, SSD_WIDTH, SSD_STATE), lambda b, c, s: (seq(b, c, s), 0, 0))]
    out_shape = [jax.ShapeDtypeStruct((nseq * seqlen, SSD_WIDTH), BF16),
                 jax.ShapeDtypeStruct((nseq, SSD_WIDTH, SSD_STATE), F32)]
    args = [proj, proj, dt_raw, conv_prev, h0, cw, cb, dtb, alog, dexp, nw, mask, sel, expand]
    rows = [_cast_job(w, ngrp * nchunk * nsq) for w in cast_ws]
    hosted = bool(cast_ws) and all(r is not None for r in rows)
    if hosted:
        for w, r in zip(cast_ws, rows):
            ispec, ospec, oshape = _cast_specs(w, r, lambda b, c, s: (b * nchunk + c) * nsq + s)
            in_specs.append(ispec)
            out_specs.append(ospec)
            out_shape.append(oshape)
            args.append(w)
    outs = pl.pallas_call(
        functools.partial(_ssd_body, nsq=nsq, qs=qs, ncast=len(cast_ws) if hosted else 0),
        grid=(ngrp, nchunk, nsq),
        in_specs=in_specs,
        out_specs=out_specs,
        out_shape=out_shape,
        scratch_shapes=[pltpu.VMEM((SSD_WIDTH, SSD_STATE), F32),
                        pltpu.VMEM((8 + ROWS, CONV_DIM), F32),
                        pltpu.VMEM((ROWS, CONV_DIM), F32),
                        pltpu.VMEM((ROWS, SSD_WIDTH), F32),
                        pltpu.VMEM((ROWS, SSD_WIDTH), F32),
                        pltpu.VMEM((ROWS, SSD_WIDTH), F32),
                        pltpu.VMEM((ROWS, SSD_WIDTH), F32),
                        pltpu.VMEM((ROWS, ROWS), F32),
                        pltpu.VMEM((ROWS, ROWS), F32),
                        pltpu.VMEM((ROWS, ROWS), F32)],
        compiler_params=_params(3, 48),
        name="ssd",
    )(*args)
    return (*outs[:2], tuple(outs[2:]) if hosted else None)


def _ffn_gate_body(xw_ref, ssq_ref, wg_ref, wu_ref, cw_ref, cb_ref, *rest, period, tiles_per_seq, cast=False):
    if cast:
        wsrc_ref, a_ref, last_ref, wdst_ref, carry_scr = rest
        wdst_ref[...] = wsrc_ref[...].astype(BF16)
    elif period is None:
        a_ref, last_ref, carry_scr = rest
    else:
        prev_ref, a_ref, g_ref = rest
    r = lax.rsqrt(jnp.sum(ssq_ref[...], axis=0) * (1.0 / D_MODEL) + EPS)
    xw = xw_ref[...]
    tn = wg_ref.shape[1]
    col = pl.program_id(1) * tn + lax.broadcasted_iota(jnp.int32, (1, tn), 1)
    g = _dot(xw, jnp.where(col < D_FF, wg_ref[...], 0)) * r
    u = _dot(xw, jnp.where(col < D_FF, wu_ref[...], 0)) * r
    tm = g.shape[0]
    row = lax.broadcasted_iota(jnp.int32, g.shape, 0)
    if period is None:
        i, j = pl.program_id(0), pl.program_id(1)
        c8 = jnp.where(i % tiles_per_seq == 0, 0.0, carry_scr[j])
        carry_scr[j] = g[tm - 8:tm]
        last_ref[0] = g[tm - 2:tm]
        g_m1 = jnp.where(row == 0, c8[7:8], pltpu.roll(g, 1, 0))
        g_m2 = jnp.where(row == 0, c8[6:7], jnp.where(row == 1, c8[7:8], pltpu.roll(g, 2, 0)))
    else:
        g_ref[...] = g
        prev = prev_ref[...]
        t = row & (period - 1)
        g_m1 = jnp.where(t == 0, pltpu.roll(prev, tm - 1, 0), pltpu.roll(g, 1, 0))
        g_m2 = jnp.where(t < 2, prev, pltpu.roll(g, 2, 0))
    gc = cb_ref[...] + cw_ref[0:1] * g_m2
    gc = gc + cw_ref[1:2] * g_m1
    gc = gc + cw_ref[2:3] * g
    a_ref[...] = ((gc * _sigmoid(gc)) * u).astype(a_ref.dtype)


def _ffn_gate(xw, ssq, w_gate, w_up, cw, cb, *, nseq, seqlen, prev_rows=None, cast_w=None, tm=1024, tn=512):
    m = xw.shape[0]
    nj = D_FF_PAD // tn
    nparts = ssq.shape[0]
    wspec = pl.BlockSpec((D_MODEL, tn), lambda i, j: (0, j))
    cwspec = pl.BlockSpec((FFN_CONV, tn), lambda i, j: (0, j))
    cbspec = pl.BlockSpec((1, tn), lambda i, j: (0, j))
    if prev_rows is None:
        tm = min(tm, seqlen)
        tps = seqlen // tm
        in_specs = [pl.BlockSpec((tm, D_MODEL), lambda i, j: (i, 0)),
                    pl.BlockSpec((nparts, tm, 1), lambda i, j: (0, i, 0)), wspec, wspec, cwspec, cbspec]
        out_specs = [pl.BlockSpec((tm, tn), lambda i, j: (i, j)),
                     pl.BlockSpec((1, FFN_CONV - 1, tn), lambda i, j: (i, 0, j))]
        out_shape = [jax.ShapeDtypeStruct((m, D_FF_PAD), BF16),
                     jax.ShapeDtypeStruct((m // tm, FFN_CONV - 1, D_FF_PAD), F32)]
        scratch = [pltpu.VMEM((nj, 8, tn), F32)]
        args = (xw, ssq, w_gate, w_up, cw, cb)
        rows = None if cast_w is None else _cast_job(cast_w, (m // tm) * nj)
        if rows is not None:
            ispec, ospec, oshape = _cast_specs(cast_w, rows, lambda i, j: i * nj + j)
            in_specs.append(ispec)
            out_specs.append(ospec)
            out_shape.append(oshape)
            args = args + (cast_w,)
        body = functools.partial(_ffn_gate_body, period=None, tiles_per_seq=tps, cast=rows is not None)
    else:
        assert seqlen & (seqlen - 1) == 0
        tm = m
        in_specs = [pl.BlockSpec((tm, D_MODEL), lambda i, j: (i, 0)),
                    pl.BlockSpec((nparts, tm, 1), lambda i, j: (0, i, 0)), wspec, wspec, cwspec, cbspec,
                    pl.BlockSpec((tm, tn), lambda i, j: (i, j))]
        out_specs = [pl.BlockSpec((tm, tn), lambda i, j: (i, j)),
                     pl.BlockSpec((tm, tn), lambda i, j: (i, j))]
        out_shape = [jax.ShapeDtypeStruct((m, D_FF_PAD), BF16),
                     jax.ShapeDtypeStruct((m, D_FF_PAD), F32)]
        scratch = []
        args = (xw, ssq, w_gate, w_up, cw, cb, prev_rows)
        body = functools.partial(_ffn_gate_body, period=seqlen, tiles_per_seq=None)
    outs = pl.pallas_call(
        body,
        grid=(m // tm, nj),
        in_specs=in_specs,
        out_specs=out_specs,
        out_shape=out_shape,
        scratch_shapes=scratch,
        compiler_params=_params(2),
        name="ffn_gate",
    )(*args)
    return outs if len(outs) == 3 else (*outs, None)


def _down_body(a_ref, w_ref, res_ref, o_ref, *, chunk):
    @pl.when(pl.program_id(2) == 0)
    def _():
        o_ref[...] = res_ref[...]

    a = a_ref[...]
    tk = w_ref.shape[0]
    row = pl.program_id(2) * tk + lax.broadcasted_iota(jnp.int32, (tk, 1), 0)
    for c in range(o_ref.shape[1] // chunk):
        cs = slice(c * chunk, (c + 1) * chunk)
        o_ref[:, cs] += _dot(a, jnp.where(row < D_FF, w_ref[:, cs], 0))


def _down_proj(a, w_down, res, *, tm=1024, tn=2048, tk=1024):
    m, kdim = a.shape
    n = w_down.shape[1]
    tm = min(tm, m)
    return pl.pallas_call(
        functools.partial(_down_body, chunk=512),
        grid=(n // tn, m // tm, kdim // tk),
        in_specs=[pl.BlockSpec((tm, tk), lambda j, i, k: (i, k)),
                  pl.BlockSpec((tk, tn), lambda j, i, k: (k, j)),
                  pl.BlockSpec((tm, tn), lambda j, i, k: (i, j))],
        out_specs=pl.BlockSpec((tm, tn), lambda j, i, k: (i, j)),
        out_shape=jax.ShapeDtypeStruct((m, n), F32),
        compiler_params=_params(3),
        name="down_proj",
    )(a, w_down, res)


def _remember_bf16(bf16_w, wts, name, hosted):
    if name not in bf16_w:
        bf16_w[name] = hosted if hosted is not None else wts[name].astype(BF16)


def _layer_head(x, wts, bf16_w):
    h = _rmsnorm(x, wts["mix_norm_w"])
    if "w_in_t" in bf16_w:
        return _in_proj(h, wts["w_in_t"], bf16_w["w_in_t"])
    if x.shape[0] <= 1024:
        proj, dt_raw, bf16_w["w_in_t"] = _in_proj(h, wts["w_in_t"], keep_bf16=True)
        return proj, dt_raw
    return _in_proj(h, wts["w_in_t"])


def _layer(x, head, nseq, seqlen, wts, bf16_w, bias16, attn_state, ssm_state, conv_state, ffn_state):
    m = x.shape[0]
    todo = lambda name: None if name in bf16_w else wts[name]
    proj, dt_raw = head
    new_conv = proj.reshape(nseq, seqlen, PROJ_W)[:, seqlen - (SSD_CONV - 1):, COL_XBC:COL_XBC + CONV_DIM]

    qw = wts["q_norm_w"].reshape(1, HEAD_DIM)
    kw = wts["k_norm_w"].reshape(1, HEAD_DIM)
    if attn_state is None:
        names = [n for n in ("w_gate",) if n not in bf16_w]
        attn, new_k, new_v, hosted = _attn_prompt(proj, wts["attn_sinks"], qw, kw, bias16, nseq, seqlen,
                                                  cast_ws=tuple(wts[n] for n in names))
        for k, n in enumerate(names):
            _remember_bf16(bf16_w, wts, n, None if hosted is None else hosted[k])
        new_k = new_k.reshape(nseq, WINDOW, KV_HEADS, HEAD_DIM)
        new_v = new_v.reshape(nseq, WINDOW, KV_HEADS, HEAD_DIM)
        conv_prev = jnp.zeros((nseq, SSD_CONV - 1, CONV_DIM), F32)
        h0 = jnp.zeros((nseq, SSD_WIDTH, SSD_STATE), F32)
    else:
        win_k, win_v = attn_state
        attn, new_k, new_v = _attn_sample(proj, win_k, win_v, wts["attn_sinks"], qw, kw, bias16, nseq, seqlen)
        _remember_bf16(bf16_w, wts, "w_gate", None)
        conv_prev = conv_state
        h0 = ssm_state.reshape(nseq, SSD_WIDTH, SSD_STATE)

    names = [n for n in ("w_up", "w_out") if n not in bf16_w]
    y, h_t, hosted = _ssd(proj, dt_raw, conv_prev, h0, wts["ssd_conv_w"], wts["ssd_conv_b"], wts["ssd_dt_bias"],
                          wts["ssd_A_log"], wts["ssd_D"], wts["ssd_norm_w"], nseq, seqlen,
                          cast_ws=tuple(wts[n] for n in names))
    for k, n in enumerate(names):
        _remember_bf16(bf16_w, wts, n, None if hosted is None else hosted[k])
    h_t = h_t.reshape(nseq, SSD_HEADS, SSD_HEAD_DIM, SSD_STATE)

    x1, xw, ssq = _out_proj(attn, y, bf16_w["w_out"], x, wts["ffn_norm_w"])
    if ffn_state is None:
        a, last, hosted = _ffn_gate(xw, ssq, bf16_w["w_gate"], bf16_w["w_up"], wts["ffn_conv_w"],
                                    wts["ffn_conv_b"], nseq=nseq, seqlen=seqlen, cast_w=todo("w_down"))
        new_ffn = last.reshape(nseq, -1, FFN_CONV - 1, D_FF_PAD)[:, -1, :, :D_FF]
    else:
        prev_rows = jnp.pad(ffn_state, ((0, 0), (0, seqlen - (FFN_CONV - 1)), (0, D_FF_PAD - D_FF)))
        a, g, hosted = _ffn_gate(xw, ssq, bf16_w["w_gate"], bf16_w["w_up"], wts["ffn_conv_w"], wts["ffn_conv_b"],
                                 nseq=nseq, seqlen=seqlen, prev_rows=prev_rows.reshape(m, D_FF_PAD))
        new_ffn = g.reshape(nseq, seqlen, D_FF_PAD)[:, seqlen - (FFN_CONV - 1):, :D_FF]
    _remember_bf16(bf16_w, wts, "w_down", hosted)
    x2 = _down_proj(a, bf16_w["w_down"], x1)
    return x2, new_k, new_v, h_t, new_conv, new_ffn


def kernel(x_prompt, x_sample, state_attn_k, state_attn_v, state_ssm, state_ssd_conv, state_ffn_conv, rel_bias, mix_norm_w, w_in, q_norm_w, k_norm_w, attn_sinks, ssd_conv_w, ssd_conv_b, ssd_dt_bias, ssd_A_log, ssd_D, ssd_norm_w, w_out, ffn_norm_w, w_gate, w_up, ffn_conv_w, ffn_conv_b, w_down):
    depth = w_in.shape[0]
    bp, lp, _ = x_prompt.shape
    bs, ls, _ = x_sample.shape
    bias16 = _bias_table(rel_bias)
    yp = x_prompt.reshape(bp * lp, D_MODEL)
    ys = x_sample.reshape(bs * ls, D_MODEL)
    outs_p, outs_s = [], []
    pad_heads = lambda v: jnp.pad(v.reshape(1, SSD_HEADS), ((0, 0), (0, ROWS - SSD_HEADS)))
    pad_ff = lambda v: jnp.pad(v, ((0, 0), (0, D_FF_PAD - D_FF)))
    for l in range(depth):
        wts = dict(
            mix_norm_w=mix_norm_w[l],
            w_in_t=w_in[l].T,

            q_norm_w=q_norm_w[l], k_norm_w=k_norm_w[l], attn_sinks=attn_sinks[l],
            ssd_conv_w=ssd_conv_w[l], ssd_conv_b=ssd_conv_b[l].reshape(1, CONV_DIM),
            ssd_dt_bias=pad_heads(ssd_dt_bias[l]), ssd_A_log=pad_heads(ssd_A_log[l]),
            ssd_D=jnp.repeat(ssd_D[l], SSD_HEAD_DIM).reshape(1, SSD_WIDTH),
            ssd_norm_w=ssd_norm_w[l].reshape(1, SSD_WIDTH),
            w_out=w_out[l], ffn_norm_w=ffn_norm_w[l],
            w_gate=w_gate[l], w_up=w_up[l],
            ffn_conv_w=pad_ff(ffn_conv_w[l]), ffn_conv_b=pad_ff(ffn_conv_b[l].reshape(1, D_FF)),
            w_down=w_down[l],
        )
        bf16_w = {}
        head_s = _layer_head(ys, wts, bf16_w)
        head_p = _layer_head(yp, wts, bf16_w)
        yp, *sp = _layer(yp, head_p, bp, lp, wts, bf16_w, bias16, None, None, None, None)
        ys, *ss = _layer(ys, head_s, bs, ls, wts, bf16_w, bias16, (state_attn_k[l], state_attn_v[l]),
                         state_ssm[l], state_ssd_conv[l], state_ffn_conv[l])
        outs_p.append(sp)
        outs_s.append(ss)
    stack = lambda outs, i: jnp.stack([o[i] for o in outs])
    return (yp.reshape(bp, lp, D_MODEL), ys.reshape(bs, ls, D_MODEL),
            stack(outs_p, 0), stack(outs_p, 1), stack(outs_p, 2), stack(outs_p, 3), stack(outs_p, 4),
            stack(outs_s, 0), stack(outs_s, 1), stack(outs_s, 2), stack(outs_s, 3), stack(outs_s, 4))
```

```python
import functools
import math

import numpy as np
import jax
import jax.numpy as jnp
from jax import lax
from jax.experimental import pallas as pl
from jax.experimental.pallas import tpu as pltpu

F32 = jnp.float32
BF16 = jnp.bfloat16

D_MODEL = 4096
HEAD_DIM = 128
ATTN_HEADS = 16
KV_HEADS = 4
Q_PER_KV = 4
ATTN_WIDTH = 2048
WINDOW = 128
N_BUCKETS = 32
MAX_DISTANCE = 128
SSD_HEAD_DIM = 64
SSD_HEADS = 32
SSD_WIDTH = 2048
SSD_GROUPS = 8
SSD_STATE = 128
SSD_CONV = 4
CONV_DIM = 4096
KV_WIDTH = KV_HEADS * HEAD_DIM
D_FF = 11008
D_FF_PAD = 11264
FFN_CONV = 3
EPS = 1e-6
NEG = -1e30
SCALE = HEAD_DIM ** -0.5

ROWS = 128
GROUP_ROWS = 4 * SSD_HEAD_DIM
COL_Z = ATTN_WIDTH
COL_XBC = COL_Z + SSD_WIDTH
COL_K = COL_XBC + CONV_DIM
COL_V = COL_K + KV_WIDTH
PROJ_W = COL_V + KV_WIDTH
VMEM_LIMIT_MB = 56


def _params(n_axes, vmem_mb=VMEM_LIMIT_MB):
    return pltpu.CompilerParams(dimension_semantics=("arbitrary",) * n_axes,
                                vmem_limit_bytes=vmem_mb << 20)


def _dot(a, b):
    return jnp.dot(a, b, preferred_element_type=F32)


def _dot_nt(a, b):
    return lax.dot_general(a, b, (((1,), (1,)), ((), ())), preferred_element_type=F32)


def _dot_tn(a, b):
    return lax.dot_general(a, b, (((0,), (0,)), ((), ())), preferred_element_type=F32)


def _split3(x):
    a = x.astype(BF16)
    r = x - a.astype(F32)
    b = r.astype(BF16)
    c = (r - b.astype(F32)).astype(BF16)
    return a, b, c


def _dot3_l(x, w):
    a, b, c = _split3(x)
    return (_dot(a, w) + _dot(b, w)) + _dot(c, w)


def _dot3_r(w, x3):
    return (_dot(w, x3[0]) + _dot(w, x3[1])) + _dot(w, x3[2])


def _sigmoid(x):
    return 1.0 / (1.0 + jnp.exp(-x))


def _rms_rows(x, w):
    r = lax.rsqrt(jnp.mean(x * x, axis=-1, keepdims=True) + EPS)
    return (x * r) * w


def _rmsnorm_body(x_ref, w_ref, o_ref):
    o_ref[...] = _rms_rows(x_ref[...], w_ref[...]).astype(o_ref.dtype)


def _rmsnorm(x, w, rows=256):
    m, d = x.shape
    rows = min(rows, m)
    return pl.pallas_call(
        _rmsnorm_body,
        grid=(m // rows,),
        in_specs=[pl.BlockSpec((rows, d), lambda i: (i, 0)),
                  pl.BlockSpec((1, d), lambda i: (0, 0))],
        out_specs=pl.BlockSpec((rows, d), lambda i: (i, 0)),
        out_shape=jax.ShapeDtypeStruct((m, d), BF16),
        compiler_params=_params(1, 32),
        name="rmsnorm",
    )(x, w.reshape(1, d))


def _in_proj_body(a_ref, bt_ref, dtw_ref, o_ref, dt_ref, *wcopy_ref):
    a = a_ref[...]
    bt = bt_ref[...].astype(BF16)
    o_ref[...] = _dot_nt(a, bt)
    if wcopy_ref:
        wcopy_ref[0][...] = bt

    @pl.when(pl.program_id(1) == 0)
    def _():
        row = lax.broadcasted_iota(jnp.int32, (dtw_ref.shape[0], 1), 0)
        dtw = jnp.where(row < SSD_HEADS, dtw_ref[...], 0.0)
        dt_ref[...] = _dot_nt(a, dtw.astype(BF16))


def _in_proj(h, w_in_t, w_main=None, *, keep_bf16=False, tm=1024):
    m = h.shape[0]
    tm = min(tm, m)
    tn = 512 if w_main is None else 1024
    nq, nkv2 = ATTN_WIDTH // tn, 2 * KV_WIDTH // tn
    nzx = (SSD_WIDTH + CONV_DIM) // tn

    def src_block(j):
        return jnp.where(j < nq, j, jnp.where(j < nq + nzx, j + nkv2, j - nzx))

    wspec = pl.BlockSpec((tn, D_MODEL), lambda i, j: (src_block(j), 0))
    out_specs = [pl.BlockSpec((tm, tn), lambda i, j: (i, j)),
                 pl.BlockSpec((tm, ROWS), lambda i, j: (i, 0))]
    out_shape = [jax.ShapeDtypeStruct((m, PROJ_W), F32),
                 jax.ShapeDtypeStruct((m, ROWS), F32)]
    if keep_bf16:
        assert w_main is None and m == tm
        out_specs.append(wspec)
        out_shape.append(jax.ShapeDtypeStruct((PROJ_W, D_MODEL), BF16))
    return pl.pallas_call(
        _in_proj_body,
        grid=(m // tm, PROJ_W // tn),
        in_specs=[pl.BlockSpec((tm, D_MODEL), lambda i, j: (i, 0)),
                  wspec,
                  pl.BlockSpec((ROWS, D_MODEL), lambda i, j: (PROJ_W // ROWS, 0))],
        out_specs=out_specs,
        out_shape=out_shape,
        compiler_params=_params(2),
        name="in_proj",
    )(h, w_in_t if w_main is None else w_main, w_in_t)


def _out_proj_body(a1_ref, a2_ref, b1_ref, b2_ref, res_ref, nw_ref, o_ref, xw_ref, ssq_ref):
    acc = (_dot(a1_ref[...].astype(BF16), b1_ref[...].astype(BF16))
           + _dot(a2_ref[...].astype(BF16), b2_ref[...].astype(BF16)))
    x1 = res_ref[...] + acc
    o_ref[...] = x1
    xw_ref[...] = (x1 * nw_ref[...]).astype(xw_ref.dtype)
    ssq_ref[0] = jnp.sum(x1 * x1, axis=-1, keepdims=True)


def _out_proj(attn, y, w_out, res, norm_w, *, tm=1024, tn=512):
    m = attn.shape[0]
    n = w_out.shape[1]
    tm = min(tm, m)
    return pl.pallas_call(
        _out_proj_body,
        grid=(m // tm, n // tn),
        in_specs=[pl.BlockSpec((tm, ATTN_WIDTH), lambda i, j: (i, 0)),
                  pl.BlockSpec((tm, SSD_WIDTH), lambda i, j: (i, 0)),
                  pl.BlockSpec((ATTN_WIDTH, tn), lambda i, j: (0, j)),
                  pl.BlockSpec((SSD_WIDTH, tn), lambda i, j: (1, j)),
                  pl.BlockSpec((tm, tn), lambda i, j: (i, j)),
                  pl.BlockSpec((1, tn), lambda i, j: (0, j))],
        out_specs=[pl.BlockSpec((tm, tn), lambda i, j: (i, j)),
                   pl.BlockSpec((tm, tn), lambda i, j: (i, j)),
                   pl.BlockSpec((1, tm, 1), lambda i, j: (j, i, 0))],
        out_shape=[jax.ShapeDtypeStruct((m, n), F32),
                   jax.ShapeDtypeStruct((m, n), BF16),
                   jax.ShapeDtypeStruct((n // tn, m, 1), F32)],
        compiler_params=_params(2),
        name="out_proj",
    )(attn, y, w_out, w_out, res, norm_w.reshape(1, n))


def _bucket_table():
    i = np.arange(ROWS)[:, None]
    j = np.arange(2 * ROWS)[None, :]
    dist = i + ROWS - j
    valid = (dist >= 0) & (dist < WINDOW)
    n = np.maximum(dist, 0)
    max_exact = N_BUCKETS // 2

    def large(dtype):
        nf = np.maximum(n, 1).astype(dtype)
        v = np.log(nf / dtype(max_exact)) / dtype(math.log(MAX_DISTANCE / max_exact)) * dtype(N_BUCKETS - max_exact)
        return np.minimum(max_exact + v.astype(np.int32), N_BUCKETS - 1)

    assert (large(np.float32) == large(np.float64)).all()
    bucket = np.where(n < max_exact, n, large(np.float32))
    return np.where(valid, bucket, -1).astype(np.int32)


def _bias_body(rb_ref, bkt_ref, o_ref):
    h = pl.program_id(0)
    bkt = bkt_ref[...]
    acc = jnp.full(bkt.shape, NEG, F32)
    for b in range(N_BUCKETS):
        acc = jnp.where(bkt == b, rb_ref[b, h], acc)
    o_ref[0] = acc


def _bias_table(rel_bias):
    return pl.pallas_call(
        _bias_body,
        grid=(ATTN_HEADS,),
        in_specs=[pl.BlockSpec(memory_space=pltpu.SMEM),
                  pl.BlockSpec((ROWS, 2 * ROWS), lambda h: (0, 0))],
        out_specs=pl.BlockSpec((1, ROWS, 2 * ROWS), lambda h: (h, 0, 0)),
        out_shape=jax.ShapeDtypeStruct((ATTN_HEADS, ROWS, 2 * ROWS), F32),
        compiler_params=_params(1, 16),
        name="bias_table",
    )(rel_bias, jnp.asarray(_bucket_table()))


def _cast_job(w, nsteps):
    rows = -(-w.shape[0] // nsteps)
    rows = -(-rows // 16) * 16
    return rows if rows * w.shape[1] * 4 <= (8 << 20) else None


def _cast_specs(w, rows, step_of):
    nblk = -(-w.shape[0] // rows)
    spec = pl.BlockSpec((rows, w.shape[1]), lambda *g: (jnp.minimum(step_of(*g), nblk - 1), 0))
    return spec, spec, jax.ShapeDtypeStruct(w.shape, BF16)


def _softmax_sink(s_scr, bias, sink, p_scr):
    sc = s_scr[...] * SCALE + bias
    m = jnp.maximum(jnp.max(sc, axis=-1, keepdims=True), sink)
    p = jnp.exp(sc - m)
    den = jnp.sum(p, axis=-1, keepdims=True) + jnp.exp(sink - m)
    p_scr[...] = (p * (1.0 / den)).astype(BF16)


def _attn_prompt_body(q_ref, kc_ref, kp_ref, vc_ref, vp_ref, qw_ref, kw_ref, bias_ref, sink_ref, *rest, ncast):
    o_ref, pk_ref, pv_ref = rest[ncast:ncast + 3]
    for src_ref, dst_ref in zip(rest[:ncast], rest[ncast + 3:2 * ncast + 3]):
        dst_ref[...] = src_ref[...].astype(BF16)
    qn_scr, kn_scr, s_scr, p_scr = rest[2 * ncast + 3:]
    qw = qw_ref[...]
    kw = kw_ref[...]
    grows = Q_PER_KV * ROWS
    for h in range(ATTN_HEADS):
        hs = slice(h * HEAD_DIM, (h + 1) * HEAD_DIM)
        qn_scr[:, hs] = _rms_rows(q_ref[:, hs], qw)
    for kv in range(KV_HEADS):
        sl = slice(kv * HEAD_DIM, (kv + 1) * HEAD_DIM)
        kn_scr[0:ROWS, sl] = _rms_rows(kp_ref[:, sl], kw)
        kn_scr[ROWS:2 * ROWS, sl] = _rms_rows(kc_ref[:, sl], kw)
    pk_ref[0] = kn_scr[ROWS:2 * ROWS, :]
    pv_ref[0] = vc_ref[...]

    for kv in range(KV_HEADS):
        sl = slice(kv * HEAD_DIM, (kv + 1) * HEAD_DIM)
        qs = jnp.concatenate([qn_scr[:, (kv * Q_PER_KV + g) * HEAD_DIM:(kv * Q_PER_KV + g + 1) * HEAD_DIM]
                              for g in range(Q_PER_KV)], axis=0).astype(BF16)
        s_scr[kv * grows:(kv + 1) * grows, :] = _dot_nt(qs, kn_scr[:, sl].astype(BF16))

    col = lax.broadcasted_iota(jnp.int32, (1, 2 * ROWS), 1)
    no_prev = jnp.where(jnp.logical_and(pl.program_id(1) == 0, col < ROWS), NEG, 0.0)
    _softmax_sink(s_scr, bias_ref[...] + no_prev, sink_ref[...], p_scr)

    for kv in range(KV_HEADS):
        sl = slice(kv * HEAD_DIM, (kv + 1) * HEAD_DIM)
        vcat = jnp.concatenate([vp_ref[:, sl], vc_ref[:, sl]], axis=0).astype(BF16)
        o = _dot(p_scr[kv * grows:(kv + 1) * grows, :], vcat)
        for g in range(Q_PER_KV):
            h = kv * Q_PER_KV + g
            o_ref[:, h * HEAD_DIM:(h + 1) * HEAD_DIM] = o[g * ROWS:(g + 1) * ROWS].astype(o_ref.dtype)


def _attn_prompt(proj, sinks, qw, kw, bias16, nseq, seqlen, cast_ws=()):
    nb = seqlen // ROWS
    kblk, vblk = COL_K // KV_WIDTH, COL_V // KV_WIDTH
    srows = ATTN_HEADS * ROWS
    cur = lambda b, i: b * nb + i
    prev = lambda b, i: b * nb + jnp.maximum(i - 1, 0)
    const = lambda b, i: (0, 0)
    in_specs = [pl.BlockSpec((ROWS, ATTN_WIDTH), lambda b, i: (cur(b, i), 0)),
                pl.BlockSpec((ROWS, KV_WIDTH), lambda b, i: (cur(b, i), kblk)),
                pl.BlockSpec((ROWS, KV_WIDTH), lambda b, i: (prev(b, i), kblk)),
                pl.BlockSpec((ROWS, KV_WIDTH), lambda b, i: (cur(b, i), vblk)),
                pl.BlockSpec((ROWS, KV_WIDTH), lambda b, i: (prev(b, i), vblk)),
                pl.BlockSpec((1, HEAD_DIM), const),
                pl.BlockSpec((1, HEAD_DIM), const),
                pl.BlockSpec((srows, 2 * ROWS), const),
                pl.BlockSpec((srows, 1), const)]
    out_specs = [pl.BlockSpec((ROWS, ATTN_WIDTH), lambda b, i: (cur(b, i), 0)),
                 pl.BlockSpec((1, ROWS, KV_WIDTH), lambda b, i: (b, 0, 0)),
                 pl.BlockSpec((1, ROWS, KV_WIDTH), lambda b, i: (b, 0, 0))]
    out_shape = [jax.ShapeDtypeStruct((nseq * seqlen, ATTN_WIDTH), BF16),
                 jax.ShapeDtypeStruct((nseq, ROWS, KV_WIDTH), F32),
                 jax.ShapeDtypeStruct((nseq, ROWS, KV_WIDTH), F32)]
    args = [proj, proj, proj, proj, proj, qw, kw, bias16.reshape(srows, 2 * ROWS),
            jnp.repeat(sinks, ROWS).reshape(srows, 1)]
    rows = [_cast_job(w, nseq * nb) for w in cast_ws]
    hosted = bool(cast_ws) and all(r is not None for r in rows)
    if hosted:
        for w, r in zip(cast_ws, rows):
            ispec, ospec, oshape = _cast_specs(w, r, cur)
            in_specs.append(ispec)
            out_specs.append(ospec)
            out_shape.append(oshape)
            args.append(w)
    ncast = len(cast_ws) if hosted else 0
    outs = pl.pallas_call(
        functools.partial(_attn_prompt_body, ncast=ncast),
        grid=(nseq, nb),
        in_specs=in_specs,
        out_specs=out_specs,
        out_shape=out_shape,
        scratch_shapes=[pltpu.VMEM((ROWS, ATTN_WIDTH), F32),
                        pltpu.VMEM((2 * ROWS, KV_WIDTH), F32),
                        pltpu.VMEM((srows, 2 * ROWS), F32),
                        pltpu.VMEM((srows, 2 * ROWS), BF16)],
        compiler_params=_params(2, 48),
        name="attn_prompt",
    )(*args)
    return (*outs[:3], tuple(outs[3:]) if hosted else None)


def _attn_sample_body(q_ref, kn_ref, vn_ref, wk_ref, wv_ref, qw_ref, kw_ref, bias_ref, sink_ref,
                      o_ref, sk_ref, sv_ref, qn_scr, kn_scr, s_scr, p_scr, *, nsq, ntok):
    qw = qw_ref[...]
    kw = kw_ref[...]
    grows = Q_PER_KV * ntok
    keep = ROWS - ntok
    for h in range(ATTN_HEADS):
        hs = slice(h * HEAD_DIM, (h + 1) * HEAD_DIM)
        qn_scr[:, hs] = _rms_rows(q_ref[:, hs], qw)
    for kv in range(KV_HEADS):
        sl = slice(kv * HEAD_DIM, (kv + 1) * HEAD_DIM)
        kn_scr[:, sl] = _rms_rows(kn_ref[:, sl], kw)
    for s in range(nsq):
        rows = slice(s * ntok, (s + 1) * ntok)
        sk_ref[s, 0:keep] = wk_ref[s, ntok:ROWS]
        sv_ref[s, 0:keep] = wv_ref[s, ntok:ROWS]
        for kv in range(KV_HEADS):
            sl = slice(kv * HEAD_DIM, (kv + 1) * HEAD_DIM)
            sk_ref[s, keep:ROWS, kv, :] = kn_scr[rows, sl]
            sv_ref[s, keep:ROWS, kv, :] = vn_ref[rows, sl]

    pad = jnp.zeros((keep, HEAD_DIM), F32)
    for s in range(nsq):
        rows = slice(s * ntok, (s + 1) * ntok)
        for kv in range(KV_HEADS):
            sl = slice(kv * HEAD_DIM, (kv + 1) * HEAD_DIM)
            grp = s * KV_HEADS + kv
            qs = jnp.concatenate([qn_scr[rows, (kv * Q_PER_KV + g) * HEAD_DIM:(kv * Q_PER_KV + g + 1) * HEAD_DIM]
                                  for g in range(Q_PER_KV)], axis=0).astype(BF16)
            kcat = jnp.concatenate([wk_ref[s, :, kv, :], kn_scr[rows, sl], pad], axis=0).astype(BF16)
            s_scr[grp * grows:(grp + 1) * grows, :] = _dot_nt(qs, kcat)

    _softmax_sink(s_scr, bias_ref[...], sink_ref[...], p_scr)

    for s in range(nsq):
        rows = slice(s * ntok, (s + 1) * ntok)
        for kv in range(KV_HEADS):
            sl = slice(kv * HEAD_DIM, (kv + 1) * HEAD_DIM)
            grp = s * KV_HEADS + kv
            vcat = jnp.concatenate([wv_ref[s, :, kv, :], vn_ref[rows, sl], pad], axis=0).astype(BF16)
            o = _dot(p_scr[grp * grows:(grp + 1) * grows, :], vcat)
            for g in range(Q_PER_KV):
                h = kv * Q_PER_KV + g
                o_ref[rows, h * HEAD_DIM:(h + 1) * HEAD_DIM] = o[g * ntok:(g + 1) * ntok]


def _attn_sample(proj, win_k, win_v, sinks, qw, kw, bias16, nseq, ntok, nsq=8):
    kblk, vblk = COL_K // KV_WIDTH, COL_V // KV_WIDTH
    rows = nsq * ntok
    srows = nsq * ATTN_HEADS * ntok
    bias = jnp.tile(bias16[:, :ntok, :].reshape(ATTN_HEADS * ntok, 2 * ROWS), (nsq, 1))
    sink = jnp.tile(jnp.repeat(sinks, ntok), nsq).reshape(srows, 1)
    return pl.pallas_call(
        functools.partial(_attn_sample_body, nsq=nsq, ntok=ntok),
        grid=(nseq // nsq,),
        in_specs=[pl.BlockSpec((rows, ATTN_WIDTH), lambda i: (i, 0)),
                  pl.BlockSpec((rows, KV_WIDTH), lambda i: (i, kblk)),
                  pl.BlockSpec((rows, KV_WIDTH), lambda i: (i, vblk)),
                  pl.BlockSpec((nsq, ROWS, KV_HEADS, HEAD_DIM), lambda i: (i, 0, 0, 0)),
                  pl.BlockSpec((nsq, ROWS, KV_HEADS, HEAD_DIM), lambda i: (i, 0, 0, 0)),
                  pl.BlockSpec((1, HEAD_DIM), lambda i: (0, 0)),
                  pl.BlockSpec((1, HEAD_DIM), lambda i: (0, 0)),
                  pl.BlockSpec((srows, 2 * ROWS), lambda i: (0, 0)),
                  pl.BlockSpec((srows, 1), lambda i: (0, 0))],
        out_specs=[pl.BlockSpec((rows, ATTN_WIDTH), lambda i: (i, 0)),
                   pl.BlockSpec((nsq, ROWS, KV_HEADS, HEAD_DIM), lambda i: (i, 0, 0, 0)),
                   pl.BlockSpec((nsq, ROWS, KV_HEADS, HEAD_DIM), lambda i: (i, 0, 0, 0))],
        out_shape=[jax.ShapeDtypeStruct((nseq * ntok, ATTN_WIDTH), F32),
                   jax.ShapeDtypeStruct((nseq, ROWS, KV_HEADS, HEAD_DIM), F32),
                   jax.ShapeDtypeStruct((nseq, ROWS, KV_HEADS, HEAD_DIM), F32)],
        scratch_shapes=[pltpu.VMEM((rows, ATTN_WIDTH), F32),
                        pltpu.VMEM((rows, KV_WIDTH), F32),
                        pltpu.VMEM((srows, 2 * ROWS), F32),
                        pltpu.VMEM((srows, 2 * ROWS), BF16)],
        compiler_params=_params(1, 32),
        name="attn_sample",
    )(proj, proj, proj, win_k, win_v, qw, kw, bias, sink)


def _ssd_constants(qs):
    r = np.arange(ROWS)
    same = (r[:, None] // qs) == (r[None, :] // qs)
    mask = same & (r[None, :] <= r[:, None])
    sel = r[None, :] == ((r[:, None] // qs) * qs + qs - 1)
    expand = np.zeros((ROWS, SSD_WIDTH), np.float32)
    for h in range(SSD_HEADS):
        expand[h, h * SSD_HEAD_DIM:(h + 1) * SSD_HEAD_DIM] = 1.0
    to_bf = lambda a: jnp.asarray(a.astype(np.float32), BF16)
    return to_bf(mask), to_bf(sel), to_bf(expand)


def _ssd_body(xbc_ref, z_ref, dt_ref, cprev_ref, h0_ref, cw_ref, cb_ref, dtb_ref, alog_ref, dexp_ref,
              nw_ref, mask_ref, sel_ref, e_ref, *rest, nsq, qs, spp, ncast):
    y_ref, ht_ref = rest[ncast:ncast + 2]
    for src_ref, dst_ref in zip(rest[:ncast], rest[ncast + 2:2 * ncast + 2]):
        dst_ref[...] = src_ref[...].astype(BF16)
    h_scr, cbuf, xc_scr, xdt_scr, xdd_scr, eacs_scr, yacc_scr, acs_scr, acst_scr, aclt_scr = rest[2 * ncast + 2:]
    c = pl.program_id(1)
    s = pl.program_id(2)

    @pl.when(c == 0)
    def _():
        h_scr[...] = h0_ref[...]

    @pl.when(jnp.logical_and(c == 0, s == 0))
    def _():
        cbuf[0:8, :] = jnp.zeros((8, CONV_DIM), F32)
        if nsq == 1:
            cbuf[5:8, :] = cprev_ref[0]

    @pl.when(s == 0)
    def _chunk_phase():
        ncc = 8 if qs == ROWS else 1
        cwid = CONV_DIM // ncc
        nrow = 8 + qs
        for sq in range(nsq):
            for cc in range(ncc):
                cs = slice(cc * cwid, (cc + 1) * cwid)
                if nsq > 1:
                    cbuf[5:8, cs] = cprev_ref[sq, :, cs]
                cbuf[8:nrow, cs] = xbc_ref[sq * qs:(sq + 1) * qs, cs]
                xa = cbuf[0:nrow, cs]
                acc = cb_ref[:, cs] + cw_ref[SSD_CONV - 1:SSD_CONV, cs] * xa[8:nrow]
                for t in range(SSD_CONV - 1):
                    acc = acc + cw_ref[t:t + 1, cs] * pltpu.roll(xa, nrow - (5 + t), 0)[0:qs]
                xc_scr[sq * qs:(sq + 1) * qs, cs] = acc * _sigmoid(acc)
                if nsq == 1:
                    cbuf[5:8, cs] = xbc_ref[qs - 3:qs, cs]

        x = dt_ref[...] + dtb_ref[...]
        dt = jnp.maximum(x, 0.0) + jnp.log1p(jnp.exp(-jnp.abs(x)))
        da = dt * (-jnp.exp(alog_ref[...]))
        mask_bf = mask_ref[...]
        acs = _dot3_r(mask_bf, _split3(da))
        acl = _dot3_r(sel_ref[...], _split3(acs))
        acs_scr[...] = acs
        acst_scr[...] = acs.T
        aclt_scr[...] = acl.T
        e = e_ref[...]
        xs = xc_scr[:, 0:SSD_WIDTH]
        xdt_scr[...] = xs * _dot3_l(dt, e)
        xdd_scr[...] = xs * _dot3_l(dt * jnp.exp(acl - acs), e)
        eacs_scr[...] = _dot3_l(jnp.exp(acs), e)

        maskb = mask_bf > 0.5
        lane = lax.broadcasted_iota(jnp.int32, (ROWS, 2 * SSD_HEAD_DIM), 1)
        for g in range(SSD_GROUPS):
            bg = xc_scr[:, SSD_WIDTH + g * SSD_STATE:SSD_WIDTH + (g + 1) * SSD_STATE].astype(BF16)
            cg = xc_scr[:, SSD_WIDTH + (SSD_GROUPS + g) * SSD_STATE:
                        SSD_WIDTH + (SSD_GROUPS + g + 1) * SSD_STATE].astype(BF16)
            cb = _dot_nt(cg, bg)
            for pp in range(2):
                ms = []
                for j in range(2):
                    h = 4 * g + 2 * pp + j
                    seg = acs_scr[:, h:h + 1] - acst_scr[h:h + 1, :]
                    ms.append((cb * jnp.exp(jnp.where(maskb, seg, NEG))).astype(BF16))
                c0 = (4 * g + 2 * pp) * SSD_HEAD_DIM
                cols = slice(c0, c0 + 2 * SSD_HEAD_DIM)
                xp = xdt_scr[:, cols]
                rhs = jnp.concatenate([jnp.where(lane < SSD_HEAD_DIM, xp, 0.0),
                                       jnp.where(lane >= SSD_HEAD_DIM, xp, 0.0)], axis=0).astype(BF16)
                yd = _dot(jnp.concatenate(ms, axis=1), rhs)
                yacc_scr[:, cols] = yd + dexp_ref[:, cols] * xc_scr[:, cols]

    tlane = lax.broadcasted_iota(jnp.int32, (ROWS, ROWS), 1)
    ridx = lax.broadcasted_iota(jnp.int32, (ROWS, 1), 0)
    for u in range(spp):
        if nsq == 1:
            r0 = 0
            rows = slice(0, qs)
        else:
            r0 = pl.multiple_of((s * spp + u) * qs, qs)
            rows = pl.ds(r0, qs)
            rowmask = jnp.logical_and(ridx >= r0, ridx < r0 + qs)
        xcol = jnp.sum(jnp.where(tlane == r0, aclt_scr[...], 0.0), axis=1, keepdims=True)
        dec = jnp.exp(jnp.broadcast_to(xcol, (ROWS, ROWS)))
        c_rows = xc_scr[rows, SSD_WIDTH + SSD_GROUPS * SSD_STATE:CONV_DIM].astype(BF16)
        yo, st = [], []
        for g in range(SSD_GROUPS):
            grows = slice(g * GROUP_ROWS, (g + 1) * GROUP_ROWS)
            yo.append(_dot_nt(c_rows[:, g * SSD_STATE:(g + 1) * SSD_STATE], h_scr[u, grows, :].astype(BF16)))
            bg = xc_scr[:, SSD_WIDTH + g * SSD_STATE:SSD_WIDTH + (g + 1) * SSD_STATE]
            if nsq > 1:
                bg = jnp.where(rowmask, bg, 0.0)
            st.append(_dot_tn(xdd_scr[:, grows].astype(BF16), bg.astype(BF16)))
        yacc_scr[rows, :] = yacc_scr[rows, :] + jnp.concatenate(yo, axis=1) * eacs_scr[rows, :]
        for h in range(SSD_HEADS):
            hrows = slice(h * SSD_HEAD_DIM, (h + 1) * SSD_HEAD_DIM)
            g, off = divmod(h * SSD_HEAD_DIM, GROUP_ROWS)
            h_new = dec[h:h + 1, :] * h_scr[u, hrows, :] + st[g][off:off + SSD_HEAD_DIM]
            h_scr[u, hrows, :] = h_new
            ht_ref[u, hrows, :] = h_new

    @pl.when(s == nsq // spp - 1)
    def _finish():
        z = z_ref[...]
        yg = yacc_scr[...] * (z * _sigmoid(z))
        y_ref[...] = _rms_rows(yg, nw_ref[...]).astype(y_ref.dtype)


def _ssd(proj, dt_raw, conv_prev, h0, cw, cb, dtb, alog, dexp, nw, nseq, seqlen, cast_ws=()):
    if seqlen % ROWS == 0:
        qs, nsq, nchunk, ngrp = ROWS, 1, seqlen // ROWS, nseq
    else:
        qs, nsq, nchunk = seqlen, ROWS // seqlen, 1
        ngrp = nseq // nsq
    spp = 2 if nsq % 2 == 0 else 1
    nstep = nsq // spp
    mask, sel, expand = _ssd_constants(qs)
    rowblk = lambda b, c, s: b * nchunk + c
    seq = lambda b, c, s: b * nstep + s
    const2 = lambda b, c, s: (0, 0)
    full = lambda shape: pl.BlockSpec(shape, const2)
    in_specs = [pl.BlockSpec((ROWS, CONV_DIM), lambda b, c, s: (rowblk(b, c, s), COL_XBC // CONV_DIM)),
                pl.BlockSpec((ROWS, SSD_WIDTH), lambda b, c, s: (rowblk(b, c, s), COL_Z // SSD_WIDTH)),
                pl.BlockSpec((ROWS, ROWS), lambda b, c, s: (rowblk(b, c, s), 0)),
                pl.BlockSpec((nsq, SSD_CONV - 1, CONV_DIM), lambda b, c, s: (b, 0, 0)),
                pl.BlockSpec((spp, SSD_WIDTH, SSD_STATE), lambda b, c, s: (seq(b, c, s), 0, 0)),
                full((SSD_CONV, CONV_DIM)), full((1, CONV_DIM)), full((1, ROWS)), full((1, ROWS)),
                full((1, SSD_WIDTH)), full((1, SSD_WIDTH)),
                full((ROWS, ROWS)), full((ROWS, ROWS)), full((ROWS, SSD_WIDTH))]
    out_specs = [pl.BlockSpec((ROWS, SSD_WIDTH), lambda b, c, s: (rowblk(b, c, s), 0)),
                 pl.BlockSpec((spp, SSD_WIDTH, SSD_STATE), lambda b, c, s: (seq(b, c, s), 0, 0))]
    out_shape = [jax.ShapeDtypeStruct((nseq * seqlen, SSD_WIDTH), BF16),
                 jax.ShapeDtypeStruct((nseq, SSD_WIDTH, SSD_STATE), F32)]
    args = [proj, proj, dt_raw, conv_prev, h0, cw, cb, dtb, alog, dexp, nw, mask, sel, expand]
    rows = [_cast_job(w, ngrp * nchunk * nstep) for w in cast_ws]
    hosted = bool(cast_ws) and all(r is not None for r in rows)
    if hosted:
        for w, r in zip(cast_ws, rows):
            ispec, ospec, oshape = _cast_specs(w, r, lambda b, c, s: (b * nchunk + c) * nstep + s)
            in_specs.append(ispec)
            out_specs.append(ospec)
            out_shape.append(oshape)
            args.append(w)
    outs = pl.pallas_call(
        functools.partial(_ssd_body, nsq=nsq, qs=qs, spp=spp, ncast=len(cast_ws) if hosted else 0),
        grid=(ngrp, nchunk, nstep),
        in_specs=in_specs,
        out_specs=out_specs,
        out_shape=out_shape,
        scratch_shapes=[pltpu.VMEM((spp, SSD_WIDTH, SSD_STATE), F32),
                        pltpu.VMEM((8 + ROWS, CONV_DIM), F32),
                        pltpu.VMEM((ROWS, CONV_DIM), F32),
                        pltpu.VMEM((ROWS, SSD_WIDTH), F32),
                        pltpu.VMEM((ROWS, SSD_WIDTH), F32),
                        pltpu.VMEM((ROWS, SSD_WIDTH), F32),
                        pltpu.VMEM((ROWS, SSD_WIDTH), F32),
                        pltpu.VMEM((ROWS, ROWS), F32),
                        pltpu.VMEM((ROWS, ROWS), F32),
                        pltpu.VMEM((ROWS, ROWS), F32)],
        compiler_params=_params(3, 48),
        name="ssd",
    )(*args)
    return (*outs[:2], tuple(outs[2:]) if hosted else None)


def _ffn_gate_body(xw_ref, ssq_ref, wg_ref, wu_ref, cw_ref, cb_ref, *rest, period, tiles_per_seq, cast=False):
    if cast:
        wsrc_ref, a_ref, last_ref, wdst_ref, carry_scr = rest
        wdst_ref[...] = wsrc_ref[...].astype(BF16)
    elif period is None:
        a_ref, last_ref, carry_scr = rest
    else:
        prev_ref, a_ref, g_ref = rest
    r = lax.rsqrt(jnp.sum(ssq_ref[...], axis=0) * (1.0 / D_MODEL) + EPS)
    xw = xw_ref[...]
    tn = wg_ref.shape[1]
    col = pl.program_id(1) * tn + lax.broadcasted_iota(jnp.int32, (1, tn), 1)
    g = _dot(xw, jnp.where(col < D_FF, wg_ref[...], 0)) * r
    u = _dot(xw, jnp.where(col < D_FF, wu_ref[...], 0)) * r
    tm = g.shape[0]
    row = lax.broadcasted_iota(jnp.int32, g.shape, 0)
    if period is None:
        i, j = pl.program_id(0), pl.program_id(1)
        c8 = jnp.where(i % tiles_per_seq == 0, 0.0, carry_scr[j])
        carry_scr[j] = g[tm - 8:tm]
        last_ref[0] = g[tm - 2:tm]
        g_m1 = jnp.where(row == 0, c8[7:8], pltpu.roll(g, 1, 0))
        g_m2 = jnp.where(row == 0, c8[6:7], jnp.where(row == 1, c8[7:8], pltpu.roll(g, 2, 0)))
    else:
        g_ref[...] = g
        prev = prev_ref[...]
        t = row & (period - 1)
        g_m1 = jnp.where(t == 0, pltpu.roll(prev, tm - 1, 0), pltpu.roll(g, 1, 0))
        g_m2 = jnp.where(t < 2, prev, pltpu.roll(g, 2, 0))
    gc = cb_ref[...] + cw_ref[0:1] * g_m2
    gc = gc + cw_ref[1:2] * g_m1
    gc = gc + cw_ref[2:3] * g
    a_ref[...] = ((gc * _sigmoid(gc)) * u).astype(a_ref.dtype)


def _ffn_gate(xw, ssq, w_gate, w_up, cw, cb, *, nseq, seqlen, prev_rows=None, cast_w=None, tm=1024, tn=512):
    m = xw.shape[0]
    nj = D_FF_PAD // tn
    nparts = ssq.shape[0]
    wspec = pl.BlockSpec((D_MODEL, tn), lambda i, j: (0, j))
    cwspec = pl.BlockSpec((FFN_CONV, tn), lambda i, j: (0, j))
    cbspec = pl.BlockSpec((1, tn), lambda i, j: (0, j))
    if prev_rows is None:
        tm = min(tm, seqlen)
        tps = seqlen // tm
        in_specs = [pl.BlockSpec((tm, D_MODEL), lambda i, j: (i, 0)),
                    pl.BlockSpec((nparts, tm, 1), lambda i, j: (0, i, 0)), wspec, wspec, cwspec, cbspec]
        out_specs = [pl.BlockSpec((tm, tn), lambda i, j: (i, j)),
                     pl.BlockSpec((1, FFN_CONV - 1, tn), lambda i, j: (i, 0, j))]
        out_shape = [jax.ShapeDtypeStruct((m, D_FF_PAD), BF16),
                     jax.ShapeDtypeStruct((m // tm, FFN_CONV - 1, D_FF_PAD), F32)]
        scratch = [pltpu.VMEM((nj, 8, tn), F32)]
        args = (xw, ssq, w_gate, w_up, cw, cb)
        rows = None if cast_w is None else _cast_job(cast_w, (m // tm) * nj)
        if rows is not None:
            ispec, ospec, oshape = _cast_specs(cast_w, rows, lambda i, j: i * nj + j)
            in_specs.append(ispec)
            out_specs.append(ospec)
            out_shape.append(oshape)
            args = args + (cast_w,)
        body = functools.partial(_ffn_gate_body, period=None, tiles_per_seq=tps, cast=rows is not None)
    else:
        assert seqlen & (seqlen - 1) == 0
        tm = m
        in_specs = [pl.BlockSpec((tm, D_MODEL), lambda i, j: (i, 0)),
                    pl.BlockSpec((nparts, tm, 1), lambda i, j: (0, i, 0)), wspec, wspec, cwspec, cbspec,
                    pl.BlockSpec((tm, tn), lambda i, j: (i, j))]
        out_specs = [pl.BlockSpec((tm, tn), lambda i, j: (i, j)),
                     pl.BlockSpec((tm, tn), lambda i, j: (i, j))]
        out_shape = [jax.ShapeDtypeStruct((m, D_FF_PAD), BF16),
                     jax.ShapeDtypeStruct((m, D_FF_PAD), F32)]
        scratch = []
        args = (xw, ssq, w_gate, w_up, cw, cb, prev_rows)
        body = functools.partial(_ffn_gate_body, period=seqlen, tiles_per_seq=None)
    outs = pl.pallas_call(
        body,
        grid=(m // tm, nj),
        in_specs=in_specs,
        out_specs=out_specs,
        out_shape=out_shape,
        scratch_shapes=scratch,
        compiler_params=_params(2),
        name="ffn_gate",
    )(*args)
    return outs if len(outs) == 3 else (*outs, None)


def _down_body(a_ref, w_ref, res_ref, o_ref, *, chunk):
    @pl.when(pl.program_id(2) == 0)
    def _():
        o_ref[...] = res_ref[...]

    a = a_ref[...]
    tk = w_ref.shape[0]
    row = pl.program_id(2) * tk + lax.broadcasted_iota(jnp.int32, (tk, 1), 0)
    for c in range(o_ref.shape[1] // chunk):
        cs = slice(c * chunk, (c + 1) * chunk)
        o_ref[:, cs] += _dot(a, jnp.where(row < D_FF, w_ref[:, cs], 0))


def _down_proj(a, w_down, res, *, tm=1024, tn=2048, tk=1024):
    m, kdim = a.shape
    n = w_down.shape[1]
    tm = min(tm, m)
    return pl.pallas_call(
        functools.partial(_down_body, chunk=512),
        grid=(n // tn, m // tm, kdim // tk),
        in_specs=[pl.BlockSpec((tm, tk), lambda j, i, k: (i, k)),
                  pl.BlockSpec((tk, tn), lambda j, i, k: (k, j)),
                  pl.BlockSpec((tm, tn), lambda j, i, k: (i, j))],
        out_specs=pl.BlockSpec((tm, tn), lambda j, i, k: (i, j)),
        out_shape=jax.ShapeDtypeStruct((m, n), F32),
        compiler_params=_params(3),
        name="down_proj",
    )(a, w_down, res)


def _remember_bf16(bf16_w, wts, name, hosted):
    if name not in bf16_w:
        bf16_w[name] = hosted if hosted is not None else wts[name].astype(BF16)


def _layer_head(x, wts, bf16_w):
    h = _rmsnorm(x, wts["mix_norm_w"])
    if "w_in_t" in bf16_w:
        return _in_proj(h, wts["w_in_t"], bf16_w["w_in_t"])
    if x.shape[0] <= 1024:
        proj, dt_raw, bf16_w["w_in_t"] = _in_proj(h, wts["w_in_t"], keep_bf16=True)
        return proj, dt_raw
    return _in_proj(h, wts["w_in_t"])


def _layer(x, head, nseq, seqlen, wts, bf16_w, bias16, attn_state, ssm_state, conv_state, ffn_state):
    m = x.shape[0]
    todo = lambda name: None if name in bf16_w else wts[name]
    proj, dt_raw = head
    new_conv = proj.reshape(nseq, seqlen, PROJ_W)[:, seqlen - (SSD_CONV - 1):, COL_XBC:COL_XBC + CONV_DIM]

    qw = wts["q_norm_w"].reshape(1, HEAD_DIM)
    kw = wts["k_norm_w"].reshape(1, HEAD_DIM)
    if attn_state is None:
        names = [n for n in ("w_gate",) if n not in bf16_w]
        attn, new_k, new_v, hosted = _attn_prompt(proj, wts["attn_sinks"], qw, kw, bias16, nseq, seqlen,
                                                  cast_ws=tuple(wts[n] for n in names))
        for k, n in enumerate(names):
            _remember_bf16(bf16_w, wts, n, None if hosted is None else hosted[k])
        new_k = new_k.reshape(nseq, WINDOW, KV_HEADS, HEAD_DIM)
        new_v = new_v.reshape(nseq, WINDOW, KV_HEADS, HEAD_DIM)
        conv_prev = jnp.zeros((nseq, SSD_CONV - 1, CONV_DIM), F32)
        h0 = jnp.zeros((nseq, SSD_WIDTH, SSD_STATE), F32)
    else:
        win_k, win_v = attn_state
        attn, new_k, new_v = _attn_sample(proj, win_k, win_v, wts["attn_sinks"], qw, kw, bias16, nseq, seqlen)
        _remember_bf16(bf16_w, wts, "w_gate", None)
        conv_prev = conv_state
        h0 = ssm_state.reshape(nseq, SSD_WIDTH, SSD_STATE)

    names = [n for n in ("w_up", "w_out") if n not in bf16_w]
    y, h_t, hosted = _ssd(proj, dt_raw, conv_prev, h0, wts["ssd_conv_w"], wts["ssd_conv_b"], wts["ssd_dt_bias"],
                          wts["ssd_A_log"], wts["ssd_D"], wts["ssd_norm_w"], nseq, seqlen,
                          cast_ws=tuple(wts[n] for n in names))
    for k, n in enumerate(names):
        _remember_bf16(bf16_w, wts, n, None if hosted is None else hosted[k])
    h_t = h_t.reshape(nseq, SSD_HEADS, SSD_HEAD_DIM, SSD_STATE)

    x1, xw, ssq = _out_proj(attn, y, bf16_w["w_out"], x, wts["ffn_norm_w"])
    if ffn_state is None:
        a, last, hosted = _ffn_gate(xw, ssq, bf16_w["w_gate"], bf16_w["w_up"], wts["ffn_conv_w"],
                                    wts["ffn_conv_b"], nseq=nseq, seqlen=seqlen, cast_w=todo("w_down"))
        new_ffn = last.reshape(nseq, -1, FFN_CONV - 1, D_FF_PAD)[:, -1, :, :D_FF]
    else:
        prev_rows = jnp.pad(ffn_state, ((0, 0), (0, seqlen - (FFN_CONV - 1)), (0, D_FF_PAD - D_FF)))
        a, g, hosted = _ffn_gate(xw, ssq, bf16_w["w_gate"], bf16_w["w_up"], wts["ffn_conv_w"], wts["ffn_conv_b"],
                                 nseq=nseq, seqlen=seqlen, prev_rows=prev_rows.reshape(m, D_FF_PAD))
        new_ffn = g.reshape(nseq, seqlen, D_FF_PAD)[:, seqlen - (FFN_CONV - 1):, :D_FF]
    _remember_bf16(bf16_w, wts, "w_down", hosted)
    x2 = _down_proj(a, bf16_w["w_down"], x1)
    return x2, new_k, new_v, h_t, new_conv, new_ffn


def kernel(x_prompt, x_sample, state_attn_k, state_attn_v, state_ssm, state_ssd_conv, state_ffn_conv, rel_bias, mix_norm_w, w_in, q_norm_w, k_norm_w, attn_sinks, ssd_conv_w, ssd_conv_b, ssd_dt_bias, ssd_A_log, ssd_D, ssd_norm_w, w_out, ffn_norm_w, w_gate, w_up, ffn_conv_w, ffn_conv_b, w_down):
    depth = w_in.shape[0]
    bp, lp, _ = x_prompt.shape
    bs, ls, _ = x_sample.shape
    bias16 = _bias_table(rel_bias)
    yp = x_prompt.reshape(bp * lp, D_MODEL)
    ys = x_sample.reshape(bs * ls, D_MODEL)
    outs_p, outs_s = [], []
    pad_heads = lambda v: jnp.pad(v.reshape(1, SSD_HEADS), ((0, 0), (0, ROWS - SSD_HEADS)))
    pad_ff = lambda v: jnp.pad(v, ((0, 0), (0, D_FF_PAD - D_FF)))
    for l in range(depth):
        wts = dict(
            mix_norm_w=mix_norm_w[l],
            w_in_t=w_in[l].T,

            q_norm_w=q_norm_w[l], k_norm_w=k_norm_w[l], attn_sinks=attn_sinks[l],
            ssd_conv_w=ssd_conv_w[l], ssd_conv_b=ssd_conv_b[l].reshape(1, CONV_DIM),
            ssd_dt_bias=pad_heads(ssd_dt_bias[l]), ssd_A_log=pad_heads(ssd_A_log[l]),
            ssd_D=jnp.repeat(ssd_D[l], SSD_HEAD_DIM).reshape(1, SSD_WIDTH),
            ssd_norm_w=ssd_norm_w[l].reshape(1, SSD_WIDTH),
            w_out=w_out[l], ffn_norm_w=ffn_norm_w[l],
            w_gate=w_gate[l], w_up=w_up[l],
            ffn_conv_w=pad_ff(ffn_conv_w[l]), ffn_conv_b=pad_ff(ffn_conv_b[l].reshape(1, D_FF)),
            w_down=w_down[l],
        )
        bf16_w = {}
        head_s = _layer_head(ys, wts, bf16_w)
        head_p = _layer_head(yp, wts, bf16_w)
        yp, *sp = _layer(yp, head_p, bp, lp, wts, bf16_w, bias16, None, None, None, None)
        ys, *ss = _layer(ys, head_s, bs, ls, wts, bf16_w, bias16, (state_attn_k[l], state_attn_v[l]),
                         state_ssm[l], state_ssd_conv[l], state_ffn_conv[l])
        outs_p.append(sp)
        outs_s.append(ss)
    stack = lambda outs, i: jnp.stack([o[i] for o in outs])
    return (yp.reshape(bp, lp, D_MODEL), ys.reshape(bs, ls, D_MODEL),
            stack(outs_p, 0), stack(outs_p, 1), stack(outs_p, 2), stack(outs_p, 3), stack(outs_p, 4),
            stack(outs_s, 0), stack(outs_s, 1), stack(outs_s, 2), stack(outs_s, 3), stack(outs_s, 4))
```

```python
import functools
import math

import numpy as np
import jax
import jax.numpy as jnp
from jax import lax
from jax.experimental import pallas as pl
from jax.experimental.pallas import tpu as pltpu

F32 = jnp.float32
BF16 = jnp.bfloat16

D_MODEL = 4096
HEAD_DIM = 128
ATTN_HEADS = 16
KV_HEADS = 4
Q_PER_KV = 4
ATTN_WIDTH = 2048
WINDOW = 128
N_BUCKETS = 32
MAX_DISTANCE = 128
SSD_HEAD_DIM = 64
SSD_HEADS = 32
SSD_WIDTH = 2048
SSD_GROUPS = 8
SSD_STATE = 128
SSD_CONV = 4
CONV_DIM = 4096
KV_WIDTH = KV_HEADS * HEAD_DIM
D_FF = 11008
D_FF_PAD = 11264
FFN_CONV = 3
EPS = 1e-6
NEG = -1e30
SCALE = HEAD_DIM ** -0.5

ROWS = 128
GROUP_ROWS = 4 * SSD_HEAD_DIM
COL_Z = ATTN_WIDTH
COL_XBC = COL_Z + SSD_WIDTH
COL_K = COL_XBC + CONV_DIM
COL_V = COL_K + KV_WIDTH
PROJ_W = COL_V + KV_WIDTH
VMEM_LIMIT_MB = 56


def _params(n_axes, vmem_mb=VMEM_LIMIT_MB):
    return pltpu.CompilerParams(dimension_semantics=("arbitrary",) * n_axes,
                                vmem_limit_bytes=vmem_mb << 20)


def _dot(a, b):
    return jnp.dot(a, b, preferred_element_type=F32)


def _dot_nt(a, b):
    return lax.dot_general(a, b, (((1,), (1,)), ((), ())), preferred_element_type=F32)


def _dot_tn(a, b):
    return lax.dot_general(a, b, (((0,), (0,)), ((), ())), preferred_element_type=F32)


def _split3(x):
    a = x.astype(BF16)
    r = x - a.astype(F32)
    b = r.astype(BF16)
    c = (r - b.astype(F32)).astype(BF16)
    return a, b, c


def _dot3_l(x, w):
    a, b, c = _split3(x)
    return (_dot(a, w) + _dot(b, w)) + _dot(c, w)


def _dot3_r(w, x3):
    return (_dot(w, x3[0]) + _dot(w, x3[1])) + _dot(w, x3[2])


def _sigmoid(x):
    return 1.0 / (1.0 + jnp.exp(-x))


def _rms_rows(x, w):
    r = lax.rsqrt(jnp.mean(x * x, axis=-1, keepdims=True) + EPS)
    return (x * r) * w


def _rmsnorm_body(x_ref, w_ref, o_ref):
    o_ref[...] = _rms_rows(x_ref[...], w_ref[...]).astype(o_ref.dtype)


def _rmsnorm(x, w, rows=512):
    m, d = x.shape
    rows = min(rows, m)
    return pl.pallas_call(
        _rmsnorm_body,
        grid=(m // rows,),
        in_specs=[pl.BlockSpec((rows, d), lambda i: (i, 0)),
                  pl.BlockSpec((1, d), lambda i: (0, 0))],
        out_specs=pl.BlockSpec((rows, d), lambda i: (i, 0)),
        out_shape=jax.ShapeDtypeStruct((m, d), BF16),
        compiler_params=_params(1, 32),
        name="rmsnorm",
    )(x, w.reshape(1, d))


def _in_proj_body(a_ref, bt_ref, dtw_ref, o_ref, dt_ref, *wcopy_ref):
    a = a_ref[...]
    bt = bt_ref[...].astype(BF16)
    o_ref[...] = _dot_nt(a, bt)
    if wcopy_ref:
        wcopy_ref[0][...] = bt

    @pl.when(pl.program_id(1) == 0)
    def _():
        row = lax.broadcasted_iota(jnp.int32, (dtw_ref.shape[0], 1), 0)
        dtw = jnp.where(row < SSD_HEADS, dtw_ref[...], 0.0)
        dt_ref[...] = _dot_nt(a, dtw.astype(BF16))


def _in_proj(h, w_in_t, w_main=None, *, keep_bf16=False, tm=1024):
    m = h.shape[0]
    tm = min(tm, m)
    tn = 512 if w_main is None else 1024
    nq, nkv2 = ATTN_WIDTH // tn, 2 * KV_WIDTH // tn
    nzx = (SSD_WIDTH + CONV_DIM) // tn

    def src_block(j):
        return jnp.where(j < nq, j, jnp.where(j < nq + nzx, j + nkv2, j - nzx))

    wspec = pl.BlockSpec((tn, D_MODEL), lambda i, j: (src_block(j), 0))
    out_specs = [pl.BlockSpec((tm, tn), lambda i, j: (i, j)),
                 pl.BlockSpec((tm, ROWS), lambda i, j: (i, 0))]
    out_shape = [jax.ShapeDtypeStruct((m, PROJ_W), F32),
                 jax.ShapeDtypeStruct((m, ROWS), F32)]
    if keep_bf16:
        assert w_main is None and m == tm
        out_specs.append(wspec)
        out_shape.append(jax.ShapeDtypeStruct((PROJ_W, D_MODEL), BF16))
    return pl.pallas_call(
        _in_proj_body,
        grid=(m // tm, PROJ_W // tn),
        in_specs=[pl.BlockSpec((tm, D_MODEL), lambda i, j: (i, 0)),
                  wspec,
                  pl.BlockSpec((ROWS, D_MODEL), lambda i, j: (PROJ_W // ROWS, 0))],
        out_specs=out_specs,
        out_shape=out_shape,
        compiler_params=_params(2),
        name="in_proj",
    )(h, w_in_t if w_main is None else w_main, w_in_t)


def _out_proj_body(a1_ref, a2_ref, b1_ref, b2_ref, res_ref, nw_ref, o_ref, xw_ref, ssq_ref):
    acc = (_dot(a1_ref[...].astype(BF16), b1_ref[...].astype(BF16))
           + _dot(a2_ref[...].astype(BF16), b2_ref[...].astype(BF16)))
    x1 = res_ref[...] + acc
    o_ref[...] = x1
    xw_ref[...] = (x1 * nw_ref[...]).astype(xw_ref.dtype)
    ssq_ref[0] = jnp.sum(x1 * x1, axis=-1, keepdims=True)


def _out_proj(attn, y, w_out, res, norm_w, *, tm=1024, tn=512):
    m = attn.shape[0]
    n = w_out.shape[1]
    tm = min(tm, m)
    return pl.pallas_call(
        _out_proj_body,
        grid=(m // tm, n // tn),
        in_specs=[pl.BlockSpec((tm, ATTN_WIDTH), lambda i, j: (i, 0)),
                  pl.BlockSpec((tm, SSD_WIDTH), lambda i, j: (i, 0)),
                  pl.BlockSpec((ATTN_WIDTH, tn), lambda i, j: (0, j)),
                  pl.BlockSpec((SSD_WIDTH, tn), lambda i, j: (1, j)),
                  pl.BlockSpec((tm, tn), lambda i, j: (i, j)),
                  pl.BlockSpec((1, tn), lambda i, j: (0, j))],
        out_specs=[pl.BlockSpec((tm, tn), lambda i, j: (i, j)),
                   pl.BlockSpec((tm, tn), lambda i, j: (i, j)),
                   pl.BlockSpec((1, tm, 1), lambda i, j: (j, i, 0))],
        out_shape=[jax.ShapeDtypeStruct((m, n), F32),
                   jax.ShapeDtypeStruct((m, n), BF16),
                   jax.ShapeDtypeStruct((n // tn, m, 1), F32)],
        compiler_params=_params(2),
        name="out_proj",
    )(attn, y, w_out, w_out, res, norm_w.reshape(1, n))


def _bucket_table():
    i = np.arange(ROWS)[:, None]
    j = np.arange(2 * ROWS)[None, :]
    dist = i + ROWS - j
    valid = (dist >= 0) & (dist < WINDOW)
    n = np.maximum(dist, 0)
    max_exact = N_BUCKETS // 2

    def large(dtype):
        nf = np.maximum(n, 1).astype(dtype)
        v = np.log(nf / dtype(max_exact)) / dtype(math.log(MAX_DISTANCE / max_exact)) * dtype(N_BUCKETS - max_exact)
        return np.minimum(max_exact + v.astype(np.int32), N_BUCKETS - 1)

    assert (large(np.float32) == large(np.float64)).all()
    bucket = np.where(n < max_exact, n, large(np.float32))
    return np.where(valid, bucket, -1).astype(np.int32)


def _bias_body(rb_ref, bkt_ref, o_ref):
    h = pl.program_id(0)
    bkt = bkt_ref[...]
    acc = jnp.full(bkt.shape, NEG, F32)
    for b in range(N_BUCKETS):
        acc = jnp.where(bkt == b, rb_ref[b, h], acc)
    o_ref[0] = acc


def _bias_table(rel_bias):
    return pl.pallas_call(
        _bias_body,
        grid=(ATTN_HEADS,),
        in_specs=[pl.BlockSpec(memory_space=pltpu.SMEM),
                  pl.BlockSpec((ROWS, 2 * ROWS), lambda h: (0, 0))],
        out_specs=pl.BlockSpec((1, ROWS, 2 * ROWS), lambda h: (h, 0, 0)),
        out_shape=jax.ShapeDtypeStruct((ATTN_HEADS, ROWS, 2 * ROWS), F32),
        compiler_params=_params(1, 16),
        name="bias_table",
    )(rel_bias, jnp.asarray(_bucket_table()))


def _cast_job(w, nsteps):
    rows = -(-w.shape[0] // nsteps)
    rows = -(-rows // 16) * 16
    return rows if rows * w.shape[1] * 4 <= (8 << 20) else None


def _cast_specs(w, rows, step_of):
    nblk = -(-w.shape[0] // rows)
    spec = pl.BlockSpec((rows, w.shape[1]), lambda *g: (jnp.minimum(step_of(*g), nblk - 1), 0))
    return spec, spec, jax.ShapeDtypeStruct(w.shape, BF16)


def _softmax_sink(s_scr, bias, sink, p_scr):
    sc = s_scr[...] * SCALE + bias
    m = jnp.maximum(jnp.max(sc, axis=-1, keepdims=True), sink)
    p = jnp.exp(sc - m)
    den = jnp.sum(p, axis=-1, keepdims=True) + jnp.exp(sink - m)
    p_scr[...] = (p * (1.0 / den)).astype(BF16)


def _attn_prompt_body(q_ref, kc_ref, kp_ref, vc_ref, vp_ref, qw_ref, kw_ref, bias_ref, sink_ref, *rest, ncast):
    o_ref, pk_ref, pv_ref = rest[ncast:ncast + 3]
    for src_ref, dst_ref in zip(rest[:ncast], rest[ncast + 3:2 * ncast + 3]):
        dst_ref[...] = src_ref[...].astype(BF16)
    qn_scr, kn_scr, s_scr, p_scr = rest[2 * ncast + 3:]
    qw = qw_ref[...]
    kw = kw_ref[...]
    grows = Q_PER_KV * ROWS
    for h in range(ATTN_HEADS):
        hs = slice(h * HEAD_DIM, (h + 1) * HEAD_DIM)
        qn_scr[:, hs] = _rms_rows(q_ref[:, hs], qw)
    for kv in range(KV_HEADS):
        sl = slice(kv * HEAD_DIM, (kv + 1) * HEAD_DIM)
        kn_scr[0:ROWS, sl] = _rms_rows(kp_ref[:, sl], kw)
        kn_scr[ROWS:2 * ROWS, sl] = _rms_rows(kc_ref[:, sl], kw)
    pk_ref[0] = kn_scr[ROWS:2 * ROWS, :]
    pv_ref[0] = vc_ref[...]

    for kv in range(KV_HEADS):
        sl = slice(kv * HEAD_DIM, (kv + 1) * HEAD_DIM)
        qs = jnp.concatenate([qn_scr[:, (kv * Q_PER_KV + g) * HEAD_DIM:(kv * Q_PER_KV + g + 1) * HEAD_DIM]
                              for g in range(Q_PER_KV)], axis=0).astype(BF16)
        s_scr[kv * grows:(kv + 1) * grows, :] = _dot_nt(qs, kn_scr[:, sl].astype(BF16))

    col = lax.broadcasted_iota(jnp.int32, (1, 2 * ROWS), 1)
    no_prev = jnp.where(jnp.logical_and(pl.program_id(1) == 0, col < ROWS), NEG, 0.0)
    _softmax_sink(s_scr, bias_ref[...] + no_prev, sink_ref[...], p_scr)

    for kv in range(KV_HEADS):
        sl = slice(kv * HEAD_DIM, (kv + 1) * HEAD_DIM)
        vcat = jnp.concatenate([vp_ref[:, sl], vc_ref[:, sl]], axis=0).astype(BF16)
        o = _dot(p_scr[kv * grows:(kv + 1) * grows, :], vcat)
        for g in range(Q_PER_KV):
            h = kv * Q_PER_KV + g
            o_ref[:, h * HEAD_DIM:(h + 1) * HEAD_DIM] = o[g * ROWS:(g + 1) * ROWS].astype(o_ref.dtype)


def _attn_prompt(proj, sinks, qw, kw, bias16, nseq, seqlen, cast_ws=()):
    nb = seqlen // ROWS
    kblk, vblk = COL_K // KV_WIDTH, COL_V // KV_WIDTH
    srows = ATTN_HEADS * ROWS
    cur = lambda b, i: b * nb + i
    prev = lambda b, i: b * nb + jnp.maximum(i - 1, 0)
    const = lambda b, i: (0, 0)
    in_specs = [pl.BlockSpec((ROWS, ATTN_WIDTH), lambda b, i: (cur(b, i), 0)),
                pl.BlockSpec((ROWS, KV_WIDTH), lambda b, i: (cur(b, i), kblk)),
                pl.BlockSpec((ROWS, KV_WIDTH), lambda b, i: (prev(b, i), kblk)),
                pl.BlockSpec((ROWS, KV_WIDTH), lambda b, i: (cur(b, i), vblk)),
                pl.BlockSpec((ROWS, KV_WIDTH), lambda b, i: (prev(b, i), vblk)),
                pl.BlockSpec((1, HEAD_DIM), const),
                pl.BlockSpec((1, HEAD_DIM), const),
                pl.BlockSpec((srows, 2 * ROWS), const),
                pl.BlockSpec((srows, 1), const)]
    out_specs = [pl.BlockSpec((ROWS, ATTN_WIDTH), lambda b, i: (cur(b, i), 0)),
                 pl.BlockSpec((1, ROWS, KV_WIDTH), lambda b, i: (b, 0, 0)),
                 pl.BlockSpec((1, ROWS, KV_WIDTH), lambda b, i: (b, 0, 0))]
    out_shape = [jax.ShapeDtypeStruct((nseq * seqlen, ATTN_WIDTH), BF16),
                 jax.ShapeDtypeStruct((nseq, ROWS, KV_WIDTH), F32),
                 jax.ShapeDtypeStruct((nseq, ROWS, KV_WIDTH), F32)]
    args = [proj, proj, proj, proj, proj, qw, kw, bias16.reshape(srows, 2 * ROWS),
            jnp.repeat(sinks, ROWS).reshape(srows, 1)]
    rows = [_cast_job(w, nseq * nb) for w in cast_ws]
    hosted = bool(cast_ws) and all(r is not None for r in rows)
    if hosted:
        for w, r in zip(cast_ws, rows):
            ispec, ospec, oshape = _cast_specs(w, r, cur)
            in_specs.append(ispec)
            out_specs.append(ospec)
            out_shape.append(oshape)
            args.append(w)
    ncast = len(cast_ws) if hosted else 0
    outs = pl.pallas_call(
        functools.partial(_attn_prompt_body, ncast=ncast),
        grid=(nseq, nb),
        in_specs=in_specs,
        out_specs=out_specs,
        out_shape=out_shape,
        scratch_shapes=[pltpu.VMEM((ROWS, ATTN_WIDTH), F32),
                        pltpu.VMEM((2 * ROWS, KV_WIDTH), F32),
                        pltpu.VMEM((srows, 2 * ROWS), F32),
                        pltpu.VMEM((srows, 2 * ROWS), BF16)],
        compiler_params=_params(2, 48),
        name="attn_prompt",
    )(*args)
    return (*outs[:3], tuple(outs[3:]) if hosted else None)


def _attn_sample_body(q_ref, kn_ref, vn_ref, wk_ref, wv_ref, qw_ref, kw_ref, bias_ref, sink_ref,
                      o_ref, sk_ref, sv_ref, qn_scr, kn_scr, s_scr, p_scr, *, nsq, ntok):
    qw = qw_ref[...]
    kw = kw_ref[...]
    grows = Q_PER_KV * ntok
    keep = ROWS - ntok
    for h in range(ATTN_HEADS):
        hs = slice(h * HEAD_DIM, (h + 1) * HEAD_DIM)
        qn_scr[:, hs] = _rms_rows(q_ref[:, hs], qw)
    for kv in range(KV_HEADS):
        sl = slice(kv * HEAD_DIM, (kv + 1) * HEAD_DIM)
        kn_scr[:, sl] = _rms_rows(kn_ref[:, sl], kw)
    for s in range(nsq):
        rows = slice(s * ntok, (s + 1) * ntok)
        sk_ref[s, 0:keep] = wk_ref[s, ntok:ROWS]
        sv_ref[s, 0:keep] = wv_ref[s, ntok:ROWS]
        for kv in range(KV_HEADS):
            sl = slice(kv * HEAD_DIM, (kv + 1) * HEAD_DIM)
            sk_ref[s, keep:ROWS, kv, :] = kn_scr[rows, sl]
            sv_ref[s, keep:ROWS, kv, :] = vn_ref[rows, sl]

    pad = jnp.zeros((keep, HEAD_DIM), F32)
    for s in range(nsq):
        rows = slice(s * ntok, (s + 1) * ntok)
        for kv in range(KV_HEADS):
            sl = slice(kv * HEAD_DIM, (kv + 1) * HEAD_DIM)
            grp = s * KV_HEADS + kv
            qs = jnp.concatenate([qn_scr[rows, (kv * Q_PER_KV + g) * HEAD_DIM:(kv * Q_PER_KV + g + 1) * HEAD_DIM]
                                  for g in range(Q_PER_KV)], axis=0).astype(BF16)
            kcat = jnp.concatenate([wk_ref[s, :, kv, :], kn_scr[rows, sl], pad], axis=0).astype(BF16)
            s_scr[grp * grows:(grp + 1) * grows, :] = _dot_nt(qs, kcat)

    _softmax_sink(s_scr, bias_ref[...], sink_ref[...], p_scr)

    for s in range(nsq):
        rows = slice(s * ntok, (s + 1) * ntok)
        for kv in range(KV_HEADS):
            sl = slice(kv * HEAD_DIM, (kv + 1) * HEAD_DIM)
            grp = s * KV_HEADS + kv
            vcat = jnp.concatenate([wv_ref[s, :, kv, :], vn_ref[rows, sl], pad], axis=0).astype(BF16)
            o = _dot(p_scr[grp * grows:(grp + 1) * grows, :], vcat)
            for g in range(Q_PER_KV):
                h = kv * Q_PER_KV + g
                o_ref[rows, h * HEAD_DIM:(h + 1) * HEAD_DIM] = o[g * ntok:(g + 1) * ntok]


def _attn_sample(proj, win_k, win_v, sinks, qw, kw, bias16, nseq, ntok, nsq=8):
    kblk, vblk = COL_K // KV_WIDTH, COL_V // KV_WIDTH
    rows = nsq * ntok
    srows = nsq * ATTN_HEADS * ntok
    bias = jnp.tile(bias16[:, :ntok, :].reshape(ATTN_HEADS * ntok, 2 * ROWS), (nsq, 1))
    sink = jnp.tile(jnp.repeat(sinks, ntok), nsq).reshape(srows, 1)
    return pl.pallas_call(
        functools.partial(_attn_sample_body, nsq=nsq, ntok=ntok),
        grid=(nseq // nsq,),
        in_specs=[pl.BlockSpec((rows, ATTN_WIDTH), lambda i: (i, 0)),
                  pl.BlockSpec((rows, KV_WIDTH), lambda i: (i, kblk)),
                  pl.BlockSpec((rows, KV_WIDTH), lambda i: (i, vblk)),
                  pl.BlockSpec((nsq, ROWS, KV_HEADS, HEAD_DIM), lambda i: (i, 0, 0, 0)),
                  pl.BlockSpec((nsq, ROWS, KV_HEADS, HEAD_DIM), lambda i: (i, 0, 0, 0)),
                  pl.BlockSpec((1, HEAD_DIM), lambda i: (0, 0)),
                  pl.BlockSpec((1, HEAD_DIM), lambda i: (0, 0)),
                  pl.BlockSpec((srows, 2 * ROWS), lambda i: (0, 0)),
                  pl.BlockSpec((srows, 1), lambda i: (0, 0))],
        out_specs=[pl.BlockSpec((rows, ATTN_WIDTH), lambda i: (i, 0)),
                   pl.BlockSpec((nsq, ROWS, KV_HEADS, HEAD_DIM), lambda i: (i, 0, 0, 0)),
                   pl.BlockSpec((nsq, ROWS, KV_HEADS, HEAD_DIM), lambda i: (i, 0, 0, 0))],
        out_shape=[jax.ShapeDtypeStruct((nseq * ntok, ATTN_WIDTH), F32),
                   jax.ShapeDtypeStruct((nseq, ROWS, KV_HEADS, HEAD_DIM), F32),
                   jax.ShapeDtypeStruct((nseq, ROWS, KV_HEADS, HEAD_DIM), F32)],
        scratch_shapes=[pltpu.VMEM((rows, ATTN_WIDTH), F32),
                        pltpu.VMEM((rows, KV_WIDTH), F32),
                        pltpu.VMEM((srows, 2 * ROWS), F32),
                        pltpu.VMEM((srows, 2 * ROWS), BF16)],
        compiler_params=_params(1, 32),
        name="attn_sample",
    )(proj, proj, proj, win_k, win_v, qw, kw, bias, sink)


def _ssd_constants(qs):
    r = np.arange(ROWS)
    same = (r[:, None] // qs) == (r[None, :] // qs)
    mask = same & (r[None, :] <= r[:, None])
    sel = r[None, :] == ((r[:, None] // qs) * qs + qs - 1)
    expand = np.zeros((ROWS, SSD_WIDTH), np.float32)
    for h in range(SSD_HEADS):
        expand[h, h * SSD_HEAD_DIM:(h + 1) * SSD_HEAD_DIM] = 1.0
    to_bf = lambda a: jnp.asarray(a.astype(np.float32), BF16)
    return to_bf(mask), to_bf(sel), to_bf(expand)


def _ssd_body(xbc_ref, z_ref, dt_ref, cprev_ref, h0_ref, cw_ref, cb_ref, dtb_ref, alog_ref, dexp_ref,
              nw_ref, mask_ref, sel_ref, e_ref, *rest, nsq, qs, spp, ncast):
    y_ref, ht_ref = rest[ncast:ncast + 2]
    for src_ref, dst_ref in zip(rest[:ncast], rest[ncast + 2:2 * ncast + 2]):
        dst_ref[...] = src_ref[...].astype(BF16)
    h_scr, cbuf, xc_scr, xdt_scr, xdd_scr, eacs_scr, yacc_scr, acs_scr, acst_scr, aclt_scr = rest[2 * ncast + 2:]
    c = pl.program_id(1)
    s = pl.program_id(2)

    @pl.when(c == 0)
    def _():
        h_scr[...] = h0_ref[...]

    @pl.when(jnp.logical_and(c == 0, s == 0))
    def _():
        cbuf[0:8, :] = jnp.zeros((8, CONV_DIM), F32)
        if nsq == 1:
            cbuf[5:8, :] = cprev_ref[0]

    @pl.when(s == 0)
    def _chunk_phase():
        ncc = 8 if qs == ROWS else 1
        cwid = CONV_DIM // ncc
        nrow = 8 + qs
        for sq in range(nsq):
            for cc in range(ncc):
                cs = slice(cc * cwid, (cc + 1) * cwid)
                if nsq > 1:
                    cbuf[5:8, cs] = cprev_ref[sq, :, cs]
                cbuf[8:nrow, cs] = xbc_ref[sq * qs:(sq + 1) * qs, cs]
                xa = cbuf[0:nrow, cs]
                acc = cb_ref[:, cs] + cw_ref[SSD_CONV - 1:SSD_CONV, cs] * xa[8:nrow]
                for t in range(SSD_CONV - 1):
                    acc = acc + cw_ref[t:t + 1, cs] * pltpu.roll(xa, nrow - (5 + t), 0)[0:qs]
                xc_scr[sq * qs:(sq + 1) * qs, cs] = acc * _sigmoid(acc)
                if nsq == 1:
                    cbuf[5:8, cs] = xbc_ref[qs - 3:qs, cs]

        x = dt_ref[...] + dtb_ref[...]
        dt = jnp.maximum(x, 0.0) + jnp.log1p(jnp.exp(-jnp.abs(x)))
        da = dt * (-jnp.exp(alog_ref[...]))
        mask_bf = mask_ref[...]
        acs = _dot3_r(mask_bf, _split3(da))
        acl = _dot3_r(sel_ref[...], _split3(acs))
        acs_scr[...] = acs
        acst_scr[...] = acs.T
        aclt_scr[...] = acl.T
        e = e_ref[...]
        xs = xc_scr[:, 0:SSD_WIDTH]
        xdt_scr[...] = xs * _dot3_l(dt, e)
        xdd_scr[...] = xs * _dot3_l(dt * jnp.exp(acl - acs), e)
        eacs_scr[...] = _dot3_l(jnp.exp(acs), e)

        maskb = mask_bf > 0.5
        lane = lax.broadcasted_iota(jnp.int32, (ROWS, 2 * SSD_HEAD_DIM), 1)
        for g in range(SSD_GROUPS):
            bg = xc_scr[:, SSD_WIDTH + g * SSD_STATE:SSD_WIDTH + (g + 1) * SSD_STATE].astype(BF16)
            cg = xc_scr[:, SSD_WIDTH + (SSD_GROUPS + g) * SSD_STATE:
                        SSD_WIDTH + (SSD_GROUPS + g + 1) * SSD_STATE].astype(BF16)
            cb = _dot_nt(cg, bg)
            for pp in range(2):
                ms = []
                for j in range(2):
                    h = 4 * g + 2 * pp + j
                    seg = acs_scr[:, h:h + 1] - acst_scr[h:h + 1, :]
                    ms.append((cb * jnp.exp(jnp.where(maskb, seg, NEG))).astype(BF16))
                c0 = (4 * g + 2 * pp) * SSD_HEAD_DIM
                cols = slice(c0, c0 + 2 * SSD_HEAD_DIM)
                xp = xdt_scr[:, cols]
                rhs = jnp.concatenate([jnp.where(lane < SSD_HEAD_DIM, xp, 0.0),
                                       jnp.where(lane >= SSD_HEAD_DIM, xp, 0.0)], axis=0).astype(BF16)
                yd = _dot(jnp.concatenate(ms, axis=1), rhs)
                yacc_scr[:, cols] = yd + dexp_ref[:, cols] * xc_scr[:, cols]

    tlane = lax.broadcasted_iota(jnp.int32, (ROWS, ROWS), 1)
    ridx = lax.broadcasted_iota(jnp.int32, (ROWS, 1), 0)
    for u in range(spp):
        if nsq == 1:
            r0 = 0
            rows = slice(0, qs)
        else:
            r0 = pl.multiple_of((s * spp + u) * qs, qs)
            rows = pl.ds(r0, qs)
            rowmask = jnp.logical_and(ridx >= r0, ridx < r0 + qs)
        xcol = jnp.sum(jnp.where(tlane == r0, aclt_scr[...], 0.0), axis=1, keepdims=True)
        dec = jnp.exp(jnp.broadcast_to(xcol, (ROWS, ROWS)))
        c_rows = xc_scr[rows, SSD_WIDTH + SSD_GROUPS * SSD_STATE:CONV_DIM].astype(BF16)
        yo, st = [], []
        for g in range(SSD_GROUPS):
            grows = slice(g * GROUP_ROWS, (g + 1) * GROUP_ROWS)
            yo.append(_dot_nt(c_rows[:, g * SSD_STATE:(g + 1) * SSD_STATE], h_scr[u, grows, :].astype(BF16)))
            bg = xc_scr[:, SSD_WIDTH + g * SSD_STATE:SSD_WIDTH + (g + 1) * SSD_STATE]
            if nsq > 1:
                bg = jnp.where(rowmask, bg, 0.0)
            st.append(_dot_tn(xdd_scr[:, grows].astype(BF16), bg.astype(BF16)))
        yacc_scr[rows, :] = yacc_scr[rows, :] + jnp.concatenate(yo, axis=1) * eacs_scr[rows, :]
        for h in range(SSD_HEADS):
            hrows = slice(h * SSD_HEAD_DIM, (h + 1) * SSD_HEAD_DIM)
            g, off = divmod(h * SSD_HEAD_DIM, GROUP_ROWS)
            h_new = dec[h:h + 1, :] * h_scr[u, hrows, :] + st[g][off:off + SSD_HEAD_DIM]
            h_scr[u, hrows, :] = h_new
            ht_ref[u, hrows, :] = h_new

    @pl.when(s == nsq // spp - 1)
    def _finish():
        z = z_ref[...]
        yg = yacc_scr[...] * (z * _sigmoid(z))
        y_ref[...] = _rms_rows(yg, nw_ref[...]).astype(y_ref.dtype)


def _ssd(proj, dt_raw, conv_prev, h0, cw, cb, dtb, alog, dexp, nw, nseq, seqlen, cast_ws=()):
    if seqlen % ROWS == 0:
        qs, nsq, nchunk, ngrp = ROWS, 1, seqlen // ROWS, nseq
    else:
        qs, nsq, nchunk = seqlen, ROWS // seqlen, 1
        ngrp = nseq // nsq
    spp = math.gcd(nsq, 4)
    nstep = nsq // spp
    mask, sel, expand = _ssd_constants(qs)
    rowblk = lambda b, c, s: b * nchunk + c
    seq = lambda b, c, s: b * nstep + s
    const2 = lambda b, c, s: (0, 0)
    full = lambda shape: pl.BlockSpec(shape, const2)
    in_specs = [pl.BlockSpec((ROWS, CONV_DIM), lambda b, c, s: (rowblk(b, c, s), COL_XBC // CONV_DIM)),
                pl.BlockSpec((ROWS, SSD_WIDTH), lambda b, c, s: (rowblk(b, c, s), COL_Z // SSD_WIDTH)),
                pl.BlockSpec((ROWS, ROWS), lambda b, c, s: (rowblk(b, c, s), 0)),
                pl.BlockSpec((nsq, SSD_CONV - 1, CONV_DIM), lambda b, c, s: (b, 0, 0)),
                pl.BlockSpec((spp, SSD_WIDTH, SSD_STATE), lambda b, c, s: (seq(b, c, s), 0, 0)),
                full((SSD_CONV, CONV_DIM)), full((1, CONV_DIM)), full((1, ROWS)), full((1, ROWS)),
                full((1, SSD_WIDTH)), full((1, SSD_WIDTH)),
                full((ROWS, ROWS)), full((ROWS, ROWS)), full((ROWS, SSD_WIDTH))]
    out_specs = [pl.BlockSpec((ROWS, SSD_WIDTH), lambda b, c, s: (rowblk(b, c, s), 0)),
                 pl.BlockSpec((spp, SSD_WIDTH, SSD_STATE), lambda b, c, s: (seq(b, c, s), 0, 0))]
    out_shape = [jax.ShapeDtypeStruct((nseq * seqlen, SSD_WIDTH), BF16),
                 jax.ShapeDtypeStruct((nseq, SSD_WIDTH, SSD_STATE), F32)]
    args = [proj, proj, dt_raw, conv_prev, h0, cw, cb, dtb, alog, dexp, nw, mask, sel, expand]
    rows = [_cast_job(w, ngrp * nchunk * nstep) for w in cast_ws]
    hosted = bool(cast_ws) and all(r is not None for r in rows)
    if hosted:
        for w, r in zip(cast_ws, rows):
            ispec, ospec, oshape = _cast_specs(w, r, lambda b, c, s: (b * nchunk + c) * nstep + s)
            in_specs.append(ispec)
            out_specs.append(ospec)
            out_shape.append(oshape)
            args.append(w)
    outs = pl.pallas_call(
        functools.partial(_ssd_body, nsq=nsq, qs=qs, spp=spp, ncast=len(cast_ws) if hosted else 0),
        grid=(ngrp, nchunk, nstep),
        in_specs=in_specs,
        out_specs=out_specs,
        out_shape=out_shape,
        scratch_shapes=[pltpu.VMEM((spp, SSD_WIDTH, SSD_STATE), F32),
                        pltpu.VMEM((8 + ROWS, CONV_DIM), F32),
                        pltpu.VMEM((ROWS, CONV_DIM), F32),
                        pltpu.VMEM((ROWS, SSD_WIDTH), F32),
                        pltpu.VMEM((ROWS, SSD_WIDTH), F32),
                        pltpu.VMEM((ROWS, SSD_WIDTH), F32),
                        pltpu.VMEM((ROWS, SSD_WIDTH), F32),
                        pltpu.VMEM((ROWS, ROWS), F32),
                        pltpu.VMEM((ROWS, ROWS), F32),
                        pltpu.VMEM((ROWS, ROWS), F32)],
        compiler_params=_params(3, 48),
        name="ssd",
    )(*args)
    return (*outs[:2], tuple(outs[2:]) if hosted else None)


def _ffn_gate_body(xw_ref, ssq_ref, wg_ref, wu_ref, cw_ref, cb_ref, *rest, period, tiles_per_seq, cast=False):
    if cast:
        wsrc_ref, a_ref, last_ref, wdst_ref, carry_scr = rest
        wdst_ref[...] = wsrc_ref[...].astype(BF16)
    elif period is None:
        a_ref, last_ref, carry_scr = rest
    else:
        prev_ref, a_ref, g_ref = rest
    r = lax.rsqrt(jnp.sum(ssq_ref[...], axis=0) * (1.0 / D_MODEL) + EPS)
    xw = xw_ref[...]
    tn = wg_ref.shape[1]
    col = pl.program_id(1) * tn + lax.broadcasted_iota(jnp.int32, (1, tn), 1)
    g = _dot(xw, jnp.where(col < D_FF, wg_ref[...], 0)) * r
    u = _dot(xw, jnp.where(col < D_FF, wu_ref[...], 0)) * r
    tm = g.shape[0]
    row = lax.broadcasted_iota(jnp.int32, g.shape, 0)
    if period is None:
        i, j = pl.program_id(0), pl.program_id(1)
        c8 = jnp.where(i % tiles_per_seq == 0, 0.0, carry_scr[j])
        carry_scr[j] = g[tm - 8:tm]
        last_ref[0] = g[tm - 2:tm]
        g_m1 = jnp.where(row == 0, c8[7:8], pltpu.roll(g, 1, 0))
        g_m2 = jnp.where(row == 0, c8[6:7], jnp.where(row == 1, c8[7:8], pltpu.roll(g, 2, 0)))
    else:
        g_ref[...] = g
        prev = prev_ref[...]
        t = row & (period - 1)
        g_m1 = jnp.where(t == 0, pltpu.roll(prev, tm - 1, 0), pltpu.roll(g, 1, 0))
        g_m2 = jnp.where(t < 2, prev, pltpu.roll(g, 2, 0))
    gc = cb_ref[...] + cw_ref[0:1] * g_m2
    gc = gc + cw_ref[1:2] * g_m1
    gc = gc + cw_ref[2:3] * g
    a_ref[...] = ((gc * _sigmoid(gc)) * u).astype(a_ref.dtype)


def _ffn_gate(xw, ssq, w_gate, w_up, cw, cb, *, nseq, seqlen, prev_rows=None, cast_w=None, tm=1024, tn=512):
    m = xw.shape[0]
    nj = D_FF_PAD // tn
    nparts = ssq.shape[0]
    wspec = pl.BlockSpec((D_MODEL, tn), lambda i, j: (0, j))
    cwspec = pl.BlockSpec((FFN_CONV, tn), lambda i, j: (0, j))
    cbspec = pl.BlockSpec((1, tn), lambda i, j: (0, j))
    if prev_rows is None:
        tm = min(tm, seqlen)
        tps = seqlen // tm
        in_specs = [pl.BlockSpec((tm, D_MODEL), lambda i, j: (i, 0)),
                    pl.BlockSpec((nparts, tm, 1), lambda i, j: (0, i, 0)), wspec, wspec, cwspec, cbspec]
        out_specs = [pl.BlockSpec((tm, tn), lambda i, j: (i, j)),
                     pl.BlockSpec((1, FFN_CONV - 1, tn), lambda i, j: (i, 0, j))]
        out_shape = [jax.ShapeDtypeStruct((m, D_FF_PAD), BF16),
                     jax.ShapeDtypeStruct((m // tm, FFN_CONV - 1, D_FF_PAD), F32)]
        scratch = [pltpu.VMEM((nj, 8, tn), F32)]
        args = (xw, ssq, w_gate, w_up, cw, cb)
        rows = None if cast_w is None else _cast_job(cast_w, (m // tm) * nj)
        if rows is not None:
            ispec, ospec, oshape = _cast_specs(cast_w, rows, lambda i, j: i * nj + j)
            in_specs.append(ispec)
            out_specs.append(ospec)
            out_shape.append(oshape)
            args = args + (cast_w,)
        body = functools.partial(_ffn_gate_body, period=None, tiles_per_seq=tps, cast=rows is not None)
    else:
        assert seqlen & (seqlen - 1) == 0
        tm = m
        in_specs = [pl.BlockSpec((tm, D_MODEL), lambda i, j: (i, 0)),
                    pl.BlockSpec((nparts, tm, 1), lambda i, j: (0, i, 0)), wspec, wspec, cwspec, cbspec,
                    pl.BlockSpec((tm, tn), lambda i, j: (i, j))]
        out_specs = [pl.BlockSpec((tm, tn), lambda i, j: (i, j)),
                     pl.BlockSpec((tm, tn), lambda i, j: (i, j))]
        out_shape = [jax.ShapeDtypeStruct((m, D_FF_PAD), BF16),
                     jax.ShapeDtypeStruct((m, D_FF_PAD), F32)]
        scratch = []
        args = (xw, ssq, w_gate, w_up, cw, cb, prev_rows)
        body = functools.partial(_ffn_gate_body, period=seqlen, tiles_per_seq=None)
    outs = pl.pallas_call(
        body,
        grid=(m // tm, nj),
        in_specs=in_specs,
        out_specs=out_specs,
        out_shape=out_shape,
        scratch_shapes=scratch,
        compiler_params=_params(2),
        name="ffn_gate",
    )(*args)
    return outs if len(outs) == 3 else (*outs, None)


def _down_body(a_ref, w_ref, res_ref, o_ref, *, chunk):
    @pl.when(pl.program_id(2) == 0)
    def _():
        o_ref[...] = res_ref[...]

    a = a_ref[...]
    tk = w_ref.shape[0]
    row = pl.program_id(2) * tk + lax.broadcasted_iota(jnp.int32, (tk, 1), 0)
    for c in range(o_ref.shape[1] // chunk):
        cs = slice(c * chunk, (c + 1) * chunk)
        o_ref[:, cs] += _dot(a, jnp.where(row < D_FF, w_ref[:, cs], 0))


def _down_proj(a, w_down, res, *, tm=1024, tn=2048, tk=1024):
    m, kdim = a.shape
    n = w_down.shape[1]
    tm = min(tm, m)
    return pl.pallas_call(
        functools.partial(_down_body, chunk=512),
        grid=(n // tn, m // tm, kdim // tk),
        in_specs=[pl.BlockSpec((tm, tk), lambda j, i, k: (i, k)),
                  pl.BlockSpec((tk, tn), lambda j, i, k: (k, j)),
                  pl.BlockSpec((tm, tn), lambda j, i, k: (i, j))],
        out_specs=pl.BlockSpec((tm, tn), lambda j, i, k: (i, j)),
        out_shape=jax.ShapeDtypeStruct((m, n), F32),
        compiler_params=_params(3),
        name="down_proj",
    )(a, w_down, res)


def _remember_bf16(bf16_w, wts, name, hosted):
    if name not in bf16_w:
        bf16_w[name] = hosted if hosted is not None else wts[name].astype(BF16)


def _layer_head(x, wts, bf16_w):
    h = _rmsnorm(x, wts["mix_norm_w"])
    if "w_in_t" in bf16_w:
        return _in_proj(h, wts["w_in_t"], bf16_w["w_in_t"])
    if x.shape[0] <= 1024:
        proj, dt_raw, bf16_w["w_in_t"] = _in_proj(h, wts["w_in_t"], keep_bf16=True)
        return proj, dt_raw
    return _in_proj(h, wts["w_in_t"])


def _layer(x, head, nseq, seqlen, wts, bf16_w, bias16, attn_state, ssm_state, conv_state, ffn_state):
    m = x.shape[0]
    todo = lambda name: None if name in bf16_w else wts[name]
    proj, dt_raw = head
    new_conv = proj.reshape(nseq, seqlen, PROJ_W)[:, seqlen - (SSD_CONV - 1):, COL_XBC:COL_XBC + CONV_DIM]

    qw = wts["q_norm_w"].reshape(1, HEAD_DIM)
    kw = wts["k_norm_w"].reshape(1, HEAD_DIM)
    if attn_state is None:
        names = [n for n in ("w_gate",) if n not in bf16_w]
        attn, new_k, new_v, hosted = _attn_prompt(proj, wts["attn_sinks"], qw, kw, bias16, nseq, seqlen,
                                                  cast_ws=tuple(wts[n] for n in names))
        for k, n in enumerate(names):
            _remember_bf16(bf16_w, wts, n, None if hosted is None else hosted[k])
        new_k = new_k.reshape(nseq, WINDOW, KV_HEADS, HEAD_DIM)
        new_v = new_v.reshape(nseq, WINDOW, KV_HEADS, HEAD_DIM)
        conv_prev = jnp.zeros((nseq, SSD_CONV - 1, CONV_DIM), F32)
        h0 = jnp.zeros((nseq, SSD_WIDTH, SSD_STATE), F32)
    else:
        win_k, win_v = attn_state
        attn, new_k, new_v = _attn_sample(proj, win_k, win_v, wts["attn_sinks"], qw, kw, bias16, nseq, seqlen)
        _remember_bf16(bf16_w, wts, "w_gate", None)
        conv_prev = conv_state
        h0 = ssm_state.reshape(nseq, SSD_WIDTH, SSD_STATE)

    names = [n for n in ("w_up", "w_out") if n not in bf16_w]
    y, h_t, hosted = _ssd(proj, dt_raw, conv_prev, h0, wts["ssd_conv_w"], wts["ssd_conv_b"], wts["ssd_dt_bias"],
                          wts["ssd_A_log"], wts["ssd_D"], wts["ssd_norm_w"], nseq, seqlen,
                          cast_ws=tuple(wts[n] for n in names))
    for k, n in enumerate(names):
        _remember_bf16(bf16_w, wts, n, None if hosted is None else hosted[k])
    h_t = h_t.reshape(nseq, SSD_HEADS, SSD_HEAD_DIM, SSD_STATE)

    x1, xw, ssq = _out_proj(attn, y, bf16_w["w_out"], x, wts["ffn_norm_w"])
    if ffn_state is None:
        a, last, hosted = _ffn_gate(xw, ssq, bf16_w["w_gate"], bf16_w["w_up"], wts["ffn_conv_w"],
                                    wts["ffn_conv_b"], nseq=nseq, seqlen=seqlen, cast_w=todo("w_down"))
        new_ffn = last.reshape(nseq, -1, FFN_CONV - 1, D_FF_PAD)[:, -1, :, :D_FF]
    else:
        prev_rows = jnp.pad(ffn_state, ((0, 0), (0, seqlen - (FFN_CONV - 1)), (0, D_FF_PAD - D_FF)))
        a, g, hosted = _ffn_gate(xw, ssq, bf16_w["w_gate"], bf16_w["w_up"], wts["ffn_conv_w"], wts["ffn_conv_b"],
                                 nseq=nseq, seqlen=seqlen, prev_rows=prev_rows.reshape(m, D_FF_PAD))
        new_ffn = g.reshape(nseq, seqlen, D_FF_PAD)[:, seqlen - (FFN_CONV - 1):, :D_FF]
    _remember_bf16(bf16_w, wts, "w_down", hosted)
    x2 = _down_proj(a, bf16_w["w_down"], x1)
    return x2, new_k, new_v, h_t, new_conv, new_ffn


def kernel(x_prompt, x_sample, state_attn_k, state_attn_v, state_ssm, state_ssd_conv, state_ffn_conv, rel_bias, mix_norm_w, w_in, q_norm_w, k_norm_w, attn_sinks, ssd_conv_w, ssd_conv_b, ssd_dt_bias, ssd_A_log, ssd_D, ssd_norm_w, w_out, ffn_norm_w, w_gate, w_up, ffn_conv_w, ffn_conv_b, w_down):
    depth = w_in.shape[0]
    bp, lp, _ = x_prompt.shape
    bs, ls, _ = x_sample.shape
    bias16 = _bias_table(rel_bias)
    yp = x_prompt.reshape(bp * lp, D_MODEL)
    ys = x_sample.reshape(bs * ls, D_MODEL)
    outs_p, outs_s = [], []
    pad_heads = lambda v: jnp.pad(v.reshape(1, SSD_HEADS), ((0, 0), (0, ROWS - SSD_HEADS)))
    pad_ff = lambda v: jnp.pad(v, ((0, 0), (0, D_FF_PAD - D_FF)))
    for l in range(depth):
        wts = dict(
            mix_norm_w=mix_norm_w[l],
            w_in_t=w_in[l].T,

            q_norm_w=q_norm_w[l], k_norm_w=k_norm_w[l], attn_sinks=attn_sinks[l],
            ssd_conv_w=ssd_conv_w[l], ssd_conv_b=ssd_conv_b[l].reshape(1, CONV_DIM),
            ssd_dt_bias=pad_heads(ssd_dt_bias[l]), ssd_A_log=pad_heads(ssd_A_log[l]),
            ssd_D=jnp.repeat(ssd_D[l], SSD_HEAD_DIM).reshape(1, SSD_WIDTH),
            ssd_norm_w=ssd_norm_w[l].reshape(1, SSD_WIDTH),
            w_out=w_out[l], ffn_norm_w=ffn_norm_w[l],
            w_gate=w_gate[l], w_up=w_up[l],
            ffn_conv_w=pad_ff(ffn_conv_w[l]), ffn_conv_b=pad_ff(ffn_conv_b[l].reshape(1, D_FF)),
            w_down=w_down[l],
        )
        bf16_w = {}
        head_s = _layer_head(ys, wts, bf16_w)
        head_p = _layer_head(yp, wts, bf16_w)
        yp, *sp = _layer(yp, head_p, bp, lp, wts, bf16_w, bias16, None, None, None, None)
        ys, *ss = _layer(ys, head_s, bs, ls, wts, bf16_w, bias16, (state_attn_k[l], state_attn_v[l]),
                         state_ssm[l], state_ssd_conv[l], state_ffn_conv[l])
        outs_p.append(sp)
        outs_s.append(ss)
    stack = lambda outs, i: jnp.stack([o[i] for o in outs])
    return (yp.reshape(bp, lp, D_MODEL), ys.reshape(bs, ls, D_MODEL),
            stack(outs_p, 0), stack(outs_p, 1), stack(outs_p, 2), stack(outs_p, 3), stack(outs_p, 4),
            stack(outs_s, 0), stack(outs_s, 1), stack(outs_s, 2), stack(outs_s, 3), stack(outs_s, 4))
```

```python
import functools
import math

import numpy as np
import jax
import jax.numpy as jnp
from jax import lax
from jax.experimental import pallas as pl
from jax.experimental.pallas import tpu as pltpu

F32 = jnp.float32
BF16 = jnp.bfloat16

D_MODEL = 4096
HEAD_DIM = 128
ATTN_HEADS = 16
KV_HEADS = 4
Q_PER_KV = 4
ATTN_WIDTH = 2048
WINDOW = 128
N_BUCKETS = 32
MAX_DISTANCE = 128
SSD_HEAD_DIM = 64
SSD_HEADS = 32
SSD_WIDTH = 2048
SSD_GROUPS = 8
SSD_STATE = 128
SSD_CONV = 4
CONV_DIM = 4096
KV_WIDTH = KV_HEADS * HEAD_DIM
D_FF = 11008
D_FF_PAD = 11264
FFN_CONV = 3
EPS = 1e-6
NEG = -1e30
SCALE = HEAD_DIM ** -0.5

ROWS = 128
GROUP_ROWS = 4 * SSD_HEAD_DIM
COL_Z = ATTN_WIDTH
COL_XBC = COL_Z + SSD_WIDTH
COL_K = COL_XBC + CONV_DIM
COL_V = COL_K + KV_WIDTH
PROJ_W = COL_V + KV_WIDTH
VMEM_LIMIT_MB = 56


def _params(n_axes, vmem_mb=VMEM_LIMIT_MB):
    return pltpu.CompilerParams(dimension_semantics=("arbitrary",) * n_axes,
                                vmem_limit_bytes=vmem_mb << 20)


def _dot(a, b):
    return jnp.dot(a, b, preferred_element_type=F32)


def _dot_nt(a, b):
    return lax.dot_general(a, b, (((1,), (1,)), ((), ())), preferred_element_type=F32)


def _dot_tn(a, b):
    return lax.dot_general(a, b, (((0,), (0,)), ((), ())), preferred_element_type=F32)


def _split3(x):
    a = x.astype(BF16)
    r = x - a.astype(F32)
    b = r.astype(BF16)
    c = (r - b.astype(F32)).astype(BF16)
    return a, b, c


def _dot3_l(x, w):
    a, b, c = _split3(x)
    return (_dot(a, w) + _dot(b, w)) + _dot(c, w)


def _dot3_r(w, x3):
    return (_dot(w, x3[0]) + _dot(w, x3[1])) + _dot(w, x3[2])


def _sigmoid(x):
    return 1.0 / (1.0 + jnp.exp(-x))


def _rms_rows(x, w):
    r = lax.rsqrt(jnp.mean(x * x, axis=-1, keepdims=True) + EPS)
    return (x * r) * w


def _rmsnorm_body(x_ref, w_ref, o_ref):
    o_ref[...] = _rms_rows(x_ref[...], w_ref[...]).astype(o_ref.dtype)


def _rmsnorm(x, w, rows=512):
    m, d = x.shape
    rows = min(rows, m)
    return pl.pallas_call(
        _rmsnorm_body,
        grid=(m // rows,),
        in_specs=[pl.BlockSpec((rows, d), lambda i: (i, 0)),
                  pl.BlockSpec((1, d), lambda i: (0, 0))],
        out_specs=pl.BlockSpec((rows, d), lambda i: (i, 0)),
        out_shape=jax.ShapeDtypeStruct((m, d), BF16),
        compiler_params=_params(1, 32),
        name="rmsnorm",
    )(x, w.reshape(1, d))


def _in_proj_body(a_ref, bt_ref, dtw_ref, o_ref, dt_ref, *wcopy_ref):
    a = a_ref[...]
    bt = bt_ref[...].astype(BF16)
    o_ref[...] = _dot_nt(a, bt)
    if wcopy_ref:
        wcopy_ref[0][...] = bt

    @pl.when(pl.program_id(1) == 0)
    def _():
        row = lax.broadcasted_iota(jnp.int32, (dtw_ref.shape[0], 1), 0)
        dtw = jnp.where(row < SSD_HEADS, dtw_ref[...], 0.0)
        dt_ref[...] = _dot_nt(a, dtw.astype(BF16))


def _in_proj(h, w_in_t, w_main=None, *, keep_bf16=False, tm=1024):
    m = h.shape[0]
    tm = min(tm, m)
    tn = 512 if w_main is None else 1024
    nq, nkv2 = ATTN_WIDTH // tn, 2 * KV_WIDTH // tn
    nzx = (SSD_WIDTH + CONV_DIM) // tn

    def src_block(j):
        return jnp.where(j < nq, j, jnp.where(j < nq + nzx, j + nkv2, j - nzx))

    wspec = pl.BlockSpec((tn, D_MODEL), lambda i, j: (src_block(j), 0))
    out_specs = [pl.BlockSpec((tm, tn), lambda i, j: (i, j)),
                 pl.BlockSpec((tm, ROWS), lambda i, j: (i, 0))]
    out_shape = [jax.ShapeDtypeStruct((m, PROJ_W), F32),
                 jax.ShapeDtypeStruct((m, ROWS), F32)]
    if keep_bf16:
        assert w_main is None and m == tm
        out_specs.append(wspec)
        out_shape.append(jax.ShapeDtypeStruct((PROJ_W, D_MODEL), BF16))
    return pl.pallas_call(
        _in_proj_body,
        grid=(m // tm, PROJ_W // tn),
        in_specs=[pl.BlockSpec((tm, D_MODEL), lambda i, j: (i, 0)),
                  wspec,
                  pl.BlockSpec((ROWS, D_MODEL), lambda i, j: (PROJ_W // ROWS, 0))],
        out_specs=out_specs,
        out_shape=out_shape,
        compiler_params=_params(2),
        name="in_proj",
    )(h, w_in_t if w_main is None else w_main, w_in_t)


def _out_proj_body(a1_ref, a2_ref, b1_ref, b2_ref, res_ref, nw_ref, o_ref, xw_ref, ssq_ref):
    acc = (_dot(a1_ref[...].astype(BF16), b1_ref[...].astype(BF16))
           + _dot(a2_ref[...].astype(BF16), b2_ref[...].astype(BF16)))
    x1 = res_ref[...] + acc
    o_ref[...] = x1
    xw_ref[...] = (x1 * nw_ref[...]).astype(xw_ref.dtype)
    ssq_ref[0] = jnp.sum(x1 * x1, axis=-1, keepdims=True)


def _out_proj(attn, y, w_out, res, norm_w, *, tm=1024, tn=512):
    m = attn.shape[0]
    n = w_out.shape[1]
    tm = min(tm, m)
    return pl.pallas_call(
        _out_proj_body,
        grid=(m // tm, n // tn),
        in_specs=[pl.BlockSpec((tm, ATTN_WIDTH), lambda i, j: (i, 0)),
                  pl.BlockSpec((tm, SSD_WIDTH), lambda i, j: (i, 0)),
                  pl.BlockSpec((ATTN_WIDTH, tn), lambda i, j: (0, j)),
                  pl.BlockSpec((SSD_WIDTH, tn), lambda i, j: (1, j)),
                  pl.BlockSpec((tm, tn), lambda i, j: (i, j)),
                  pl.BlockSpec((1, tn), lambda i, j: (0, j))],
        out_specs=[pl.BlockSpec((tm, tn), lambda i, j: (i, j)),
                   pl.BlockSpec((tm, tn), lambda i, j: (i, j)),
                   pl.BlockSpec((1, tm, 1), lambda i, j: (j, i, 0))],
        out_shape=[jax.ShapeDtypeStruct((m, n), F32),
                   jax.ShapeDtypeStruct((m, n), BF16),
                   jax.ShapeDtypeStruct((n // tn, m, 1), F32)],
        compiler_params=_params(2),
        name="out_proj",
    )(attn, y, w_out, w_out, res, norm_w.reshape(1, n))


def _bucket_table():
    i = np.arange(ROWS)[:, None]
    j = np.arange(2 * ROWS)[None, :]
    dist = i + ROWS - j
    valid = (dist >= 0) & (dist < WINDOW)
    n = np.maximum(dist, 0)
    max_exact = N_BUCKETS // 2

    def large(dtype):
        nf = np.maximum(n, 1).astype(dtype)
        v = np.log(nf / dtype(max_exact)) / dtype(math.log(MAX_DISTANCE / max_exact)) * dtype(N_BUCKETS - max_exact)
        return np.minimum(max_exact + v.astype(np.int32), N_BUCKETS - 1)

    assert (large(np.float32) == large(np.float64)).all()
    bucket = np.where(n < max_exact, n, large(np.float32))
    return np.where(valid, bucket, -1).astype(np.int32)


def _bias_body(rb_ref, bkt_ref, o_ref):
    h = pl.program_id(0)
    bkt = bkt_ref[...]
    acc = jnp.full(bkt.shape, NEG, F32)
    for b in range(N_BUCKETS):
        acc = jnp.where(bkt == b, rb_ref[b, h], acc)
    o_ref[0] = acc


def _bias_table(rel_bias):
    return pl.pallas_call(
        _bias_body,
        grid=(ATTN_HEADS,),
        in_specs=[pl.BlockSpec(memory_space=pltpu.SMEM),
                  pl.BlockSpec((ROWS, 2 * ROWS), lambda h: (0, 0))],
        out_specs=pl.BlockSpec((1, ROWS, 2 * ROWS), lambda h: (h, 0, 0)),
        out_shape=jax.ShapeDtypeStruct((ATTN_HEADS, ROWS, 2 * ROWS), F32),
        compiler_params=_params(1, 16),
        name="bias_table",
    )(rel_bias, jnp.asarray(_bucket_table()))


def _cast_job(w, nsteps):
    rows = -(-w.shape[0] // nsteps)
    rows = -(-rows // 16) * 16
    return rows if rows * w.shape[1] * 4 <= (8 << 20) else None


def _cast_specs(w, rows, step_of):
    nblk = -(-w.shape[0] // rows)
    spec = pl.BlockSpec((rows, w.shape[1]), lambda *g: (jnp.minimum(step_of(*g), nblk - 1), 0))
    return spec, spec, jax.ShapeDtypeStruct(w.shape, BF16)


def _softmax_sink(s_scr, bias, sink, p_scr):
    sc = s_scr[...] * SCALE + bias
    m = jnp.maximum(jnp.max(sc, axis=-1, keepdims=True), sink)
    p = jnp.exp(sc - m)
    den = jnp.sum(p, axis=-1, keepdims=True) + jnp.exp(sink - m)
    p_scr[...] = (p * (1.0 / den)).astype(BF16)


def _attn_prompt_body(q_ref, kc_ref, kp_ref, vc_ref, vp_ref, qw_ref, kw_ref, bias_ref, sink_ref, *rest, ncast, bps):
    o_ref, pk_ref, pv_ref = rest[ncast:ncast + 3]
    for src_ref, dst_ref in zip(rest[:ncast], rest[ncast + 3:2 * ncast + 3]):
        dst_ref[...] = src_ref[...].astype(BF16)
    qn_scr, kn_scr, s_scr, p_scr = rest[2 * ncast + 3:]
    qw = qw_ref[...]
    kw = kw_ref[...]
    grows = Q_PER_KV * ROWS
    for h in range(ATTN_HEADS):
        hs = slice(h * HEAD_DIM, (h + 1) * HEAD_DIM)
        qn_scr[:, hs] = _rms_rows(q_ref[:, hs], qw)
    for kv in range(KV_HEADS):
        sl = slice(kv * HEAD_DIM, (kv + 1) * HEAD_DIM)
        kn_scr[0:ROWS, sl] = _rms_rows(kp_ref[:, sl], kw)
        kn_scr[ROWS:(bps + 1) * ROWS, sl] = _rms_rows(kc_ref[:, sl], kw)
    pk_ref[0] = kn_scr[bps * ROWS:(bps + 1) * ROWS, :]
    pv_ref[0] = vc_ref[(bps - 1) * ROWS:bps * ROWS, :]

    col = lax.broadcasted_iota(jnp.int32, (1, 2 * ROWS), 1)
    for t in range(bps):
        qrows = slice(t * ROWS, (t + 1) * ROWS)
        krows = slice(t * ROWS, (t + 2) * ROWS)
        for kv in range(KV_HEADS):
            sl = slice(kv * HEAD_DIM, (kv + 1) * HEAD_DIM)
            qs = jnp.concatenate([qn_scr[qrows, (kv * Q_PER_KV + g) * HEAD_DIM:(kv * Q_PER_KV + g + 1) * HEAD_DIM]
                                  for g in range(Q_PER_KV)], axis=0).astype(BF16)
            s_scr[kv * grows:(kv + 1) * grows, :] = _dot_nt(qs, kn_scr[krows, sl].astype(BF16))

        bias = bias_ref[...]
        if t == 0:
            bias = bias + jnp.where(jnp.logical_and(pl.program_id(1) == 0, col < ROWS), NEG, 0.0)
        _softmax_sink(s_scr, bias, sink_ref[...], p_scr)

        for kv in range(KV_HEADS):
            sl = slice(kv * HEAD_DIM, (kv + 1) * HEAD_DIM)
            vprev = vp_ref[:, sl] if t == 0 else vc_ref[(t - 1) * ROWS:t * ROWS, sl]
            vcat = jnp.concatenate([vprev, vc_ref[qrows, sl]], axis=0).astype(BF16)
            o = _dot(p_scr[kv * grows:(kv + 1) * grows, :], vcat)
            for g in range(Q_PER_KV):
                h = kv * Q_PER_KV + g
                o_ref[qrows, h * HEAD_DIM:(h + 1) * HEAD_DIM] = o[g * ROWS:(g + 1) * ROWS].astype(o_ref.dtype)


def _attn_prompt(proj, sinks, qw, kw, bias16, nseq, seqlen, cast_ws=()):
    nb = seqlen // ROWS
    bps = 2 if nb % 2 == 0 else 1
    nstep = nb // bps
    kblk, vblk = COL_K // KV_WIDTH, COL_V // KV_WIDTH
    srows = ATTN_HEADS * ROWS
    cur = lambda b, i: b * nstep + i
    prev = lambda b, i: b * nb + jnp.maximum(i * bps - 1, 0)
    const = lambda b, i: (0, 0)
    in_specs = [pl.BlockSpec((bps * ROWS, ATTN_WIDTH), lambda b, i: (cur(b, i), 0)),
                pl.BlockSpec((bps * ROWS, KV_WIDTH), lambda b, i: (cur(b, i), kblk)),
                pl.BlockSpec((ROWS, KV_WIDTH), lambda b, i: (prev(b, i), kblk)),
                pl.BlockSpec((bps * ROWS, KV_WIDTH), lambda b, i: (cur(b, i), vblk)),
                pl.BlockSpec((ROWS, KV_WIDTH), lambda b, i: (prev(b, i), vblk)),
                pl.BlockSpec((1, HEAD_DIM), const),
                pl.BlockSpec((1, HEAD_DIM), const),
                pl.BlockSpec((srows, 2 * ROWS), const),
                pl.BlockSpec((srows, 1), const)]
    out_specs = [pl.BlockSpec((bps * ROWS, ATTN_WIDTH), lambda b, i: (cur(b, i), 0)),
                 pl.BlockSpec((1, ROWS, KV_WIDTH), lambda b, i: (b, 0, 0)),
                 pl.BlockSpec((1, ROWS, KV_WIDTH), lambda b, i: (b, 0, 0))]
    out_shape = [jax.ShapeDtypeStruct((nseq * seqlen, ATTN_WIDTH), BF16),
                 jax.ShapeDtypeStruct((nseq, ROWS, KV_WIDTH), F32),
                 jax.ShapeDtypeStruct((nseq, ROWS, KV_WIDTH), F32)]
    args = [proj, proj, proj, proj, proj, qw, kw, bias16.reshape(srows, 2 * ROWS),
            jnp.repeat(sinks, ROWS).reshape(srows, 1)]
    rows = [_cast_job(w, nseq * nstep) for w in cast_ws]
    hosted = bool(cast_ws) and all(r is not None for r in rows)
    if hosted:
        for w, r in zip(cast_ws, rows):
            ispec, ospec, oshape = _cast_specs(w, r, cur)
            in_specs.append(ispec)
            out_specs.append(ospec)
            out_shape.append(oshape)
            args.append(w)
    ncast = len(cast_ws) if hosted else 0
    outs = pl.pallas_call(
        functools.partial(_attn_prompt_body, ncast=ncast, bps=bps),
        grid=(nseq, nstep),
        in_specs=in_specs,
        out_specs=out_specs,
        out_shape=out_shape,
        scratch_shapes=[pltpu.VMEM((bps * ROWS, ATTN_WIDTH), F32),
                        pltpu.VMEM(((bps + 1) * ROWS, KV_WIDTH), F32),
                        pltpu.VMEM((srows, 2 * ROWS), F32),
                        pltpu.VMEM((srows, 2 * ROWS), BF16)],
        compiler_params=_params(2, 48),
        name="attn_prompt",
    )(*args)
    return (*outs[:3], tuple(outs[3:]) if hosted else None)


def _attn_sample_body(q_ref, kn_ref, vn_ref, wk_ref, wv_ref, qw_ref, kw_ref, bias_ref, sink_ref,
                      o_ref, sk_ref, sv_ref, qn_scr, kn_scr, s_scr, p_scr, *, nsq, ntok):
    qw = qw_ref[...]
    kw = kw_ref[...]
    grows = Q_PER_KV * ntok
    keep = ROWS - ntok
    for h in range(ATTN_HEADS):
        hs = slice(h * HEAD_DIM, (h + 1) * HEAD_DIM)
        qn_scr[:, hs] = _rms_rows(q_ref[:, hs], qw)
    for kv in range(KV_HEADS):
        sl = slice(kv * HEAD_DIM, (kv + 1) * HEAD_DIM)
        kn_scr[:, sl] = _rms_rows(kn_ref[:, sl], kw)
    for s in range(nsq):
        rows = slice(s * ntok, (s + 1) * ntok)
        sk_ref[s, 0:keep] = wk_ref[s, ntok:ROWS]
        sv_ref[s, 0:keep] = wv_ref[s, ntok:ROWS]
        for kv in range(KV_HEADS):
            sl = slice(kv * HEAD_DIM, (kv + 1) * HEAD_DIM)
            sk_ref[s, keep:ROWS, kv, :] = kn_scr[rows, sl]
            sv_ref[s, keep:ROWS, kv, :] = vn_ref[rows, sl]

    pad = jnp.zeros((keep, HEAD_DIM), F32)
    for s in range(nsq):
        rows = slice(s * ntok, (s + 1) * ntok)
        for kv in range(KV_HEADS):
            sl = slice(kv * HEAD_DIM, (kv + 1) * HEAD_DIM)
            grp = s * KV_HEADS + kv
            qs = jnp.concatenate([qn_scr[rows, (kv * Q_PER_KV + g) * HEAD_DIM:(kv * Q_PER_KV + g + 1) * HEAD_DIM]
                                  for g in range(Q_PER_KV)], axis=0).astype(BF16)
            kcat = jnp.concatenate([wk_ref[s, :, kv, :], kn_scr[rows, sl], pad], axis=0).astype(BF16)
            s_scr[grp * grows:(grp + 1) * grows, :] = _dot_nt(qs, kcat)

    _softmax_sink(s_scr, bias_ref[...], sink_ref[...], p_scr)

    for s in range(nsq):
        rows = slice(s * ntok, (s + 1) * ntok)
        for kv in range(KV_HEADS):
            sl = slice(kv * HEAD_DIM, (kv + 1) * HEAD_DIM)
            grp = s * KV_HEADS + kv
            vcat = jnp.concatenate([wv_ref[s, :, kv, :], vn_ref[rows, sl], pad], axis=0).astype(BF16)
            o = _dot(p_scr[grp * grows:(grp + 1) * grows, :], vcat)
            for g in range(Q_PER_KV):
                h = kv * Q_PER_KV + g
                o_ref[rows, h * HEAD_DIM:(h + 1) * HEAD_DIM] = o[g * ntok:(g + 1) * ntok]


def _attn_sample(proj, win_k, win_v, sinks, qw, kw, bias16, nseq, ntok, nsq=8):
    kblk, vblk = COL_K // KV_WIDTH, COL_V // KV_WIDTH
    rows = nsq * ntok
    srows = nsq * ATTN_HEADS * ntok
    bias = jnp.tile(bias16[:, :ntok, :].reshape(ATTN_HEADS * ntok, 2 * ROWS), (nsq, 1))
    sink = jnp.tile(jnp.repeat(sinks, ntok), nsq).reshape(srows, 1)
    return pl.pallas_call(
        functools.partial(_attn_sample_body, nsq=nsq, ntok=ntok),
        grid=(nseq // nsq,),
        in_specs=[pl.BlockSpec((rows, ATTN_WIDTH), lambda i: (i, 0)),
                  pl.BlockSpec((rows, KV_WIDTH), lambda i: (i, kblk)),
                  pl.BlockSpec((rows, KV_WIDTH), lambda i: (i, vblk)),
                  pl.BlockSpec((nsq, ROWS, KV_HEADS, HEAD_DIM), lambda i: (i, 0, 0, 0)),
                  pl.BlockSpec((nsq, ROWS, KV_HEADS, HEAD_DIM), lambda i: (i, 0, 0, 0)),
                  pl.BlockSpec((1, HEAD_DIM), lambda i: (0, 0)),
                  pl.BlockSpec((1, HEAD_DIM), lambda i: (0, 0)),
                  pl.BlockSpec((srows, 2 * ROWS), lambda i: (0, 0)),
                  pl.BlockSpec((srows, 1), lambda i: (0, 0))],
        out_specs=[pl.BlockSpec((rows, ATTN_WIDTH), lambda i: (i, 0)),
                   pl.BlockSpec((nsq, ROWS, KV_HEADS, HEAD_DIM), lambda i: (i, 0, 0, 0)),
                   pl.BlockSpec((nsq, ROWS, KV_HEADS, HEAD_DIM), lambda i: (i, 0, 0, 0))],
        out_shape=[jax.ShapeDtypeStruct((nseq * ntok, ATTN_WIDTH), F32),
                   jax.ShapeDtypeStruct((nseq, ROWS, KV_HEADS, HEAD_DIM), F32),
                   jax.ShapeDtypeStruct((nseq, ROWS, KV_HEADS, HEAD_DIM), F32)],
        scratch_shapes=[pltpu.VMEM((rows, ATTN_WIDTH), F32),
                        pltpu.VMEM((rows, KV_WIDTH), F32),
                        pltpu.VMEM((srows, 2 * ROWS), F32),
                        pltpu.VMEM((srows, 2 * ROWS), BF16)],
        compiler_params=_params(1, 32),
        name="attn_sample",
    )(proj, proj, proj, win_k, win_v, qw, kw, bias, sink)


def _ssd_constants(qs):
    r = np.arange(ROWS)
    same = (r[:, None] // qs) == (r[None, :] // qs)
    mask = same & (r[None, :] <= r[:, None])
    sel = r[None, :] == ((r[:, None] // qs) * qs + qs - 1)
    expand = np.zeros((ROWS, SSD_WIDTH), np.float32)
    for h in range(SSD_HEADS):
        expand[h, h * SSD_HEAD_DIM:(h + 1) * SSD_HEAD_DIM] = 1.0
    to_bf = lambda a: jnp.asarray(a.astype(np.float32), BF16)
    return to_bf(mask), to_bf(sel), to_bf(expand)


def _ssd_body(xbc_ref, z_ref, dt_ref, *rest, cps, ncast, **static):
    consts, rest = rest[:11], rest[11:]
    y_ref, ht_ref = rest[ncast:ncast + 2]
    for src_ref, dst_ref in zip(rest[:ncast], rest[ncast + 2:2 * ncast + 2]):
        dst_ref[...] = src_ref[...].astype(BF16)
    scratch = rest[2 * ncast + 2:]
    for ci in range(cps):
        rows = pl.ds(ci * ROWS, ROWS)
        first = pl.program_id(1) == 0 if ci == 0 else None
        _ssd_chunk(first, xbc_ref.at[rows, :], z_ref.at[rows, :], dt_ref.at[rows, :], *consts,
                   y_ref.at[rows, :], ht_ref, *scratch, **static)


def _ssd_chunk(first, xbc_ref, z_ref, dt_ref, cprev_ref, h0_ref, cw_ref, cb_ref, dtb_ref, alog_ref, dexp_ref,
               nw_ref, mask_ref, sel_ref, e_ref, y_ref, ht_ref,
               h_scr, cbuf, xc_scr, xdt_scr, xdd_scr, eacs_scr, yacc_scr, acs_scr, acst_scr, aclt_scr,
               *, nsq, qs, spp):
    s = pl.program_id(2)
    if first is not None:
        @pl.when(first)
        def _():
            h_scr[...] = h0_ref[...]

        @pl.when(jnp.logical_and(first, s == 0))
        def _():
            cbuf[0:8, :] = jnp.zeros((8, CONV_DIM), F32)
            if nsq == 1:
                cbuf[5:8, :] = cprev_ref[0]

    @pl.when(s == 0)
    def _chunk_phase():
        ncc = 8 if qs == ROWS else 1
        cwid = CONV_DIM // ncc
        nrow = 8 + qs
        for sq in range(nsq):
            for cc in range(ncc):
                cs = slice(cc * cwid, (cc + 1) * cwid)
                if nsq > 1:
                    cbuf[5:8, cs] = cprev_ref[sq, :, cs]
                cbuf[8:nrow, cs] = xbc_ref[sq * qs:(sq + 1) * qs, cs]
                xa = cbuf[0:nrow, cs]
                acc = cb_ref[:, cs] + cw_ref[SSD_CONV - 1:SSD_CONV, cs] * xa[8:nrow]
                for t in range(SSD_CONV - 1):
                    acc = acc + cw_ref[t:t + 1, cs] * pltpu.roll(xa, nrow - (5 + t), 0)[0:qs]
                xc_scr[sq * qs:(sq + 1) * qs, cs] = acc * _sigmoid(acc)
                if nsq == 1:
                    cbuf[5:8, cs] = xbc_ref[qs - 3:qs, cs]

        x = dt_ref[...] + dtb_ref[...]
        dt = jnp.maximum(x, 0.0) + jnp.log1p(jnp.exp(-jnp.abs(x)))
        da = dt * (-jnp.exp(alog_ref[...]))
        mask_bf = mask_ref[...]
        acs = _dot3_r(mask_bf, _split3(da))
        acl = _dot3_r(sel_ref[...], _split3(acs))
        acs_scr[...] = acs
        acst_scr[...] = acs.T
        aclt_scr[...] = acl.T
        e = e_ref[...]
        xs = xc_scr[:, 0:SSD_WIDTH]
        xdt_scr[...] = xs * _dot3_l(dt, e)
        xdd_scr[...] = xs * _dot3_l(dt * jnp.exp(acl - acs), e)
        eacs_scr[...] = _dot3_l(jnp.exp(acs), e)

        maskb = mask_bf > 0.5
        lane = lax.broadcasted_iota(jnp.int32, (ROWS, 2 * SSD_HEAD_DIM), 1)
        for g in range(SSD_GROUPS):
            bg = xc_scr[:, SSD_WIDTH + g * SSD_STATE:SSD_WIDTH + (g + 1) * SSD_STATE].astype(BF16)
            cg = xc_scr[:, SSD_WIDTH + (SSD_GROUPS + g) * SSD_STATE:
                        SSD_WIDTH + (SSD_GROUPS + g + 1) * SSD_STATE].astype(BF16)
            cb = _dot_nt(cg, bg)
            for pp in range(2):
                ms = []
                for j in range(2):
                    h = 4 * g + 2 * pp + j
                    seg = acs_scr[:, h:h + 1] - acst_scr[h:h + 1, :]
                    ms.append((cb * jnp.exp(jnp.where(maskb, seg, NEG))).astype(BF16))
                c0 = (4 * g + 2 * pp) * SSD_HEAD_DIM
                cols = slice(c0, c0 + 2 * SSD_HEAD_DIM)
                xp = xdt_scr[:, cols]
                rhs = jnp.concatenate([jnp.where(lane < SSD_HEAD_DIM, xp, 0.0),
                                       jnp.where(lane >= SSD_HEAD_DIM, xp, 0.0)], axis=0).astype(BF16)
                yd = _dot(jnp.concatenate(ms, axis=1), rhs)
                yacc_scr[:, cols] = yd + dexp_ref[:, cols] * xc_scr[:, cols]

    tlane = lax.broadcasted_iota(jnp.int32, (ROWS, ROWS), 1)
    ridx = lax.broadcasted_iota(jnp.int32, (ROWS, 1), 0)
    for u in range(spp):
        if nsq == 1:
            r0 = 0
            rows = slice(0, qs)
        else:
            r0 = pl.multiple_of((s * spp + u) * qs, qs)
            rows = pl.ds(r0, qs)
            rowmask = jnp.logical_and(ridx >= r0, ridx < r0 + qs)
        xcol = jnp.sum(jnp.where(tlane == r0, aclt_scr[...], 0.0), axis=1, keepdims=True)
        dec = jnp.exp(jnp.broadcast_to(xcol, (ROWS, ROWS)))
        c_rows = xc_scr[rows, SSD_WIDTH + SSD_GROUPS * SSD_STATE:CONV_DIM].astype(BF16)
        yo, st = [], []
        for g in range(SSD_GROUPS):
            grows = slice(g * GROUP_ROWS, (g + 1) * GROUP_ROWS)
            yo.append(_dot_nt(c_rows[:, g * SSD_STATE:(g + 1) * SSD_STATE], h_scr[u, grows, :].astype(BF16)))
            bg = xc_scr[:, SSD_WIDTH + g * SSD_STATE:SSD_WIDTH + (g + 1) * SSD_STATE]
            if nsq > 1:
                bg = jnp.where(rowmask, bg, 0.0)
            st.append(_dot_tn(xdd_scr[:, grows].astype(BF16), bg.astype(BF16)))
        yacc_scr[rows, :] = yacc_scr[rows, :] + jnp.concatenate(yo, axis=1) * eacs_scr[rows, :]
        for h in range(SSD_HEADS):
            hrows = slice(h * SSD_HEAD_DIM, (h + 1) * SSD_HEAD_DIM)
            g, off = divmod(h * SSD_HEAD_DIM, GROUP_ROWS)
            h_new = dec[h:h + 1, :] * h_scr[u, hrows, :] + st[g][off:off + SSD_HEAD_DIM]
            h_scr[u, hrows, :] = h_new
            ht_ref[u, hrows, :] = h_new

    @pl.when(s == nsq // spp - 1)
    def _finish():
        z = z_ref[...]
        yg = yacc_scr[...] * (z * _sigmoid(z))
        y_ref[...] = _rms_rows(yg, nw_ref[...]).astype(y_ref.dtype)


def _ssd(proj, dt_raw, conv_prev, h0, cw, cb, dtb, alog, dexp, nw, nseq, seqlen, cast_ws=()):
    if seqlen % ROWS == 0:
        qs, nsq, nchunk, ngrp = ROWS, 1, seqlen // ROWS, nseq
    else:
        qs, nsq, nchunk = seqlen, ROWS // seqlen, 1
        ngrp = nseq // nsq
    spp = math.gcd(nsq, 4)
    nstep = nsq // spp
    cps = 2 if nchunk % 2 == 0 else 1
    ncstep = nchunk // cps
    crows = cps * ROWS
    mask, sel, expand = _ssd_constants(qs)
    rowblk = lambda b, c, s: b * ncstep + c
    seq = lambda b, c, s: b * nstep + s
    const2 = lambda b, c, s: (0, 0)
    full = lambda shape: pl.BlockSpec(shape, const2)
    in_specs = [pl.BlockSpec((crows, CONV_DIM), lambda b, c, s: (rowblk(b, c, s), COL_XBC // CONV_DIM)),
                pl.BlockSpec((crows, SSD_WIDTH), lambda b, c, s: (rowblk(b, c, s), COL_Z // SSD_WIDTH)),
                pl.BlockSpec((crows, ROWS), lambda b, c, s: (rowblk(b, c, s), 0)),
                pl.BlockSpec((nsq, SSD_CONV - 1, CONV_DIM), lambda b, c, s: (b, 0, 0)),
                pl.BlockSpec((spp, SSD_WIDTH, SSD_STATE), lambda b, c, s: (seq(b, c, s), 0, 0)),
                full((SSD_CONV, CONV_DIM)), full((1, CONV_DIM)), full((1, ROWS)), full((1, ROWS)),
                full((1, SSD_WIDTH)), full((1, SSD_WIDTH)),
                full((ROWS, ROWS)), full((ROWS, ROWS)), full((ROWS, SSD_WIDTH))]
    out_specs = [pl.BlockSpec((crows, SSD_WIDTH), lambda b, c, s: (rowblk(b, c, s), 0)),
                 pl.BlockSpec((spp, SSD_WIDTH, SSD_STATE), lambda b, c, s: (seq(b, c, s), 0, 0))]
    out_shape = [jax.ShapeDtypeStruct((nseq * seqlen, SSD_WIDTH), BF16),
                 jax.ShapeDtypeStruct((nseq, SSD_WIDTH, SSD_STATE), F32)]
    args = [proj, proj, dt_raw, conv_prev, h0, cw, cb, dtb, alog, dexp, nw, mask, sel, expand]
    rows = [_cast_job(w, ngrp * ncstep * nstep) for w in cast_ws]
    hosted = bool(cast_ws) and all(r is not None for r in rows)
    if hosted:
        for w, r in zip(cast_ws, rows):
            ispec, ospec, oshape = _cast_specs(w, r, lambda b, c, s: (b * ncstep + c) * nstep + s)
            in_specs.append(ispec)
            out_specs.append(ospec)
            out_shape.append(oshape)
            args.append(w)
    outs = pl.pallas_call(
        functools.partial(_ssd_body, cps=cps, ncast=len(cast_ws) if hosted else 0, nsq=nsq, qs=qs, spp=spp),
        grid=(ngrp, ncstep, nstep),
        in_specs=in_specs,
        out_specs=out_specs,
        out_shape=out_shape,
        scratch_shapes=[pltpu.VMEM((spp, SSD_WIDTH, SSD_STATE), F32),
                        pltpu.VMEM((8 + ROWS, CONV_DIM), F32),
                        pltpu.VMEM((ROWS, CONV_DIM), F32),
                        pltpu.VMEM((ROWS, SSD_WIDTH), F32),
                        pltpu.VMEM((ROWS, SSD_WIDTH), F32),
                        pltpu.VMEM((ROWS, SSD_WIDTH), F32),
                        pltpu.VMEM((ROWS, SSD_WIDTH), F32),
                        pltpu.VMEM((ROWS, ROWS), F32),
                        pltpu.VMEM((ROWS, ROWS), F32),
                        pltpu.VMEM((ROWS, ROWS), F32)],
        compiler_params=_params(3),
        name="ssd",
    )(*args)
    return (*outs[:2], tuple(outs[2:]) if hosted else None)


def _ffn_gate_body(xw_ref, ssq_ref, wg_ref, wu_ref, cw_ref, cb_ref, *rest, period, tiles_per_seq, cast=False):
    if cast:
        wsrc_ref, a_ref, last_ref, wdst_ref, carry_scr = rest
        wdst_ref[...] = wsrc_ref[...].astype(BF16)
    elif period is None:
        a_ref, last_ref, carry_scr = rest
    else:
        prev_ref, a_ref, g_ref = rest
    r = lax.rsqrt(jnp.sum(ssq_ref[...], axis=0) * (1.0 / D_MODEL) + EPS)
    xw = xw_ref[...]
    tn = wg_ref.shape[1]
    col = pl.program_id(1) * tn + lax.broadcasted_iota(jnp.int32, (1, tn), 1)
    g = _dot(xw, jnp.where(col < D_FF, wg_ref[...], 0)) * r
    u = _dot(xw, jnp.where(col < D_FF, wu_ref[...], 0)) * r
    tm = g.shape[0]
    row = lax.broadcasted_iota(jnp.int32, g.shape, 0)
    if period is None:
        i, j = pl.program_id(0), pl.program_id(1)
        c8 = jnp.where(i % tiles_per_seq == 0, 0.0, carry_scr[j])
        carry_scr[j] = g[tm - 8:tm]
        last_ref[0] = g[tm - 2:tm]
        g_m1 = jnp.where(row == 0, c8[7:8], pltpu.roll(g, 1, 0))
        g_m2 = jnp.where(row == 0, c8[6:7], jnp.where(row == 1, c8[7:8], pltpu.roll(g, 2, 0)))
    else:
        g_ref[...] = g
        prev = prev_ref[...]
        t = row & (period - 1)
        g_m1 = jnp.where(t == 0, pltpu.roll(prev, tm - 1, 0), pltpu.roll(g, 1, 0))
        g_m2 = jnp.where(t < 2, prev, pltpu.roll(g, 2, 0))
    gc = cb_ref[...] + cw_ref[0:1] * g_m2
    gc = gc + cw_ref[1:2] * g_m1
    gc = gc + cw_ref[2:3] * g
    a_ref[...] = ((gc * _sigmoid(gc)) * u).astype(a_ref.dtype)


def _ffn_gate(xw, ssq, w_gate, w_up, cw, cb, *, nseq, seqlen, prev_rows=None, cast_w=None, tm=1024, tn=512):
    m = xw.shape[0]
    nj = D_FF_PAD // tn
    nparts = ssq.shape[0]
    wspec = pl.BlockSpec((D_MODEL, tn), lambda i, j: (0, j))
    cwspec = pl.BlockSpec((FFN_CONV, tn), lambda i, j: (0, j))
    cbspec = pl.BlockSpec((1, tn), lambda i, j: (0, j))
    if prev_rows is None:
        tm = min(tm, seqlen)
        tps = seqlen // tm
        in_specs = [pl.BlockSpec((tm, D_MODEL), lambda i, j: (i, 0)),
                    pl.BlockSpec((nparts, tm, 1), lambda i, j: (0, i, 0)), wspec, wspec, cwspec, cbspec]
        out_specs = [pl.BlockSpec((tm, tn), lambda i, j: (i, j)),
                     pl.BlockSpec((1, FFN_CONV - 1, tn), lambda i, j: (i, 0, j))]
        out_shape = [jax.ShapeDtypeStruct((m, D_FF_PAD), BF16),
                     jax.ShapeDtypeStruct((m // tm, FFN_CONV - 1, D_FF_PAD), F32)]
        scratch = [pltpu.VMEM((nj, 8, tn), F32)]
        args = (xw, ssq, w_gate, w_up, cw, cb)
        rows = None if cast_w is None else _cast_job(cast_w, (m // tm) * nj)
        if rows is not None:
            ispec, ospec, oshape = _cast_specs(cast_w, rows, lambda i, j: i * nj + j)
            in_specs.append(ispec)
            out_specs.append(ospec)
            out_shape.append(oshape)
            args = args + (cast_w,)
        body = functools.partial(_ffn_gate_body, period=None, tiles_per_seq=tps, cast=rows is not None)
    else:
        assert seqlen & (seqlen - 1) == 0
        tm = m
        in_specs = [pl.BlockSpec((tm, D_MODEL), lambda i, j: (i, 0)),
                    pl.BlockSpec((nparts, tm, 1), lambda i, j: (0, i, 0)), wspec, wspec, cwspec, cbspec,
                    pl.BlockSpec((tm, tn), lambda i, j: (i, j))]
        out_specs = [pl.BlockSpec((tm, tn), lambda i, j: (i, j)),
                     pl.BlockSpec((tm, tn), lambda i, j: (i, j))]
        out_shape = [jax.ShapeDtypeStruct((m, D_FF_PAD), BF16),
                     jax.ShapeDtypeStruct((m, D_FF_PAD), F32)]
        scratch = []
        args = (xw, ssq, w_gate, w_up, cw, cb, prev_rows)
        body = functools.partial(_ffn_gate_body, period=seqlen, tiles_per_seq=None)
    outs = pl.pallas_call(
        body,
        grid=(m // tm, nj),
        in_specs=in_specs,
        out_specs=out_specs,
        out_shape=out_shape,
        scratch_shapes=scratch,
        compiler_params=_params(2),
        name="ffn_gate",
    )(*args)
    return outs if len(outs) == 3 else (*outs, None)


def _down_body(a_ref, w_ref, res_ref, o_ref, *, chunk):
    @pl.when(pl.program_id(2) == 0)
    def _():
        o_ref[...] = res_ref[...]

    a = a_ref[...]
    tk = w_ref.shape[0]
    row = pl.program_id(2) * tk + lax.broadcasted_iota(jnp.int32, (tk, 1), 0)
    for c in range(o_ref.shape[1] // chunk):
        cs = slice(c * chunk, (c + 1) * chunk)
        o_ref[:, cs] += _dot(a, jnp.where(row < D_FF, w_ref[:, cs], 0))


def _down_proj(a, w_down, res, *, tm=1024, tn=2048, tk=1024):
    m, kdim = a.shape
    n = w_down.shape[1]
    tm = min(tm, m)
    return pl.pallas_call(
        functools.partial(_down_body, chunk=512),
        grid=(n // tn, m // tm, kdim // tk),
        in_specs=[pl.BlockSpec((tm, tk), lambda j, i, k: (i, k)),
                  pl.BlockSpec((tk, tn), lambda j, i, k: (k, j)),
                  pl.BlockSpec((tm, tn), lambda j, i, k: (i, j))],
        out_specs=pl.BlockSpec((tm, tn), lambda j, i, k: (i, j)),
        out_shape=jax.ShapeDtypeStruct((m, n), F32),
        compiler_params=_params(3),
        name="down_proj",
    )(a, w_down, res)


def _remember_bf16(bf16_w, wts, name, hosted):
    if name not in bf16_w:
        bf16_w[name] = hosted if hosted is not None else wts[name].astype(BF16)


def _layer_head(x, wts, bf16_w):
    h = _rmsnorm(x, wts["mix_norm_w"])
    if "w_in_t" in bf16_w:
        return _in_proj(h, wts["w_in_t"], bf16_w["w_in_t"])
    if x.shape[0] <= 1024:
        proj, dt_raw, bf16_w["w_in_t"] = _in_proj(h, wts["w_in_t"], keep_bf16=True)
        return proj, dt_raw
    return _in_proj(h, wts["w_in_t"])


def _layer(x, head, nseq, seqlen, wts, bf16_w, bias16, attn_state, ssm_state, conv_state, ffn_state):
    m = x.shape[0]
    todo = lambda name: None if name in bf16_w else wts[name]
    proj, dt_raw = head
    new_conv = proj.reshape(nseq, seqlen, PROJ_W)[:, seqlen - (SSD_CONV - 1):, COL_XBC:COL_XBC + CONV_DIM]

    qw = wts["q_norm_w"].reshape(1, HEAD_DIM)
    kw = wts["k_norm_w"].reshape(1, HEAD_DIM)
    if attn_state is None:
        names = [n for n in ("w_gate",) if n not in bf16_w]
        attn, new_k, new_v, hosted = _attn_prompt(proj, wts["attn_sinks"], qw, kw, bias16, nseq, seqlen,
                                                  cast_ws=tuple(wts[n] for n in names))
        for k, n in enumerate(names):
            _remember_bf16(bf16_w, wts, n, None if hosted is None else hosted[k])
        new_k = new_k.reshape(nseq, WINDOW, KV_HEADS, HEAD_DIM)
        new_v = new_v.reshape(nseq, WINDOW, KV_HEADS, HEAD_DIM)
        conv_prev = jnp.zeros((nseq, SSD_CONV - 1, CONV_DIM), F32)
        h0 = jnp.zeros((nseq, SSD_WIDTH, SSD_STATE), F32)
    else:
        win_k, win_v = attn_state
        attn, new_k, new_v = _attn_sample(proj, win_k, win_v, wts["attn_sinks"], qw, kw, bias16, nseq, seqlen)
        _remember_bf16(bf16_w, wts, "w_gate", None)
        conv_prev = conv_state
        h0 = ssm_state.reshape(nseq, SSD_WIDTH, SSD_STATE)

    names = [n for n in ("w_up", "w_out") if n not in bf16_w]
    y, h_t, hosted = _ssd(proj, dt_raw, conv_prev, h0, wts["ssd_conv_w"], wts["ssd_conv_b"], wts["ssd_dt_bias"],
                          wts["ssd_A_log"], wts["ssd_D"], wts["ssd_norm_w"], nseq, seqlen,
                          cast_ws=tuple(wts[n] for n in names))
    for k, n in enumerate(names):
        _remember_bf16(bf16_w, wts, n, None if hosted is None else hosted[k])
    h_t = h_t.reshape(nseq, SSD_HEADS, SSD_HEAD_DIM, SSD_STATE)

    x1, xw, ssq = _out_proj(attn, y, bf16_w["w_out"], x, wts["ffn_norm_w"])
    if ffn_state is None:
        a, last, hosted = _ffn_gate(xw, ssq, bf16_w["w_gate"], bf16_w["w_up"], wts["ffn_conv_w"],
                                    wts["ffn_conv_b"], nseq=nseq, seqlen=seqlen, cast_w=todo("w_down"))
        new_ffn = last.reshape(nseq, -1, FFN_CONV - 1, D_FF_PAD)[:, -1, :, :D_FF]
    else:
        prev_rows = jnp.pad(ffn_state, ((0, 0), (0, seqlen - (FFN_CONV - 1)), (0, D_FF_PAD - D_FF)))
        a, g, hosted = _ffn_gate(xw, ssq, bf16_w["w_gate"], bf16_w["w_up"], wts["ffn_conv_w"], wts["ffn_conv_b"],
                                 nseq=nseq, seqlen=seqlen, prev_rows=prev_rows.reshape(m, D_FF_PAD))
        new_ffn = g.reshape(nseq, seqlen, D_FF_PAD)[:, seqlen - (FFN_CONV - 1):, :D_FF]
    _remember_bf16(bf16_w, wts, "w_down", hosted)
    x2 = _down_proj(a, bf16_w["w_down"], x1)
    return x2, new_k, new_v, h_t, new_conv, new_ffn


def kernel(x_prompt, x_sample, state_attn_k, state_attn_v, state_ssm, state_ssd_conv, state_ffn_conv, rel_bias, mix_norm_w, w_in, q_norm_w, k_norm_w, attn_sinks, ssd_conv_w, ssd_conv_b, ssd_dt_bias, ssd_A_log, ssd_D, ssd_norm_w, w_out, ffn_norm_w, w_gate, w_up, ffn_conv_w, ffn_conv_b, w_down):
    depth = w_in.shape[0]
    bp, lp, _ = x_prompt.shape
    bs, ls, _ = x_sample.shape
    bias16 = _bias_table(rel_bias)
    yp = x_prompt.reshape(bp * lp, D_MODEL)
    ys = x_sample.reshape(bs * ls, D_MODEL)
    outs_p, outs_s = [], []
    pad_heads = lambda v: jnp.pad(v.reshape(1, SSD_HEADS), ((0, 0), (0, ROWS - SSD_HEADS)))
    pad_ff = lambda v: jnp.pad(v, ((0, 0), (0, D_FF_PAD - D_FF)))
    for l in range(depth):
        wts = dict(
            mix_norm_w=mix_norm_w[l],
            w_in_t=w_in[l].T,
            q_norm_w=q_norm_w[l], k_norm_w=k_norm_w[l], attn_sinks=attn_sinks[l],
            ssd_conv_w=ssd_conv_w[l], ssd_conv_b=ssd_conv_b[l].reshape(1, CONV_DIM),
            ssd_dt_bias=pad_heads(ssd_dt_bias[l]), ssd_A_log=pad_heads(ssd_A_log[l]),
            ssd_D=jnp.repeat(ssd_D[l], SSD_HEAD_DIM).reshape(1, SSD_WIDTH),
            ssd_norm_w=ssd_norm_w[l].reshape(1, SSD_WIDTH),
            w_out=w_out[l], ffn_norm_w=ffn_norm_w[l],
            w_gate=w_gate[l], w_up=w_up[l],
            ffn_conv_w=pad_ff(ffn_conv_w[l]), ffn_conv_b=pad_ff(ffn_conv_b[l].reshape(1, D_FF)),
            w_down=w_down[l],
        )
        bf16_w = {}
        head_s = _layer_head(ys, wts, bf16_w)
        head_p = _layer_head(yp, wts, bf16_w)
        yp, *sp = _layer(yp, head_p, bp, lp, wts, bf16_w, bias16, None, None, None, None)
        ys, *ss = _layer(ys, head_s, bs, ls, wts, bf16_w, bias16, (state_attn_k[l], state_attn_v[l]),
                         state_ssm[l], state_ssd_conv[l], state_ffn_conv[l])
        outs_p.append(sp)
        outs_s.append(ss)
    stack = lambda outs, i: jnp.stack([o[i] for o in outs])
    return (yp.reshape(bp, lp, D_MODEL), ys.reshape(bs, ls, D_MODEL),
            stack(outs_p, 0), stack(outs_p, 1), stack(outs_p, 2), stack(outs_p, 3), stack(outs_p, 4),
            stack(outs_s, 0), stack(outs_s, 1), stack(outs_s, 2), stack(outs_s, 3), stack(outs_s, 4))
```

```python
import functools
import math

import numpy as np
import jax
import jax.numpy as jnp
from jax import lax
from jax.experimental import pallas as pl
from jax.experimental.pallas import tpu as pltpu

F32 = jnp.float32
BF16 = jnp.bfloat16

D_MODEL = 4096
HEAD_DIM = 128
ATTN_HEADS = 16
KV_HEADS = 4
Q_PER_KV = 4
ATTN_WIDTH = 2048
WINDOW = 128
N_BUCKETS = 32
MAX_DISTANCE = 128
SSD_HEAD_DIM = 64
SSD_HEADS = 32
SSD_WIDTH = 2048
SSD_GROUPS = 8
SSD_STATE = 128
SSD_CONV = 4
CONV_DIM = 4096
KV_WIDTH = KV_HEADS * HEAD_DIM
D_FF = 11008
D_FF_PAD = 11264
FFN_CONV = 3
EPS = 1e-6
NEG = -1e30
SCALE = HEAD_DIM ** -0.5

ROWS = 128
GROUP_ROWS = 4 * SSD_HEAD_DIM
COL_Z = ATTN_WIDTH
COL_XBC = COL_Z + SSD_WIDTH
COL_K = COL_XBC + CONV_DIM
COL_V = COL_K + KV_WIDTH
PROJ_W = COL_V + KV_WIDTH
VMEM_LIMIT_MB = 56


def _params(n_axes, vmem_mb=VMEM_LIMIT_MB):
    return pltpu.CompilerParams(dimension_semantics=("arbitrary",) * n_axes,
                                vmem_limit_bytes=vmem_mb << 20)


def _dot(a, b):
    return jnp.dot(a, b, preferred_element_type=F32)


def _dot_nt(a, b):
    return lax.dot_general(a, b, (((1,), (1,)), ((), ())), preferred_element_type=F32)


def _dot_tn(a, b):
    return lax.dot_general(a, b, (((0,), (0,)), ((), ())), preferred_element_type=F32)


def _split3(x):
    a = x.astype(BF16)
    r = x - a.astype(F32)
    b = r.astype(BF16)
    c = (r - b.astype(F32)).astype(BF16)
    return a, b, c


def _dot3_l(x, w):
    a, b, c = _split3(x)
    return (_dot(a, w) + _dot(b, w)) + _dot(c, w)


def _dot3_r(w, x3):
    return (_dot(w, x3[0]) + _dot(w, x3[1])) + _dot(w, x3[2])


def _sigmoid(x):
    return 1.0 / (1.0 + jnp.exp(-x))


def _rms_rows(x, w):
    r = lax.rsqrt(jnp.mean(x * x, axis=-1, keepdims=True) + EPS)
    return (x * r) * w


def _rmsnorm_body(x_ref, w_ref, o_ref):
    o_ref[...] = _rms_rows(x_ref[...], w_ref[...]).astype(o_ref.dtype)


def _rmsnorm(x, w, rows=512):
    m, d = x.shape
    rows = min(rows, m)
    return pl.pallas_call(
        _rmsnorm_body,
        grid=(m // rows,),
        in_specs=[pl.BlockSpec((rows, d), lambda i: (i, 0)),
                  pl.BlockSpec((1, d), lambda i: (0, 0))],
        out_specs=pl.BlockSpec((rows, d), lambda i: (i, 0)),
        out_shape=jax.ShapeDtypeStruct((m, d), BF16),
        compiler_params=_params(1, 32),
        name="rmsnorm",
    )(x, w.reshape(1, d))


def _in_proj_body(a_ref, bt_ref, dtw_ref, o_ref, dt_ref, *wcopy_ref):
    a = a_ref[...]
    bt = bt_ref[...].astype(BF16)
    o_ref[...] = _dot_nt(a, bt)
    if wcopy_ref:
        wcopy_ref[0][...] = bt

    @pl.when(pl.program_id(1) == 0)
    def _():
        row = lax.broadcasted_iota(jnp.int32, (dtw_ref.shape[0], 1), 0)
        dtw = jnp.where(row < SSD_HEADS, dtw_ref[...], 0.0)
        dt_ref[...] = _dot_nt(a, dtw.astype(BF16))


def _in_proj(h, w_in_t, w_main=None, *, keep_bf16=False, tm=1024):
    m = h.shape[0]
    tm = min(tm, m)
    tn = 512 if w_main is None else 1024
    nq, nkv2 = ATTN_WIDTH // tn, 2 * KV_WIDTH // tn
    nzx = (SSD_WIDTH + CONV_DIM) // tn

    def src_block(j):
        return jnp.where(j < nq, j, jnp.where(j < nq + nzx, j + nkv2, j - nzx))

    wspec = pl.BlockSpec((tn, D_MODEL), lambda i, j: (src_block(j), 0))
    out_specs = [pl.BlockSpec((tm, tn), lambda i, j: (i, j)),
                 pl.BlockSpec((tm, ROWS), lambda i, j: (i, 0))]
    out_shape = [jax.ShapeDtypeStruct((m, PROJ_W), F32),
                 jax.ShapeDtypeStruct((m, ROWS), F32)]
    if keep_bf16:
        assert w_main is None and m == tm
        out_specs.append(wspec)
        out_shape.append(jax.ShapeDtypeStruct((PROJ_W, D_MODEL), BF16))
    return pl.pallas_call(
        _in_proj_body,
        grid=(m // tm, PROJ_W // tn),
        in_specs=[pl.BlockSpec((tm, D_MODEL), lambda i, j: (i, 0)),
                  wspec,
                  pl.BlockSpec((ROWS, D_MODEL), lambda i, j: (PROJ_W // ROWS, 0))],
        out_specs=out_specs,
        out_shape=out_shape,
        compiler_params=_params(2),
        name="in_proj",
    )(h, w_in_t if w_main is None else w_main, w_in_t)


def _out_proj_body(a1_ref, a2_ref, b1_ref, b2_ref, res_ref, nw_ref, o_ref, xw_ref, ssq_ref):
    acc = (_dot(a1_ref[...].astype(BF16), b1_ref[...].astype(BF16))
           + _dot(a2_ref[...].astype(BF16), b2_ref[...].astype(BF16)))
    x1 = res_ref[...] + acc
    o_ref[...] = x1
    xw_ref[...] = (x1 * nw_ref[...]).astype(xw_ref.dtype)
    ssq_ref[0] = jnp.sum(x1 * x1, axis=-1, keepdims=True)


def _out_proj(attn, y, w_out, res, norm_w, *, tm=1024, tn=512):
    m = attn.shape[0]
    n = w_out.shape[1]
    tm = min(tm, m)
    return pl.pallas_call(
        _out_proj_body,
        grid=(m // tm, n // tn),
        in_specs=[pl.BlockSpec((tm, ATTN_WIDTH), lambda i, j: (i, 0)),
                  pl.BlockSpec((tm, SSD_WIDTH), lambda i, j: (i, 0)),
                  pl.BlockSpec((ATTN_WIDTH, tn), lambda i, j: (0, j)),
                  pl.BlockSpec((SSD_WIDTH, tn), lambda i, j: (1, j)),
                  pl.BlockSpec((tm, tn), lambda i, j: (i, j)),
                  pl.BlockSpec((1, tn), lambda i, j: (0, j))],
        out_specs=[pl.BlockSpec((tm, tn), lambda i, j: (i, j)),
                   pl.BlockSpec((tm, tn), lambda i, j: (i, j)),
                   pl.BlockSpec((1, tm, 1), lambda i, j: (j, i, 0))],
        out_shape=[jax.ShapeDtypeStruct((m, n), F32),
                   jax.ShapeDtypeStruct((m, n), BF16),
                   jax.ShapeDtypeStruct((n // tn, m, 1), F32)],
        compiler_params=_params(2),
        name="out_proj",
    )(attn, y, w_out, w_out, res, norm_w.reshape(1, n))


def _bucket_table():
    i = np.arange(ROWS)[:, None]
    j = np.arange(2 * ROWS)[None, :]
    dist = i + ROWS - j
    valid = (dist >= 0) & (dist < WINDOW)
    n = np.maximum(dist, 0)
    max_exact = N_BUCKETS // 2

    def large(dtype):
        nf = np.maximum(n, 1).astype(dtype)
        v = np.log(nf / dtype(max_exact)) / dtype(math.log(MAX_DISTANCE / max_exact)) * dtype(N_BUCKETS - max_exact)
        return np.minimum(max_exact + v.astype(np.int32), N_BUCKETS - 1)

    assert (large(np.float32) == large(np.float64)).all()
    bucket = np.where(n < max_exact, n, large(np.float32))
    return np.where(valid, bucket, -1).astype(np.int32)


def _bias_body(rb_ref, bkt_ref, o_ref):
    h = pl.program_id(0)
    bkt = bkt_ref[...]
    acc = jnp.full(bkt.shape, NEG, F32)
    for b in range(N_BUCKETS):
        acc = jnp.where(bkt == b, rb_ref[b, h], acc)
    o_ref[0] = acc


def _bias_table(rel_bias):
    return pl.pallas_call(
        _bias_body,
        grid=(ATTN_HEADS,),
        in_specs=[pl.BlockSpec(memory_space=pltpu.SMEM),
                  pl.BlockSpec((ROWS, 2 * ROWS), lambda h: (0, 0))],
        out_specs=pl.BlockSpec((1, ROWS, 2 * ROWS), lambda h: (h, 0, 0)),
        out_shape=jax.ShapeDtypeStruct((ATTN_HEADS, ROWS, 2 * ROWS), F32),
        compiler_params=_params(1, 16),
        name="bias_table",
    )(rel_bias, jnp.asarray(_bucket_table()))


def _cast_job(w, nsteps):
    rows = -(-w.shape[0] // nsteps)
    rows = -(-rows // 16) * 16
    return rows if rows * w.shape[1] * 4 <= (8 << 20) else None


def _cast_specs(w, rows, step_of):
    nblk = -(-w.shape[0] // rows)
    spec = pl.BlockSpec((rows, w.shape[1]), lambda *g: (jnp.minimum(step_of(*g), nblk - 1), 0))
    return spec, spec, jax.ShapeDtypeStruct(w.shape, BF16)


def _softmax_sink(s_scr, bias, sink, p_scr):
    sc = s_scr[...] * SCALE + bias
    m = jnp.maximum(jnp.max(sc, axis=-1, keepdims=True), sink)
    p = jnp.exp(sc - m)
    den = jnp.sum(p, axis=-1, keepdims=True) + jnp.exp(sink - m)
    p_scr[...] = (p * (1.0 / den)).astype(BF16)


def _attn_prompt_body(q_ref, kc_ref, kp_ref, vc_ref, vp_ref, qw_ref, kw_ref, bias_ref, sink_ref, *rest, ncast, bps):
    o_ref, pk_ref, pv_ref = rest[ncast:ncast + 3]
    for src_ref, dst_ref in zip(rest[:ncast], rest[ncast + 3:2 * ncast + 3]):
        dst_ref[...] = src_ref[...].astype(BF16)
    qn_scr, kn_scr, s_scr, p_scr = rest[2 * ncast + 3:]
    qw = qw_ref[...]
    kw = kw_ref[...]
    grows = Q_PER_KV * ROWS
    for h in range(ATTN_HEADS):
        hs = slice(h * HEAD_DIM, (h + 1) * HEAD_DIM)
        qn_scr[:, hs] = _rms_rows(q_ref[:, hs], qw)
    for kv in range(KV_HEADS):
        sl = slice(kv * HEAD_DIM, (kv + 1) * HEAD_DIM)
        kn_scr[0:ROWS, sl] = _rms_rows(kp_ref[:, sl], kw)
        kn_scr[ROWS:(bps + 1) * ROWS, sl] = _rms_rows(kc_ref[:, sl], kw)
    pk_ref[0] = kn_scr[bps * ROWS:(bps + 1) * ROWS, :]
    pv_ref[0] = vc_ref[(bps - 1) * ROWS:bps * ROWS, :]

    col = lax.broadcasted_iota(jnp.int32, (1, 2 * ROWS), 1)
    for t in range(bps):
        qrows = slice(t * ROWS, (t + 1) * ROWS)
        krows = slice(t * ROWS, (t + 2) * ROWS)
        for kv in range(KV_HEADS):
            sl = slice(kv * HEAD_DIM, (kv + 1) * HEAD_DIM)
            qs = jnp.concatenate([qn_scr[qrows, (kv * Q_PER_KV + g) * HEAD_DIM:(kv * Q_PER_KV + g + 1) * HEAD_DIM]
                                  for g in range(Q_PER_KV)], axis=0).astype(BF16)
            s_scr[kv * grows:(kv + 1) * grows, :] = _dot_nt(qs, kn_scr[krows, sl].astype(BF16))

        bias = bias_ref[...]
        if t == 0:
            bias = bias + jnp.where(jnp.logical_and(pl.program_id(1) == 0, col < ROWS), NEG, 0.0)
        _softmax_sink(s_scr, bias, sink_ref[...], p_scr)

        for kv in range(KV_HEADS):
            sl = slice(kv * HEAD_DIM, (kv + 1) * HEAD_DIM)
            vprev = vp_ref[:, sl] if t == 0 else vc_ref[(t - 1) * ROWS:t * ROWS, sl]
            vcat = jnp.concatenate([vprev, vc_ref[qrows, sl]], axis=0).astype(BF16)
            o = _dot(p_scr[kv * grows:(kv + 1) * grows, :], vcat)
            for g in range(Q_PER_KV):
                h = kv * Q_PER_KV + g
                o_ref[qrows, h * HEAD_DIM:(h + 1) * HEAD_DIM] = o[g * ROWS:(g + 1) * ROWS].astype(o_ref.dtype)


def _attn_prompt(proj, sinks, qw, kw, bias16, nseq, seqlen, cast_ws=()):
    nb = seqlen // ROWS
    bps = 2 if nb % 2 == 0 else 1
    nstep = nb // bps
    kblk, vblk = COL_K // KV_WIDTH, COL_V // KV_WIDTH
    srows = ATTN_HEADS * ROWS
    cur = lambda b, i: b * nstep + i
    prev = lambda b, i: b * nb + jnp.maximum(i * bps - 1, 0)
    const = lambda b, i: (0, 0)
    in_specs = [pl.BlockSpec((bps * ROWS, ATTN_WIDTH), lambda b, i: (cur(b, i), 0)),
                pl.BlockSpec((bps * ROWS, KV_WIDTH), lambda b, i: (cur(b, i), kblk)),
                pl.BlockSpec((ROWS, KV_WIDTH), lambda b, i: (prev(b, i), kblk)),
                pl.BlockSpec((bps * ROWS, KV_WIDTH), lambda b, i: (cur(b, i), vblk)),
                pl.BlockSpec((ROWS, KV_WIDTH), lambda b, i: (prev(b, i), vblk)),
                pl.BlockSpec((1, HEAD_DIM), const),
                pl.BlockSpec((1, HEAD_DIM), const),
                pl.BlockSpec((srows, 2 * ROWS), const),
                pl.BlockSpec((srows, 1), const)]
    out_specs = [pl.BlockSpec((bps * ROWS, ATTN_WIDTH), lambda b, i: (cur(b, i), 0)),
                 pl.BlockSpec((1, ROWS, KV_WIDTH), lambda b, i: (b, 0, 0)),
                 pl.BlockSpec((1, ROWS, KV_WIDTH), lambda b, i: (b, 0, 0))]
    out_shape = [jax.ShapeDtypeStruct((nseq * seqlen, ATTN_WIDTH), BF16),
                 jax.ShapeDtypeStruct((nseq, ROWS, KV_WIDTH), F32),
                 jax.ShapeDtypeStruct((nseq, ROWS, KV_WIDTH), F32)]
    args = [proj, proj, proj, proj, proj, qw, kw, bias16.reshape(srows, 2 * ROWS),
            jnp.repeat(sinks, ROWS).reshape(srows, 1)]
    rows = [_cast_job(w, nseq * nstep) for w in cast_ws]
    hosted = bool(cast_ws) and all(r is not None for r in rows)
    if hosted:
        for w, r in zip(cast_ws, rows):
            ispec, ospec, oshape = _cast_specs(w, r, cur)
            in_specs.append(ispec)
            out_specs.append(ospec)
            out_shape.append(oshape)
            args.append(w)
    ncast = len(cast_ws) if hosted else 0
    outs = pl.pallas_call(
        functools.partial(_attn_prompt_body, ncast=ncast, bps=bps),
        grid=(nseq, nstep),
        in_specs=in_specs,
        out_specs=out_specs,
        out_shape=out_shape,
        scratch_shapes=[pltpu.VMEM((bps * ROWS, ATTN_WIDTH), F32),
                        pltpu.VMEM(((bps + 1) * ROWS, KV_WIDTH), F32),
                        pltpu.VMEM((srows, 2 * ROWS), F32),
                        pltpu.VMEM((srows, 2 * ROWS), BF16)],
        compiler_params=_params(2, 48),
        name="attn_prompt",
    )(*args)
    return (*outs[:3], tuple(outs[3:]) if hosted else None)


def _attn_sample_body(q_ref, kn_ref, vn_ref, wk_ref, wv_ref, qw_ref, kw_ref, bias_ref, sink_ref,
                      o_ref, sk_ref, sv_ref, qn_scr, kn_scr, s_scr, p_scr, *, nsq, ntok):
    qw = qw_ref[...]
    kw = kw_ref[...]
    grows = Q_PER_KV * ntok
    keep = ROWS - ntok
    for h in range(ATTN_HEADS):
        hs = slice(h * HEAD_DIM, (h + 1) * HEAD_DIM)
        qn_scr[:, hs] = _rms_rows(q_ref[:, hs], qw)
    for kv in range(KV_HEADS):
        sl = slice(kv * HEAD_DIM, (kv + 1) * HEAD_DIM)
        kn_scr[:, sl] = _rms_rows(kn_ref[:, sl], kw)
    for s in range(nsq):
        rows = slice(s * ntok, (s + 1) * ntok)
        sk_ref[s, 0:keep] = wk_ref[s, ntok:ROWS]
        sv_ref[s, 0:keep] = wv_ref[s, ntok:ROWS]
        for kv in range(KV_HEADS):
            sl = slice(kv * HEAD_DIM, (kv + 1) * HEAD_DIM)
            sk_ref[s, keep:ROWS, kv, :] = kn_scr[rows, sl]
            sv_ref[s, keep:ROWS, kv, :] = vn_ref[rows, sl]

    pad = jnp.zeros((keep, HEAD_DIM), F32)
    for s in range(nsq):
        rows = slice(s * ntok, (s + 1) * ntok)
        for kv in range(KV_HEADS):
            sl = slice(kv * HEAD_DIM, (kv + 1) * HEAD_DIM)
            grp = s * KV_HEADS + kv
            qs = jnp.concatenate([qn_scr[rows, (kv * Q_PER_KV + g) * HEAD_DIM:(kv * Q_PER_KV + g + 1) * HEAD_DIM]
                                  for g in range(Q_PER_KV)], axis=0).astype(BF16)
            kcat = jnp.concatenate([wk_ref[s, :, kv, :], kn_scr[rows, sl], pad], axis=0).astype(BF16)
            s_scr[grp * grows:(grp + 1) * grows, :] = _dot_nt(qs, kcat)

    _softmax_sink(s_scr, bias_ref[...], sink_ref[...], p_scr)

    for s0 in range(0, nsq, 2):
        for kv in range(KV_HEADS):
            sl = slice(kv * HEAD_DIM, (kv + 1) * HEAD_DIM)
            outs = []
            for s in (s0, s0 + 1):
                rows = slice(s * ntok, (s + 1) * ntok)
                grp = s * KV_HEADS + kv
                vcat = jnp.concatenate([wv_ref[s, :, kv, :], vn_ref[rows, sl], pad], axis=0).astype(BF16)
                outs.append(_dot(p_scr[grp * grows:(grp + 1) * grows, :], vcat))
            for g in range(Q_PER_KV):
                h = kv * Q_PER_KV + g
                pair = jnp.concatenate([o[g * ntok:(g + 1) * ntok] for o in outs], axis=0)
                o_ref[s0 * ntok:(s0 + 2) * ntok, h * HEAD_DIM:(h + 1) * HEAD_DIM] = pair.astype(o_ref.dtype)


def _attn_sample(proj, win_k, win_v, sinks, qw, kw, bias16, nseq, ntok, nsq=8):
    kblk, vblk = COL_K // KV_WIDTH, COL_V // KV_WIDTH
    rows = nsq * ntok
    srows = nsq * ATTN_HEADS * ntok
    bias = jnp.tile(bias16[:, :ntok, :].reshape(ATTN_HEADS * ntok, 2 * ROWS), (nsq, 1))
    sink = jnp.tile(jnp.repeat(sinks, ntok), nsq).reshape(srows, 1)
    return pl.pallas_call(
        functools.partial(_attn_sample_body, nsq=nsq, ntok=ntok),
        grid=(nseq // nsq,),
        in_specs=[pl.BlockSpec((rows, ATTN_WIDTH), lambda i: (i, 0)),
                  pl.BlockSpec((rows, KV_WIDTH), lambda i: (i, kblk)),
                  pl.BlockSpec((rows, KV_WIDTH), lambda i: (i, vblk)),
                  pl.BlockSpec((nsq, ROWS, KV_HEADS, HEAD_DIM), lambda i: (i, 0, 0, 0)),
                  pl.BlockSpec((nsq, ROWS, KV_HEADS, HEAD_DIM), lambda i: (i, 0, 0, 0)),
                  pl.BlockSpec((1, HEAD_DIM), lambda i: (0, 0)),
                  pl.BlockSpec((1, HEAD_DIM), lambda i: (0, 0)),
                  pl.BlockSpec((srows, 2 * ROWS), lambda i: (0, 0)),
                  pl.BlockSpec((srows, 1), lambda i: (0, 0))],
        out_specs=[pl.BlockSpec((rows, ATTN_WIDTH), lambda i: (i, 0)),
                   pl.BlockSpec((nsq, ROWS, KV_HEADS, HEAD_DIM), lambda i: (i, 0, 0, 0)),
                   pl.BlockSpec((nsq, ROWS, KV_HEADS, HEAD_DIM), lambda i: (i, 0, 0, 0))],
        out_shape=[jax.ShapeDtypeStruct((nseq * ntok, ATTN_WIDTH), BF16),
                   jax.ShapeDtypeStruct((nseq, ROWS, KV_HEADS, HEAD_DIM), F32),
                   jax.ShapeDtypeStruct((nseq, ROWS, KV_HEADS, HEAD_DIM), F32)],
        scratch_shapes=[pltpu.VMEM((rows, ATTN_WIDTH), F32),
                        pltpu.VMEM((rows, KV_WIDTH), F32),
                        pltpu.VMEM((srows, 2 * ROWS), F32),
                        pltpu.VMEM((srows, 2 * ROWS), BF16)],
        compiler_params=_params(1, 32),
        name="attn_sample",
    )(proj, proj, proj, win_k, win_v, qw, kw, bias, sink)


def _ssd_constants(qs):
    r = np.arange(ROWS)
    same = (r[:, None] // qs) == (r[None, :] // qs)
    mask = same & (r[None, :] <= r[:, None])
    sel = r[None, :] == ((r[:, None] // qs) * qs + qs - 1)
    expand = np.zeros((ROWS, SSD_WIDTH), np.float32)
    for h in range(SSD_HEADS):
        expand[h, h * SSD_HEAD_DIM:(h + 1) * SSD_HEAD_DIM] = 1.0
    to_bf = lambda a: jnp.asarray(a.astype(np.float32), BF16)
    return to_bf(mask), to_bf(sel), to_bf(expand)


def _ssd_body(xbc_ref, z_ref, dt_ref, *rest, cps, ncast, **static):
    consts, rest = rest[:11], rest[11:]
    y_ref, ht_ref = rest[ncast:ncast + 2]
    for src_ref, dst_ref in zip(rest[:ncast], rest[ncast + 2:2 * ncast + 2]):
        dst_ref[...] = src_ref[...].astype(BF16)
    scratch = rest[2 * ncast + 2:]
    for ci in range(cps):
        rows = pl.ds(ci * ROWS, ROWS)
        first = pl.program_id(1) == 0 if ci == 0 else None
        _ssd_chunk(first, xbc_ref.at[rows, :], z_ref.at[rows, :], dt_ref.at[rows, :], *consts,
                   y_ref.at[rows, :], ht_ref, *scratch, **static)


def _ssd_chunk(first, xbc_ref, z_ref, dt_ref, cprev_ref, h0_ref, cw_ref, cb_ref, dtb_ref, alog_ref, dexp_ref,
               nw_ref, mask_ref, sel_ref, e_ref, y_ref, ht_ref,
               h_scr, cbuf, xc_scr, xdt_scr, xdd_scr, eacs_scr, yacc_scr, acs_scr, acst_scr, aclt_scr,
               *, nsq, qs, spp):
    s = pl.program_id(2)
    if first is not None:
        @pl.when(first)
        def _():
            h_scr[...] = h0_ref[...]

        @pl.when(jnp.logical_and(first, s == 0))
        def _():
            cbuf[0:8, :] = jnp.zeros((8, CONV_DIM), F32)
            if nsq == 1:
                cbuf[5:8, :] = cprev_ref[0]

    @pl.when(s == 0)
    def _chunk_phase():
        ncc = 8 if qs == ROWS else 1
        cwid = CONV_DIM // ncc
        nrow = 8 + qs
        for sq in range(nsq):
            for cc in range(ncc):
                cs = slice(cc * cwid, (cc + 1) * cwid)
                if nsq > 1:
                    cbuf[5:8, cs] = cprev_ref[sq, :, cs]
                cbuf[8:nrow, cs] = xbc_ref[sq * qs:(sq + 1) * qs, cs]
                xa = cbuf[0:nrow, cs]
                acc = cb_ref[:, cs] + cw_ref[SSD_CONV - 1:SSD_CONV, cs] * xa[8:nrow]
                for t in range(SSD_CONV - 1):
                    acc = acc + cw_ref[t:t + 1, cs] * pltpu.roll(xa, nrow - (5 + t), 0)[0:qs]
                xc_scr[sq * qs:(sq + 1) * qs, cs] = acc * _sigmoid(acc)
                if nsq == 1:
                    cbuf[5:8, cs] = xbc_ref[qs - 3:qs, cs]

        x = dt_ref[...] + dtb_ref[...]
        dt = jnp.maximum(x, 0.0) + jnp.log1p(jnp.exp(-jnp.abs(x)))
        da = dt * (-jnp.exp(alog_ref[...]))
        mask_bf = mask_ref[...]
        acs = _dot3_r(mask_bf, _split3(da))
        acl = _dot3_r(sel_ref[...], _split3(acs))
        acs_scr[...] = acs
        acst_scr[...] = acs.T
        aclt_scr[...] = acl.T
        e = e_ref[...]
        xs = xc_scr[:, 0:SSD_WIDTH]
        xdt_scr[...] = xs * _dot3_l(dt, e)
        xdd_scr[...] = xs * _dot3_l(dt * jnp.exp(acl - acs), e)
        eacs_scr[...] = _dot3_l(jnp.exp(acs), e)

        maskb = mask_bf > 0.5
        lane = lax.broadcasted_iota(jnp.int32, (ROWS, 2 * SSD_HEAD_DIM), 1)
        for g in range(SSD_GROUPS):
            bg = xc_scr[:, SSD_WIDTH + g * SSD_STATE:SSD_WIDTH + (g + 1) * SSD_STATE].astype(BF16)
            cg = xc_scr[:, SSD_WIDTH + (SSD_GROUPS + g) * SSD_STATE:
                        SSD_WIDTH + (SSD_GROUPS + g + 1) * SSD_STATE].astype(BF16)
            cb = _dot_nt(cg, bg)
            for pp in range(2):
                ms = []
                for j in range(2):
                    h = 4 * g + 2 * pp + j
                    seg = acs_scr[:, h:h + 1] - acst_scr[h:h + 1, :]
                    ms.append((cb * jnp.exp(jnp.where(maskb, seg, NEG))).astype(BF16))
                c0 = (4 * g + 2 * pp) * SSD_HEAD_DIM
                cols = slice(c0, c0 + 2 * SSD_HEAD_DIM)
                xp = xdt_scr[:, cols]
                rhs = jnp.concatenate([jnp.where(lane < SSD_HEAD_DIM, xp, 0.0),
                                       jnp.where(lane >= SSD_HEAD_DIM, xp, 0.0)], axis=0).astype(BF16)
                yd = _dot(jnp.concatenate(ms, axis=1), rhs)
                yacc_scr[:, cols] = yd + dexp_ref[:, cols] * xc_scr[:, cols]

    tlane = lax.broadcasted_iota(jnp.int32, (ROWS, ROWS), 1)
    ridx = lax.broadcasted_iota(jnp.int32, (ROWS, 1), 0)
    for u in range(spp):
        if nsq == 1:
            r0 = 0
            rows = slice(0, qs)
        else:
            r0 = pl.multiple_of((s * spp + u) * qs, qs)
            rows = pl.ds(r0, qs)
            rowmask = jnp.logical_and(ridx >= r0, ridx < r0 + qs)
        xcol = jnp.sum(jnp.where(tlane == r0, aclt_scr[...], 0.0), axis=1, keepdims=True)
        dec = jnp.exp(jnp.broadcast_to(xcol, (ROWS, ROWS)))
        c_rows = xc_scr[rows, SSD_WIDTH + SSD_GROUPS * SSD_STATE:CONV_DIM].astype(BF16)
        yo, st = [], []
        for g in range(SSD_GROUPS):
            grows = slice(g * GROUP_ROWS, (g + 1) * GROUP_ROWS)
            yo.append(_dot_nt(c_rows[:, g * SSD_STATE:(g + 1) * SSD_STATE], h_scr[u, grows, :].astype(BF16)))
            bg = xc_scr[:, SSD_WIDTH + g * SSD_STATE:SSD_WIDTH + (g + 1) * SSD_STATE]
            if nsq > 1:
                bg = jnp.where(rowmask, bg, 0.0)
            st.append(_dot_tn(xdd_scr[:, grows].astype(BF16), bg.astype(BF16)))
        yacc_scr[rows, :] = yacc_scr[rows, :] + jnp.concatenate(yo, axis=1) * eacs_scr[rows, :]
        for h in range(SSD_HEADS):
            hrows = slice(h * SSD_HEAD_DIM, (h + 1) * SSD_HEAD_DIM)
            g, off = divmod(h * SSD_HEAD_DIM, GROUP_ROWS)
            h_new = dec[h:h + 1, :] * h_scr[u, hrows, :] + st[g][off:off + SSD_HEAD_DIM]
            h_scr[u, hrows, :] = h_new
            ht_ref[u, hrows, :] = h_new

    @pl.when(s == nsq // spp - 1)
    def _finish():
        z = z_ref[...]
        yg = yacc_scr[...] * (z * _sigmoid(z))
        y_ref[...] = _rms_rows(yg, nw_ref[...]).astype(y_ref.dtype)


def _ssd(proj, dt_raw, conv_prev, h0, cw, cb, dtb, alog, dexp, nw, nseq, seqlen, cast_ws=()):
    if seqlen % ROWS == 0:
        qs, nsq, nchunk, ngrp = ROWS, 1, seqlen // ROWS, nseq
    else:
        qs, nsq, nchunk = seqlen, ROWS // seqlen, 1
        ngrp = nseq // nsq
    spp = math.gcd(nsq, 4)
    nstep = nsq // spp
    cps = 2 if nchunk % 2 == 0 else 1
    ncstep = nchunk // cps
    crows = cps * ROWS
    mask, sel, expand = _ssd_constants(qs)
    rowblk = lambda b, c, s: b * ncstep + c
    seq = lambda b, c, s: b * nstep + s
    const2 = lambda b, c, s: (0, 0)
    full = lambda shape: pl.BlockSpec(shape, const2)
    in_specs = [pl.BlockSpec((crows, CONV_DIM), lambda b, c, s: (rowblk(b, c, s), COL_XBC // CONV_DIM)),
                pl.BlockSpec((crows, SSD_WIDTH), lambda b, c, s: (rowblk(b, c, s), COL_Z // SSD_WIDTH)),
                pl.BlockSpec((crows, ROWS), lambda b, c, s: (rowblk(b, c, s), 0)),
                pl.BlockSpec((nsq, SSD_CONV - 1, CONV_DIM), lambda b, c, s: (b, 0, 0)),
                pl.BlockSpec((spp, SSD_WIDTH, SSD_STATE), lambda b, c, s: (seq(b, c, s), 0, 0)),
                full((SSD_CONV, CONV_DIM)), full((1, CONV_DIM)), full((1, ROWS)), full((1, ROWS)),
                full((1, SSD_WIDTH)), full((1, SSD_WIDTH)),
                full((ROWS, ROWS)), full((ROWS, ROWS)), full((ROWS, SSD_WIDTH))]
    out_specs = [pl.BlockSpec((crows, SSD_WIDTH), lambda b, c, s: (rowblk(b, c, s), 0)),
                 pl.BlockSpec((spp, SSD_WIDTH, SSD_STATE), lambda b, c, s: (seq(b, c, s), 0, 0))]
    out_shape = [jax.ShapeDtypeStruct((nseq * seqlen, SSD_WIDTH), BF16),
                 jax.ShapeDtypeStruct((nseq, SSD_WIDTH, SSD_STATE), F32)]
    args = [proj, proj, dt_raw, conv_prev, h0, cw, cb, dtb, alog, dexp, nw, mask, sel, expand]
    rows = [_cast_job(w, ngrp * ncstep * nstep) for w in cast_ws]
    hosted = bool(cast_ws) and all(r is not None for r in rows)
    if hosted:
        for w, r in zip(cast_ws, rows):
            ispec, ospec, oshape = _cast_specs(w, r, lambda b, c, s: (b * ncstep + c) * nstep + s)
            in_specs.append(ispec)
            out_specs.append(ospec)
            out_shape.append(oshape)
            args.append(w)
    outs = pl.pallas_call(
        functools.partial(_ssd_body, cps=cps, ncast=len(cast_ws) if hosted else 0, nsq=nsq, qs=qs, spp=spp),
        grid=(ngrp, ncstep, nstep),
        in_specs=in_specs,
        out_specs=out_specs,
        out_shape=out_shape,
        scratch_shapes=[pltpu.VMEM((spp, SSD_WIDTH, SSD_STATE), F32),
                        pltpu.VMEM((8 + ROWS, CONV_DIM), F32),
                        pltpu.VMEM((ROWS, CONV_DIM), F32),
                        pltpu.VMEM((ROWS, SSD_WIDTH), F32),
                        pltpu.VMEM((ROWS, SSD_WIDTH), F32),
                        pltpu.VMEM((ROWS, SSD_WIDTH), F32),
                        pltpu.VMEM((ROWS, SSD_WIDTH), F32),
                        pltpu.VMEM((ROWS, ROWS), F32),
                        pltpu.VMEM((ROWS, ROWS), F32),
                        pltpu.VMEM((ROWS, ROWS), F32)],
        compiler_params=_params(3),
        name="ssd",
    )(*args)
    return (*outs[:2], tuple(outs[2:]) if hosted else None)


def _ffn_gate_body(xw_ref, ssq_ref, wg_ref, wu_ref, cw_ref, cb_ref, *rest, period, tiles_per_seq, cast=False):
    if cast:
        wsrc_ref, a_ref, last_ref, wdst_ref, carry_scr = rest
        wdst_ref[...] = wsrc_ref[...].astype(BF16)
    elif period is None:
        a_ref, last_ref, carry_scr = rest
    else:
        prev_ref, a_ref, g_ref = rest
    r = lax.rsqrt(jnp.sum(ssq_ref[...], axis=0) * (1.0 / D_MODEL) + EPS)
    xw = xw_ref[...]
    tn = wg_ref.shape[1]
    col = pl.program_id(1) * tn + lax.broadcasted_iota(jnp.int32, (1, tn), 1)
    g = _dot(xw, jnp.where(col < D_FF, wg_ref[...], 0)) * r
    u = _dot(xw, jnp.where(col < D_FF, wu_ref[...], 0)) * r
    tm = g.shape[0]
    row = lax.broadcasted_iota(jnp.int32, g.shape, 0)
    if period is None:
        i, j = pl.program_id(0), pl.program_id(1)
        c8 = jnp.where(i % tiles_per_seq == 0, 0.0, carry_scr[j])
        carry_scr[j] = g[tm - 8:tm]
        last_ref[0] = g[tm - 2:tm]
        g_m1 = jnp.where(row == 0, c8[7:8], pltpu.roll(g, 1, 0))
        g_m2 = jnp.where(row == 0, c8[6:7], jnp.where(row == 1, c8[7:8], pltpu.roll(g, 2, 0)))
    else:
        g_ref[...] = g
        prev = prev_ref[...]
        t = row & (period - 1)
        g_m1 = jnp.where(t == 0, pltpu.roll(prev, tm - 1, 0), pltpu.roll(g, 1, 0))
        g_m2 = jnp.where(t < 2, prev, pltpu.roll(g, 2, 0))
    gc = cb_ref[...] + cw_ref[0:1] * g_m2
    gc = gc + cw_ref[1:2] * g_m1
    gc = gc + cw_ref[2:3] * g
    a_ref[...] = ((gc * _sigmoid(gc)) * u).astype(a_ref.dtype)


def _ffn_gate(xw, ssq, w_gate, w_up, cw, cb, *, nseq, seqlen, prev_rows=None, cast_w=None, tm=1024, tn=512):
    m = xw.shape[0]
    nj = D_FF_PAD // tn
    nparts = ssq.shape[0]
    wspec = pl.BlockSpec((D_MODEL, tn), lambda i, j: (0, j))
    cwspec = pl.BlockSpec((FFN_CONV, tn), lambda i, j: (0, j))
    cbspec = pl.BlockSpec((1, tn), lambda i, j: (0, j))
    if prev_rows is None:
        tm = min(tm, seqlen)
        tps = seqlen // tm
        in_specs = [pl.BlockSpec((tm, D_MODEL), lambda i, j: (i, 0)),
                    pl.BlockSpec((nparts, tm, 1), lambda i, j: (0, i, 0)), wspec, wspec, cwspec, cbspec]
        out_specs = [pl.BlockSpec((tm, tn), lambda i, j: (i, j)),
                     pl.BlockSpec((1, FFN_CONV - 1, tn), lambda i, j: (i, 0, j))]
        out_shape = [jax.ShapeDtypeStruct((m, D_FF_PAD), BF16),
                     jax.ShapeDtypeStruct((m // tm, FFN_CONV - 1, D_FF_PAD), F32)]
        scratch = [pltpu.VMEM((nj, 8, tn), F32)]
        args = (xw, ssq, w_gate, w_up, cw, cb)
        rows = None if cast_w is None else _cast_job(cast_w, (m // tm) * nj)
        if rows is not None:
            ispec, ospec, oshape = _cast_specs(cast_w, rows, lambda i, j: i * nj + j)
            in_specs.append(ispec)
            out_specs.append(ospec)
            out_shape.append(oshape)
            args = args + (cast_w,)
        body = functools.partial(_ffn_gate_body, period=None, tiles_per_seq=tps, cast=rows is not None)
    else:
        assert seqlen & (seqlen - 1) == 0
        tm = m
        in_specs = [pl.BlockSpec((tm, D_MODEL), lambda i, j: (i, 0)),
                    pl.BlockSpec((nparts, tm, 1), lambda i, j: (0, i, 0)), wspec, wspec, cwspec, cbspec,
                    pl.BlockSpec((tm, tn), lambda i, j: (i, j))]
        out_specs = [pl.BlockSpec((tm, tn), lambda i, j: (i, j)),
                     pl.BlockSpec((tm, tn), lambda i, j: (i, j))]
        out_shape = [jax.ShapeDtypeStruct((m, D_FF_PAD), BF16),
                     jax.ShapeDtypeStruct((m, D_FF_PAD), F32)]
        scratch = []
        args = (xw, ssq, w_gate, w_up, cw, cb, prev_rows)
        body = functools.partial(_ffn_gate_body, period=seqlen, tiles_per_seq=None)
    outs = pl.pallas_call(
        body,
        grid=(m // tm, nj),
        in_specs=in_specs,
        out_specs=out_specs,
        out_shape=out_shape,
        scratch_shapes=scratch,
        compiler_params=_params(2),
        name="ffn_gate",
    )(*args)
    return outs if len(outs) == 3 else (*outs, None)


def _down_body(a_ref, w_ref, res_ref, o_ref, *, chunk):
    @pl.when(pl.program_id(2) == 0)
    def _():
        o_ref[...] = res_ref[...]

    a = a_ref[...]
    tk = w_ref.shape[0]
    row = pl.program_id(2) * tk + lax.broadcasted_iota(jnp.int32, (tk, 1), 0)
    for c in range(o_ref.shape[1] // chunk):
        cs = slice(c * chunk, (c + 1) * chunk)
        o_ref[:, cs] += _dot(a, jnp.where(row < D_FF, w_ref[:, cs], 0))


def _down_proj(a, w_down, res, *, tm=1024, tn=2048, tk=1024):
    m, kdim = a.shape
    n = w_down.shape[1]
    tm = min(tm, m)
    return pl.pallas_call(
        functools.partial(_down_body, chunk=512),
        grid=(n // tn, m // tm, kdim // tk),
        in_specs=[pl.BlockSpec((tm, tk), lambda j, i, k: (i, k)),
                  pl.BlockSpec((tk, tn), lambda j, i, k: (k, j)),
                  pl.BlockSpec((tm, tn), lambda j, i, k: (i, j))],
        out_specs=pl.BlockSpec((tm, tn), lambda j, i, k: (i, j)),
        out_shape=jax.ShapeDtypeStruct((m, n), F32),
        compiler_params=_params(3),
        name="down_proj",
    )(a, w_down, res)


def _remember_bf16(bf16_w, wts, name, hosted):
    if name not in bf16_w:
        bf16_w[name] = hosted if hosted is not None else wts[name].astype(BF16)


def _layer_head(x, wts, bf16_w):
    h = _rmsnorm(x, wts["mix_norm_w"])
    if "w_in_t" in bf16_w:
        return _in_proj(h, wts["w_in_t"], bf16_w["w_in_t"])
    if x.shape[0] <= 1024:
        proj, dt_raw, bf16_w["w_in_t"] = _in_proj(h, wts["w_in_t"], keep_bf16=True)
        return proj, dt_raw
    return _in_proj(h, wts["w_in_t"])


def _layer(x, head, nseq, seqlen, wts, bf16_w, bias16, attn_state, ssm_state, conv_state, ffn_state):
    m = x.shape[0]
    todo = lambda name: None if name in bf16_w else wts[name]
    proj, dt_raw = head
    new_conv = proj.reshape(nseq, seqlen, PROJ_W)[:, seqlen - (SSD_CONV - 1):, COL_XBC:COL_XBC + CONV_DIM]

    qw = wts["q_norm_w"].reshape(1, HEAD_DIM)
    kw = wts["k_norm_w"].reshape(1, HEAD_DIM)
    if attn_state is None:
        names = [n for n in ("w_gate",) if n not in bf16_w]
        attn, new_k, new_v, hosted = _attn_prompt(proj, wts["attn_sinks"], qw, kw, bias16, nseq, seqlen,
                                                  cast_ws=tuple(wts[n] for n in names))
        for k, n in enumerate(names):
            _remember_bf16(bf16_w, wts, n, None if hosted is None else hosted[k])
        new_k = new_k.reshape(nseq, WINDOW, KV_HEADS, HEAD_DIM)
        new_v = new_v.reshape(nseq, WINDOW, KV_HEADS, HEAD_DIM)
        conv_prev = jnp.zeros((nseq, SSD_CONV - 1, CONV_DIM), F32)
        h0 = jnp.zeros((nseq, SSD_WIDTH, SSD_STATE), F32)
    else:
        win_k, win_v = attn_state
        attn, new_k, new_v = _attn_sample(proj, win_k, win_v, wts["attn_sinks"], qw, kw, bias16, nseq, seqlen)
        _remember_bf16(bf16_w, wts, "w_gate", None)
        conv_prev = conv_state
        h0 = ssm_state.reshape(nseq, SSD_WIDTH, SSD_STATE)

    names = [n for n in ("w_up", "w_out") if n not in bf16_w]
    y, h_t, hosted = _ssd(proj, dt_raw, conv_prev, h0, wts["ssd_conv_w"], wts["ssd_conv_b"], wts["ssd_dt_bias"],
                          wts["ssd_A_log"], wts["ssd_D"], wts["ssd_norm_w"], nseq, seqlen,
                          cast_ws=tuple(wts[n] for n in names))
    for k, n in enumerate(names):
        _remember_bf16(bf16_w, wts, n, None if hosted is None else hosted[k])
    h_t = h_t.reshape(nseq, SSD_HEADS, SSD_HEAD_DIM, SSD_STATE)

    x1, xw, ssq = _out_proj(attn, y, bf16_w["w_out"], x, wts["ffn_norm_w"])
    if ffn_state is None:
        a, last, hosted = _ffn_gate(xw, ssq, bf16_w["w_gate"], bf16_w["w_up"], wts["ffn_conv_w"],
                                    wts["ffn_conv_b"], nseq=nseq, seqlen=seqlen, cast_w=todo("w_down"))
        new_ffn = last.reshape(nseq, -1, FFN_CONV - 1, D_FF_PAD)[:, -1, :, :D_FF]
    else:
        prev_rows = jnp.pad(ffn_state, ((0, 0), (0, seqlen - (FFN_CONV - 1)), (0, D_FF_PAD - D_FF)))
        a, g, hosted = _ffn_gate(xw, ssq, bf16_w["w_gate"], bf16_w["w_up"], wts["ffn_conv_w"], wts["ffn_conv_b"],
                                 nseq=nseq, seqlen=seqlen, prev_rows=prev_rows.reshape(m, D_FF_PAD))
        new_ffn = g.reshape(nseq, seqlen, D_FF_PAD)[:, seqlen - (FFN_CONV - 1):, :D_FF]
    _remember_bf16(bf16_w, wts, "w_down", hosted)
    x2 = _down_proj(a, bf16_w["w_down"], x1)
    return x2, new_k, new_v, h_t, new_conv, new_ffn


def kernel(x_prompt, x_sample, state_attn_k, state_attn_v, state_ssm, state_ssd_conv, state_ffn_conv, rel_bias, mix_norm_w, w_in, q_norm_w, k_norm_w, attn_sinks, ssd_conv_w, ssd_conv_b, ssd_dt_bias, ssd_A_log, ssd_D, ssd_norm_w, w_out, ffn_norm_w, w_gate, w_up, ffn_conv_w, ffn_conv_b, w_down):
    depth = w_in.shape[0]
    bp, lp, _ = x_prompt.shape
    bs, ls, _ = x_sample.shape
    bias16 = _bias_table(rel_bias)
    yp = x_prompt.reshape(bp * lp, D_MODEL)
    ys = x_sample.reshape(bs * ls, D_MODEL)
    outs_p, outs_s = [], []
    pad_heads = lambda v: jnp.pad(v.reshape(1, SSD_HEADS), ((0, 0), (0, ROWS - SSD_HEADS)))
    pad_ff = lambda v: jnp.pad(v, ((0, 0), (0, D_FF_PAD - D_FF)))
    for l in range(depth):
        wts = dict(
            mix_norm_w=mix_norm_w[l],
            w_in_t=w_in[l].T,
            q_norm_w=q_norm_w[l], k_norm_w=k_norm_w[l], attn_sinks=attn_sinks[l],
            ssd_conv_w=ssd_conv_w[l], ssd_conv_b=ssd_conv_b[l].reshape(1, CONV_DIM),
            ssd_dt_bias=pad_heads(ssd_dt_bias[l]), ssd_A_log=pad_heads(ssd_A_log[l]),
            ssd_D=jnp.repeat(ssd_D[l], SSD_HEAD_DIM).reshape(1, SSD_WIDTH),
            ssd_norm_w=ssd_norm_w[l].reshape(1, SSD_WIDTH),
            w_out=w_out[l], ffn_norm_w=ffn_norm_w[l],
            w_gate=w_gate[l], w_up=w_up[l],
            ffn_conv_w=pad_ff(ffn_conv_w[l]), ffn_conv_b=pad_ff(ffn_conv_b[l].reshape(1, D_FF)),
            w_down=w_down[l],
        )
        bf16_w = {}
        head_s = _layer_head(ys, wts, bf16_w)
        head_p = _layer_head(yp, wts, bf16_w)
        yp, *sp = _layer(yp, head_p, bp, lp, wts, bf16_w, bias16, None, None, None, None)
        ys, *ss = _layer(ys, head_s, bs, ls, wts, bf16_w, bias16, (state_attn_k[l], state_attn_v[l]),
                         state_ssm[l], state_ssd_conv[l], state_ffn_conv[l])
        outs_p.append(sp)
        outs_s.append(ss)
    stack = lambda outs, i: jnp.stack([o[i] for o in outs])
    return (yp.reshape(bp, lp, D_MODEL), ys.reshape(bs, ls, D_MODEL),
            stack(outs_p, 0), stack(outs_p, 1), stack(outs_p, 2), stack(outs_p, 3), stack(outs_p, 4),
            stack(outs_s, 0), stack(outs_s, 1), stack(outs_s, 2), stack(outs_s, 3), stack(outs_s, 4))
```
